```python
import math
import jax, jax.numpy as jnp
from jax import lax
import numpy as np

D_MODEL = 1024
BATCH = 4
SEQ = 4096
DEPTH = 2


CHUNK = 64
HEAD_DIM = 64
FOX_HEADS = D_MODEL // HEAD_DIM
FOX_BLOCK = 128
SWA_Q_HEADS = D_MODEL // HEAD_DIM
SWA_KV_HEADS = 4
SWA_GROUP = SWA_Q_HEADS // SWA_KV_HEADS
WINDOW = 128
WINDOW_CHUNKS = WINDOW // CHUNK
BAND = (WINDOW_CHUNKS + 1) * CHUNK
REL_BUCKETS = 32
REL_MAX_DIST = 128
D_FF = 7 * D_MODEL // 2
N_EXPERTS = 8
TOP_K = 2
N_MOD = 6
EPS = 1e-6
N_EVEN = (DEPTH + 1) // 2
N_ODD = DEPTH // 2

kernel_name = "hybrid_fox_swa_sink_moe_adaln"


def rmsnorm(x, g):
    xf = x.astype(jnp.float32)
    y = xf * lax.rsqrt(jnp.mean(xf * xf, axis=-1, keepdims=True) + EPS)
    return (y * g.astype(jnp.float32)).astype(x.dtype)


def modulate(h, shift, scale):
    return h * (1.0 + scale[:, None, :]) + shift[:, None, :]


def t5_band_buckets():
    q = np.arange(CHUNK)[:, None]
    k = np.arange(BAND)[None, :] - WINDOW_CHUNKS * CHUNK
    rel = k - q
    nb = REL_BUCKETS // 2
    max_exact = nb // 2
    ret = (rel > 0).astype(np.int32) * nb
    n = np.abs(rel)
    large = max_exact + (np.log(np.maximum(n, 1) / max_exact)
                         / np.log(REL_MAX_DIST / max_exact) * (nb - max_exact)).astype(np.int32)
    large = np.minimum(large, nb - 1)
    return (ret + np.where(n < max_exact, n, large)).astype(np.int32)


def fox_attention(xn, w_in, b_f, w_out):
    B, S, D = xn.shape
    H, Dh = FOX_HEADS, HEAD_DIM
    q, k, v, f_logit = jnp.split(xn @ w_in, [D, 2 * D, 3 * D], axis=-1)
    q = q.reshape(B, S, H, Dh) * (Dh ** -0.5)
    k = k.reshape(B, S, H, Dh)
    v = v.reshape(B, S, H, Dh)
    log_f = jax.nn.log_sigmoid(f_logit.astype(jnp.float32) + b_f.astype(jnp.float32))
    cum = jnp.cumsum(log_f, axis=1).transpose(0, 2, 1)
    outs = []
    for blk in range(S // FOX_BLOCK):
        q0, q1 = blk * FOX_BLOCK, (blk + 1) * FOX_BLOCK
        s = jnp.einsum('bqhd,bkhd->bhqk', q[:, q0:q1], k[:, :q1]).astype(jnp.float32)
        s = s + cum[:, :, q0:q1, None] - cum[:, :, None, :q1]
        causal = jnp.arange(q1)[None, :] <= jnp.arange(q0, q1)[:, None]
        s = jnp.where(causal, s, -jnp.inf)
        p = jax.nn.softmax(s, axis=-1).astype(v.dtype)
        outs.append(jnp.einsum('bhqk,bkhd->bqhd', p, v[:, :q1]))
    o = jnp.concatenate(outs, axis=1).reshape(B, S, D)
    return o @ w_out


def swa_sink_attention(xn, w_in, sinks, rel_table, w_out):
    B, S, D = xn.shape
    Hk, G, Dh = SWA_KV_HEADS, SWA_GROUP, HEAD_DIM
    nC = S // CHUNK
    kvw = Hk * Dh
    q, k, v = jnp.split(xn @ w_in, [D, D + kvw], axis=-1)
    q = q.reshape(B, nC, CHUNK, Hk, G, Dh) * (Dh ** -0.5)

    def band(t):
        t = t.reshape(B, S, Hk, Dh)
        t = jnp.pad(t, ((0, 0), (WINDOW_CHUNKS * CHUNK, 0), (0, 0), (0, 0)))
        t = t.reshape(B, nC + WINDOW_CHUNKS, CHUNK, Hk, Dh)
        return jnp.concatenate([t[:, j:j + nC] for j in range(WINDOW_CHUNKS + 1)], axis=2)

    kb, vb = band(k), band(v)
    s = jnp.einsum('bcqhgd,bckhd->bchgqk', q, kb).astype(jnp.float32)
    bias = rel_table[t5_band_buckets()]
    bias = bias.transpose(2, 0, 1).reshape(Hk, G, CHUNK, BAND).astype(jnp.float32)
    s = s + bias
    kpos = (jnp.arange(nC)[:, None] * CHUNK - WINDOW_CHUNKS * CHUNK
            + jnp.arange(BAND)[None, :])
    s = jnp.where((kpos >= 0)[None, :, None, None, None, :], s, -jnp.inf)
    sink = sinks.astype(jnp.float32).reshape(1, 1, Hk, G, 1, 1)
    m = jnp.maximum(jnp.max(s, axis=-1, keepdims=True), sink)
    p = jnp.exp(s - m)
    p = p / (jnp.sum(p, axis=-1, keepdims=True) + jnp.exp(sink - m))
    o = jnp.einsum('bchgqk,bckhd->bcqhgd', p.astype(vb.dtype), vb).reshape(B, S, D)
    return o @ w_out


def swiglu(h, w_gu, w_down):
    g, u = jnp.split(h @ w_gu, 2, axis=-1)
    return (jax.nn.silu(g) * u) @ w_down


def moe_swiglu(xn, w_router, b_router, w_gu, w_down):
    B, S, D = xn.shape
    xt = xn.reshape(B * S, D)
    logits = (xt @ w_router).astype(jnp.float32) + b_router.astype(jnp.float32)
    top_val, top_idx = lax.top_k(logits, TOP_K)
    top_w = jax.nn.softmax(top_val, axis=-1)
    gates = jnp.sum(jax.nn.one_hot(top_idx, N_EXPERTS, dtype=jnp.float32) * top_w[..., None],
                    axis=1).astype(xt.dtype)
    out = jnp.zeros_like(xt)
    for e in range(N_EXPERTS):
        out = out + gates[:, e:e + 1] * swiglu(xt, w_gu[e], w_down[e])
    return out.reshape(B, S, D)


def setup_inputs(seed: int = 0) -> dict:
    key = jax.random.key(seed)
    ks = jax.random.split(key, 20)
    D, F, E = D_MODEL, D_FF, N_EXPERTS
    kvw = SWA_KV_HEADS * HEAD_DIM
    nrm = jax.random.normal
    sd = D ** -0.5
    sf = F ** -0.5
    return {
        "x": nrm(ks[0], (BATCH, SEQ, D), jnp.float32),
        "c": nrm(ks[1], (BATCH, D), jnp.float32),
        "w_ada": nrm(ks[2], (DEPTH, D, N_MOD * D), jnp.float32) * (0.5 * sd),
        "b_ada": nrm(ks[3], (DEPTH, N_MOD * D), jnp.float32) * 0.02,
        "g_norm_mix": 1.0 + 0.05 * nrm(ks[4], (DEPTH, D), jnp.float32),
        "g_norm_ffn": 1.0 + 0.05 * nrm(ks[5], (DEPTH, D), jnp.float32),
        "g_final": 1.0 + 0.05 * nrm(ks[6], (D,), jnp.float32),
        "fox_w_in": nrm(ks[7], (N_EVEN, D, 3 * D + FOX_HEADS), jnp.float32) * sd,
        "fox_b_f": 2.0 + 0.5 * nrm(ks[8], (N_EVEN, FOX_HEADS), jnp.float32),
        "fox_w_out": nrm(ks[9], (N_EVEN, D, D), jnp.float32) * sd,
        "swa_w_in": nrm(ks[10], (N_ODD, D, D + 2 * kvw), jnp.float32) * sd,
        "swa_sinks": 0.5 * nrm(ks[11], (N_ODD, SWA_Q_HEADS), jnp.float32),
        "swa_w_out": nrm(ks[12], (N_ODD, D, D), jnp.float32) * sd,
        "rel_bias": 0.5 * nrm(ks[13], (REL_BUCKETS, SWA_Q_HEADS), jnp.float32),
        "ffn_w_gu": nrm(ks[14], (N_EVEN, D, 2 * F), jnp.float32) * sd,
        "ffn_w_down": nrm(ks[15], (N_EVEN, F, D), jnp.float32) * sf,
        "moe_w_router": nrm(ks[16], (N_ODD, D, E), jnp.float32) * sd,
        "moe_b_router": 0.01 * nrm(ks[17], (N_ODD, E), jnp.float32),
        "moe_w_gu": nrm(ks[18], (N_ODD, E, D, 2 * F), jnp.float32) * sd,
        "moe_w_down": nrm(ks[19], (N_ODD, E, F, D), jnp.float32) * sf,
    }


def reference(x, c, w_ada, b_ada, g_norm_mix, g_norm_ffn, g_final,
              fox_w_in, fox_b_f, fox_w_out,
              swa_w_in, swa_sinks, swa_w_out, rel_bias,
              ffn_w_gu, ffn_w_down,
              moe_w_router, moe_b_router, moe_w_gu, moe_w_down):
    cond = jax.nn.silu(c)
    for i in range(DEPTH):
        j = i // 2
        mod = cond @ w_ada[i] + b_ada[i]
        sh1, sc1, gt1, sh2, sc2, gt2 = jnp.split(mod, N_MOD, axis=-1)
        h = modulate(rmsnorm(x, g_norm_mix[i]), sh1, sc1)
        if i % 2 == 0:
            y = fox_attention(h, fox_w_in[j], fox_b_f[j], fox_w_out[j])
        else:
            y = swa_sink_attention(h, swa_w_in[j], swa_sinks[j], rel_bias, swa_w_out[j])
        x = x + gt1[:, None, :] * y
        h = modulate(rmsnorm(x, g_norm_ffn[i]), sh2, sc2)
        if i % 2 == 0:
            y = swiglu(h, ffn_w_gu[j], ffn_w_down[j])
        else:
            y = moe_swiglu(h, moe_w_router[j], moe_b_router[j], moe_w_gu[j], moe_w_down[j])
        x = x + gt2[:, None, :] * y
    return rmsnorm(x, g_final)
```

```python
import functools

import numpy as np
import jax
import jax.numpy as jnp
from jax import lax
from jax.experimental import pallas as pl
from jax.experimental.pallas import tpu as pltpu

F32 = jnp.float32
BF16 = jnp.bfloat16
HIGHEST = lax.Precision.HIGHEST

HEAD_DIM = 64
CHUNK = 64
WINDOW_CHUNKS = 2
REL_BUCKETS = 32
REL_MAX_DIST = 128
SWA_KV_HEADS = 4
SWA_GROUP = 4
N_EXPERTS = 8
EPS = 1e-6

LANES = 128
SUBLANES = 8
VMEM_LIMIT = 56 * 1024 * 1024

NEG_BIG = -1e30


def _params(sem, vmem=VMEM_LIMIT):
    return pltpu.CompilerParams(dimension_semantics=sem, vmem_limit_bytes=vmem)


def _rms_mod(x, g, shift, scale):
    ms = jnp.mean(x * x, axis=-1, keepdims=True)
    y = x * lax.rsqrt(ms + EPS) * g
    return y * (1.0 + scale) + shift


def _silu(x):
    return x / (1.0 + jnp.exp(-x))


def _ada_kernel(c_ref, w_ref, b_ref, o_ref):
    cond = _silu(c_ref[...])
    o_ref[0] = jnp.dot(cond, w_ref[0], preferred_element_type=F32,
                       precision=HIGHEST) + b_ref[0]


def _ada_call(c_pad, w_ada, b_ada):
    depth, d, n = w_ada.shape
    tn = 1536
    return pl.pallas_call(
        _ada_kernel,
        out_shape=jax.ShapeDtypeStruct((depth, SUBLANES, n), F32),
        grid=(depth, n // tn),
        in_specs=[
            pl.BlockSpec((SUBLANES, d), lambda l, j: (0, 0)),
            pl.BlockSpec((1, d, tn), lambda l, j: (l, 0, j)),
            pl.BlockSpec((1, 1, tn), lambda l, j: (l, 0, j)),
        ],
        out_specs=pl.BlockSpec((1, SUBLANES, tn), lambda l, j: (l, 0, j)),
        compiler_params=_params(("parallel", "parallel")),
        name="ada_mod",
    )(c_pad, w_ada, b_ada.reshape(depth, 1, n))


def _norm_proj_kernel(*refs, splits, tiles_per_batch, with_gate):
    x_ref, g_ref, sh_ref, sc_ref, w_ref = refs[:5]
    pos = 5
    if with_gate:
        wf_ref = refs[pos]
        pos += 1
    out_refs = refs[pos:pos + len(splits)]
    b = pl.program_id(0) // tiles_per_batch
    h = _rms_mod(x_ref[...], g_ref[...], sh_ref[pl.ds(b, 1), :], sc_ref[pl.ds(b, 1), :])
    hb = h.astype(BF16)
    off = 0
    for o_ref, n in zip(out_refs, splits):
        o_ref[...] = jnp.dot(hb, w_ref[:, off:off + n],
                             preferred_element_type=F32).astype(o_ref.dtype)
        off += n
    if with_gate:
        f_ref = refs[pos + len(splits)]
        f_ref[...] = lax.dot_general(wf_ref[...], h, (((1,), (1,)), ((), ())),
                                     preferred_element_type=F32, precision=HIGHEST)


def _norm_proj_call(x, g, mod, sh_col, sc_col, w, splits, seq, wf_t=None, tm=512):
    n, d = x.shape
    n_out = w.shape[1]
    with_gate = wf_t is not None
    in_specs = [
        pl.BlockSpec((tm, d), lambda i: (i, 0)),
        pl.BlockSpec((1, d), lambda i: (0, 0)),
        pl.BlockSpec((SUBLANES, d), lambda i: (0, sh_col)),
        pl.BlockSpec((SUBLANES, d), lambda i: (0, sc_col)),
        pl.BlockSpec((d, n_out), lambda i: (0, 0)),
    ]
    args = [x, g.reshape(1, d), mod, mod, w]
    out_shape = [jax.ShapeDtypeStruct((n, s), BF16) for s in splits]
    out_specs = [pl.BlockSpec((tm, s), lambda i: (i, 0)) for s in splits]
    if with_gate:
        nh = wf_t.shape[0]
        in_specs.append(pl.BlockSpec((nh, d), lambda i: (0, 0)))
        args.append(wf_t)
        out_shape.append(jax.ShapeDtypeStruct((nh, n), F32))
        out_specs.append(pl.BlockSpec((nh, tm), lambda i: (0, i)))
    return pl.pallas_call(
        functools.partial(_norm_proj_kernel, splits=tuple(splits),
                          tiles_per_batch=seq // tm, with_gate=with_gate),
        out_shape=out_shape,
        grid=(n // tm,),
        in_specs=in_specs,
        out_specs=out_specs,
        compiler_params=_params(("parallel",)),
        name="norm_proj_gate" if with_gate else "norm_proj",
    )(*args)


_CUM_CHUNK = 512


def _cum_kernel(f_ref, bf_ref, o_ref):
    x = f_ref[...] + bf_ref[...]
    logf = jnp.minimum(x, 0.0) - jnp.log(1.0 + jnp.exp(-jnp.abs(x)))
    nh, seq = logf.shape
    r = lax.broadcasted_iota(jnp.int32, (_CUM_CHUNK, _CUM_CHUNK), 0)
    c = lax.broadcasted_iota(jnp.int32, (_CUM_CHUNK, _CUM_CHUNK), 1)
    upper = (r <= c).astype(F32)
    carry = jnp.zeros((nh, 1), F32)
    for ch in range(seq // _CUM_CHUNK):
        seg = logf[:, ch * _CUM_CHUNK:(ch + 1) * _CUM_CHUNK]
        cs = jnp.dot(seg, upper, preferred_element_type=F32, precision=HIGHEST) + carry
        o_ref[0, :, ch * _CUM_CHUNK:(ch + 1) * _CUM_CHUNK] = cs
        carry = cs[:, _CUM_CHUNK - 1:_CUM_CHUNK]


def _cum_call(f_t, b_f, batch, seq):
    nh = f_t.shape[0]
    return pl.pallas_call(
        _cum_kernel,
        out_shape=jax.ShapeDtypeStruct((batch, nh, seq), F32),
        grid=(batch,),
        in_specs=[
            pl.BlockSpec((nh, seq), lambda b: (0, b)),
            pl.BlockSpec((nh, 1), lambda b: (0, 0)),
        ],
        out_specs=pl.BlockSpec((1, nh, seq), lambda b: (b, 0, 0)),
        compiler_params=_params(("parallel",)),
        name="forget_cumsum",
    )(f_t, b_f.reshape(nh, 1))


_FOX_TQ = 512
_FOX_TK = 256


def _fox_kernel(q_ref, k_ref, v_ref, f_ref, o_ref):
    tq, tk = _FOX_TQ, _FOX_TK
    hp = pl.program_id(1)
    qi = pl.program_id(2)
    q2 = q_ref[0]
    lane = lax.broadcasted_iota(jnp.int32, (1, LANES), 1)
    row = lax.broadcasted_iota(jnp.int32, (tq, tk), 0)
    col = lax.broadcasted_iota(jnp.int32, (tq, tk), 1)
    n_full = qi * (tq // tk)
    q0 = pl.multiple_of(qi * tq, tq)
    out = jnp.zeros((tq, LANES), F32)
    for hh in range(2):
        head_lanes = (lane < HEAD_DIM) if hh == 0 else (lane >= HEAD_DIM)
        f_ref0 = f_ref[0, hh, :, pl.ds(q0, LANES)][:, 0:1]

        def step(kb, carry, mask):
            m, l, acc = carry
            k0 = pl.multiple_of(kb * tk, tk)
            kh = jnp.where(head_lanes, k_ref[0, pl.ds(k0, tk), :], 0)
            vh = jnp.where(head_lanes, v_ref[0, pl.ds(k0, tk), :], 0)
            s = lax.dot_general(q2, kh, (((1,), (1,)), ((), ())),
                                preferred_element_type=F32)
            s = s - (f_ref[0, hh, :, pl.ds(k0, tk)] - f_ref0)
            if mask is not None:
                s = jnp.where(mask, s, -jnp.inf)
            m_new = jnp.maximum(m, jnp.max(s, axis=1, keepdims=True))
            alpha = jnp.exp(m - m_new)
            p = jnp.exp(s - m_new)
            l = alpha * l + jnp.sum(p, axis=1, keepdims=True)
            acc = alpha * acc + jnp.dot(p.astype(BF16), vh, preferred_element_type=F32)
            return m_new, l, acc

        carry = (jnp.full((tq, 1), -jnp.inf, F32), jnp.zeros((tq, 1), F32),
                 jnp.zeros((tq, LANES), F32))
        carry = lax.fori_loop(0, n_full, lambda kb, c: step(kb, c, None), carry)
        for d in range(tq // tk):
            carry = step(n_full + d, carry, (col + d * tk) <= row)
        _, l, acc = carry
        out = out + acc / l
    o_ref[0] = out.astype(o_ref.dtype)


def _fox_call(q, k, v, f):
    batch, seq, d = q.shape
    n_pairs = d // LANES
    return pl.pallas_call(
        _fox_kernel,
        out_shape=jax.ShapeDtypeStruct((batch, seq, d), BF16),
        grid=(batch, n_pairs, seq // _FOX_TQ),
        in_specs=[
            pl.BlockSpec((1, _FOX_TQ, LANES), lambda b, h, i: (b, i, h)),
            pl.BlockSpec((1, seq, LANES), lambda b, h, i: (b, 0, h)),
            pl.BlockSpec((1, seq, LANES), lambda b, h, i: (b, 0, h)),
            pl.BlockSpec((1, 2, 1, seq), lambda b, h, i: (b, h, 0, 0)),
        ],
        out_specs=pl.BlockSpec((1, _FOX_TQ, LANES), lambda b, h, i: (b, i, h)),
        compiler_params=_params(("parallel", "parallel", "arbitrary")),
        name="fox_attention",
    )(q, k, v, f)


def _out_kernel(*refs, tiles_per_batch, with_router):
    o_ref, w_ref, x_ref, gt_ref = refs[:4]
    b = pl.program_id(0) // tiles_per_batch
    y = jnp.dot(o_ref[...], w_ref[...], preferred_element_type=F32)
    xn = x_ref[...] + gt_ref[pl.ds(b, 1), :] * y
    if not with_router:
        refs[4][...] = xn
        return
    g_ref, sh_ref, sc_ref, wr_ref, br_ref, xo_ref, h_ref, gates_ref = refs[4:]
    xo_ref[...] = xn
    h = _rms_mod(xn, g_ref[...], sh_ref[pl.ds(b, 1), :], sc_ref[pl.ds(b, 1), :])
    h_ref[...] = h.astype(BF16)
    logits = jnp.dot(h, wr_ref[...], preferred_element_type=F32,
                     precision=HIGHEST) + br_ref[...]
    lane = lax.broadcasted_iota(jnp.int32, logits.shape, 1).astype(F32)
    logits = jnp.where(lane < N_EXPERTS, logits, -jnp.inf)
    m1 = jnp.max(logits, axis=1, keepdims=True)
    i1 = jnp.min(jnp.where(logits == m1, lane, float(LANES)), axis=1, keepdims=True)
    rest = jnp.where(lane == i1, -jnp.inf, logits)
    m2 = jnp.max(rest, axis=1, keepdims=True)
    i2 = jnp.min(jnp.where(rest == m2, lane, float(LANES)), axis=1, keepdims=True)
    e2 = jnp.exp(m2 - m1)
    den = 1.0 + e2
    gates_ref[...] = (jnp.where(lane == i1, 1.0 / den, 0.0)
                      + jnp.where(lane == i2, e2 / den, 0.0))


def _out_call(o, w_out, x, mod, gate_col, seq, router=None, tm=512):
    n, d = x.shape
    with_router = router is not None
    in_specs = [
        pl.BlockSpec((tm, d), lambda i: (i, 0)),
        pl.BlockSpec((d, d), lambda i: (0, 0)),
        pl.BlockSpec((tm, d), lambda i: (i, 0)),
        pl.BlockSpec((SUBLANES, d), lambda i: (0, gate_col)),
    ]
    args = [o, w_out, x, mod]
    out_shape = [jax.ShapeDtypeStruct((n, d), F32)]
    out_specs = [pl.BlockSpec((tm, d), lambda i: (i, 0))]
    if with_router:
        g, sh_col, sc_col, w_r, b_r = router
        in_specs += [
            pl.BlockSpec((1, d), lambda i: (0, 0)),
            pl.BlockSpec((SUBLANES, d), lambda i: (0, sh_col)),
            pl.BlockSpec((SUBLANES, d), lambda i: (0, sc_col)),
            pl.BlockSpec((d, LANES), lambda i: (0, 0)),
            pl.BlockSpec((1, LANES), lambda i: (0, 0)),
        ]
        args += [g.reshape(1, d), mod, mod, w_r, b_r]
        out_shape += [jax.ShapeDtypeStruct((n, d), BF16),
                      jax.ShapeDtypeStruct((n, LANES), F32)]
        out_specs += [pl.BlockSpec((tm, d), lambda i: (i, 0)),
                      pl.BlockSpec((tm, LANES), lambda i: (i, 0))]
    res = pl.pallas_call(
        functools.partial(_out_kernel, tiles_per_batch=seq // tm, with_router=with_router),
        out_shape=out_shape,
        grid=(n // tm,),
        in_specs=in_specs,
        out_specs=out_specs,
        compiler_params=_params(("parallel",)),
        name="out_proj_router" if with_router else "out_proj",
    )(*args)
    return res if with_router else res[0]


def _ffn_kernel(x_ref, g_ref, sh_ref, sc_ref, gt_ref, wg_ref, wu_ref, wd_ref,
                o_ref, h_sc, acc_sc, *, tiles_per_batch):
    b = pl.program_id(0) // tiles_per_batch
    j = pl.program_id(1)

    @pl.when(j == 0)
    def _():
        h = _rms_mod(x_ref[...], g_ref[...], sh_ref[pl.ds(b, 1), :], sc_ref[pl.ds(b, 1), :])
        h_sc[...] = h.astype(BF16)
        acc_sc[...] = jnp.zeros_like(acc_sc)

    h = h_sc[...]
    g = jnp.dot(h, wg_ref[...], preferred_element_type=F32)
    u = jnp.dot(h, wu_ref[...], preferred_element_type=F32)
    a = (_silu(g) * u).astype(BF16)
    acc_sc[...] += jnp.dot(a, wd_ref[...], preferred_element_type=F32)

    @pl.when(j == pl.num_programs(1) - 1)
    def _():
        o_ref[...] = x_ref[...] + gt_ref[pl.ds(b, 1), :] * acc_sc[...]


def _ffn_call(x, g, mod, w_gu, w_down, seq, tm=1024, tf=512):
    n, d = x.shape
    f = w_down.shape[0]
    nj = f // tf
    return pl.pallas_call(
        functools.partial(_ffn_kernel, tiles_per_batch=seq // tm),
        out_shape=jax.ShapeDtypeStruct((n, d), F32),
        grid=(n // tm, nj),
        in_specs=[
            pl.BlockSpec((tm, d), lambda i, j: (i, 0)),
            pl.BlockSpec((1, d), lambda i, j: (0, 0)),
            pl.BlockSpec((SUBLANES, d), lambda i, j: (0, 3)),
            pl.BlockSpec((SUBLANES, d), lambda i, j: (0, 4)),
            pl.BlockSpec((SUBLANES, d), lambda i, j: (0, 5)),
            pl.BlockSpec((d, tf), lambda i, j: (0, j)),
            pl.BlockSpec((d, tf), lambda i, j: (0, j + nj)),
            pl.BlockSpec((tf, d), lambda i, j: (j, 0)),
        ],
        out_specs=pl.BlockSpec((tm, d), lambda i, j: (i, 0)),
        scratch_shapes=[pltpu.VMEM((tm, d), BF16), pltpu.VMEM((tm, d), F32)],
        compiler_params=_params(("parallel", "arbitrary")),
        name="ffn_swiglu",
    )(x, g.reshape(1, d), mod, mod, mod, w_gu, w_gu, w_down)


def _moe_kernel(h_ref, gates_ref, x_ref, gt_ref, gf_ref, wg_ref, wu_ref, wd_ref,
                o_ref, acc_sc, *, tiles_per_batch):
    b = pl.program_id(0) // tiles_per_batch
    e = pl.program_id(1)
    j = pl.program_id(2)

    @pl.when((e == 0) & (j == 0))
    def _():
        acc_sc[...] = jnp.zeros_like(acc_sc)

    gates = gates_ref[...]
    lane = lax.broadcasted_iota(jnp.int32, gates.shape, 1)
    gate_e = jnp.sum(jnp.where(lane == e, gates, 0.0), axis=1, keepdims=True)
    h = h_ref[...]
    g = jnp.dot(h, wg_ref[0], preferred_element_type=F32)
    u = jnp.dot(h, wu_ref[0], preferred_element_type=F32)
    a = (_silu(g) * u * gate_e).astype(BF16)
    acc_sc[...] += jnp.dot(a, wd_ref[0], preferred_element_type=F32)

    @pl.when((e == pl.num_programs(1) - 1) & (j == pl.num_programs(2) - 1))
    def _():
        xn = x_ref[...] + gt_ref[pl.ds(b, 1), :] * acc_sc[...]
        ms = jnp.mean(xn * xn, axis=-1, keepdims=True)
        o_ref[...] = xn * lax.rsqrt(ms + EPS) * gf_ref[...]


def _moe_call(h, gates, x, mod, g_final, w_gu, w_down, seq, tm=1024, tf=512):
    n, d = x.shape
    ne, f, _ = w_down.shape
    nj = f // tf
    return pl.pallas_call(
        functools.partial(_moe_kernel, tiles_per_batch=seq // tm),
        out_shape=jax.ShapeDtypeStruct((n, d), F32),
        grid=(n // tm, ne, nj),
        in_specs=[
            pl.BlockSpec((tm, d), lambda i, e, j: (i, 0)),
            pl.BlockSpec((tm, LANES), lambda i, e, j: (i, 0)),
            pl.BlockSpec((tm, d), lambda i, e, j: (i, 0)),
            pl.BlockSpec((SUBLANES, d), lambda i, e, j: (0, 5)),
            pl.BlockSpec((1, d), lambda i, e, j: (0, 0)),
            pl.BlockSpec((1, d, tf), lambda i, e, j: (e, 0, j)),
            pl.BlockSpec((1, d, tf), lambda i, e, j: (e, 0, j + nj)),
            pl.BlockSpec((1, tf, d), lambda i, e, j: (e, j, 0)),
        ],
        out_specs=pl.BlockSpec((tm, d), lambda i, e, j: (i, 0)),
        scratch_shapes=[pltpu.VMEM((tm, d), F32)],
        compiler_params=_params(("parallel", "arbitrary", "arbitrary")),
        name="moe_swiglu",
    )(h, gates, x, mod, g_final.reshape(1, d), w_gu, w_gu, w_down)


_SWA_TQ = 2 * CHUNK
_SWA_BAND = 2 * _SWA_TQ


def _swa_bucket_tile():
    r = np.arange(_SWA_TQ)[:, None]
    cc = np.arange(_SWA_BAND)[None, :]
    rel = cc - _SWA_TQ - r
    nb = REL_BUCKETS // 2
    max_exact = nb // 2
    ret = (rel > 0).astype(np.int32) * nb
    n = np.abs(rel)
    large = max_exact + (np.log(np.maximum(n, 1) / max_exact)
                         / np.log(REL_MAX_DIST / max_exact) * (nb - max_exact)).astype(np.int32)
    large = np.minimum(large, nb - 1)
    bucket = (ret + np.where(n < max_exact, n, large)).astype(np.int32)
    q_chunk = r // CHUNK
    k_chunk = cc // CHUNK
    visible = (k_chunk >= q_chunk) & (k_chunk <= q_chunk + WINDOW_CHUNKS)
    return np.where(visible, bucket, -1).astype(np.int32)


def _swa_bias_kernel(tbl_ref, bkt_ref, o_ref):
    bkt = bkt_ref[...]
    for hk in range(SWA_KV_HEADS):
        for par in range(2):
            for gg in range(2):
                head = hk * SWA_GROUP + par + 2 * gg
                tile = jnp.full(bkt.shape, NEG_BIG, F32)
                for bk in range(REL_BUCKETS):
                    tile = jnp.where(bkt == bk, tbl_ref[head, bk], tile)
                o_ref[hk, par, gg * _SWA_TQ:(gg + 1) * _SWA_TQ, :] = tile


def _swa_bias_call(rel_bias):
    bkt = jnp.asarray(_swa_bucket_tile())
    return pl.pallas_call(
        _swa_bias_kernel,
        out_shape=jax.ShapeDtypeStruct((SWA_KV_HEADS, 2, 2 * _SWA_TQ, _SWA_BAND), F32),
        in_specs=[
            pl.BlockSpec(memory_space=pltpu.SMEM),
            pl.BlockSpec(memory_space=pltpu.VMEM),
        ],
        out_specs=pl.BlockSpec(memory_space=pltpu.VMEM),
        name="swa_bias",
    )(rel_bias.T, bkt)


def _swa_kernel(q_ref, kp_ref, kc_ref, vp_ref, vc_ref, bias_ref, sink_ref, o_ref):
    tq = _SWA_TQ
    qi = pl.program_id(1)
    lane = lax.broadcasted_iota(jnp.int32, (1, LANES), 1)
    col = lax.broadcasted_iota(jnp.int32, (1, _SWA_BAND), 1)
    in_seq = (col + (qi - 1) * tq) >= 0
    for hk in range(SWA_KV_HEADS):
        ksl = slice(hk * LANES, (hk + 1) * LANES)
        kb = jnp.concatenate([kp_ref[0, :, ksl], kc_ref[0, :, ksl]], axis=0)
        vb = jnp.concatenate([vp_ref[0, :, ksl], vc_ref[0, :, ksl]], axis=0)
        qs = jnp.concatenate([q_ref[0, :, (2 * hk) * LANES:(2 * hk + 1) * LANES],
                              q_ref[0, :, (2 * hk + 1) * LANES:(2 * hk + 2) * LANES]],
                             axis=0)
        acc = jnp.zeros((2 * tq, LANES), F32)
        for par in range(2):
            head_lanes = (lane < HEAD_DIM) if par == 0 else (lane >= HEAD_DIM)
            kh = jnp.where(head_lanes, kb, 0)
            vh = jnp.where(head_lanes, vb, 0)
            s = lax.dot_general(qs, kh, (((1,), (1,)), ((), ())),
                                preferred_element_type=F32)
            s = s + bias_ref[hk, par]
            s = jnp.where(in_seq, s, -jnp.inf)
            sink = sink_ref[hk, par]
            m = jnp.maximum(jnp.max(s, axis=1, keepdims=True), sink)
            p = jnp.exp(s - m)
            den = jnp.sum(p, axis=1, keepdims=True) + jnp.exp(sink - m)
            acc = acc + jnp.dot((p / den).astype(BF16), vh, preferred_element_type=F32)
        o_ref[0, :, (2 * hk) * LANES:(2 * hk + 1) * LANES] = acc[:tq].astype(o_ref.dtype)
        o_ref[0, :, (2 * hk + 1) * LANES:(2 * hk + 2) * LANES] = acc[tq:].astype(o_ref.dtype)


def _swa_call(q, k, v, bias, sink):
    batch, seq, d = q.shape
    kw = k.shape[2]
    tq = _SWA_TQ
    prev = lambda b, i: (b, jnp.maximum(i - 1, 0), 0)
    cur = lambda b, i: (b, i, 0)
    return pl.pallas_call(
        _swa_kernel,
        out_shape=jax.ShapeDtypeStruct((batch, seq, d), BF16),
        grid=(batch, seq // tq),
        in_specs=[
            pl.BlockSpec((1, tq, d), cur),
            pl.BlockSpec((1, tq, kw), prev),
            pl.BlockSpec((1, tq, kw), cur),
            pl.BlockSpec((1, tq, kw), prev),
            pl.BlockSpec((1, tq, kw), cur),
            pl.BlockSpec(bias.shape, lambda b, i: (0, 0, 0, 0)),
            pl.BlockSpec(sink.shape, lambda b, i: (0, 0, 0, 0)),
        ],
        out_specs=pl.BlockSpec((1, tq, d), cur),
        compiler_params=_params(("parallel", "arbitrary")),
        name="swa_attention",
    )(q, k, k, v, v, bias, sink)


def kernel(x, c, w_ada, b_ada, g_norm_mix, g_norm_ffn, g_final, fox_w_in, fox_b_f, fox_w_out, swa_w_in, swa_sinks, swa_w_out, rel_bias, ffn_w_gu, ffn_w_down, moe_w_router, moe_b_router, moe_w_gu, moe_w_down):
    batch, seq, d = x.shape
    n = batch * seq
    q_scale = HEAD_DIM ** -0.5
    xf = x.reshape(n, d)

    c_pad = jnp.zeros((SUBLANES, d), F32).at[:batch].set(c)
    mod = _ada_call(c_pad, w_ada, b_ada)
    mod0, mod1 = mod[0], mod[1]

    w_in = fox_w_in[0]
    w_qkv = jnp.concatenate([w_in[:, :d] * q_scale, w_in[:, d:3 * d]], axis=1).astype(BF16)
    wf_t = w_in[:, 3 * d:].T
    q, k, v, f_t = _norm_proj_call(xf, g_norm_mix[0], mod0, 0, 1, w_qkv, (d, d, d), seq, wf_t=wf_t)
    f_cum = _cum_call(f_t, fox_b_f[0], batch, seq)
    o = _fox_call(q.reshape(batch, seq, d), k.reshape(batch, seq, d), v.reshape(batch, seq, d),
                  f_cum.reshape(batch, -1, 1, seq))
    x1 = _out_call(o.reshape(n, d), fox_w_out[0].astype(BF16), xf, mod0, 2, seq)
    x2 = _ffn_call(x1, g_norm_ffn[0], mod0, ffn_w_gu[0].astype(BF16), ffn_w_down[0].astype(BF16), seq)

    w_in = swa_w_in[0]
    kvw = SWA_KV_HEADS * HEAD_DIM
    dup = lambda w: jnp.repeat(w.reshape(d, SWA_KV_HEADS, 1, HEAD_DIM), 2, axis=2).reshape(d, 2 * kvw)
    w_qkv = jnp.concatenate([w_in[:, :d] * q_scale, dup(w_in[:, d:d + kvw]), dup(w_in[:, d + kvw:])],
                            axis=1).astype(BF16)
    q, k, v = _norm_proj_call(x2, g_norm_mix[1], mod1, 0, 1, w_qkv, (d, 2 * kvw, 2 * kvw), seq)
    bias = _swa_bias_call(rel_bias)
    sink = swa_sinks[0].reshape(SWA_KV_HEADS, 2, 2).transpose(0, 2, 1)
    sink = jnp.repeat(sink, _SWA_TQ, axis=2).reshape(SWA_KV_HEADS, 2, 2 * _SWA_TQ, 1)
    o = _swa_call(q.reshape(batch, seq, d), k.reshape(batch, seq, 2 * kvw),
                  v.reshape(batch, seq, 2 * kvw), bias, sink)
    w_r = jnp.zeros((d, LANES), F32).at[:, :N_EXPERTS].set(moe_w_router[0])
    b_r = jnp.zeros((1, LANES), F32).at[0, :N_EXPERTS].set(moe_b_router[0])
    x3, h4, gates = _out_call(o.reshape(n, d), swa_w_out[0].astype(BF16), x2, mod1, 2, seq,
                              router=(g_norm_ffn[1], 3, 4, w_r, b_r))
    out = _moe_call(h4, gates, x3, mod1, g_final, moe_w_gu[0].astype(BF16),
                    moe_w_down[0].astype(BF16), seq)
    return out.reshape(batch, seq, d)
```

```python
import functools

import numpy as np
import jax
import jax.numpy as jnp
from jax import lax
from jax.experimental import pallas as pl
from jax.experimental.pallas import tpu as pltpu

F32 = jnp.float32
BF16 = jnp.bfloat16
HIGHEST = lax.Precision.HIGHEST

HEAD_DIM = 64
CHUNK = 64
WINDOW_CHUNKS = 2
REL_BUCKETS = 32
REL_MAX_DIST = 128
SWA_KV_HEADS = 4
SWA_GROUP = 4
N_EXPERTS = 8
EPS = 1e-6

LANES = 128
SUBLANES = 8
VMEM_LIMIT = 56 * 1024 * 1024

NEG_BIG = -1e30


def _params(sem, vmem=VMEM_LIMIT):
    return pltpu.CompilerParams(dimension_semantics=sem, vmem_limit_bytes=vmem)


def _rms_mod(x, g, shift, scale):
    ms = jnp.mean(x * x, axis=-1, keepdims=True)
    y = x * lax.rsqrt(ms + EPS) * g
    return y * (1.0 + scale) + shift


def _silu(x):
    return x / (1.0 + jnp.exp(-x))


def _ada_kernel(c_ref, w_ref, b_ref, o_ref):
    cond = _silu(c_ref[...])
    o_ref[0] = jnp.dot(cond, w_ref[0], preferred_element_type=F32,
                       precision=HIGHEST) + b_ref[0]


def _ada_call(c_pad, w_ada, b_ada):
    depth, d, n = w_ada.shape
    tn = 1536
    return pl.pallas_call(
        _ada_kernel,
        out_shape=jax.ShapeDtypeStruct((depth, SUBLANES, n), F32),
        grid=(depth, n // tn),
        in_specs=[
            pl.BlockSpec((SUBLANES, d), lambda l, j: (0, 0)),
            pl.BlockSpec((1, d, tn), lambda l, j: (l, 0, j)),
            pl.BlockSpec((1, 1, tn), lambda l, j: (l, 0, j)),
        ],
        out_specs=pl.BlockSpec((1, SUBLANES, tn), lambda l, j: (l, 0, j)),
        compiler_params=_params(("parallel", "parallel")),
        name="ada_mod",
    )(c_pad, w_ada, b_ada.reshape(depth, 1, n))


def _norm_proj_kernel(*refs, splits, tiles_per_batch, with_gate):
    x_ref, g_ref, sh_ref, sc_ref, w_ref = refs[:5]
    pos = 5
    if with_gate:
        wf_ref = refs[pos]
        pos += 1
    out_refs = refs[pos:pos + len(splits)]
    b = pl.program_id(0) // tiles_per_batch
    h = _rms_mod(x_ref[...], g_ref[...], sh_ref[pl.ds(b, 1), :], sc_ref[pl.ds(b, 1), :])
    hb = h.astype(BF16)
    off = 0
    for o_ref, n in zip(out_refs, splits):
        o_ref[...] = jnp.dot(hb, w_ref[:, off:off + n],
                             preferred_element_type=F32).astype(o_ref.dtype)
        off += n
    if with_gate:
        f_ref = refs[pos + len(splits)]
        f_ref[...] = lax.dot_general(wf_ref[...], h, (((1,), (1,)), ((), ())),
                                     preferred_element_type=F32, precision=HIGHEST)


def _norm_proj_call(x, g, mod, sh_col, sc_col, w, splits, seq, wf_t=None, tm=512):
    n, d = x.shape
    n_out = w.shape[1]
    with_gate = wf_t is not None
    in_specs = [
        pl.BlockSpec((tm, d), lambda i: (i, 0)),
        pl.BlockSpec((1, d), lambda i: (0, 0)),
        pl.BlockSpec((SUBLANES, d), lambda i: (0, sh_col)),
        pl.BlockSpec((SUBLANES, d), lambda i: (0, sc_col)),
        pl.BlockSpec((d, n_out), lambda i: (0, 0)),
    ]
    args = [x, g.reshape(1, d), mod, mod, w]
    out_shape = [jax.ShapeDtypeStruct((n, s), BF16) for s in splits]
    out_specs = [pl.BlockSpec((tm, s), lambda i: (i, 0)) for s in splits]
    if with_gate:
        nh = wf_t.shape[0]
        in_specs.append(pl.BlockSpec((nh, d), lambda i: (0, 0)))
        args.append(wf_t)
        out_shape.append(jax.ShapeDtypeStruct((nh, n), F32))
        out_specs.append(pl.BlockSpec((nh, tm), lambda i: (0, i)))
    return pl.pallas_call(
        functools.partial(_norm_proj_kernel, splits=tuple(splits),
                          tiles_per_batch=seq // tm, with_gate=with_gate),
        out_shape=out_shape,
        grid=(n // tm,),
        in_specs=in_specs,
        out_specs=out_specs,
        compiler_params=_params(("parallel",)),
        name="norm_proj_gate" if with_gate else "norm_proj",
    )(*args)


_CUM_CHUNK = 512


def _cum_kernel(f_ref, bf_ref, o_ref):
    x = f_ref[...] + bf_ref[...]
    logf = jnp.minimum(x, 0.0) - jnp.log(1.0 + jnp.exp(-jnp.abs(x)))
    nh, seq = logf.shape
    r = lax.broadcasted_iota(jnp.int32, (_CUM_CHUNK, _CUM_CHUNK), 0)
    c = lax.broadcasted_iota(jnp.int32, (_CUM_CHUNK, _CUM_CHUNK), 1)
    upper = (r <= c).astype(F32)
    carry = jnp.zeros((nh, 1), F32)
    for ch in range(seq // _CUM_CHUNK):
        seg = logf[:, ch * _CUM_CHUNK:(ch + 1) * _CUM_CHUNK]
        cs = jnp.dot(seg, upper, preferred_element_type=F32, precision=HIGHEST) + carry
        o_ref[0, :, ch * _CUM_CHUNK:(ch + 1) * _CUM_CHUNK] = cs
        carry = cs[:, _CUM_CHUNK - 1:_CUM_CHUNK]


def _cum_call(f_t, b_f, batch, seq):
    nh = f_t.shape[0]
    return pl.pallas_call(
        _cum_kernel,
        out_shape=jax.ShapeDtypeStruct((batch, nh, seq), F32),
        grid=(batch,),
        in_specs=[
            pl.BlockSpec((nh, seq), lambda b: (0, b)),
            pl.BlockSpec((nh, 1), lambda b: (0, 0)),
        ],
        out_specs=pl.BlockSpec((1, nh, seq), lambda b: (b, 0, 0)),
        compiler_params=_params(("parallel",)),
        name="forget_cumsum",
    )(f_t, b_f.reshape(nh, 1))


_FOX_TQ = 512
_FOX_TK = 256


def _fox_kernel(q_ref, k_ref, v_ref, f_ref, o_ref):
    tq, tk = _FOX_TQ, _FOX_TK
    hp = pl.program_id(1)
    qi = pl.program_id(2)
    q2 = q_ref[0]
    lane = lax.broadcasted_iota(jnp.int32, (1, LANES), 1)
    row = lax.broadcasted_iota(jnp.int32, (tq, tk), 0)
    col = lax.broadcasted_iota(jnp.int32, (tq, tk), 1)
    n_full = qi * (tq // tk)
    q0 = pl.multiple_of(qi * tq, tq)
    out = jnp.zeros((tq, LANES), F32)
    for hh in range(2):
        head_lanes = (lane < HEAD_DIM) if hh == 0 else (lane >= HEAD_DIM)
        f_ref0 = f_ref[0, hh, :, pl.ds(q0, LANES)][:, 0:1]

        def step(kb, carry, mask):
            m, l, acc = carry
            k0 = pl.multiple_of(kb * tk, tk)
            kh = jnp.where(head_lanes, k_ref[0, pl.ds(k0, tk), :], 0)
            vh = jnp.where(head_lanes, v_ref[0, pl.ds(k0, tk), :], 0)
            s = lax.dot_general(q2, kh, (((1,), (1,)), ((), ())),
                                preferred_element_type=F32)
            s = s - (f_ref[0, hh, :, pl.ds(k0, tk)] - f_ref0)
            if mask is not None:
                s = jnp.where(mask, s, -jnp.inf)
            m_new = jnp.maximum(m, jnp.max(s, axis=1, keepdims=True))
            alpha = jnp.exp(m - m_new)
            p = jnp.exp(s - m_new)
            l = alpha * l + jnp.sum(p, axis=1, keepdims=True)
            acc = alpha * acc + jnp.dot(p.astype(BF16), vh, preferred_element_type=F32)
            return m_new, l, acc

        carry = (jnp.full((tq, 1), -jnp.inf, F32), jnp.zeros((tq, 1), F32),
                 jnp.zeros((tq, LANES), F32))
        carry = lax.fori_loop(0, n_full, lambda kb, c: step(kb, c, None), carry)
        for d in range(tq // tk):
            carry = step(n_full + d, carry, (col + d * tk) <= row)
        _, l, acc = carry
        out = out + acc / l
    o_ref[0] = out.astype(o_ref.dtype)


def _fox_call(q, k, v, f):
    batch, seq, d = q.shape
    n_pairs = d // LANES
    return pl.pallas_call(
        _fox_kernel,
        out_shape=jax.ShapeDtypeStruct((batch, seq, d), BF16),
        grid=(batch, n_pairs, seq // _FOX_TQ),
        in_specs=[
            pl.BlockSpec((1, _FOX_TQ, LANES), lambda b, h, i: (b, i, h)),
            pl.BlockSpec((1, seq, LANES), lambda b, h, i: (b, 0, h)),
            pl.BlockSpec((1, seq, LANES), lambda b, h, i: (b, 0, h)),
            pl.BlockSpec((1, 2, 1, seq), lambda b, h, i: (b, h, 0, 0)),
        ],
        out_specs=pl.BlockSpec((1, _FOX_TQ, LANES), lambda b, h, i: (b, i, h)),
        compiler_params=_params(("parallel", "parallel", "arbitrary")),
        name="fox_attention",
    )(q, k, v, f)


def _out_kernel(*refs, tiles_per_batch, with_router):
    o_ref, w_ref, x_ref, gt_ref = refs[:4]
    b = pl.program_id(0) // tiles_per_batch
    y = jnp.dot(o_ref[...], w_ref[...], preferred_element_type=F32)
    xn = x_ref[...] + gt_ref[pl.ds(b, 1), :] * y
    if not with_router:
        refs[4][...] = xn
        return
    g_ref, sh_ref, sc_ref, wr_ref, br_ref, xo_ref, h_ref, route_ref, cnt_ref = refs[4:]
    xo_ref[...] = xn
    h = _rms_mod(xn, g_ref[...], sh_ref[pl.ds(b, 1), :], sc_ref[pl.ds(b, 1), :])
    h_ref[...] = h
    logits = jnp.dot(h, wr_ref[...], preferred_element_type=F32,
                     precision=HIGHEST) + br_ref[...]
    tm = logits.shape[0]
    lane = lax.broadcasted_iota(jnp.int32, logits.shape, 1).astype(F32)
    logits = jnp.where(lane < N_EXPERTS, logits, -jnp.inf)
    m1 = jnp.max(logits, axis=1, keepdims=True)
    i1 = jnp.min(jnp.where(logits == m1, lane, float(LANES)), axis=1, keepdims=True)
    rest = jnp.where(lane == i1, -jnp.inf, logits)
    m2 = jnp.max(rest, axis=1, keepdims=True)
    i2 = jnp.min(jnp.where(rest == m2, lane, float(LANES)), axis=1, keepdims=True)
    e2 = jnp.exp(m2 - m1)
    den = 1.0 + e2

    @pl.when(pl.program_id(0) == 0)
    def _():
        cnt_ref[...] = jnp.zeros_like(cnt_ref)

    sel1 = lane == i1
    sel2 = lane == i2
    onehot = jnp.where(sel1 | sel2, 1.0, 0.0)
    r = lax.broadcasted_iota(jnp.int32, (tm, tm), 0)
    c = lax.broadcasted_iota(jnp.int32, (tm, tm), 1)
    lower = jnp.where(c < r, 1.0, 0.0).astype(BF16)
    before = jnp.dot(lower, onehot.astype(BF16), preferred_element_type=F32) + cnt_ref[0:1, :]
    rank1 = jnp.sum(jnp.where(sel1, before, 0.0), axis=1, keepdims=True)
    rank2 = jnp.sum(jnp.where(sel2, before, 0.0), axis=1, keepdims=True)
    cnt_ref[0:1, :] = cnt_ref[0:1, :] + jnp.sum(onehot, axis=0, keepdims=True)
    route = jnp.where(lane == 0.0, i1, 0.0)
    for k, val in enumerate((i2, 1.0 / den, e2 / den, rank1, rank2), start=1):
        route = jnp.where(lane == float(k), val, route)
    route_ref[...] = route


def _out_call(o, w_out, x, mod, gate_col, seq, router=None, tm=512):
    n, d = x.shape
    with_router = router is not None
    in_specs = [
        pl.BlockSpec((tm, d), lambda i: (i, 0)),
        pl.BlockSpec((d, d), lambda i: (0, 0)),
        pl.BlockSpec((tm, d), lambda i: (i, 0)),
        pl.BlockSpec((SUBLANES, d), lambda i: (0, gate_col)),
    ]
    args = [o, w_out, x, mod]
    out_shape = [jax.ShapeDtypeStruct((n, d), F32)]
    out_specs = [pl.BlockSpec((tm, d), lambda i: (i, 0))]
    if with_router:
        g, sh_col, sc_col, w_r, b_r = router
        in_specs += [
            pl.BlockSpec((1, d), lambda i: (0, 0)),
            pl.BlockSpec((SUBLANES, d), lambda i: (0, sh_col)),
            pl.BlockSpec((SUBLANES, d), lambda i: (0, sc_col)),
            pl.BlockSpec((d, LANES), lambda i: (0, 0)),
            pl.BlockSpec((1, LANES), lambda i: (0, 0)),
        ]
        args += [g.reshape(1, d), mod, mod, w_r, b_r]
        out_shape += [jax.ShapeDtypeStruct((n, d), F32),
                      jax.ShapeDtypeStruct((n, LANES), F32),
                      jax.ShapeDtypeStruct((SUBLANES, LANES), F32)]
        out_specs += [pl.BlockSpec((tm, d), lambda i: (i, 0)),
                      pl.BlockSpec((tm, LANES), lambda i: (i, 0)),
                      pl.BlockSpec((SUBLANES, LANES), lambda i: (0, 0))]
    res = pl.pallas_call(
        functools.partial(_out_kernel, tiles_per_batch=seq // tm, with_router=with_router),
        out_shape=out_shape,
        grid=(n // tm,),
        in_specs=in_specs,
        out_specs=out_specs,
        compiler_params=_params(("arbitrary",) if with_router else ("parallel",)),
        name="out_proj_router" if with_router else "out_proj",
    )(*args)
    return res if with_router else res[0]


def _ffn_kernel(x_ref, g_ref, sh_ref, sc_ref, gt_ref, wg_ref, wu_ref, wd_ref,
                o_ref, h_sc, acc_sc, *, tiles_per_batch):
    b = pl.program_id(0) // tiles_per_batch
    j = pl.program_id(1)

    @pl.when(j == 0)
    def _():
        h = _rms_mod(x_ref[...], g_ref[...], sh_ref[pl.ds(b, 1), :], sc_ref[pl.ds(b, 1), :])
        h_sc[...] = h.astype(BF16)
        acc_sc[...] = jnp.zeros_like(acc_sc)

    h = h_sc[...]
    g = jnp.dot(h, wg_ref[...], preferred_element_type=F32)
    u = jnp.dot(h, wu_ref[...], preferred_element_type=F32)
    a = (_silu(g) * u).astype(BF16)
    acc_sc[...] += jnp.dot(a, wd_ref[...], preferred_element_type=F32)

    @pl.when(j == pl.num_programs(1) - 1)
    def _():
        o_ref[...] = x_ref[...] + gt_ref[pl.ds(b, 1), :] * acc_sc[...]


def _ffn_call(x, g, mod, w_gu, w_down, seq, tm=1024, tf=512):
    n, d = x.shape
    f = w_down.shape[0]
    nj = f // tf
    return pl.pallas_call(
        functools.partial(_ffn_kernel, tiles_per_batch=seq // tm),
        out_shape=jax.ShapeDtypeStruct((n, d), F32),
        grid=(n // tm, nj),
        in_specs=[
            pl.BlockSpec((tm, d), lambda i, j: (i, 0)),
            pl.BlockSpec((1, d), lambda i, j: (0, 0)),
            pl.BlockSpec((SUBLANES, d), lambda i, j: (0, 3)),
            pl.BlockSpec((SUBLANES, d), lambda i, j: (0, 4)),
            pl.BlockSpec((SUBLANES, d), lambda i, j: (0, 5)),
            pl.BlockSpec((d, tf), lambda i, j: (0, j)),
            pl.BlockSpec((d, tf), lambda i, j: (0, j + nj)),
            pl.BlockSpec((tf, d), lambda i, j: (j, 0)),
        ],
        out_specs=pl.BlockSpec((tm, d), lambda i, j: (i, 0)),
        scratch_shapes=[pltpu.VMEM((tm, d), BF16), pltpu.VMEM((tm, d), F32)],
        compiler_params=_params(("parallel", "arbitrary")),
        name="ffn_swiglu",
    )(x, g.reshape(1, d), mod, mod, mod, w_gu, w_gu, w_down)


_MOE_TM = 512


def _dispatch_kernel(d1_ref, d2_ref, h_ref, xs_in_ref, xs_ref, sem):
    del xs_in_ref
    tm = h_ref.shape[0]
    base = pl.program_id(0) * tm

    def issue(r, carry):
        src = h_ref.at[pl.ds(r, 1)]
        pltpu.make_async_copy(src, xs_ref.at[pl.ds(d1_ref[base + r], 1)], sem.at[0]).start()
        pltpu.make_async_copy(src, xs_ref.at[pl.ds(d2_ref[base + r], 1)], sem.at[1]).start()
        return carry

    lax.fori_loop(0, tm, issue, 0, unroll=8)
    pltpu.make_async_copy(h_ref, xs_ref.at[pl.ds(0, tm)], sem.at[0]).wait()
    pltpu.make_async_copy(h_ref, xs_ref.at[pl.ds(0, tm)], sem.at[1]).wait()


def _dispatch_call(dest1, dest2, h, n_rows, tm=512):
    n, d = h.shape
    zeros = jnp.zeros((n_rows, d), h.dtype)
    return pl.pallas_call(
        _dispatch_kernel,
        out_shape=jax.ShapeDtypeStruct((n_rows, d), h.dtype),
        grid_spec=pltpu.PrefetchScalarGridSpec(
            num_scalar_prefetch=2,
            grid=(n // tm,),
            in_specs=[
                pl.BlockSpec((tm, d), lambda i, d1, d2: (i, 0)),
                pl.BlockSpec(memory_space=pl.ANY),
            ],
            out_specs=pl.BlockSpec(memory_space=pl.ANY),
            scratch_shapes=[pltpu.SemaphoreType.DMA((2,))],
        ),
        input_output_aliases={3: 0},
        compiler_params=_params(("arbitrary",)),
        name="moe_dispatch",
    )(dest1, dest2, h, zeros)


def _experts_kernel(te_ref, nt_ref, xs_ref, wg_ref, wu_ref, wd_ref, ye_ref, h_sc, acc_sc):
    del te_ref
    t = pl.program_id(0)
    j = pl.program_id(1)

    @pl.when(t < nt_ref[0])
    def _():
        @pl.when(j == 0)
        def _():
            h_sc[...] = xs_ref[...].astype(BF16)
            acc_sc[...] = jnp.zeros_like(acc_sc)

        h = h_sc[...]
        g = jnp.dot(h, wg_ref[0], preferred_element_type=F32)
        u = jnp.dot(h, wu_ref[0], preferred_element_type=F32)
        a = (_silu(g) * u).astype(BF16)
        acc_sc[...] += jnp.dot(a, wd_ref[0], preferred_element_type=F32)

        @pl.when(j == pl.num_programs(1) - 1)
        def _():
            ye_ref[...] = acc_sc[...]

    @pl.when((t >= nt_ref[0]) & (j == 0))
    def _():
        ye_ref[...] = jnp.zeros_like(ye_ref)


def _experts_call(tile_expert, n_tiles, xs, w_gu, w_down, tf=896):
    n_rows, d = xs.shape
    _, f, _ = w_down.shape
    nj = f // tf
    tm = _MOE_TM
    tile = lambda t, nt: jnp.minimum(t, nt[0] - 1)
    jj = lambda t, j, nt: jnp.where(t < nt[0], j, nj - 1)
    return pl.pallas_call(
        _experts_kernel,
        out_shape=jax.ShapeDtypeStruct((n_rows, d), F32),
        grid_spec=pltpu.PrefetchScalarGridSpec(
            num_scalar_prefetch=2,
            grid=(n_rows // tm, nj),
            in_specs=[
                pl.BlockSpec((tm, d), lambda t, j, te, nt: (tile(t, nt), 0)),
                pl.BlockSpec((1, d, tf), lambda t, j, te, nt: (te[tile(t, nt)], 0, jj(t, j, nt))),
                pl.BlockSpec((1, d, tf), lambda t, j, te, nt: (te[tile(t, nt)], 0, jj(t, j, nt) + nj)),
                pl.BlockSpec((1, tf, d), lambda t, j, te, nt: (te[tile(t, nt)], jj(t, j, nt), 0)),
            ],
            out_specs=pl.BlockSpec((tm, d), lambda t, j, te, nt: (t, 0)),
            scratch_shapes=[pltpu.VMEM((tm, d), BF16), pltpu.VMEM((tm, d), F32)],
        ),
        compiler_params=_params(("arbitrary", "arbitrary")),
        name="moe_experts",
    )(tile_expert, n_tiles, xs, w_gu, w_gu, w_down)


def _combine_kernel(d1_ref, d2_ref, ye_ref, x_ref, route_ref, gt_ref, gf_ref, o_ref,
                    y1_sc, y2_sc, sem, *, tiles_per_batch):
    tm = x_ref.shape[0]
    i = pl.program_id(0)
    b = i // tiles_per_batch
    base = i * tm

    def issue(r, carry):
        pltpu.make_async_copy(ye_ref.at[pl.ds(d1_ref[base + r], 1)], y1_sc.at[pl.ds(r, 1)],
                              sem.at[0]).start()
        pltpu.make_async_copy(ye_ref.at[pl.ds(d2_ref[base + r], 1)], y2_sc.at[pl.ds(r, 1)],
                              sem.at[1]).start()
        return carry

    lax.fori_loop(0, tm, issue, 0, unroll=8)
    pltpu.make_async_copy(ye_ref.at[pl.ds(0, tm)], y1_sc, sem.at[0]).wait()
    pltpu.make_async_copy(ye_ref.at[pl.ds(0, tm)], y2_sc, sem.at[1]).wait()
    route = route_ref[...]
    y = route[:, 2:3] * y1_sc[...] + route[:, 3:4] * y2_sc[...]
    xn = x_ref[...] + gt_ref[pl.ds(b, 1), :] * y
    ms = jnp.mean(xn * xn, axis=-1, keepdims=True)
    o_ref[...] = xn * lax.rsqrt(ms + EPS) * gf_ref[...]


def _combine_call(dest1, dest2, ye, x, route, mod, g_final, seq, tm=512):
    n, d = x.shape
    return pl.pallas_call(
        functools.partial(_combine_kernel, tiles_per_batch=seq // tm),
        out_shape=jax.ShapeDtypeStruct((n, d), F32),
        grid_spec=pltpu.PrefetchScalarGridSpec(
            num_scalar_prefetch=2,
            grid=(n // tm,),
            in_specs=[
                pl.BlockSpec(memory_space=pl.ANY),
                pl.BlockSpec((tm, d), lambda i, d1, d2: (i, 0)),
                pl.BlockSpec((tm, LANES), lambda i, d1, d2: (i, 0)),
                pl.BlockSpec((SUBLANES, d), lambda i, d1, d2: (0, 5)),
                pl.BlockSpec((1, d), lambda i, d1, d2: (0, 0)),
            ],
            out_specs=pl.BlockSpec((tm, d), lambda i, d1, d2: (i, 0)),
            scratch_shapes=[pltpu.VMEM((tm, d), F32), pltpu.VMEM((tm, d), F32),
                            pltpu.SemaphoreType.DMA((2,))],
        ),
        compiler_params=_params(("arbitrary",)),
        name="moe_combine",
    )(dest1, dest2, ye, x, route, mod, g_final.reshape(1, d))


def _moe_call(h, route, counts, x, mod, g_final, w_gu, w_down, seq):
    n, d = x.shape
    ne = w_down.shape[0]
    tm = _MOE_TM
    max_tiles = (2 * n) // tm + ne
    e1 = route[:, 0].astype(jnp.int32)
    e2 = route[:, 1].astype(jnp.int32)
    cnt = counts[0, :ne].astype(jnp.int32)
    tiles_e = (cnt + tm - 1) // tm
    tile_end = jnp.cumsum(tiles_e)
    row_start = (tile_end - tiles_e) * tm
    dest1 = row_start[e1] + route[:, 4].astype(jnp.int32)
    dest2 = row_start[e2] + route[:, 5].astype(jnp.int32)
    n_tiles = tile_end[-1:]
    tile_expert = jnp.minimum(
        jnp.searchsorted(tile_end, jnp.arange(max_tiles, dtype=jnp.int32), side="right"),
        ne - 1).astype(jnp.int32)
    xs = _dispatch_call(dest1, dest2, h, max_tiles * tm)
    ye = _experts_call(tile_expert, n_tiles, xs, w_gu, w_down)
    return _combine_call(dest1, dest2, ye, x, route, mod, g_final, seq)


_SWA_TQ = 2 * CHUNK
_SWA_BAND = 2 * _SWA_TQ


def _swa_bucket_tile():
    r = np.arange(_SWA_TQ)[:, None]
    cc = np.arange(_SWA_BAND)[None, :]
    rel = cc - _SWA_TQ - r
    nb = REL_BUCKETS // 2
    max_exact = nb // 2
    ret = (rel > 0).astype(np.int32) * nb
    n = np.abs(rel)
    large = max_exact + (np.log(np.maximum(n, 1) / max_exact)
                         / np.log(REL_MAX_DIST / max_exact) * (nb - max_exact)).astype(np.int32)
    large = np.minimum(large, nb - 1)
    bucket = (ret + np.where(n < max_exact, n, large)).astype(np.int32)
    q_chunk = r // CHUNK
    k_chunk = cc // CHUNK
    visible = (k_chunk >= q_chunk) & (k_chunk <= q_chunk + WINDOW_CHUNKS)
    return np.where(visible, bucket, -1).astype(np.int32)


def _swa_bias_kernel(tbl_ref, bkt_ref, o_ref):
    bkt = bkt_ref[...]
    for hk in range(SWA_KV_HEADS):
        for par in range(2):
            for gg in range(2):
                head = hk * SWA_GROUP + par + 2 * gg
                tile = jnp.full(bkt.shape, NEG_BIG, F32)
                for bk in range(REL_BUCKETS):
                    tile = jnp.where(bkt == bk, tbl_ref[head, bk], tile)
                o_ref[hk, par, gg * _SWA_TQ:(gg + 1) * _SWA_TQ, :] = tile


def _swa_bias_call(rel_bias):
    bkt = jnp.asarray(_swa_bucket_tile())
    return pl.pallas_call(
        _swa_bias_kernel,
        out_shape=jax.ShapeDtypeStruct((SWA_KV_HEADS, 2, 2 * _SWA_TQ, _SWA_BAND), F32),
        in_specs=[
            pl.BlockSpec(memory_space=pltpu.SMEM),
            pl.BlockSpec(memory_space=pltpu.VMEM),
        ],
        out_specs=pl.BlockSpec(memory_space=pltpu.VMEM),
        name="swa_bias",
    )(rel_bias.T, bkt)


def _swa_kernel(q_ref, kp_ref, kc_ref, vp_ref, vc_ref, bias_ref, sink_ref, o_ref):
    tq = _SWA_TQ
    qi = pl.program_id(1)
    lane = lax.broadcasted_iota(jnp.int32, (1, LANES), 1)
    col = lax.broadcasted_iota(jnp.int32, (1, _SWA_BAND), 1)
    in_seq = (col + (qi - 1) * tq) >= 0
    for hk in range(SWA_KV_HEADS):
        ksl = slice(hk * LANES, (hk + 1) * LANES)
        kb = jnp.concatenate([kp_ref[0, :, ksl], kc_ref[0, :, ksl]], axis=0)
        vb = jnp.concatenate([vp_ref[0, :, ksl], vc_ref[0, :, ksl]], axis=0)
        qs = jnp.concatenate([q_ref[0, :, (2 * hk) * LANES:(2 * hk + 1) * LANES],
                              q_ref[0, :, (2 * hk + 1) * LANES:(2 * hk + 2) * LANES]],
                             axis=0)
        acc = jnp.zeros((2 * tq, LANES), F32)
        for par in range(2):
            head_lanes = (lane < HEAD_DIM) if par == 0 else (lane >= HEAD_DIM)
            kh = jnp.where(head_lanes, kb, 0)
            vh = jnp.where(head_lanes, vb, 0)
            s = lax.dot_general(qs, kh, (((1,), (1,)), ((), ())),
                                preferred_element_type=F32)
            s = s + bias_ref[hk, par]
            s = jnp.where(in_seq, s, -jnp.inf)
            sink = sink_ref[hk, par]
            m = jnp.maximum(jnp.max(s, axis=1, keepdims=True), sink)
            p = jnp.exp(s - m)
            den = jnp.sum(p, axis=1, keepdims=True) + jnp.exp(sink - m)
            acc = acc + jnp.dot((p / den).astype(BF16), vh, preferred_element_type=F32)
        o_ref[0, :, (2 * hk) * LANES:(2 * hk + 1) * LANES] = acc[:tq].astype(o_ref.dtype)
        o_ref[0, :, (2 * hk + 1) * LANES:(2 * hk + 2) * LANES] = acc[tq:].astype(o_ref.dtype)


def _swa_call(q, k, v, bias, sink):
    batch, seq, d = q.shape
    kw = k.shape[2]
    tq = _SWA_TQ
    prev = lambda b, i: (b, jnp.maximum(i - 1, 0), 0)
    cur = lambda b, i: (b, i, 0)
    return pl.pallas_call(
        _swa_kernel,
        out_shape=jax.ShapeDtypeStruct((batch, seq, d), BF16),
        grid=(batch, seq // tq),
        in_specs=[
            pl.BlockSpec((1, tq, d), cur),
            pl.BlockSpec((1, tq, kw), prev),
            pl.BlockSpec((1, tq, kw), cur),
            pl.BlockSpec((1, tq, kw), prev),
            pl.BlockSpec((1, tq, kw), cur),
            pl.BlockSpec(bias.shape, lambda b, i: (0, 0, 0, 0)),
            pl.BlockSpec(sink.shape, lambda b, i: (0, 0, 0, 0)),
        ],
        out_specs=pl.BlockSpec((1, tq, d), cur),
        compiler_params=_params(("parallel", "arbitrary")),
        name="swa_attention",
    )(q, k, k, v, v, bias, sink)


def kernel(x, c, w_ada, b_ada, g_norm_mix, g_norm_ffn, g_final, fox_w_in, fox_b_f, fox_w_out, swa_w_in, swa_sinks, swa_w_out, rel_bias, ffn_w_gu, ffn_w_down, moe_w_router, moe_b_router, moe_w_gu, moe_w_down):
    batch, seq, d = x.shape
    n = batch * seq
    q_scale = HEAD_DIM ** -0.5
    xf = x.reshape(n, d)

    c_pad = jnp.zeros((SUBLANES, d), F32).at[:batch].set(c)
    mod = _ada_call(c_pad, w_ada, b_ada)
    mod0, mod1 = mod[0], mod[1]

    w_in = fox_w_in[0]
    w_qkv = jnp.concatenate([w_in[:, :d] * q_scale, w_in[:, d:3 * d]], axis=1).astype(BF16)
    wf_t = w_in[:, 3 * d:].T
    q, k, v, f_t = _norm_proj_call(xf, g_norm_mix[0], mod0, 0, 1, w_qkv, (d, d, d), seq, wf_t=wf_t)
    f_cum = _cum_call(f_t, fox_b_f[0], batch, seq)
    o = _fox_call(q.reshape(batch, seq, d), k.reshape(batch, seq, d), v.reshape(batch, seq, d),
                  f_cum.reshape(batch, -1, 1, seq))
    x1 = _out_call(o.reshape(n, d), fox_w_out[0].astype(BF16), xf, mod0, 2, seq)
    x2 = _ffn_call(x1, g_norm_ffn[0], mod0, ffn_w_gu[0].astype(BF16), ffn_w_down[0].astype(BF16), seq)

    w_in = swa_w_in[0]
    kvw = SWA_KV_HEADS * HEAD_DIM
    dup = lambda w: jnp.repeat(w.reshape(d, SWA_KV_HEADS, 1, HEAD_DIM), 2, axis=2).reshape(d, 2 * kvw)
    w_qkv = jnp.concatenate([w_in[:, :d] * q_scale, dup(w_in[:, d:d + kvw]), dup(w_in[:, d + kvw:])],
                            axis=1).astype(BF16)
    q, k, v = _norm_proj_call(x2, g_norm_mix[1], mod1, 0, 1, w_qkv, (d, 2 * kvw, 2 * kvw), seq)
    bias = _swa_bias_call(rel_bias)
    sink = swa_sinks[0].reshape(SWA_KV_HEADS, 2, 2).transpose(0, 2, 1)
    sink = jnp.repeat(sink, _SWA_TQ, axis=2).reshape(SWA_KV_HEADS, 2, 2 * _SWA_TQ, 1)
    o = _swa_call(q.reshape(batch, seq, d), k.reshape(batch, seq, 2 * kvw),
                  v.reshape(batch, seq, 2 * kvw), bias, sink)
    w_r = jnp.zeros((d, LANES), F32).at[:, :N_EXPERTS].set(moe_w_router[0])
    b_r = jnp.zeros((1, LANES), F32).at[0, :N_EXPERTS].set(moe_b_router[0])
    x3, h4, route, counts = _out_call(o.reshape(n, d), swa_w_out[0].astype(BF16), x2, mod1, 2, seq,
                                      router=(g_norm_ffn[1], 3, 4, w_r, b_r))
    out = _moe_call(h4, route, counts, x3, mod1, g_final, moe_w_gu[0].astype(BF16),
                    moe_w_down[0].astype(BF16), seq)
    return out.reshape(batch, seq, d)
```

```python
import functools

import numpy as np
import jax
import jax.numpy as jnp
from jax import lax
from jax.experimental import pallas as pl
from jax.experimental.pallas import tpu as pltpu

F32 = jnp.float32
BF16 = jnp.bfloat16
HIGHEST = lax.Precision.HIGHEST

HEAD_DIM = 64
CHUNK = 64
WINDOW_CHUNKS = 2
REL_BUCKETS = 32
REL_MAX_DIST = 128
SWA_KV_HEADS = 4
SWA_GROUP = 4
N_EXPERTS = 8
EPS = 1e-6

LANES = 128
SUBLANES = 8
VMEM_LIMIT = 56 * 1024 * 1024

NEG_BIG = -1e30


def _params(sem, vmem=VMEM_LIMIT):
    return pltpu.CompilerParams(dimension_semantics=sem, vmem_limit_bytes=vmem)


def _rms_mod(x, g, shift, scale):
    ms = jnp.mean(x * x, axis=-1, keepdims=True)
    y = x * lax.rsqrt(ms + EPS) * g
    return y * (1.0 + scale) + shift


def _silu(x):
    return x / (1.0 + jnp.exp(-x))


def _ada_kernel(c_ref, w_ref, b_ref, o_ref):
    cond = _silu(c_ref[...])
    o_ref[0] = jnp.dot(cond, w_ref[0], preferred_element_type=F32,
                       precision=HIGHEST) + b_ref[0]


def _ada_call(c_pad, w_ada, b_ada):
    depth, d, n = w_ada.shape
    tn = 1536
    return pl.pallas_call(
        _ada_kernel,
        out_shape=jax.ShapeDtypeStruct((depth, SUBLANES, n), F32),
        grid=(depth, n // tn),
        in_specs=[
            pl.BlockSpec((SUBLANES, d), lambda l, j: (0, 0)),
            pl.BlockSpec((1, d, tn), lambda l, j: (l, 0, j)),
            pl.BlockSpec((1, 1, tn), lambda l, j: (l, 0, j)),
        ],
        out_specs=pl.BlockSpec((1, SUBLANES, tn), lambda l, j: (l, 0, j)),
        compiler_params=_params(("parallel", "parallel")),
        name="ada_mod",
    )(c_pad, w_ada, b_ada.reshape(depth, 1, n))


def _norm_proj_kernel(*refs, splits, tiles_per_batch):
    x_ref, g_ref, sh_ref, sc_ref, w_ref = refs[:5]
    out_refs = refs[5:]
    b = pl.program_id(0) // tiles_per_batch
    h = _rms_mod(x_ref[...], g_ref[...], sh_ref[pl.ds(b, 1), :], sc_ref[pl.ds(b, 1), :])
    hb = h.astype(BF16)
    off = 0
    for o_ref, n in zip(out_refs, splits):
        o_ref[...] = jnp.dot(hb, w_ref[:, off:off + n],
                             preferred_element_type=F32).astype(o_ref.dtype)
        off += n


def _norm_proj_call(x, g, mod, sh_col, sc_col, w, splits, seq, tm=512):
    n, d = x.shape
    n_out = w.shape[1]
    return pl.pallas_call(
        functools.partial(_norm_proj_kernel, splits=tuple(splits), tiles_per_batch=seq // tm),
        out_shape=[jax.ShapeDtypeStruct((n, s), BF16) for s in splits],
        grid=(n // tm,),
        in_specs=[
            pl.BlockSpec((tm, d), lambda i: (i, 0)),
            pl.BlockSpec((1, d), lambda i: (0, 0)),
            pl.BlockSpec((SUBLANES, d), lambda i: (0, sh_col)),
            pl.BlockSpec((SUBLANES, d), lambda i: (0, sc_col)),
            pl.BlockSpec((d, n_out), lambda i: (0, 0)),
        ],
        out_specs=[pl.BlockSpec((tm, s), lambda i: (i, 0)) for s in splits],
        compiler_params=_params(("parallel",)),
        name="norm_proj",
    )(x, g.reshape(1, d), mod, mod, w)


_NT = (((1,), (1,)), ((), ()))


def _fox_proj_kernel(x_ref, g_ref, sh_ref, sc_ref, wk_ref, wqt_ref, wvt_ref, wf_ref,
                     k_ref, qt_ref, vt_ref, f_ref, *, tiles_per_batch):
    b = pl.program_id(0) // tiles_per_batch
    h = _rms_mod(x_ref[...], g_ref[...], sh_ref[pl.ds(b, 1), :], sc_ref[pl.ds(b, 1), :])
    hb = h.astype(BF16)
    k_ref[...] = jnp.dot(hb, wk_ref[...], preferred_element_type=F32).astype(BF16)
    qt_ref[...] = lax.dot_general(wqt_ref[...], hb, _NT, preferred_element_type=F32).astype(BF16)
    vt_ref[...] = lax.dot_general(wvt_ref[...], hb, _NT, preferred_element_type=F32).astype(BF16)
    f_ref[...] = jnp.dot(h, wf_ref[...], preferred_element_type=F32, precision=HIGHEST)


def _fox_proj_call(x, g, mod, w_k, w_qt, w_vt, w_f, seq, tm=512):
    n, d = x.shape
    nh = w_f.shape[1]
    full = lambda shape: pl.BlockSpec(shape, lambda i: (0, 0))
    return pl.pallas_call(
        functools.partial(_fox_proj_kernel, tiles_per_batch=seq // tm),
        out_shape=[jax.ShapeDtypeStruct((n, d), BF16), jax.ShapeDtypeStruct((d, n), BF16),
                   jax.ShapeDtypeStruct((d, n), BF16), jax.ShapeDtypeStruct((n, nh), F32)],
        grid=(n // tm,),
        in_specs=[
            pl.BlockSpec((tm, d), lambda i: (i, 0)),
            full((1, d)),
            pl.BlockSpec((SUBLANES, d), lambda i: (0, 0)),
            pl.BlockSpec((SUBLANES, d), lambda i: (0, 1)),
            full((d, d)), full((d, d)), full((d, d)), full((d, nh)),
        ],
        out_specs=[pl.BlockSpec((tm, d), lambda i: (i, 0)), pl.BlockSpec((d, tm), lambda i: (0, i)),
                   pl.BlockSpec((d, tm), lambda i: (0, i)), pl.BlockSpec((tm, nh), lambda i: (i, 0))],
        compiler_params=_params(("parallel",)),
        name="fox_proj",
    )(x, g.reshape(1, d), mod, mod, w_k, w_qt, w_vt, w_f)


_FOX_T = 512
LOG2E = 1.4426950408889634


def _cum_kernel(f_ref, bf_ref, g_ref, r_ref):
    x = f_ref[...] + bf_ref[...]
    logf = (jnp.minimum(x, 0.0) - jnp.log(1.0 + jnp.exp(-jnp.abs(x)))) * LOG2E
    seq, nh = logf.shape
    r = lax.broadcasted_iota(jnp.int32, (_FOX_T, _FOX_T), 0)
    c = lax.broadcasted_iota(jnp.int32, (_FOX_T, _FOX_T), 1)
    lower = (c <= r).astype(F32)
    carry = jnp.zeros((1, nh), F32)
    for ch in range(seq // _FOX_T):
        rows = slice(ch * _FOX_T, (ch + 1) * _FOX_T)
        cs = jnp.dot(lower, logf[rows, :], preferred_element_type=F32, precision=HIGHEST)
        g_ref[0, rows, :] = cs
        r_ref[0, ch:ch + 1, :] = carry
        carry = carry + cs[_FOX_T - 1:_FOX_T, :]


def _cum_call(f, b_f, batch, seq):
    nh = f.shape[1]
    return pl.pallas_call(
        _cum_kernel,
        out_shape=[jax.ShapeDtypeStruct((batch, seq, nh), F32),
                   jax.ShapeDtypeStruct((batch, seq // _FOX_T, nh), F32)],
        grid=(batch,),
        in_specs=[
            pl.BlockSpec((seq, nh), lambda b: (b, 0)),
            pl.BlockSpec((1, nh), lambda b: (0, 0)),
        ],
        out_specs=[pl.BlockSpec((1, seq, nh), lambda b: (b, 0, 0)),
                   pl.BlockSpec((1, seq // _FOX_T, nh), lambda b: (b, 0, 0))],
        compiler_params=_params(("parallel",)),
        name="forget_cumsum",
    )(f, b_f.reshape(1, nh))


_ONES_ROWS = 16


def _fox_kernel(r_ref, qt_ref, k_ref, vt_ref, g_ref, o_ref, gb_sc, m_sc, acc_sc, sa_sc, sb_sc):
    t = _FOX_T
    b, hp, qi = pl.program_id(0), pl.program_id(1), pl.program_id(2)
    seq = k_ref.shape[1]
    n_blocks = seq // t
    n_heads = 2 * pl.num_programs(1)

    @pl.when(qi == 0)
    def _():
        for hh in range(2):
            gb_sc[hh] = jnp.broadcast_to(g_ref[0, 0, :, hh:hh + 1], (seq, LANES))

    qt2 = qt_ref[...]
    feat = lax.broadcasted_iota(jnp.int32, (LANES, 1), 0)
    key = lax.broadcasted_iota(jnp.int32, (t, t), 0)
    qry = lax.broadcasted_iota(jnp.int32, (t, t), 1)
    ones = jnp.ones((_ONES_ROWS, t), BF16)
    qh = [jnp.where((feat >= hh * HEAD_DIM) & (feat < (hh + 1) * HEAD_DIM), qt2, 0)
          for hh in range(2)]
    r_base = [(b * n_heads + 2 * hp + hh) * n_blocks for hh in range(2)]
    r_q = [r_ref[r_base[hh] + qi] for hh in range(2)]
    m_sc[...] = jnp.full_like(m_sc, -jnp.inf)
    acc_sc[...] = jnp.zeros_like(acc_sc)

    def scores(kb, dst):
        k0 = pl.multiple_of(kb * t, t)
        k2 = k_ref[0, pl.ds(k0, t), :]
        for hh in range(2):
            dst[hh] = jnp.dot(k2, qh[hh], preferred_element_type=F32)

    def step(kb, cur, nxt, causal):
        if nxt is not None:
            scores(kb + 1, nxt)
        k0 = pl.multiple_of(kb * t, t)
        for hh in range(2):
            gk = gb_sc[hh, pl.ds(k0, t), :]
            st = cur[hh] - jnp.concatenate([gk] * (t // LANES), axis=1)
            if causal:
                st = jnp.where(key <= qry, st, -jnp.inf)
            c = r_ref[r_base[hh] + kb] - r_q[hh]
            m_old = m_sc[hh]
            m_new = jnp.maximum(m_old, jnp.max(st, axis=0, keepdims=True) - c)
            alpha = jnp.exp2(m_old - m_new)
            p = jnp.exp2(st - (m_new + c)).astype(BF16)
            vt1 = jnp.concatenate(
                [vt_ref[hh * HEAD_DIM:(hh + 1) * HEAD_DIM, pl.ds(k0, t)], ones], axis=0)
            acc_sc[hh] = alpha * acc_sc[hh] + jnp.dot(vt1, p, preferred_element_type=F32)
            m_sc[hh] = m_new

    scores(0, sa_sc)

    def pair(i, carry):
        step(2 * i, sa_sc, sb_sc, False)
        step(2 * i + 1, sb_sc, sa_sc, False)
        return carry

    lax.fori_loop(0, qi // 2, pair, 0)

    @pl.when(qi % 2 == 0)
    def _():
        step(qi, sa_sc, None, True)

    @pl.when(qi % 2 == 1)
    def _():
        step(qi - 1, sa_sc, sb_sc, False)
        step(qi, sb_sc, None, True)

    outs = [acc_sc[hh, :HEAD_DIM, :] / acc_sc[hh, HEAD_DIM:HEAD_DIM + 1, :] for hh in range(2)]
    o_ref[0] = jnp.concatenate(outs, axis=0).T.astype(o_ref.dtype)


def _fox_call(r_flat, q_t, k, v_t, g_pairs):
    batch, seq, d = k.shape
    n_pairs = d // LANES
    t = _FOX_T
    nq = seq // t
    return pl.pallas_call(
        _fox_kernel,
        out_shape=jax.ShapeDtypeStruct((batch, seq, d), BF16),
        grid=(batch, n_pairs, nq),
        in_specs=[
            pl.BlockSpec(memory_space=pltpu.SMEM),
            pl.BlockSpec((LANES, t), lambda b, h, i: (h, b * nq + i)),
            pl.BlockSpec((1, seq, LANES), lambda b, h, i: (b, 0, h)),
            pl.BlockSpec((LANES, seq), lambda b, h, i: (h, b)),
            pl.BlockSpec((1, 1, seq, 2), lambda b, h, i: (b, h, 0, 0)),
        ],
        out_specs=pl.BlockSpec((1, t, LANES), lambda b, h, i: (b, i, h)),
        scratch_shapes=[pltpu.VMEM((2, seq, LANES), F32), pltpu.VMEM((2, 1, t), F32),
                        pltpu.VMEM((2, HEAD_DIM + _ONES_ROWS, t), F32),
                        pltpu.VMEM((2, t, t), F32), pltpu.VMEM((2, t, t), F32)],
        compiler_params=_params(("parallel", "parallel", "arbitrary")),
        name="fox_attention",
    )(r_flat, q_t, k, v_t, g_pairs)


def _out_kernel(*refs, tiles_per_batch, with_router):
    o_ref, w_ref, x_ref, gt_ref = refs[:4]
    b = pl.program_id(0) // tiles_per_batch
    y = jnp.dot(o_ref[...], w_ref[...], preferred_element_type=F32)
    xn = x_ref[...] + gt_ref[pl.ds(b, 1), :] * y
    if not with_router:
        refs[4][...] = xn
        return
    g_ref, sh_ref, sc_ref, wr_ref, br_ref, xo_ref, h_ref, route_ref, cnt_ref = refs[4:]
    xo_ref[...] = xn
    h = _rms_mod(xn, g_ref[...], sh_ref[pl.ds(b, 1), :], sc_ref[pl.ds(b, 1), :])
    h_ref[...] = h
    logits = jnp.dot(h, wr_ref[...], preferred_element_type=F32,
                     precision=HIGHEST) + br_ref[...]
    tm = logits.shape[0]
    lane = lax.broadcasted_iota(jnp.int32, logits.shape, 1).astype(F32)
    logits = jnp.where(lane < N_EXPERTS, logits, -jnp.inf)
    m1 = jnp.max(logits, axis=1, keepdims=True)
    i1 = jnp.min(jnp.where(logits == m1, lane, float(LANES)), axis=1, keepdims=True)
    rest = jnp.where(lane == i1, -jnp.inf, logits)
    m2 = jnp.max(rest, axis=1, keepdims=True)
    i2 = jnp.min(jnp.where(rest == m2, lane, float(LANES)), axis=1, keepdims=True)
    e2 = jnp.exp(m2 - m1)
    den = 1.0 + e2

    @pl.when(pl.program_id(0) == 0)
    def _():
        cnt_ref[...] = jnp.zeros_like(cnt_ref)

    sel1 = lane == i1
    sel2 = lane == i2
    onehot = jnp.where(sel1 | sel2, 1.0, 0.0)
    r = lax.broadcasted_iota(jnp.int32, (tm, tm), 0)
    c = lax.broadcasted_iota(jnp.int32, (tm, tm), 1)
    lower = jnp.where(c < r, 1.0, 0.0).astype(BF16)
    before = jnp.dot(lower, onehot.astype(BF16), preferred_element_type=F32) + cnt_ref[0:1, :]
    rank1 = jnp.sum(jnp.where(sel1, before, 0.0), axis=1, keepdims=True)
    rank2 = jnp.sum(jnp.where(sel2, before, 0.0), axis=1, keepdims=True)
    cnt_ref[0:1, :] = cnt_ref[0:1, :] + jnp.sum(onehot, axis=0, keepdims=True)
    route = jnp.where(lane == 0.0, i1, 0.0)
    for k, val in enumerate((i2, 1.0 / den, e2 / den, rank1, rank2), start=1):
        route = jnp.where(lane == float(k), val, route)
    route_ref[...] = route


def _out_call(o, w_out, x, mod, gate_col, seq, router=None, tm=512):
    n, d = x.shape
    with_router = router is not None
    in_specs = [
        pl.BlockSpec((tm, d), lambda i: (i, 0)),
        pl.BlockSpec((d, d), lambda i: (0, 0)),
        pl.BlockSpec((tm, d), lambda i: (i, 0)),
        pl.BlockSpec((SUBLANES, d), lambda i: (0, gate_col)),
    ]
    args = [o, w_out, x, mod]
    out_shape = [jax.ShapeDtypeStruct((n, d), F32)]
    out_specs = [pl.BlockSpec((tm, d), lambda i: (i, 0))]
    if with_router:
        g, sh_col, sc_col, w_r, b_r = router
        in_specs += [
            pl.BlockSpec((1, d), lambda i: (0, 0)),
            pl.BlockSpec((SUBLANES, d), lambda i: (0, sh_col)),
            pl.BlockSpec((SUBLANES, d), lambda i: (0, sc_col)),
            pl.BlockSpec((d, LANES), lambda i: (0, 0)),
            pl.BlockSpec((1, LANES), lambda i: (0, 0)),
        ]
        args += [g.reshape(1, d), mod, mod, w_r, b_r]
        out_shape += [jax.ShapeDtypeStruct((n, d), F32),
                      jax.ShapeDtypeStruct((n, LANES), F32),
                      jax.ShapeDtypeStruct((SUBLANES, LANES), F32)]
        out_specs += [pl.BlockSpec((tm, d), lambda i: (i, 0)),
                      pl.BlockSpec((tm, LANES), lambda i: (i, 0)),
                      pl.BlockSpec((SUBLANES, LANES), lambda i: (0, 0))]
    res = pl.pallas_call(
        functools.partial(_out_kernel, tiles_per_batch=seq // tm, with_router=with_router),
        out_shape=out_shape,
        grid=(n // tm,),
        in_specs=in_specs,
        out_specs=out_specs,
        compiler_params=_params(("arbitrary",) if with_router else ("parallel",)),
        name="out_proj_router" if with_router else "out_proj",
    )(*args)
    return res if with_router else res[0]


def _ffn_kernel(x_ref, g_ref, sh_ref, sc_ref, gt_ref, wg_ref, wu_ref, wd_ref,
                o_ref, h_sc, acc_sc, *, tiles_per_batch):
    b = pl.program_id(0) // tiles_per_batch
    j = pl.program_id(1)

    @pl.when(j == 0)
    def _():
        h = _rms_mod(x_ref[...], g_ref[...], sh_ref[pl.ds(b, 1), :], sc_ref[pl.ds(b, 1), :])
        h_sc[...] = h.astype(BF16)
        acc_sc[...] = jnp.zeros_like(acc_sc)

    h = h_sc[...]
    g = jnp.dot(h, wg_ref[...], preferred_element_type=F32)
    u = jnp.dot(h, wu_ref[...], preferred_element_type=F32)
    a = (_silu(g) * u).astype(BF16)
    acc_sc[...] += jnp.dot(a, wd_ref[...], preferred_element_type=F32)

    @pl.when(j == pl.num_programs(1) - 1)
    def _():
        o_ref[...] = x_ref[...] + gt_ref[pl.ds(b, 1), :] * acc_sc[...]


def _ffn_call(x, g, mod, w_gu, w_down, seq, tm=1024, tf=512):
    n, d = x.shape
    f = w_down.shape[0]
    nj = f // tf
    return pl.pallas_call(
        functools.partial(_ffn_kernel, tiles_per_batch=seq // tm),
        out_shape=jax.ShapeDtypeStruct((n, d), F32),
        grid=(n // tm, nj),
        in_specs=[
            pl.BlockSpec((tm, d), lambda i, j: (i, 0)),
            pl.BlockSpec((1, d), lambda i, j: (0, 0)),
            pl.BlockSpec((SUBLANES, d), lambda i, j: (0, 3)),
            pl.BlockSpec((SUBLANES, d), lambda i, j: (0, 4)),
            pl.BlockSpec((SUBLANES, d), lambda i, j: (0, 5)),
            pl.BlockSpec((d, tf), lambda i, j: (0, j)),
            pl.BlockSpec((d, tf), lambda i, j: (0, j + nj)),
            pl.BlockSpec((tf, d), lambda i, j: (j, 0)),
        ],
        out_specs=pl.BlockSpec((tm, d), lambda i, j: (i, 0)),
        scratch_shapes=[pltpu.VMEM((tm, d), BF16), pltpu.VMEM((tm, d), F32)],
        compiler_params=_params(("parallel", "arbitrary")),
        name="ffn_swiglu",
    )(x, g.reshape(1, d), mod, mod, mod, w_gu, w_gu, w_down)


_MOE_TM = 512


def _dispatch_kernel(d1_ref, d2_ref, h_ref, xs_in_ref, xs_ref, sem):
    del xs_in_ref
    tm = h_ref.shape[0]
    base = pl.program_id(0) * tm

    def issue(r, carry):
        src = h_ref.at[pl.ds(r, 1)]
        pltpu.make_async_copy(src, xs_ref.at[pl.ds(d1_ref[base + r], 1)], sem.at[0]).start()
        pltpu.make_async_copy(src, xs_ref.at[pl.ds(d2_ref[base + r], 1)], sem.at[1]).start()
        return carry

    lax.fori_loop(0, tm, issue, 0, unroll=8)
    pltpu.make_async_copy(h_ref, xs_ref.at[pl.ds(0, tm)], sem.at[0]).wait()
    pltpu.make_async_copy(h_ref, xs_ref.at[pl.ds(0, tm)], sem.at[1]).wait()


def _dispatch_call(dest1, dest2, h, n_rows, tm=512):
    n, d = h.shape
    zeros = jnp.zeros((n_rows, d), h.dtype)
    return pl.pallas_call(
        _dispatch_kernel,
        out_shape=jax.ShapeDtypeStruct((n_rows, d), h.dtype),
        grid_spec=pltpu.PrefetchScalarGridSpec(
            num_scalar_prefetch=2,
            grid=(n // tm,),
            in_specs=[
                pl.BlockSpec((tm, d), lambda i, d1, d2: (i, 0)),
                pl.BlockSpec(memory_space=pl.ANY),
            ],
            out_specs=pl.BlockSpec(memory_space=pl.ANY),
            scratch_shapes=[pltpu.SemaphoreType.DMA((2,))],
        ),
        input_output_aliases={3: 0},
        compiler_params=_params(("arbitrary",)),
        name="moe_dispatch",
    )(dest1, dest2, h, zeros)


def _experts_kernel(te_ref, nt_ref, xs_ref, wg_ref, wu_ref, wd_ref, ye_ref, h_sc, acc_sc):
    del te_ref
    t = pl.program_id(0)
    j = pl.program_id(1)

    @pl.when(t < nt_ref[0])
    def _():
        @pl.when(j == 0)
        def _():
            h_sc[...] = xs_ref[...].astype(BF16)
            acc_sc[...] = jnp.zeros_like(acc_sc)

        h = h_sc[...]
        g = jnp.dot(h, wg_ref[0], preferred_element_type=F32)
        u = jnp.dot(h, wu_ref[0], preferred_element_type=F32)
        a = (_silu(g) * u).astype(BF16)
        acc_sc[...] += jnp.dot(a, wd_ref[0], preferred_element_type=F32)

        @pl.when(j == pl.num_programs(1) - 1)
        def _():
            ye_ref[...] = acc_sc[...]

    @pl.when((t >= nt_ref[0]) & (j == 0))
    def _():
        ye_ref[...] = jnp.zeros_like(ye_ref)


def _experts_call(tile_expert, n_tiles, xs, w_gu, w_down, tf=896):
    n_rows, d = xs.shape
    _, f, _ = w_down.shape
    nj = f // tf
    tm = _MOE_TM
    tile = lambda t, nt: jnp.minimum(t, nt[0] - 1)
    jj = lambda t, j, nt: jnp.where(t < nt[0], j, nj - 1)
    return pl.pallas_call(
        _experts_kernel,
        out_shape=jax.ShapeDtypeStruct((n_rows, d), F32),
        grid_spec=pltpu.PrefetchScalarGridSpec(
            num_scalar_prefetch=2,
            grid=(n_rows // tm, nj),
            in_specs=[
                pl.BlockSpec((tm, d), lambda t, j, te, nt: (tile(t, nt), 0)),
                pl.BlockSpec((1, d, tf), lambda t, j, te, nt: (te[tile(t, nt)], 0, jj(t, j, nt))),
                pl.BlockSpec((1, d, tf), lambda t, j, te, nt: (te[tile(t, nt)], 0, jj(t, j, nt) + nj)),
                pl.BlockSpec((1, tf, d), lambda t, j, te, nt: (te[tile(t, nt)], jj(t, j, nt), 0)),
            ],
            out_specs=pl.BlockSpec((tm, d), lambda t, j, te, nt: (t, 0)),
            scratch_shapes=[pltpu.VMEM((tm, d), BF16), pltpu.VMEM((tm, d), F32)],
        ),
        compiler_params=_params(("arbitrary", "arbitrary")),
        name="moe_experts",
    )(tile_expert, n_tiles, xs, w_gu, w_gu, w_down)


def _combine_kernel(d1_ref, d2_ref, ye_ref, x_ref, route_ref, gt_ref, gf_ref, o_ref,
                    y1_sc, y2_sc, sem, *, tiles_per_batch):
    tm = x_ref.shape[0]
    i = pl.program_id(0)
    b = i // tiles_per_batch
    base = i * tm

    def issue(r, carry):
        pltpu.make_async_copy(ye_ref.at[pl.ds(d1_ref[base + r], 1)], y1_sc.at[pl.ds(r, 1)],
                              sem.at[0]).start()
        pltpu.make_async_copy(ye_ref.at[pl.ds(d2_ref[base + r], 1)], y2_sc.at[pl.ds(r, 1)],
                              sem.at[1]).start()
        return carry

    lax.fori_loop(0, tm, issue, 0, unroll=8)
    pltpu.make_async_copy(ye_ref.at[pl.ds(0, tm)], y1_sc, sem.at[0]).wait()
    pltpu.make_async_copy(ye_ref.at[pl.ds(0, tm)], y2_sc, sem.at[1]).wait()
    route = route_ref[...]
    y = route[:, 2:3] * y1_sc[...] + route[:, 3:4] * y2_sc[...]
    xn = x_ref[...] + gt_ref[pl.ds(b, 1), :] * y
    ms = jnp.mean(xn * xn, axis=-1, keepdims=True)
    o_ref[...] = xn * lax.rsqrt(ms + EPS) * gf_ref[...]


def _combine_call(dest1, dest2, ye, x, route, mod, g_final, seq, tm=512):
    n, d = x.shape
    return pl.pallas_call(
        functools.partial(_combine_kernel, tiles_per_batch=seq // tm),
        out_shape=jax.ShapeDtypeStruct((n, d), F32),
        grid_spec=pltpu.PrefetchScalarGridSpec(
            num_scalar_prefetch=2,
            grid=(n // tm,),
            in_specs=[
                pl.BlockSpec(memory_space=pl.ANY),
                pl.BlockSpec((tm, d), lambda i, d1, d2: (i, 0)),
                pl.BlockSpec((tm, LANES), lambda i, d1, d2: (i, 0)),
                pl.BlockSpec((SUBLANES, d), lambda i, d1, d2: (0, 5)),
                pl.BlockSpec((1, d), lambda i, d1, d2: (0, 0)),
            ],
            out_specs=pl.BlockSpec((tm, d), lambda i, d1, d2: (i, 0)),
            scratch_shapes=[pltpu.VMEM((tm, d), F32), pltpu.VMEM((tm, d), F32),
                            pltpu.SemaphoreType.DMA((2,))],
        ),
        compiler_params=_params(("arbitrary",)),
        name="moe_combine",
    )(dest1, dest2, ye, x, route, mod, g_final.reshape(1, d))


def _moe_call(h, route, counts, x, mod, g_final, w_gu, w_down, seq):
    n, d = x.shape
    ne = w_down.shape[0]
    tm = _MOE_TM
    max_tiles = (2 * n) // tm + ne
    e1 = route[:, 0].astype(jnp.int32)
    e2 = route[:, 1].astype(jnp.int32)
    cnt = counts[0, :ne].astype(jnp.int32)
    tiles_e = (cnt + tm - 1) // tm
    tile_end = jnp.cumsum(tiles_e)
    row_start = (tile_end - tiles_e) * tm
    dest1 = row_start[e1] + route[:, 4].astype(jnp.int32)
    dest2 = row_start[e2] + route[:, 5].astype(jnp.int32)
    n_tiles = tile_end[-1:]
    tile_expert = jnp.minimum(
        jnp.searchsorted(tile_end, jnp.arange(max_tiles, dtype=jnp.int32), side="right"),
        ne - 1).astype(jnp.int32)
    xs = _dispatch_call(dest1, dest2, h, max_tiles * tm)
    ye = _experts_call(tile_expert, n_tiles, xs, w_gu, w_down)
    return _combine_call(dest1, dest2, ye, x, route, mod, g_final, seq)


_SWA_TQ = 2 * CHUNK
_SWA_BAND = 2 * _SWA_TQ


def _swa_bucket_tile():
    r = np.arange(_SWA_TQ)[:, None]
    cc = np.arange(_SWA_BAND)[None, :]
    rel = cc - _SWA_TQ - r
    nb = REL_BUCKETS // 2
    max_exact = nb // 2
    ret = (rel > 0).astype(np.int32) * nb
    n = np.abs(rel)
    large = max_exact + (np.log(np.maximum(n, 1) / max_exact)
                         / np.log(REL_MAX_DIST / max_exact) * (nb - max_exact)).astype(np.int32)
    large = np.minimum(large, nb - 1)
    bucket = (ret + np.where(n < max_exact, n, large)).astype(np.int32)
    q_chunk = r // CHUNK
    k_chunk = cc // CHUNK
    visible = (k_chunk >= q_chunk) & (k_chunk <= q_chunk + WINDOW_CHUNKS)
    return np.where(visible, bucket, -1).astype(np.int32)


def _swa_bias_kernel(tbl_ref, bkt_ref, o_ref):
    bkt = bkt_ref[...]
    for hk in range(SWA_KV_HEADS):
        for par in range(2):
            for gg in range(2):
                head = hk * SWA_GROUP + par + 2 * gg
                tile = jnp.full(bkt.shape, NEG_BIG, F32)
                for bk in range(REL_BUCKETS):
                    tile = jnp.where(bkt == bk, tbl_ref[head, bk], tile)
                o_ref[hk, par, gg * _SWA_TQ:(gg + 1) * _SWA_TQ, :] = tile


def _swa_bias_call(rel_bias):
    bkt = jnp.asarray(_swa_bucket_tile())
    return pl.pallas_call(
        _swa_bias_kernel,
        out_shape=jax.ShapeDtypeStruct((SWA_KV_HEADS, 2, 2 * _SWA_TQ, _SWA_BAND), F32),
        in_specs=[
            pl.BlockSpec(memory_space=pltpu.SMEM),
            pl.BlockSpec(memory_space=pltpu.VMEM),
        ],
        out_specs=pl.BlockSpec(memory_space=pltpu.VMEM),
        name="swa_bias",
    )(rel_bias.T, bkt)


def _swa_kernel(q_ref, kp_ref, kc_ref, vp_ref, vc_ref, bias_ref, sink_ref, o_ref):
    tq = _SWA_TQ
    qi = pl.program_id(1)
    lane = lax.broadcasted_iota(jnp.int32, (1, LANES), 1)
    col = lax.broadcasted_iota(jnp.int32, (1, _SWA_BAND), 1)
    in_seq = (col + (qi - 1) * tq) >= 0
    for hk in range(SWA_KV_HEADS):
        ksl = slice(hk * LANES, (hk + 1) * LANES)
        kb = jnp.concatenate([kp_ref[0, :, ksl], kc_ref[0, :, ksl]], axis=0)
        vb = jnp.concatenate([vp_ref[0, :, ksl], vc_ref[0, :, ksl]], axis=0)
        qs = jnp.concatenate([q_ref[0, :, (2 * hk) * LANES:(2 * hk + 1) * LANES],
                              q_ref[0, :, (2 * hk + 1) * LANES:(2 * hk + 2) * LANES]],
                             axis=0)
        acc = jnp.zeros((2 * tq, LANES), F32)
        for par in range(2):
            head_lanes = (lane < HEAD_DIM) if par == 0 else (lane >= HEAD_DIM)
            kh = jnp.where(head_lanes, kb, 0)
            vh = jnp.where(head_lanes, vb, 0)
            s = lax.dot_general(qs, kh, (((1,), (1,)), ((), ())),
                                preferred_element_type=F32)
            s = s + bias_ref[hk, par]
            s = jnp.where(in_seq, s, -jnp.inf)
            sink = sink_ref[hk, par]
            m = jnp.maximum(jnp.max(s, axis=1, keepdims=True), sink)
            p = jnp.exp(s - m)
            den = jnp.sum(p, axis=1, keepdims=True) + jnp.exp(sink - m)
            acc = acc + jnp.dot((p / den).astype(BF16), vh, preferred_element_type=F32)
        o_ref[0, :, (2 * hk) * LANES:(2 * hk + 1) * LANES] = acc[:tq].astype(o_ref.dtype)
        o_ref[0, :, (2 * hk + 1) * LANES:(2 * hk + 2) * LANES] = acc[tq:].astype(o_ref.dtype)


def _swa_call(q, k, v, bias, sink):
    batch, seq, d = q.shape
    kw = k.shape[2]
    tq = _SWA_TQ
    prev = lambda b, i: (b, jnp.maximum(i - 1, 0), 0)
    cur = lambda b, i: (b, i, 0)
    return pl.pallas_call(
        _swa_kernel,
        out_shape=jax.ShapeDtypeStruct((batch, seq, d), BF16),
        grid=(batch, seq // tq),
        in_specs=[
            pl.BlockSpec((1, tq, d), cur),
            pl.BlockSpec((1, tq, kw), prev),
            pl.BlockSpec((1, tq, kw), cur),
            pl.BlockSpec((1, tq, kw), prev),
            pl.BlockSpec((1, tq, kw), cur),
            pl.BlockSpec(bias.shape, lambda b, i: (0, 0, 0, 0)),
            pl.BlockSpec(sink.shape, lambda b, i: (0, 0, 0, 0)),
        ],
        out_specs=pl.BlockSpec((1, tq, d), cur),
        compiler_params=_params(("parallel", "arbitrary")),
        name="swa_attention",
    )(q, k, k, v, v, bias, sink)


def kernel(x, c, w_ada, b_ada, g_norm_mix, g_norm_ffn, g_final, fox_w_in, fox_b_f, fox_w_out, swa_w_in, swa_sinks, swa_w_out, rel_bias, ffn_w_gu, ffn_w_down, moe_w_router, moe_b_router, moe_w_gu, moe_w_down):
    batch, seq, d = x.shape
    n = batch * seq
    q_scale = HEAD_DIM ** -0.5
    xf = x.reshape(n, d)

    c_pad = jnp.zeros((SUBLANES, d), F32).at[:batch].set(c)
    mod = _ada_call(c_pad, w_ada, b_ada)
    mod0, mod1 = mod[0], mod[1]

    w_in = fox_w_in[0]
    n_heads = d // HEAD_DIM
    k, q_t, v_t, f = _fox_proj_call(
        xf, g_norm_mix[0], mod0, w_in[:, d:2 * d].astype(BF16),
        (w_in[:, :d] * (q_scale * LOG2E)).T.astype(BF16), w_in[:, 2 * d:3 * d].T.astype(BF16),
        w_in[:, 3 * d:], seq)
    g_cum, r_cum = _cum_call(f, fox_b_f[0], batch, seq)
    g_pairs = g_cum.reshape(batch, seq, n_heads // 2, 2).transpose(0, 2, 1, 3)
    r_flat = r_cum.transpose(0, 2, 1).reshape(-1)
    o = _fox_call(r_flat, q_t, k.reshape(batch, seq, d), v_t, g_pairs)
    x1 = _out_call(o.reshape(n, d), fox_w_out[0].astype(BF16), xf, mod0, 2, seq)
    x2 = _ffn_call(x1, g_norm_ffn[0], mod0, ffn_w_gu[0].astype(BF16), ffn_w_down[0].astype(BF16), seq)

    w_in = swa_w_in[0]
    kvw = SWA_KV_HEADS * HEAD_DIM
    dup = lambda w: jnp.repeat(w.reshape(d, SWA_KV_HEADS, 1, HEAD_DIM), 2, axis=2).reshape(d, 2 * kvw)
    w_qkv = jnp.concatenate([w_in[:, :d] * q_scale, dup(w_in[:, d:d + kvw]), dup(w_in[:, d + kvw:])],
                            axis=1).astype(BF16)
    q, k, v = _norm_proj_call(x2, g_norm_mix[1], mod1, 0, 1, w_qkv, (d, 2 * kvw, 2 * kvw), seq)
    bias = _swa_bias_call(rel_bias)
    sink = swa_sinks[0].reshape(SWA_KV_HEADS, 2, 2).transpose(0, 2, 1)
    sink = jnp.repeat(sink, _SWA_TQ, axis=2).reshape(SWA_KV_HEADS, 2, 2 * _SWA_TQ, 1)
    o = _swa_call(q.reshape(batch, seq, d), k.reshape(batch, seq, 2 * kvw),
                  v.reshape(batch, seq, 2 * kvw), bias, sink)
    w_r = jnp.zeros((d, LANES), F32).at[:, :N_EXPERTS].set(moe_w_router[0])
    b_r = jnp.zeros((1, LANES), F32).at[0, :N_EXPERTS].set(moe_b_router[0])
    x3, h4, route, counts = _out_call(o.reshape(n, d), swa_w_out[0].astype(BF16), x2, mod1, 2, seq,
                                      router=(g_norm_ffn[1], 3, 4, w_r, b_r))
    out = _moe_call(h4, route, counts, x3, mod1, g_final, moe_w_gu[0].astype(BF16),
                    moe_w_down[0].astype(BF16), seq)
    return out.reshape(batch, seq, d)
```

```python
import functools

import numpy as np
import jax
import jax.numpy as jnp
from jax import lax
from jax.experimental import pallas as pl
from jax.experimental.pallas import tpu as pltpu

F32 = jnp.float32
BF16 = jnp.bfloat16
HIGHEST = lax.Precision.HIGHEST

HEAD_DIM = 64
CHUNK = 64
WINDOW_CHUNKS = 2
REL_BUCKETS = 32
REL_MAX_DIST = 128
SWA_KV_HEADS = 4
SWA_GROUP = 4
N_EXPERTS = 8
EPS = 1e-6

LANES = 128
SUBLANES = 8
VMEM_LIMIT = 56 * 1024 * 1024

NEG_BIG = -1e30


def _params(sem, vmem=VMEM_LIMIT):
    return pltpu.CompilerParams(dimension_semantics=sem, vmem_limit_bytes=vmem)


def _rms_mod(x, g, shift, scale):
    ms = jnp.mean(x * x, axis=-1, keepdims=True)
    y = x * lax.rsqrt(ms + EPS) * g
    return y * (1.0 + scale) + shift


def _silu(x):
    return x / (1.0 + jnp.exp(-x))


def _ada_kernel(c_ref, w_ref, b_ref, o_ref):
    cond = _silu(c_ref[...])
    o_ref[0] = jnp.dot(cond, w_ref[0], preferred_element_type=F32,
                       precision=HIGHEST) + b_ref[0]


def _ada_call(c_pad, w_ada, b_ada):
    depth, d, n = w_ada.shape
    tn = 1536
    return pl.pallas_call(
        _ada_kernel,
        out_shape=jax.ShapeDtypeStruct((depth, SUBLANES, n), F32),
        grid=(depth, n // tn),
        in_specs=[
            pl.BlockSpec((SUBLANES, d), lambda l, j: (0, 0)),
            pl.BlockSpec((1, d, tn), lambda l, j: (l, 0, j)),
            pl.BlockSpec((1, 1, tn), lambda l, j: (l, 0, j)),
        ],
        out_specs=pl.BlockSpec((1, SUBLANES, tn), lambda l, j: (l, 0, j)),
        compiler_params=_params(("parallel", "parallel")),
        name="ada_mod",
    )(c_pad, w_ada, b_ada.reshape(depth, 1, n))


_NT = (((1,), (1,)), ((), ()))


def _attn_proj_kernel(*refs, tiles_per_batch, with_gate):
    x_ref, g_ref, sh_ref, sc_ref, wk_ref, wqt_ref, wvt_ref = refs[:7]
    k_ref, qt_ref, vt_ref = refs[7 + with_gate:10 + with_gate]
    b = pl.program_id(0) // tiles_per_batch
    h = _rms_mod(x_ref[...], g_ref[...], sh_ref[pl.ds(b, 1), :], sc_ref[pl.ds(b, 1), :])
    hb = h.astype(BF16)
    k_ref[...] = jnp.dot(hb, wk_ref[...], preferred_element_type=F32).astype(BF16)
    qt_ref[...] = lax.dot_general(wqt_ref[...], hb, _NT, preferred_element_type=F32).astype(BF16)
    vt_ref[...] = lax.dot_general(wvt_ref[...], hb, _NT, preferred_element_type=F32).astype(BF16)
    if with_gate:
        refs[-1][...] = jnp.dot(h, refs[7][...], preferred_element_type=F32, precision=HIGHEST)


def _attn_proj_call(x, g, mod, w_k, w_qt, w_vt, seq, w_f=None, tm=512):
    n, d = x.shape
    with_gate = w_f is not None
    full = lambda a: pl.BlockSpec(a.shape, lambda i: (0, 0))
    in_specs = [
        pl.BlockSpec((tm, d), lambda i: (i, 0)),
        pl.BlockSpec((1, d), lambda i: (0, 0)),
        pl.BlockSpec((SUBLANES, d), lambda i: (0, 0)),
        pl.BlockSpec((SUBLANES, d), lambda i: (0, 1)),
        full(w_k), full(w_qt), full(w_vt),
    ]
    args = [x, g.reshape(1, d), mod, mod, w_k, w_qt, w_vt]
    out_shape = [jax.ShapeDtypeStruct((n, w_k.shape[1]), BF16),
                 jax.ShapeDtypeStruct((w_qt.shape[0], n), BF16),
                 jax.ShapeDtypeStruct((w_vt.shape[0], n), BF16)]
    out_specs = [pl.BlockSpec((tm, w_k.shape[1]), lambda i: (i, 0)),
                 pl.BlockSpec((w_qt.shape[0], tm), lambda i: (0, i)),
                 pl.BlockSpec((w_vt.shape[0], tm), lambda i: (0, i))]
    if with_gate:
        in_specs.append(full(w_f))
        args.append(w_f)
        out_shape.append(jax.ShapeDtypeStruct((n, w_f.shape[1]), F32))
        out_specs.append(pl.BlockSpec((tm, w_f.shape[1]), lambda i: (i, 0)))
    return pl.pallas_call(
        functools.partial(_attn_proj_kernel, tiles_per_batch=seq // tm, with_gate=with_gate),
        out_shape=out_shape,
        grid=(n // tm,),
        in_specs=in_specs,
        out_specs=out_specs,
        compiler_params=_params(("parallel",)),
        name="attn_proj_gate" if with_gate else "attn_proj",
    )(*args)


_FOX_T = 512
LOG2E = 1.4426950408889634


def _cum_kernel(f_ref, bf_ref, g_ref, r_ref):
    x = f_ref[...] + bf_ref[...]
    logf = (jnp.minimum(x, 0.0) - jnp.log(1.0 + jnp.exp(-jnp.abs(x)))) * LOG2E
    seq, nh = logf.shape
    r = lax.broadcasted_iota(jnp.int32, (_FOX_T, _FOX_T), 0)
    c = lax.broadcasted_iota(jnp.int32, (_FOX_T, _FOX_T), 1)
    lower = (c <= r).astype(F32)
    carry = jnp.zeros((1, nh), F32)
    for ch in range(seq // _FOX_T):
        rows = slice(ch * _FOX_T, (ch + 1) * _FOX_T)
        cs = jnp.dot(lower, logf[rows, :], preferred_element_type=F32, precision=HIGHEST)
        g_ref[0, rows, :] = cs
        r_ref[0, ch:ch + 1, :] = carry
        carry = carry + cs[_FOX_T - 1:_FOX_T, :]


def _cum_call(f, b_f, batch, seq):
    nh = f.shape[1]
    return pl.pallas_call(
        _cum_kernel,
        out_shape=[jax.ShapeDtypeStruct((batch, seq, nh), F32),
                   jax.ShapeDtypeStruct((batch, seq // _FOX_T, nh), F32)],
        grid=(batch,),
        in_specs=[
            pl.BlockSpec((seq, nh), lambda b: (b, 0)),
            pl.BlockSpec((1, nh), lambda b: (0, 0)),
        ],
        out_specs=[pl.BlockSpec((1, seq, nh), lambda b: (b, 0, 0)),
                   pl.BlockSpec((1, seq // _FOX_T, nh), lambda b: (b, 0, 0))],
        compiler_params=_params(("parallel",)),
        name="forget_cumsum",
    )(f, b_f.reshape(1, nh))


_ONES_ROWS = 16


def _fox_kernel(r_ref, qt_ref, k_ref, vt_ref, g_ref, o_ref, gb_sc, m_sc, acc_sc, sa_sc, sb_sc):
    t = _FOX_T
    b, hp, qi = pl.program_id(0), pl.program_id(1), pl.program_id(2)
    seq = k_ref.shape[1]
    n_blocks = seq // t
    n_heads = 2 * pl.num_programs(1)

    @pl.when(qi == 0)
    def _():
        for hh in range(2):
            gb_sc[hh] = jnp.broadcast_to(g_ref[0, 0, :, hh:hh + 1], (seq, LANES))

    qt2 = qt_ref[...]
    feat = lax.broadcasted_iota(jnp.int32, (LANES, 1), 0)
    key = lax.broadcasted_iota(jnp.int32, (t, t), 0)
    qry = lax.broadcasted_iota(jnp.int32, (t, t), 1)
    ones = jnp.ones((_ONES_ROWS, t), BF16)
    qh = [jnp.where((feat >= hh * HEAD_DIM) & (feat < (hh + 1) * HEAD_DIM), qt2, 0)
          for hh in range(2)]
    r_base = [(b * n_heads + 2 * hp + hh) * n_blocks for hh in range(2)]
    r_q = [r_ref[r_base[hh] + qi] for hh in range(2)]
    m_sc[...] = jnp.full_like(m_sc, -jnp.inf)
    acc_sc[...] = jnp.zeros_like(acc_sc)

    def scores(kb, dst):
        k0 = pl.multiple_of(kb * t, t)
        k2 = k_ref[0, pl.ds(k0, t), :]
        for hh in range(2):
            gk = gb_sc[hh, pl.ds(k0, t), :]
            dst[hh] = (jnp.dot(k2, qh[hh], preferred_element_type=F32)
                       - jnp.concatenate([gk] * (t // LANES), axis=1))

    def step(kb, cur, nxt, causal):
        if nxt is not None:
            scores(kb + 1, nxt)
        k0 = pl.multiple_of(kb * t, t)
        for hh in range(2):
            st = cur[hh]
            if causal:
                st = jnp.where(key <= qry, st, -jnp.inf)
            c = r_ref[r_base[hh] + kb] - r_q[hh]
            m_old = m_sc[hh]
            m_new = jnp.maximum(m_old, jnp.max(st, axis=0, keepdims=True) - c)
            alpha = jnp.exp2(m_old - m_new)
            p = jnp.exp2(st - (m_new + c)).astype(BF16)
            vt1 = jnp.concatenate(
                [vt_ref[hh * HEAD_DIM:(hh + 1) * HEAD_DIM, pl.ds(k0, t)], ones], axis=0)
            acc_sc[hh] = alpha * acc_sc[hh] + jnp.dot(vt1, p, preferred_element_type=F32)
            m_sc[hh] = m_new

    scores(0, sa_sc)

    def pair(i, carry):
        step(2 * i, sa_sc, sb_sc, False)
        step(2 * i + 1, sb_sc, sa_sc, False)
        return carry

    lax.fori_loop(0, qi // 2, pair, 0)

    @pl.when(qi % 2 == 0)
    def _():
        step(qi, sa_sc, None, True)

    @pl.when(qi % 2 == 1)
    def _():
        step(qi - 1, sa_sc, sb_sc, False)
        step(qi, sb_sc, None, True)

    outs = [acc_sc[hh, :HEAD_DIM, :] / acc_sc[hh, HEAD_DIM:HEAD_DIM + 1, :] for hh in range(2)]
    o_ref[0] = jnp.concatenate(outs, axis=0).T.astype(o_ref.dtype)


def _fox_call(r_flat, q_t, k, v_t, g_pairs):
    batch, seq, d = k.shape
    n_pairs = d // LANES
    t = _FOX_T
    nq = seq // t
    return pl.pallas_call(
        _fox_kernel,
        out_shape=jax.ShapeDtypeStruct((batch, seq, d), BF16),
        grid=(batch, n_pairs, nq),
        in_specs=[
            pl.BlockSpec(memory_space=pltpu.SMEM),
            pl.BlockSpec((LANES, t), lambda b, h, i: (h, b * nq + i)),
            pl.BlockSpec((1, seq, LANES), lambda b, h, i: (b, 0, h)),
            pl.BlockSpec((LANES, seq), lambda b, h, i: (h, b)),
            pl.BlockSpec((1, 1, seq, 2), lambda b, h, i: (b, h, 0, 0)),
        ],
        out_specs=pl.BlockSpec((1, t, LANES), lambda b, h, i: (b, i, h)),
        scratch_shapes=[pltpu.VMEM((2, seq, LANES), F32), pltpu.VMEM((2, 1, t), F32),
                        pltpu.VMEM((2, HEAD_DIM + _ONES_ROWS, t), F32),
                        pltpu.VMEM((2, t, t), F32), pltpu.VMEM((2, t, t), F32)],
        compiler_params=_params(("parallel", "parallel", "arbitrary")),
        name="fox_attention",
    )(r_flat, q_t, k, v_t, g_pairs)


def _out_kernel(*refs, tiles_per_batch, with_router):
    o_ref, w_ref, x_ref, gt_ref = refs[:4]
    b = pl.program_id(0) // tiles_per_batch
    y = jnp.dot(o_ref[...], w_ref[...], preferred_element_type=F32)
    xn = x_ref[...] + gt_ref[pl.ds(b, 1), :] * y
    if not with_router:
        refs[4][...] = xn
        return
    g_ref, sh_ref, sc_ref, wr_ref, br_ref, xo_ref, h_ref, route_ref, cnt_ref = refs[4:]
    xo_ref[...] = xn
    h = _rms_mod(xn, g_ref[...], sh_ref[pl.ds(b, 1), :], sc_ref[pl.ds(b, 1), :])
    h_ref[...] = h
    logits = jnp.dot(h, wr_ref[...], preferred_element_type=F32,
                     precision=HIGHEST) + br_ref[...]
    tm = logits.shape[0]
    lane = lax.broadcasted_iota(jnp.int32, logits.shape, 1).astype(F32)
    logits = jnp.where(lane < N_EXPERTS, logits, -jnp.inf)
    m1 = jnp.max(logits, axis=1, keepdims=True)
    i1 = jnp.min(jnp.where(logits == m1, lane, float(LANES)), axis=1, keepdims=True)
    rest = jnp.where(lane == i1, -jnp.inf, logits)
    m2 = jnp.max(rest, axis=1, keepdims=True)
    i2 = jnp.min(jnp.where(rest == m2, lane, float(LANES)), axis=1, keepdims=True)
    e2 = jnp.exp(m2 - m1)
    den = 1.0 + e2

    @pl.when(pl.program_id(0) == 0)
    def _():
        cnt_ref[...] = jnp.zeros_like(cnt_ref)

    sel1 = lane == i1
    sel2 = lane == i2
    onehot = jnp.where(sel1 | sel2, 1.0, 0.0)
    r = lax.broadcasted_iota(jnp.int32, (tm, tm), 0)
    c = lax.broadcasted_iota(jnp.int32, (tm, tm), 1)
    lower = jnp.where(c < r, 1.0, 0.0).astype(BF16)
    before = jnp.dot(lower, onehot.astype(BF16), preferred_element_type=F32) + cnt_ref[0:1, :]
    rank1 = jnp.sum(jnp.where(sel1, before, 0.0), axis=1, keepdims=True)
    rank2 = jnp.sum(jnp.where(sel2, before, 0.0), axis=1, keepdims=True)
    cnt_ref[0:1, :] = cnt_ref[0:1, :] + jnp.sum(onehot, axis=0, keepdims=True)
    route = jnp.where(lane == 0.0, i1, 0.0)
    for k, val in enumerate((i2, 1.0 / den, e2 / den, rank1, rank2), start=1):
        route = jnp.where(lane == float(k), val, route)
    route_ref[...] = route


def _out_call(o, w_out, x, mod, gate_col, seq, router=None, tm=512):
    n, d = x.shape
    with_router = router is not None
    in_specs = [
        pl.BlockSpec((tm, d), lambda i: (i, 0)),
        pl.BlockSpec((d, d), lambda i: (0, 0)),
        pl.BlockSpec((tm, d), lambda i: (i, 0)),
        pl.BlockSpec((SUBLANES, d), lambda i: (0, gate_col)),
    ]
    args = [o, w_out, x, mod]
    out_shape = [jax.ShapeDtypeStruct((n, d), F32)]
    out_specs = [pl.BlockSpec((tm, d), lambda i: (i, 0))]
    if with_router:
        g, sh_col, sc_col, w_r, b_r = router
        in_specs += [
            pl.BlockSpec((1, d), lambda i: (0, 0)),
            pl.BlockSpec((SUBLANES, d), lambda i: (0, sh_col)),
            pl.BlockSpec((SUBLANES, d), lambda i: (0, sc_col)),
            pl.BlockSpec((d, LANES), lambda i: (0, 0)),
            pl.BlockSpec((1, LANES), lambda i: (0, 0)),
        ]
        args += [g.reshape(1, d), mod, mod, w_r, b_r]
        out_shape += [jax.ShapeDtypeStruct((n, d), F32),
                      jax.ShapeDtypeStruct((n, LANES), F32),
                      jax.ShapeDtypeStruct((SUBLANES, LANES), F32)]
        out_specs += [pl.BlockSpec((tm, d), lambda i: (i, 0)),
                      pl.BlockSpec((tm, LANES), lambda i: (i, 0)),
                      pl.BlockSpec((SUBLANES, LANES), lambda i: (0, 0))]
    res = pl.pallas_call(
        functools.partial(_out_kernel, tiles_per_batch=seq // tm, with_router=with_router),
        out_shape=out_shape,
        grid=(n // tm,),
        in_specs=in_specs,
        out_specs=out_specs,
        compiler_params=_params(("arbitrary",) if with_router else ("parallel",)),
        name="out_proj_router" if with_router else "out_proj",
    )(*args)
    return res if with_router else res[0]


def _ffn_kernel(x_ref, g_ref, sh_ref, sc_ref, gt_ref, wg_ref, wu_ref, wd_ref,
                o_ref, h_sc, acc_sc, *, tiles_per_batch):
    b = pl.program_id(0) // tiles_per_batch
    j = pl.program_id(1)

    @pl.when(j == 0)
    def _():
        h = _rms_mod(x_ref[...], g_ref[...], sh_ref[pl.ds(b, 1), :], sc_ref[pl.ds(b, 1), :])
        h_sc[...] = h.astype(BF16)
        acc_sc[...] = jnp.zeros_like(acc_sc)

    h = h_sc[...]
    g = jnp.dot(h, wg_ref[...], preferred_element_type=F32)
    u = jnp.dot(h, wu_ref[...], preferred_element_type=F32)
    a = (_silu(g) * u).astype(BF16)
    acc_sc[...] += jnp.dot(a, wd_ref[...], preferred_element_type=F32)

    @pl.when(j == pl.num_programs(1) - 1)
    def _():
        o_ref[...] = x_ref[...] + gt_ref[pl.ds(b, 1), :] * acc_sc[...]


def _ffn_call(x, g, mod, w_gu, w_down, seq, tm=1024, tf=512):
    n, d = x.shape
    f = w_down.shape[0]
    nj = f // tf
    return pl.pallas_call(
        functools.partial(_ffn_kernel, tiles_per_batch=seq // tm),
        out_shape=jax.ShapeDtypeStruct((n, d), F32),
        grid=(n // tm, nj),
        in_specs=[
            pl.BlockSpec((tm, d), lambda i, j: (i, 0)),
            pl.BlockSpec((1, d), lambda i, j: (0, 0)),
            pl.BlockSpec((SUBLANES, d), lambda i, j: (0, 3)),
            pl.BlockSpec((SUBLANES, d), lambda i, j: (0, 4)),
            pl.BlockSpec((SUBLANES, d), lambda i, j: (0, 5)),
            pl.BlockSpec((d, tf), lambda i, j: (0, j)),
            pl.BlockSpec((d, tf), lambda i, j: (0, j + nj)),
            pl.BlockSpec((tf, d), lambda i, j: (j, 0)),
        ],
        out_specs=pl.BlockSpec((tm, d), lambda i, j: (i, 0)),
        scratch_shapes=[pltpu.VMEM((tm, d), BF16), pltpu.VMEM((tm, d), F32)],
        compiler_params=_params(("parallel", "arbitrary")),
        name="ffn_swiglu",
    )(x, g.reshape(1, d), mod, mod, mod, w_gu, w_gu, w_down)


_MOE_TM = 512


def _dispatch_kernel(d1_ref, d2_ref, h_ref, xs_in_ref, xs_ref, sem):
    del xs_in_ref
    tm = h_ref.shape[0]
    base = pl.program_id(0) * tm

    def issue(r, carry):
        src = h_ref.at[pl.ds(r, 1)]
        pltpu.make_async_copy(src, xs_ref.at[pl.ds(d1_ref[base + r], 1)], sem.at[0]).start()
        pltpu.make_async_copy(src, xs_ref.at[pl.ds(d2_ref[base + r], 1)], sem.at[1]).start()
        return carry

    lax.fori_loop(0, tm, issue, 0, unroll=8)
    pltpu.make_async_copy(h_ref, xs_ref.at[pl.ds(0, tm)], sem.at[0]).wait()
    pltpu.make_async_copy(h_ref, xs_ref.at[pl.ds(0, tm)], sem.at[1]).wait()


def _dispatch_call(dest1, dest2, h, n_rows, tm=512):
    n, d = h.shape
    zeros = jnp.zeros((n_rows, d), h.dtype)
    return pl.pallas_call(
        _dispatch_kernel,
        out_shape=jax.ShapeDtypeStruct((n_rows, d), h.dtype),
        grid_spec=pltpu.PrefetchScalarGridSpec(
            num_scalar_prefetch=2,
            grid=(n // tm,),
            in_specs=[
                pl.BlockSpec((tm, d), lambda i, d1, d2: (i, 0)),
                pl.BlockSpec(memory_space=pl.ANY),
            ],
            out_specs=pl.BlockSpec(memory_space=pl.ANY),
            scratch_shapes=[pltpu.SemaphoreType.DMA((2,))],
        ),
        input_output_aliases={3: 0},
        compiler_params=_params(("arbitrary",)),
        name="moe_dispatch",
    )(dest1, dest2, h, zeros)


def _experts_kernel(te_ref, nt_ref, xs_ref, wg_ref, wu_ref, wd_ref, ye_ref, h_sc, acc_sc):
    del te_ref
    t = pl.program_id(0)
    j = pl.program_id(1)

    @pl.when(t < nt_ref[0])
    def _():
        @pl.when(j == 0)
        def _():
            h_sc[...] = xs_ref[...].astype(BF16)
            acc_sc[...] = jnp.zeros_like(acc_sc)

        h = h_sc[...]
        g = jnp.dot(h, wg_ref[0], preferred_element_type=F32)
        u = jnp.dot(h, wu_ref[0], preferred_element_type=F32)
        a = (_silu(g) * u).astype(BF16)
        acc_sc[...] += jnp.dot(a, wd_ref[0], preferred_element_type=F32)

        @pl.when(j == pl.num_programs(1) - 1)
        def _():
            ye_ref[...] = acc_sc[...]

    @pl.when((t >= nt_ref[0]) & (j == 0))
    def _():
        ye_ref[...] = jnp.zeros_like(ye_ref)


def _experts_call(tile_expert, n_tiles, xs, w_gu, w_down, tf=896):
    n_rows, d = xs.shape
    _, f, _ = w_down.shape
    nj = f // tf
    tm = _MOE_TM
    tile = lambda t, nt: jnp.maximum(jnp.minimum(t, nt[0] - 1), 0)
    jj = lambda t, j, nt: jnp.where(t < nt[0], j, nj - 1)
    return pl.pallas_call(
        _experts_kernel,
        out_shape=jax.ShapeDtypeStruct((n_rows, d), F32),
        grid_spec=pltpu.PrefetchScalarGridSpec(
            num_scalar_prefetch=2,
            grid=(n_rows // tm, nj),
            in_specs=[
                pl.BlockSpec((tm, d), lambda t, j, te, nt: (tile(t, nt), 0)),
                pl.BlockSpec((1, d, tf), lambda t, j, te, nt: (te[tile(t, nt)], 0, jj(t, j, nt))),
                pl.BlockSpec((1, d, tf), lambda t, j, te, nt: (te[tile(t, nt)], 0, jj(t, j, nt) + nj)),
                pl.BlockSpec((1, tf, d), lambda t, j, te, nt: (te[tile(t, nt)], jj(t, j, nt), 0)),
            ],
            out_specs=pl.BlockSpec((tm, d), lambda t, j, te, nt: (t, 0)),
            scratch_shapes=[pltpu.VMEM((tm, d), BF16), pltpu.VMEM((tm, d), F32)],
        ),
        compiler_params=_params(("arbitrary", "arbitrary")),
        name="moe_experts",
    )(tile_expert, n_tiles, xs, w_gu, w_gu, w_down)


def _combine_kernel(d1_ref, d2_ref, ye_ref, x_ref, route_ref, gt_ref, gf_ref, o_ref,
                    y1_sc, y2_sc, sem, *, tiles_per_batch):
    tm = x_ref.shape[0]
    i = pl.program_id(0)
    b = i // tiles_per_batch
    base = i * tm

    def issue(r, carry):
        pltpu.make_async_copy(ye_ref.at[pl.ds(d1_ref[base + r], 1)], y1_sc.at[pl.ds(r, 1)],
                              sem.at[0]).start()
        pltpu.make_async_copy(ye_ref.at[pl.ds(d2_ref[base + r], 1)], y2_sc.at[pl.ds(r, 1)],
                              sem.at[1]).start()
        return carry

    lax.fori_loop(0, tm, issue, 0, unroll=8)
    pltpu.make_async_copy(ye_ref.at[pl.ds(0, tm)], y1_sc, sem.at[0]).wait()
    pltpu.make_async_copy(ye_ref.at[pl.ds(0, tm)], y2_sc, sem.at[1]).wait()
    route = route_ref[...]
    y = route[:, 2:3] * y1_sc[...] + route[:, 3:4] * y2_sc[...]
    xn = x_ref[...] + gt_ref[pl.ds(b, 1), :] * y
    ms = jnp.mean(xn * xn, axis=-1, keepdims=True)
    o_ref[...] = xn * lax.rsqrt(ms + EPS) * gf_ref[...]


def _combine_call(dest1, dest2, ye, x, route, mod, g_final, seq, tm=512):
    n, d = x.shape
    return pl.pallas_call(
        functools.partial(_combine_kernel, tiles_per_batch=seq // tm),
        out_shape=jax.ShapeDtypeStruct((n, d), F32),
        grid_spec=pltpu.PrefetchScalarGridSpec(
            num_scalar_prefetch=2,
            grid=(n // tm,),
            in_specs=[
                pl.BlockSpec(memory_space=pl.ANY),
                pl.BlockSpec((tm, d), lambda i, d1, d2: (i, 0)),
                pl.BlockSpec((tm, LANES), lambda i, d1, d2: (i, 0)),
                pl.BlockSpec((SUBLANES, d), lambda i, d1, d2: (0, 5)),
                pl.BlockSpec((1, d), lambda i, d1, d2: (0, 0)),
            ],
            out_specs=pl.BlockSpec((tm, d), lambda i, d1, d2: (i, 0)),
            scratch_shapes=[pltpu.VMEM((tm, d), F32), pltpu.VMEM((tm, d), F32),
                            pltpu.SemaphoreType.DMA((2,))],
        ),
        compiler_params=_params(("arbitrary",)),
        name="moe_combine",
    )(dest1, dest2, ye, x, route, mod, g_final.reshape(1, d))


def _moe_call(h, route, counts, x, mod, g_final, w_gu, w_down, seq):
    n, d = x.shape
    ne = w_down.shape[0]
    tm = _MOE_TM
    max_tiles = (2 * n) // tm + ne
    e1 = route[:, 0].astype(jnp.int32)
    e2 = route[:, 1].astype(jnp.int32)
    cnt = counts[0, :ne].astype(jnp.int32)
    tiles_e = (cnt + tm - 1) // tm
    tile_end = jnp.cumsum(tiles_e)
    row_start = (tile_end - tiles_e) * tm
    dest1 = row_start[e1] + route[:, 4].astype(jnp.int32)
    dest2 = row_start[e2] + route[:, 5].astype(jnp.int32)
    n_tiles = tile_end[-1:]
    tile_expert = jnp.minimum(
        jnp.searchsorted(tile_end, jnp.arange(max_tiles, dtype=jnp.int32), side="right"),
        ne - 1).astype(jnp.int32)
    xs = _dispatch_call(dest1, dest2, h, max_tiles * tm)
    ye = _experts_call(tile_expert, n_tiles, xs, w_gu, w_down)
    return _combine_call(dest1, dest2, ye, x, route, mod, g_final, seq)


_SWA_TQ = 2 * CHUNK
_SWA_BAND = 2 * _SWA_TQ


def _swa_bucket_tiles():
    cc = np.arange(_SWA_BAND)[:, None]
    r = np.arange(_SWA_TQ)[None, :]
    rel = cc - _SWA_TQ - r
    nb = REL_BUCKETS // 2
    max_exact = nb // 2
    ret = (rel > 0).astype(np.int32) * nb
    n = np.abs(rel)
    large = max_exact + (np.log(np.maximum(n, 1) / max_exact)
                         / np.log(REL_MAX_DIST / max_exact) * (nb - max_exact)).astype(np.int32)
    large = np.minimum(large, nb - 1)
    bucket = (ret + np.where(n < max_exact, n, large)).astype(np.int32)
    q_chunk = r // CHUNK
    k_chunk = cc // CHUNK
    visible = (k_chunk >= q_chunk) & (k_chunk <= q_chunk + WINDOW_CHUNKS)
    later = np.where(visible, bucket, -1)
    first = np.where(cc >= _SWA_TQ, later, -1)
    return np.stack([first, later]).astype(np.int32)


def _swa_bias_kernel(tbl_ref, bkt_ref, o_ref):
    n_heads = o_ref.shape[1]
    for v in range(2):
        bkt = bkt_ref[v]
        for head in range(n_heads):
            tile = jnp.full(bkt.shape, NEG_BIG, F32)
            for bk in range(REL_BUCKETS):
                tile = jnp.where(bkt == bk, tbl_ref[head, bk] * LOG2E, tile)
            o_ref[v, head] = tile


def _swa_bias_call(rel_bias):
    bkt = jnp.asarray(_swa_bucket_tiles())
    n_heads = rel_bias.shape[1]
    return pl.pallas_call(
        _swa_bias_kernel,
        out_shape=jax.ShapeDtypeStruct((2, n_heads, _SWA_BAND, _SWA_TQ), F32),
        in_specs=[
            pl.BlockSpec(memory_space=pltpu.SMEM),
            pl.BlockSpec(memory_space=pltpu.VMEM),
        ],
        out_specs=pl.BlockSpec(memory_space=pltpu.VMEM),
        name="swa_bias",
    )(rel_bias.T, bkt)


def _swa_kernel(qt_ref, kp_ref, kc_ref, vtp_ref, vtc_ref, bias_ref, sink_ref, o_ref):
    tq = _SWA_TQ
    lane = lax.broadcasted_iota(jnp.int32, (1, LANES), 1)
    ones = jnp.ones((_ONES_ROWS, _SWA_BAND), BF16)
    outs = [None] * (SWA_KV_HEADS * SWA_GROUP)
    units = [(hk, par) for hk in range(SWA_KV_HEADS) for par in range(2)]

    def scores(hk, par):
        ksl = slice(hk * LANES, (hk + 1) * LANES)
        kb = jnp.concatenate([kp_ref[0, :, ksl], kc_ref[0, :, ksl]], axis=0)
        f0 = hk * SWA_GROUP * HEAD_DIM
        wq = jnp.concatenate([qt_ref[f0:f0 + LANES, :], qt_ref[f0 + LANES:f0 + 2 * LANES, :]],
                             axis=1)
        head_lanes = (lane < HEAD_DIM) if par == 0 else (lane >= HEAD_DIM)
        return jnp.dot(jnp.where(head_lanes, kb, 0), wq, preferred_element_type=F32)

    sts = [scores(hk, par) for hk, par in units]
    for (hk, par), st in zip(units, sts):
        vsl = slice(hk * HEAD_DIM, (hk + 1) * HEAD_DIM)
        vt1 = jnp.concatenate(
            [jnp.concatenate([vtp_ref[vsl, :], vtc_ref[vsl, :]], axis=1), ones], axis=0)
        heads = (hk * SWA_GROUP + par, hk * SWA_GROUP + par + 2)
        ps, ms = [], []
        for i, head in enumerate(heads):
            s = st[:, i * tq:(i + 1) * tq] + bias_ref[0, head]
            m = jnp.maximum(jnp.max(s, axis=0, keepdims=True), sink_ref[head])
            ps.append(jnp.exp2(s - m).astype(BF16))
            ms.append(m)
        acc = jnp.dot(vt1, jnp.concatenate(ps, axis=1), preferred_element_type=F32)
        for i, head in enumerate(heads):
            a = acc[:, i * tq:(i + 1) * tq]
            den = a[HEAD_DIM:HEAD_DIM + 1] + jnp.exp2(sink_ref[head] - ms[i])
            outs[head] = a[:HEAD_DIM] / den
    o_ref[0] = jnp.concatenate(outs, axis=0).T.astype(o_ref.dtype)


def _swa_call(q_t, k, v_t, bias, sink, batch, seq):
    d = q_t.shape[0]
    kw = k.shape[2]
    vw = v_t.shape[0]
    tq = _SWA_TQ
    nq = seq // tq
    return pl.pallas_call(
        _swa_kernel,
        out_shape=jax.ShapeDtypeStruct((batch, seq, d), BF16),
        grid=(batch, nq),
        in_specs=[
            pl.BlockSpec((d, tq), lambda b, i: (0, b * nq + i)),
            pl.BlockSpec((1, tq, kw), lambda b, i: (b, jnp.maximum(i - 1, 0), 0)),
            pl.BlockSpec((1, tq, kw), lambda b, i: (b, i, 0)),
            pl.BlockSpec((vw, tq), lambda b, i: (0, b * nq + jnp.maximum(i - 1, 0))),
            pl.BlockSpec((vw, tq), lambda b, i: (0, b * nq + i)),
            pl.BlockSpec((1,) + bias.shape[1:], lambda b, i: (jnp.minimum(i, 1), 0, 0, 0)),
            pl.BlockSpec(sink.shape, lambda b, i: (0, 0, 0)),
        ],
        out_specs=pl.BlockSpec((1, tq, d), lambda b, i: (b, i, 0)),
        compiler_params=_params(("parallel", "arbitrary")),
        name="swa_attention",
    )(q_t, k, k, v_t, v_t, bias, sink)


def kernel(x, c, w_ada, b_ada, g_norm_mix, g_norm_ffn, g_final, fox_w_in, fox_b_f, fox_w_out, swa_w_in, swa_sinks, swa_w_out, rel_bias, ffn_w_gu, ffn_w_down, moe_w_router, moe_b_router, moe_w_gu, moe_w_down):
    batch, seq, d = x.shape
    n = batch * seq
    q_scale = HEAD_DIM ** -0.5
    xf = x.reshape(n, d)

    c_pad = jnp.zeros((SUBLANES, d), F32).at[:batch].set(c)
    mod = _ada_call(c_pad, w_ada, b_ada)
    mod0, mod1 = mod[0], mod[1]

    w_in = fox_w_in[0]
    n_heads = d // HEAD_DIM
    k, q_t, v_t, f = _attn_proj_call(
        xf, g_norm_mix[0], mod0, w_in[:, d:2 * d].astype(BF16),
        (w_in[:, :d] * (q_scale * LOG2E)).T.astype(BF16), w_in[:, 2 * d:3 * d].T.astype(BF16),
        seq, w_f=w_in[:, 3 * d:])
    g_cum, r_cum = _cum_call(f, fox_b_f[0], batch, seq)
    g_pairs = g_cum.reshape(batch, seq, n_heads // 2, 2).transpose(0, 2, 1, 3)
    r_flat = r_cum.transpose(0, 2, 1).reshape(-1)
    o = _fox_call(r_flat, q_t, k.reshape(batch, seq, d), v_t, g_pairs)
    x1 = _out_call(o.reshape(n, d), fox_w_out[0].astype(BF16), xf, mod0, 2, seq)
    x2 = _ffn_call(x1, g_norm_ffn[0], mod0, ffn_w_gu[0].astype(BF16), ffn_w_down[0].astype(BF16), seq)

    w_in = swa_w_in[0]
    kvw = SWA_KV_HEADS * HEAD_DIM
    dup = lambda w: jnp.repeat(w.reshape(d, SWA_KV_HEADS, 1, HEAD_DIM), 2, axis=2).reshape(d, 2 * kvw)
    k, q_t, v_t = _attn_proj_call(
        x2, g_norm_mix[1], mod1, dup(w_in[:, d:d + kvw]).astype(BF16),
        (w_in[:, :d] * (q_scale * LOG2E)).T.astype(BF16), w_in[:, d + kvw:].T.astype(BF16), seq)
    bias = _swa_bias_call(rel_bias)
    sink = jnp.broadcast_to((swa_sinks[0] * LOG2E)[:, None, None], (n_heads, 1, _SWA_TQ))
    o = _swa_call(q_t, k.reshape(batch, seq, 2 * kvw), v_t, bias, sink, batch, seq)
    w_r = jnp.zeros((d, LANES), F32).at[:, :N_EXPERTS].set(moe_w_router[0])
    b_r = jnp.zeros((1, LANES), F32).at[0, :N_EXPERTS].set(moe_b_router[0])
    x3, h4, route, counts = _out_call(o.reshape(n, d), swa_w_out[0].astype(BF16), x2, mod1, 2, seq,
                                      router=(g_norm_ffn[1], 3, 4, w_r, b_r))
    out = _moe_call(h4, route, counts, x3, mod1, g_final, moe_w_gu[0].astype(BF16),
                    moe_w_down[0].astype(BF16), seq)
    return out.reshape(batch, seq, d)
```

```python
import functools

import numpy as np
import jax
import jax.numpy as jnp
from jax import lax
from jax.experimental import pallas as pl
from jax.experimental.pallas import tpu as pltpu

F32 = jnp.float32
BF16 = jnp.bfloat16
HIGHEST = lax.Precision.HIGHEST

HEAD_DIM = 64
CHUNK = 64
WINDOW_CHUNKS = 2
REL_BUCKETS = 32
REL_MAX_DIST = 128
SWA_KV_HEADS = 4
SWA_GROUP = 4
N_EXPERTS = 8
EPS = 1e-6

LANES = 128
SUBLANES = 8
VMEM_LIMIT = 56 * 1024 * 1024

NEG_BIG = -1e30


def _params(sem, vmem=VMEM_LIMIT):
    return pltpu.CompilerParams(dimension_semantics=sem, vmem_limit_bytes=vmem)


def _rms_mod(x, g, shift, scale):
    ms = jnp.mean(x * x, axis=-1, keepdims=True)
    y = x * lax.rsqrt(ms + EPS) * g
    return y * (1.0 + scale) + shift


def _silu(x):
    return x / (1.0 + jnp.exp(-x))


def _ada_kernel(c_ref, w_ref, b_ref, o_ref):
    cond = _silu(c_ref[...])
    o_ref[0] = jnp.dot(cond, w_ref[0], preferred_element_type=F32,
                       precision=HIGHEST) + b_ref[0]


def _ada_call(c_pad, w_ada, b_ada):
    depth, d, n = w_ada.shape
    tn = 1536
    return pl.pallas_call(
        _ada_kernel,
        out_shape=jax.ShapeDtypeStruct((depth, SUBLANES, n), F32),
        grid=(depth, n // tn),
        in_specs=[
            pl.BlockSpec((SUBLANES, d), lambda l, j: (0, 0)),
            pl.BlockSpec((1, d, tn), lambda l, j: (l, 0, j)),
            pl.BlockSpec((1, 1, tn), lambda l, j: (l, 0, j)),
        ],
        out_specs=pl.BlockSpec((1, SUBLANES, tn), lambda l, j: (l, 0, j)),
        compiler_params=_params(("parallel", "parallel")),
        name="ada_mod",
    )(c_pad, w_ada, b_ada.reshape(depth, 1, n))


_NT = (((1,), (1,)), ((), ()))


def _attn_proj_kernel(*refs, tiles_per_batch, with_gate):
    x_ref, g_ref, sh_ref, sc_ref, wk_ref, wqt_ref, wvt_ref = refs[:7]
    k_ref, qt_ref, vt_ref = refs[7 + with_gate:10 + with_gate]
    b = pl.program_id(0) // tiles_per_batch
    h = _rms_mod(x_ref[...], g_ref[...], sh_ref[pl.ds(b, 1), :], sc_ref[pl.ds(b, 1), :])
    hb = h.astype(BF16)
    k_ref[...] = jnp.dot(hb, wk_ref[...], preferred_element_type=F32).astype(BF16)
    qt_ref[...] = lax.dot_general(wqt_ref[...], hb, _NT, preferred_element_type=F32).astype(BF16)
    vt_ref[...] = lax.dot_general(wvt_ref[...], hb, _NT, preferred_element_type=F32).astype(BF16)
    if with_gate:
        refs[-1][...] = jnp.dot(h, refs[7][...], preferred_element_type=F32, precision=HIGHEST)


def _attn_proj_call(x, g, mod, w_k, w_qt, w_vt, seq, w_f=None, tm=512):
    n, d = x.shape
    with_gate = w_f is not None
    full = lambda a: pl.BlockSpec(a.shape, lambda i: (0, 0))
    in_specs = [
        pl.BlockSpec((tm, d), lambda i: (i, 0)),
        pl.BlockSpec((1, d), lambda i: (0, 0)),
        pl.BlockSpec((SUBLANES, d), lambda i: (0, 0)),
        pl.BlockSpec((SUBLANES, d), lambda i: (0, 1)),
        full(w_k), full(w_qt), full(w_vt),
    ]
    args = [x, g.reshape(1, d), mod, mod, w_k, w_qt, w_vt]
    out_shape = [jax.ShapeDtypeStruct((n, w_k.shape[1]), BF16),
                 jax.ShapeDtypeStruct((w_qt.shape[0], n), BF16),
                 jax.ShapeDtypeStruct((w_vt.shape[0], n), BF16)]
    out_specs = [pl.BlockSpec((tm, w_k.shape[1]), lambda i: (i, 0)),
                 pl.BlockSpec((w_qt.shape[0], tm), lambda i: (0, i)),
                 pl.BlockSpec((w_vt.shape[0], tm), lambda i: (0, i))]
    if with_gate:
        in_specs.append(full(w_f))
        args.append(w_f)
        out_shape.append(jax.ShapeDtypeStruct((n, w_f.shape[1]), F32))
        out_specs.append(pl.BlockSpec((tm, w_f.shape[1]), lambda i: (i, 0)))
    return pl.pallas_call(
        functools.partial(_attn_proj_kernel, tiles_per_batch=seq // tm, with_gate=with_gate),
        out_shape=out_shape,
        grid=(n // tm,),
        in_specs=in_specs,
        out_specs=out_specs,
        compiler_params=_params(("parallel",)),
        name="attn_proj_gate" if with_gate else "attn_proj",
    )(*args)


_FOX_T = 512
LOG2E = 1.4426950408889634


def _cum_kernel(f_ref, bf_ref, g_ref, r_ref):
    x = f_ref[...] + bf_ref[...]
    logf = (jnp.minimum(x, 0.0) - jnp.log(1.0 + jnp.exp(-jnp.abs(x)))) * LOG2E
    seq, nh = logf.shape
    r = lax.broadcasted_iota(jnp.int32, (_FOX_T, _FOX_T), 0)
    c = lax.broadcasted_iota(jnp.int32, (_FOX_T, _FOX_T), 1)
    lower = (c <= r).astype(F32)
    carry = jnp.zeros((1, nh), F32)
    for ch in range(seq // _FOX_T):
        rows = slice(ch * _FOX_T, (ch + 1) * _FOX_T)
        cs = jnp.dot(lower, logf[rows, :], preferred_element_type=F32, precision=HIGHEST)
        g_ref[0, rows, :] = cs
        r_ref[0, ch:ch + 1, :] = carry
        carry = carry + cs[_FOX_T - 1:_FOX_T, :]


def _cum_call(f, b_f, batch, seq):
    nh = f.shape[1]
    return pl.pallas_call(
        _cum_kernel,
        out_shape=[jax.ShapeDtypeStruct((batch, seq, nh), F32),
                   jax.ShapeDtypeStruct((batch, seq // _FOX_T, nh), F32)],
        grid=(batch,),
        in_specs=[
            pl.BlockSpec((seq, nh), lambda b: (b, 0)),
            pl.BlockSpec((1, nh), lambda b: (0, 0)),
        ],
        out_specs=[pl.BlockSpec((1, seq, nh), lambda b: (b, 0, 0)),
                   pl.BlockSpec((1, seq // _FOX_T, nh), lambda b: (b, 0, 0))],
        compiler_params=_params(("parallel",)),
        name="forget_cumsum",
    )(f, b_f.reshape(1, nh))


_ONES_ROWS = 16


def _fox_kernel(r_ref, qt_ref, k_ref, vt_ref, g_ref, o_ref, gb_sc, m_sc, acc_sc, sa_sc, sb_sc):
    t = _FOX_T
    b, hp, qi = pl.program_id(0), pl.program_id(1), pl.program_id(2)
    seq = k_ref.shape[1]
    n_blocks = seq // t
    n_heads = 2 * pl.num_programs(1)

    @pl.when(qi == 0)
    def _():
        for hh in range(2):
            gb_sc[hh] = jnp.broadcast_to(g_ref[0, 0, :, hh:hh + 1], (seq, LANES))

    qt2 = qt_ref[...]
    feat = lax.broadcasted_iota(jnp.int32, (LANES, 1), 0)
    key = lax.broadcasted_iota(jnp.int32, (t, t), 0)
    qry = lax.broadcasted_iota(jnp.int32, (t, t), 1)
    ones = jnp.ones((_ONES_ROWS, t), BF16)
    qh = [jnp.where((feat >= hh * HEAD_DIM) & (feat < (hh + 1) * HEAD_DIM), qt2, 0)
          for hh in range(2)]
    r_base = [(b * n_heads + 2 * hp + hh) * n_blocks for hh in range(2)]
    r_q = [r_ref[r_base[hh] + qi] for hh in range(2)]
    m_sc[...] = jnp.full_like(m_sc, -jnp.inf)
    acc_sc[...] = jnp.zeros_like(acc_sc)

    def scores(kb, dst):
        k0 = pl.multiple_of(kb * t, t)
        k2 = k_ref[0, pl.ds(k0, t), :]
        for hh in range(2):
            gk = gb_sc[hh, pl.ds(k0, t), :]
            dst[hh] = (jnp.dot(k2, qh[hh], preferred_element_type=F32)
                       - jnp.concatenate([gk] * (t // LANES), axis=1))

    def step(kb, cur, nxt, causal):
        if nxt is not None:
            scores(kb + 1, nxt)
        k0 = pl.multiple_of(kb * t, t)
        for hh in range(2):
            st = cur[hh]
            if causal:
                st = jnp.where(key <= qry, st, -jnp.inf)
            c = r_ref[r_base[hh] + kb] - r_q[hh]
            m_old = m_sc[hh]
            m_new = jnp.maximum(m_old, jnp.max(st, axis=0, keepdims=True) - c)
            alpha = jnp.exp2(m_old - m_new)
            p = jnp.exp2(st - (m_new + c)).astype(BF16)
            vt1 = jnp.concatenate(
                [vt_ref[hh * HEAD_DIM:(hh + 1) * HEAD_DIM, pl.ds(k0, t)], ones], axis=0)
            acc_sc[hh] = alpha * acc_sc[hh] + jnp.dot(vt1, p, preferred_element_type=F32)
            m_sc[hh] = m_new

    scores(0, sa_sc)

    def pair(i, carry):
        step(2 * i, sa_sc, sb_sc, False)
        step(2 * i + 1, sb_sc, sa_sc, False)
        return carry

    lax.fori_loop(0, qi // 2, pair, 0)

    @pl.when(qi % 2 == 0)
    def _():
        step(qi, sa_sc, None, True)

    @pl.when(qi % 2 == 1)
    def _():
        step(qi - 1, sa_sc, sb_sc, False)
        step(qi, sb_sc, None, True)

    outs = [acc_sc[hh, :HEAD_DIM, :] / acc_sc[hh, HEAD_DIM:HEAD_DIM + 1, :] for hh in range(2)]
    o_ref[0] = jnp.concatenate(outs, axis=0).T.astype(o_ref.dtype)


def _fox_call(r_flat, q_t, k, v_t, g_pairs):
    batch, seq, d = k.shape
    n_pairs = d // LANES
    t = _FOX_T
    nq = seq // t
    return pl.pallas_call(
        _fox_kernel,
        out_shape=jax.ShapeDtypeStruct((batch, seq, d), BF16),
        grid=(batch, n_pairs, nq),
        in_specs=[
            pl.BlockSpec(memory_space=pltpu.SMEM),
            pl.BlockSpec((LANES, t), lambda b, h, i: (h, b * nq + i)),
            pl.BlockSpec((1, seq, LANES), lambda b, h, i: (b, 0, h)),
            pl.BlockSpec((LANES, seq), lambda b, h, i: (h, b)),
            pl.BlockSpec((1, 1, seq, 2), lambda b, h, i: (b, h, 0, 0)),
        ],
        out_specs=pl.BlockSpec((1, t, LANES), lambda b, h, i: (b, i, h)),
        scratch_shapes=[pltpu.VMEM((2, seq, LANES), F32), pltpu.VMEM((2, 1, t), F32),
                        pltpu.VMEM((2, HEAD_DIM + _ONES_ROWS, t), F32),
                        pltpu.VMEM((2, t, t), F32), pltpu.VMEM((2, t, t), F32)],
        compiler_params=_params(("parallel", "parallel", "arbitrary")),
        name="fox_attention",
    )(r_flat, q_t, k, v_t, g_pairs)


def _out_kernel(*refs, tiles_per_batch, with_router):
    o_ref, w_ref, x_ref, gt_ref = refs[:4]
    b = pl.program_id(0) // tiles_per_batch
    y = jnp.dot(o_ref[...], w_ref[...], preferred_element_type=F32)
    xn = x_ref[...] + gt_ref[pl.ds(b, 1), :] * y
    if not with_router:
        refs[4][...] = xn
        return
    g_ref, sh_ref, sc_ref, wr_ref, br_ref, xo_ref, h_ref, route_ref, cnt_ref = refs[4:]
    xo_ref[...] = xn
    h = _rms_mod(xn, g_ref[...], sh_ref[pl.ds(b, 1), :], sc_ref[pl.ds(b, 1), :])
    h_ref[...] = h
    logits = jnp.dot(h, wr_ref[...], preferred_element_type=F32,
                     precision=HIGHEST) + br_ref[...]
    tm = logits.shape[0]
    lane = lax.broadcasted_iota(jnp.int32, logits.shape, 1).astype(F32)
    logits = jnp.where(lane < N_EXPERTS, logits, -jnp.inf)
    m1 = jnp.max(logits, axis=1, keepdims=True)
    i1 = jnp.min(jnp.where(logits == m1, lane, float(LANES)), axis=1, keepdims=True)
    rest = jnp.where(lane == i1, -jnp.inf, logits)
    m2 = jnp.max(rest, axis=1, keepdims=True)
    i2 = jnp.min(jnp.where(rest == m2, lane, float(LANES)), axis=1, keepdims=True)
    e2 = jnp.exp(m2 - m1)
    den = 1.0 + e2

    @pl.when(pl.program_id(0) == 0)
    def _():
        cnt_ref[...] = jnp.zeros_like(cnt_ref)

    sel1 = lane == i1
    sel2 = lane == i2
    onehot = jnp.where(sel1 | sel2, 1.0, 0.0)
    r = lax.broadcasted_iota(jnp.int32, (tm, tm), 0)
    c = lax.broadcasted_iota(jnp.int32, (tm, tm), 1)
    lower = jnp.where(c < r, 1.0, 0.0).astype(BF16)
    before = jnp.dot(lower, onehot.astype(BF16), preferred_element_type=F32) + cnt_ref[0:1, :]
    rank1 = jnp.sum(jnp.where(sel1, before, 0.0), axis=1, keepdims=True)
    rank2 = jnp.sum(jnp.where(sel2, before, 0.0), axis=1, keepdims=True)
    cnt_ref[0:1, :] = cnt_ref[0:1, :] + jnp.sum(onehot, axis=0, keepdims=True)
    route = jnp.where(lane == 0.0, i1, 0.0)
    for k, val in enumerate((i2, 1.0 / den, e2 / den, rank1, rank2), start=1):
        route = jnp.where(lane == float(k), val, route)
    route_ref[...] = route


def _out_call(o, w_out, x, mod, gate_col, seq, router=None, tm=512):
    n, d = x.shape
    with_router = router is not None
    in_specs = [
        pl.BlockSpec((tm, d), lambda i: (i, 0)),
        pl.BlockSpec((d, d), lambda i: (0, 0)),
        pl.BlockSpec((tm, d), lambda i: (i, 0)),
        pl.BlockSpec((SUBLANES, d), lambda i: (0, gate_col)),
    ]
    args = [o, w_out, x, mod]
    out_shape = [jax.ShapeDtypeStruct((n, d), F32)]
    out_specs = [pl.BlockSpec((tm, d), lambda i: (i, 0))]
    if with_router:
        g, sh_col, sc_col, w_r, b_r = router
        in_specs += [
            pl.BlockSpec((1, d), lambda i: (0, 0)),
            pl.BlockSpec((SUBLANES, d), lambda i: (0, sh_col)),
            pl.BlockSpec((SUBLANES, d), lambda i: (0, sc_col)),
            pl.BlockSpec((d, LANES), lambda i: (0, 0)),
            pl.BlockSpec((1, LANES), lambda i: (0, 0)),
        ]
        args += [g.reshape(1, d), mod, mod, w_r, b_r]
        out_shape += [jax.ShapeDtypeStruct((n, d), F32),
                      jax.ShapeDtypeStruct((n, LANES), F32),
                      jax.ShapeDtypeStruct((SUBLANES, LANES), F32)]
        out_specs += [pl.BlockSpec((tm, d), lambda i: (i, 0)),
                      pl.BlockSpec((tm, LANES), lambda i: (i, 0)),
                      pl.BlockSpec((SUBLANES, LANES), lambda i: (0, 0))]
    res = pl.pallas_call(
        functools.partial(_out_kernel, tiles_per_batch=seq // tm, with_router=with_router),
        out_shape=out_shape,
        grid=(n // tm,),
        in_specs=in_specs,
        out_specs=out_specs,
        compiler_params=_params(("arbitrary",) if with_router else ("parallel",)),
        name="out_proj_router" if with_router else "out_proj",
    )(*args)
    return res if with_router else res[0]


_FF_TILE = 1792


def _snake(i, j, nj):
    return jnp.where(i % 2 == 0, j, nj - 1 - j)


def _ffn_kernel(x_ref, g_ref, sh_ref, sc_ref, gt_ref, wg_ref, wu_ref, wd_ref,
                o_ref, h_sc, *, tiles_per_batch):
    b = pl.program_id(0) // tiles_per_batch

    @pl.when(pl.program_id(1) == 0)
    def _():
        x = x_ref[...]
        h = _rms_mod(x, g_ref[...], sh_ref[pl.ds(b, 1), :], sc_ref[pl.ds(b, 1), :])
        h_sc[...] = h.astype(BF16)
        o_ref[...] = x

    h = h_sc[...]
    g = jnp.dot(h, wg_ref[...], preferred_element_type=F32)
    u = jnp.dot(h, wu_ref[...], preferred_element_type=F32)
    a = (_silu(g) * u).astype(BF16)
    o_ref[...] += gt_ref[pl.ds(b, 1), :] * jnp.dot(a, wd_ref[...], preferred_element_type=F32)


def _ffn_call(x, g, mod, w_gu, w_down, seq, tm=512, tf=_FF_TILE):
    n, d = x.shape
    f = w_down.shape[0]
    nj = f // tf
    return pl.pallas_call(
        functools.partial(_ffn_kernel, tiles_per_batch=seq // tm),
        out_shape=jax.ShapeDtypeStruct((n, d), F32),
        grid=(n // tm, nj),
        in_specs=[
            pl.BlockSpec((tm, d), lambda i, j: (i, 0)),
            pl.BlockSpec((1, d), lambda i, j: (0, 0)),
            pl.BlockSpec((SUBLANES, d), lambda i, j: (0, 3)),
            pl.BlockSpec((SUBLANES, d), lambda i, j: (0, 4)),
            pl.BlockSpec((SUBLANES, d), lambda i, j: (0, 5)),
            pl.BlockSpec((d, tf), lambda i, j: (0, _snake(i, j, nj))),
            pl.BlockSpec((d, tf), lambda i, j: (0, _snake(i, j, nj) + nj)),
            pl.BlockSpec((tf, d), lambda i, j: (_snake(i, j, nj), 0)),
        ],
        out_specs=pl.BlockSpec((tm, d), lambda i, j: (i, 0)),
        scratch_shapes=[pltpu.VMEM((tm, d), BF16)],
        compiler_params=_params(("parallel", "arbitrary")),
        name="ffn_swiglu",
    )(x, g.reshape(1, d), mod, mod, mod, w_gu, w_gu, w_down)


_MOE_TM = 512


def _dispatch_kernel(d1_ref, d2_ref, h_ref, xs_in_ref, xs_ref, sem):
    del xs_in_ref
    tm = h_ref.shape[0]
    base = pl.program_id(0) * tm

    def issue(r, carry):
        src = h_ref.at[pl.ds(r, 1)]
        pltpu.make_async_copy(src, xs_ref.at[pl.ds(d1_ref[base + r], 1)], sem.at[0]).start()
        pltpu.make_async_copy(src, xs_ref.at[pl.ds(d2_ref[base + r], 1)], sem.at[1]).start()
        return carry

    lax.fori_loop(0, tm, issue, 0, unroll=8)
    pltpu.make_async_copy(h_ref, xs_ref.at[pl.ds(0, tm)], sem.at[0]).wait()
    pltpu.make_async_copy(h_ref, xs_ref.at[pl.ds(0, tm)], sem.at[1]).wait()


def _dispatch_call(dest1, dest2, h, n_rows, tm=512):
    n, d = h.shape
    zeros = jnp.zeros((n_rows, d), h.dtype)
    return pl.pallas_call(
        _dispatch_kernel,
        out_shape=jax.ShapeDtypeStruct((n_rows, d), h.dtype),
        grid_spec=pltpu.PrefetchScalarGridSpec(
            num_scalar_prefetch=2,
            grid=(n // tm,),
            in_specs=[
                pl.BlockSpec((tm, d), lambda i, d1, d2: (i, 0)),
                pl.BlockSpec(memory_space=pl.ANY),
            ],
            out_specs=pl.BlockSpec(memory_space=pl.ANY),
            scratch_shapes=[pltpu.SemaphoreType.DMA((2,))],
        ),
        input_output_aliases={3: 0},
        compiler_params=_params(("arbitrary",)),
        name="moe_dispatch",
    )(dest1, dest2, h, zeros)


def _experts_kernel(te_ref, nt_ref, xs_ref, wg_ref, wu_ref, wd_ref, ye_ref):
    del te_ref
    t = pl.program_id(0)

    @pl.when(pl.program_id(1) == 0)
    def _():
        ye_ref[...] = jnp.zeros_like(ye_ref)

    @pl.when(t < nt_ref[0])
    def _():
        h = xs_ref[...].astype(BF16)
        g = jnp.dot(h, wg_ref[0], preferred_element_type=F32)
        u = jnp.dot(h, wu_ref[0], preferred_element_type=F32)
        a = (_silu(g) * u).astype(BF16)
        ye_ref[...] += jnp.dot(a, wd_ref[0], preferred_element_type=F32)


def _experts_call(tile_expert, n_tiles, xs, w_gu, w_down, tf=_FF_TILE):
    n_rows, d = xs.shape
    _, f, _ = w_down.shape
    nj = f // tf
    tm = _MOE_TM
    tile = lambda t, nt: jnp.maximum(jnp.minimum(t, nt[0] - 1), 0)
    jj = lambda t, j, nt: _snake(tile(t, nt), jnp.where(t < nt[0], j, nj - 1), nj)
    return pl.pallas_call(
        _experts_kernel,
        out_shape=jax.ShapeDtypeStruct((n_rows, d), F32),
        grid_spec=pltpu.PrefetchScalarGridSpec(
            num_scalar_prefetch=2,
            grid=(n_rows // tm, nj),
            in_specs=[
                pl.BlockSpec((tm, d), lambda t, j, te, nt: (tile(t, nt), 0)),
                pl.BlockSpec((1, d, tf), lambda t, j, te, nt: (te[tile(t, nt)], 0, jj(t, j, nt))),
                pl.BlockSpec((1, d, tf), lambda t, j, te, nt: (te[tile(t, nt)], 0, jj(t, j, nt) + nj)),
                pl.BlockSpec((1, tf, d), lambda t, j, te, nt: (te[tile(t, nt)], jj(t, j, nt), 0)),
            ],
            out_specs=pl.BlockSpec((tm, d), lambda t, j, te, nt: (t, 0)),
        ),
        compiler_params=_params(("arbitrary", "arbitrary")),
        name="moe_experts",
    )(tile_expert, n_tiles, xs, w_gu, w_gu, w_down)


def _combine_kernel(d1_ref, d2_ref, ye_ref, x_ref, route_ref, gt_ref, gf_ref, o_ref,
                    y1_sc, y2_sc, sem, *, tiles_per_batch):
    tm = x_ref.shape[0]
    i = pl.program_id(0)
    b = i // tiles_per_batch
    base = i * tm

    def issue(r, carry):
        pltpu.make_async_copy(ye_ref.at[pl.ds(d1_ref[base + r], 1)], y1_sc.at[pl.ds(r, 1)],
                              sem.at[0]).start()
        pltpu.make_async_copy(ye_ref.at[pl.ds(d2_ref[base + r], 1)], y2_sc.at[pl.ds(r, 1)],
                              sem.at[1]).start()
        return carry

    lax.fori_loop(0, tm, issue, 0, unroll=8)
    pltpu.make_async_copy(ye_ref.at[pl.ds(0, tm)], y1_sc, sem.at[0]).wait()
    pltpu.make_async_copy(ye_ref.at[pl.ds(0, tm)], y2_sc, sem.at[1]).wait()
    route = route_ref[...]
    y = route[:, 2:3] * y1_sc[...] + route[:, 3:4] * y2_sc[...]
    xn = x_ref[...] + gt_ref[pl.ds(b, 1), :] * y
    ms = jnp.mean(xn * xn, axis=-1, keepdims=True)
    o_ref[...] = xn * lax.rsqrt(ms + EPS) * gf_ref[...]


def _combine_call(dest1, dest2, ye, x, route, mod, g_final, seq, tm=512):
    n, d = x.shape
    return pl.pallas_call(
        functools.partial(_combine_kernel, tiles_per_batch=seq // tm),
        out_shape=jax.ShapeDtypeStruct((n, d), F32),
        grid_spec=pltpu.PrefetchScalarGridSpec(
            num_scalar_prefetch=2,
            grid=(n // tm,),
            in_specs=[
                pl.BlockSpec(memory_space=pl.ANY),
                pl.BlockSpec((tm, d), lambda i, d1, d2: (i, 0)),
                pl.BlockSpec((tm, LANES), lambda i, d1, d2: (i, 0)),
                pl.BlockSpec((SUBLANES, d), lambda i, d1, d2: (0, 5)),
                pl.BlockSpec((1, d), lambda i, d1, d2: (0, 0)),
            ],
            out_specs=pl.BlockSpec((tm, d), lambda i, d1, d2: (i, 0)),
            scratch_shapes=[pltpu.VMEM((tm, d), F32), pltpu.VMEM((tm, d), F32),
                            pltpu.SemaphoreType.DMA((2,))],
        ),
        compiler_params=_params(("arbitrary",)),
        name="moe_combine",
    )(dest1, dest2, ye, x, route, mod, g_final.reshape(1, d))


def _moe_call(h, route, counts, x, mod, g_final, w_gu, w_down, seq):
    n, d = x.shape
    ne = w_down.shape[0]
    tm = _MOE_TM
    max_tiles = (2 * n) // tm + ne
    e1 = route[:, 0].astype(jnp.int32)
    e2 = route[:, 1].astype(jnp.int32)
    cnt = counts[0, :ne].astype(jnp.int32)
    tiles_e = (cnt + tm - 1) // tm
    tile_end = jnp.cumsum(tiles_e)
    row_start = (tile_end - tiles_e) * tm
    dest1 = row_start[e1] + route[:, 4].astype(jnp.int32)
    dest2 = row_start[e2] + route[:, 5].astype(jnp.int32)
    n_tiles = tile_end[-1:]
    tile_expert = jnp.minimum(
        jnp.searchsorted(tile_end, jnp.arange(max_tiles, dtype=jnp.int32), side="right"),
        ne - 1).astype(jnp.int32)
    xs = _dispatch_call(dest1, dest2, h, max_tiles * tm)
    ye = _experts_call(tile_expert, n_tiles, xs, w_gu, w_down)
    return _combine_call(dest1, dest2, ye, x, route, mod, g_final, seq)


_SWA_TQ = 2 * CHUNK
_SWA_BAND = 2 * _SWA_TQ


def _swa_bucket_tiles():
    cc = np.arange(_SWA_BAND)[:, None]
    r = np.arange(_SWA_TQ)[None, :]
    rel = cc - _SWA_TQ - r
    nb = REL_BUCKETS // 2
    max_exact = nb // 2
    ret = (rel > 0).astype(np.int32) * nb
    n = np.abs(rel)
    large = max_exact + (np.log(np.maximum(n, 1) / max_exact)
                         / np.log(REL_MAX_DIST / max_exact) * (nb - max_exact)).astype(np.int32)
    large = np.minimum(large, nb - 1)
    bucket = (ret + np.where(n < max_exact, n, large)).astype(np.int32)
    q_chunk = r // CHUNK
    k_chunk = cc // CHUNK
    visible = (k_chunk >= q_chunk) & (k_chunk <= q_chunk + WINDOW_CHUNKS)
    later = np.where(visible, bucket, -1)
    first = np.where(cc >= _SWA_TQ, later, -1)
    return np.stack([first, later]).astype(np.int32)


def _swa_bias_kernel(tbl_ref, bkt_ref, o_ref):
    n_heads = o_ref.shape[1]
    for v in range(2):
        bkt = bkt_ref[v]
        for head in range(n_heads):
            tile = jnp.full(bkt.shape, NEG_BIG, F32)
            for bk in range(REL_BUCKETS):
                tile = jnp.where(bkt == bk, tbl_ref[head, bk] * LOG2E, tile)
            o_ref[v, head] = tile


def _swa_bias_call(rel_bias):
    bkt = jnp.asarray(_swa_bucket_tiles())
    n_heads = rel_bias.shape[1]
    return pl.pallas_call(
        _swa_bias_kernel,
        out_shape=jax.ShapeDtypeStruct((2, n_heads, _SWA_BAND, _SWA_TQ), F32),
        in_specs=[
            pl.BlockSpec(memory_space=pltpu.SMEM),
            pl.BlockSpec(memory_space=pltpu.VMEM),
        ],
        out_specs=pl.BlockSpec(memory_space=pltpu.VMEM),
        name="swa_bias",
    )(rel_bias.T, bkt)


def _swa_kernel(qt_ref, kp_ref, kc_ref, vtp_ref, vtc_ref, bias_ref, sink_ref, o_ref):
    tq = _SWA_TQ
    lane = lax.broadcasted_iota(jnp.int32, (1, LANES), 1)
    ones = jnp.ones((_ONES_ROWS, _SWA_BAND), BF16)
    outs = [None] * (SWA_KV_HEADS * SWA_GROUP)
    units = [(hk, par) for hk in range(SWA_KV_HEADS) for par in range(2)]

    def scores(hk, par):
        ksl = slice(hk * LANES, (hk + 1) * LANES)
        kb = jnp.concatenate([kp_ref[0, :, ksl], kc_ref[0, :, ksl]], axis=0)
        f0 = hk * SWA_GROUP * HEAD_DIM
        wq = jnp.concatenate([qt_ref[f0:f0 + LANES, :], qt_ref[f0 + LANES:f0 + 2 * LANES, :]],
                             axis=1)
        head_lanes = (lane < HEAD_DIM) if par == 0 else (lane >= HEAD_DIM)
        return jnp.dot(jnp.where(head_lanes, kb, 0), wq, preferred_element_type=F32)

    sts = [scores(hk, par) for hk, par in units]
    for (hk, par), st in zip(units, sts):
        vsl = slice(hk * HEAD_DIM, (hk + 1) * HEAD_DIM)
        vt1 = jnp.concatenate(
            [jnp.concatenate([vtp_ref[vsl, :], vtc_ref[vsl, :]], axis=1), ones], axis=0)
        heads = (hk * SWA_GROUP + par, hk * SWA_GROUP + par + 2)
        ps, ms = [], []
        for i, head in enumerate(heads):
            s = st[:, i * tq:(i + 1) * tq] + bias_ref[0, head]
            m = jnp.maximum(jnp.max(s, axis=0, keepdims=True), sink_ref[head])
            ps.append(jnp.exp2(s - m).astype(BF16))
            ms.append(m)
        acc = jnp.dot(vt1, jnp.concatenate(ps, axis=1), preferred_element_type=F32)
        for i, head in enumerate(heads):
            a = acc[:, i * tq:(i + 1) * tq]
            den = a[HEAD_DIM:HEAD_DIM + 1] + jnp.exp2(sink_ref[head] - ms[i])
            outs[head] = a[:HEAD_DIM] / den
    o_ref[0] = jnp.concatenate(outs, axis=0).T.astype(o_ref.dtype)


def _swa_call(q_t, k, v_t, bias, sink, batch, seq):
    d = q_t.shape[0]
    kw = k.shape[2]
    vw = v_t.shape[0]
    tq = _SWA_TQ
    nq = seq // tq
    return pl.pallas_call(
        _swa_kernel,
        out_shape=jax.ShapeDtypeStruct((batch, seq, d), BF16),
        grid=(batch, nq),
        in_specs=[
            pl.BlockSpec((d, tq), lambda b, i: (0, b * nq + i)),
            pl.BlockSpec((1, tq, kw), lambda b, i: (b, jnp.maximum(i - 1, 0), 0)),
            pl.BlockSpec((1, tq, kw), lambda b, i: (b, i, 0)),
            pl.BlockSpec((vw, tq), lambda b, i: (0, b * nq + jnp.maximum(i - 1, 0))),
            pl.BlockSpec((vw, tq), lambda b, i: (0, b * nq + i)),
            pl.BlockSpec((1,) + bias.shape[1:], lambda b, i: (jnp.minimum(i, 1), 0, 0, 0)),
            pl.BlockSpec(sink.shape, lambda b, i: (0, 0, 0)),
        ],
        out_specs=pl.BlockSpec((1, tq, d), lambda b, i: (b, i, 0)),
        compiler_params=_params(("parallel", "arbitrary")),
        name="swa_attention",
    )(q_t, k, k, v_t, v_t, bias, sink)


def kernel(x, c, w_ada, b_ada, g_norm_mix, g_norm_ffn, g_final, fox_w_in, fox_b_f, fox_w_out, swa_w_in, swa_sinks, swa_w_out, rel_bias, ffn_w_gu, ffn_w_down, moe_w_router, moe_b_router, moe_w_gu, moe_w_down):
    batch, seq, d = x.shape
    n = batch * seq
    q_scale = HEAD_DIM ** -0.5
    xf = x.reshape(n, d)

    c_pad = jnp.zeros((SUBLANES, d), F32).at[:batch].set(c)
    mod = _ada_call(c_pad, w_ada, b_ada)
    mod0, mod1 = mod[0], mod[1]

    w_in = fox_w_in[0]
    n_heads = d // HEAD_DIM
    k, q_t, v_t, f = _attn_proj_call(
        xf, g_norm_mix[0], mod0, w_in[:, d:2 * d].astype(BF16),
        (w_in[:, :d] * (q_scale * LOG2E)).T.astype(BF16), w_in[:, 2 * d:3 * d].T.astype(BF16),
        seq, w_f=w_in[:, 3 * d:])
    g_cum, r_cum = _cum_call(f, fox_b_f[0], batch, seq)
    g_pairs = g_cum.reshape(batch, seq, n_heads // 2, 2).transpose(0, 2, 1, 3)
    r_flat = r_cum.transpose(0, 2, 1).reshape(-1)
    o = _fox_call(r_flat, q_t, k.reshape(batch, seq, d), v_t, g_pairs)
    x1 = _out_call(o.reshape(n, d), fox_w_out[0].astype(BF16), xf, mod0, 2, seq)
    x2 = _ffn_call(x1, g_norm_ffn[0], mod0, ffn_w_gu[0].astype(BF16), ffn_w_down[0].astype(BF16), seq)

    w_in = swa_w_in[0]
    kvw = SWA_KV_HEADS * HEAD_DIM
    dup = lambda w: jnp.repeat(w.reshape(d, SWA_KV_HEADS, 1, HEAD_DIM), 2, axis=2).reshape(d, 2 * kvw)
    k, q_t, v_t = _attn_proj_call(
        x2, g_norm_mix[1], mod1, dup(w_in[:, d:d + kvw]).astype(BF16),
        (w_in[:, :d] * (q_scale * LOG2E)).T.astype(BF16), w_in[:, d + kvw:].T.astype(BF16), seq)
    bias = _swa_bias_call(rel_bias)
    sink = jnp.broadcast_to((swa_sinks[0] * LOG2E)[:, None, None], (n_heads, 1, _SWA_TQ))
    o = _swa_call(q_t, k.reshape(batch, seq, 2 * kvw), v_t, bias, sink, batch, seq)
    w_r = jnp.zeros((d, LANES), F32).at[:, :N_EXPERTS].set(moe_w_router[0])
    b_r = jnp.zeros((1, LANES), F32).at[0, :N_EXPERTS].set(moe_b_router[0])
    x3, h4, route, counts = _out_call(o.reshape(n, d), swa_w_out[0].astype(BF16), x2, mod1, 2, seq,
                                      router=(g_norm_ffn[1], 3, 4, w_r, b_r))
    out = _moe_call(h4, route, counts, x3, mod1, g_final, moe_w_gu[0].astype(BF16),
                    moe_w_down[0].astype(BF16), seq)
    return out.reshape(batch, seq, d)
```

```python
import functools

import numpy as np
import jax
import jax.numpy as jnp
from jax import lax
from jax.experimental import pallas as pl
from jax.experimental.pallas import tpu as pltpu

F32 = jnp.float32
BF16 = jnp.bfloat16
HIGHEST = lax.Precision.HIGHEST

HEAD_DIM = 64
CHUNK = 64
WINDOW_CHUNKS = 2
REL_BUCKETS = 32
REL_MAX_DIST = 128
SWA_KV_HEADS = 4
SWA_GROUP = 4
N_EXPERTS = 8
EPS = 1e-6

LANES = 128
SUBLANES = 8
VMEM_LIMIT = 56 * 1024 * 1024

NEG_BIG = -1e30


def _params(sem, vmem=VMEM_LIMIT):
    return pltpu.CompilerParams(dimension_semantics=sem, vmem_limit_bytes=vmem)


def _rms_mod(x, g, shift, scale):
    ms = jnp.mean(x * x, axis=-1, keepdims=True)
    y = x * lax.rsqrt(ms + EPS) * g
    return y * (1.0 + scale) + shift


def _silu(x):
    return x / (1.0 + jnp.exp(-x))


def _split_bf16(x):
    hi = x.astype(BF16)
    return hi, (x - hi.astype(F32)).astype(BF16)


def _dot_split(x_hi, x_lo, w_ref):
    return (jnp.dot(x_hi, w_ref[0], preferred_element_type=F32)
            + jnp.dot(x_lo, w_ref[0], preferred_element_type=F32)
            + jnp.dot(x_hi, w_ref[1], preferred_element_type=F32))


def _ada_kernel(c_ref, w_ref, b_ref, o_ref):
    cond = _silu(c_ref[...])
    o_ref[0] = jnp.dot(cond, w_ref[0], preferred_element_type=F32,
                       precision=HIGHEST) + b_ref[0]


def _ada_call(c_pad, w_ada, b_ada):
    depth, d, n = w_ada.shape
    tn = 1536
    return pl.pallas_call(
        _ada_kernel,
        out_shape=jax.ShapeDtypeStruct((depth, SUBLANES, n), F32),
        grid=(depth, n // tn),
        in_specs=[
            pl.BlockSpec((SUBLANES, d), lambda l, j: (0, 0)),
            pl.BlockSpec((1, d, tn), lambda l, j: (l, 0, j)),
            pl.BlockSpec((1, 1, tn), lambda l, j: (l, 0, j)),
        ],
        out_specs=pl.BlockSpec((1, SUBLANES, tn), lambda l, j: (l, 0, j)),
        compiler_params=_params(("parallel", "parallel")),
        name="ada_mod",
    )(c_pad, w_ada, b_ada.reshape(depth, 1, n))


_NT = (((1,), (1,)), ((), ()))


def _attn_proj_kernel(*refs, tiles_per_batch, with_gate):
    x_ref, g_ref, sh_ref, sc_ref, wk_ref, wqt_ref, wvt_ref = refs[:7]
    k_ref, qt_ref, vt_ref = refs[7 + with_gate:10 + with_gate]
    b = pl.program_id(0) // tiles_per_batch
    h = _rms_mod(x_ref[...], g_ref[...], sh_ref[pl.ds(b, 1), :], sc_ref[pl.ds(b, 1), :])
    hb = h.astype(BF16)
    k_ref[...] = jnp.dot(hb, wk_ref[...], preferred_element_type=F32).astype(BF16)
    qt_ref[...] = lax.dot_general(wqt_ref[...], hb, _NT, preferred_element_type=F32).astype(BF16)
    vt_ref[...] = lax.dot_general(wvt_ref[...], hb, _NT, preferred_element_type=F32).astype(BF16)
    if with_gate:
        refs[-1][...] = _dot_split(hb, (h - hb.astype(F32)).astype(BF16), refs[7])


def _attn_proj_call(x, g, mod, w_k, w_qt, w_vt, seq, w_f=None, tm=512):
    n, d = x.shape
    with_gate = w_f is not None
    full = lambda a: pl.BlockSpec(a.shape, lambda i: (0, 0))
    in_specs = [
        pl.BlockSpec((tm, d), lambda i: (i, 0)),
        pl.BlockSpec((1, d), lambda i: (0, 0)),
        pl.BlockSpec((SUBLANES, d), lambda i: (0, 0)),
        pl.BlockSpec((SUBLANES, d), lambda i: (0, 1)),
        full(w_k), full(w_qt), full(w_vt),
    ]
    args = [x, g.reshape(1, d), mod, mod, w_k, w_qt, w_vt]
    out_shape = [jax.ShapeDtypeStruct((n, w_k.shape[1]), BF16),
                 jax.ShapeDtypeStruct((w_qt.shape[0], n), BF16),
                 jax.ShapeDtypeStruct((w_vt.shape[0], n), BF16)]
    out_specs = [pl.BlockSpec((tm, w_k.shape[1]), lambda i: (i, 0)),
                 pl.BlockSpec((w_qt.shape[0], tm), lambda i: (0, i)),
                 pl.BlockSpec((w_vt.shape[0], tm), lambda i: (0, i))]
    if with_gate:
        in_specs.append(pl.BlockSpec(w_f.shape, lambda i: (0, 0, 0)))
        args.append(w_f)
        out_shape.append(jax.ShapeDtypeStruct((n, w_f.shape[2]), F32))
        out_specs.append(pl.BlockSpec((tm, w_f.shape[2]), lambda i: (i, 0)))
    return pl.pallas_call(
        functools.partial(_attn_proj_kernel, tiles_per_batch=seq // tm, with_gate=with_gate),
        out_shape=out_shape,
        grid=(n // tm,),
        in_specs=in_specs,
        out_specs=out_specs,
        compiler_params=_params(("parallel",)),
        name="attn_proj_gate" if with_gate else "attn_proj",
    )(*args)


_FOX_T = 512
LOG2E = 1.4426950408889634


def _cum_kernel(f_ref, bf_ref, g_ref, r_ref):
    x = f_ref[...] + bf_ref[...]
    logf = (jnp.minimum(x, 0.0) - jnp.log(1.0 + jnp.exp(-jnp.abs(x)))) * LOG2E
    seq, nh = logf.shape
    r = lax.broadcasted_iota(jnp.int32, (_FOX_T, _FOX_T), 0)
    c = lax.broadcasted_iota(jnp.int32, (_FOX_T, _FOX_T), 1)
    lower = (c <= r).astype(F32)
    carry = jnp.zeros((1, nh), F32)
    for ch in range(seq // _FOX_T):
        rows = slice(ch * _FOX_T, (ch + 1) * _FOX_T)
        cs = jnp.dot(lower, logf[rows, :], preferred_element_type=F32, precision=HIGHEST)
        g_ref[0, rows, :] = cs
        r_ref[0, ch:ch + 1, :] = carry
        carry = carry + cs[_FOX_T - 1:_FOX_T, :]


def _cum_call(f, b_f, batch, seq):
    nh = f.shape[1]
    return pl.pallas_call(
        _cum_kernel,
        out_shape=[jax.ShapeDtypeStruct((batch, seq, nh), F32),
                   jax.ShapeDtypeStruct((batch, seq // _FOX_T, nh), F32)],
        grid=(batch,),
        in_specs=[
            pl.BlockSpec((seq, nh), lambda b: (b, 0)),
            pl.BlockSpec((1, nh), lambda b: (0, 0)),
        ],
        out_specs=[pl.BlockSpec((1, seq, nh), lambda b: (b, 0, 0)),
                   pl.BlockSpec((1, seq // _FOX_T, nh), lambda b: (b, 0, 0))],
        compiler_params=_params(("parallel",)),
        name="forget_cumsum",
    )(f, b_f.reshape(1, nh))


_ONES_ROWS = 16


def _fox_kernel(r_ref, qt_ref, k_ref, vt_ref, g_ref, o_ref, ka_sc, m_sc, acc_sc, sa_sc, sb_sc):
    t = _FOX_T
    b, hp, qi = pl.program_id(0), pl.program_id(1), pl.program_id(2)
    seq = k_ref.shape[1]
    n_blocks = seq // t
    n_heads = 2 * pl.num_programs(1)

    lane = lax.broadcasted_iota(jnp.int32, (1, LANES), 1)
    feat = lax.broadcasted_iota(jnp.int32, (LANES, 1), 0)
    aug0 = [HEAD_DIM, 0]

    @pl.when(qi == 0)
    def _():
        for hh in range(2):
            gb = jnp.broadcast_to(g_ref[0, 0, :, hh:hh + 1], (seq, LANES))
            hi = gb.astype(BF16).astype(F32)
            mid = (gb - hi).astype(BF16).astype(F32)
            lo = gb - hi - mid
            aug = jnp.where(lane == aug0[hh], hi,
                            jnp.where(lane == aug0[hh] + 1, mid,
                                      jnp.where(lane == aug0[hh] + 2, lo, 0.0)))
            own = (lane >= hh * HEAD_DIM) & (lane < (hh + 1) * HEAD_DIM)
            ka_sc[hh] = jnp.where(own, k_ref[0].astype(F32), aug).astype(BF16)

    qt2 = qt_ref[...]
    key = lax.broadcasted_iota(jnp.int32, (t, t), 0)
    qry = lax.broadcasted_iota(jnp.int32, (t, t), 1)
    ones = jnp.ones((_ONES_ROWS, t), BF16)
    feat_t = lax.broadcasted_iota(jnp.int32, (LANES, t), 0)
    qh = [jnp.where((feat >= hh * HEAD_DIM) & (feat < (hh + 1) * HEAD_DIM), qt2, 0)
          + jnp.where((feat_t >= aug0[hh]) & (feat_t < aug0[hh] + 3), -1.0, 0.0).astype(BF16)
          for hh in range(2)]
    r_base = [(b * n_heads + 2 * hp + hh) * n_blocks for hh in range(2)]
    r_q = [r_ref[r_base[hh] + qi] for hh in range(2)]
    m_sc[...] = jnp.full_like(m_sc, -jnp.inf)
    acc_sc[...] = jnp.zeros_like(acc_sc)

    def scores(kb, dst):
        k0 = pl.multiple_of(kb * t, t)
        for hh in range(2):
            dst[hh] = jnp.dot(ka_sc[hh, pl.ds(k0, t), :], qh[hh],
                              preferred_element_type=F32)

    def step(kb, cur, nxt, causal):
        if nxt is not None:
            scores(kb + 1, nxt)
        k0 = pl.multiple_of(kb * t, t)
        for hh in range(2):
            st = cur[hh]
            if causal:
                st = jnp.where(key <= qry, st, -jnp.inf)
            c = r_ref[r_base[hh] + kb] - r_q[hh]
            m_old = m_sc[hh]
            m_new = jnp.maximum(m_old, jnp.max(st, axis=0, keepdims=True) - c)
            alpha = jnp.exp2(m_old - m_new)
            p = jnp.exp2(st - (m_new + c)).astype(BF16)
            vt1 = jnp.concatenate(
                [vt_ref[hh * HEAD_DIM:(hh + 1) * HEAD_DIM, pl.ds(k0, t)], ones], axis=0)
            acc_sc[hh] = alpha * acc_sc[hh] + jnp.dot(vt1, p, preferred_element_type=F32)
            m_sc[hh] = m_new

    scores(0, sa_sc)

    def pair(i, carry):
        step(2 * i, sa_sc, sb_sc, False)
        step(2 * i + 1, sb_sc, sa_sc, False)
        return carry

    lax.fori_loop(0, qi // 2, pair, 0)

    @pl.when(qi % 2 == 0)
    def _():
        step(qi, sa_sc, None, True)

    @pl.when(qi % 2 == 1)
    def _():
        step(qi - 1, sa_sc, sb_sc, False)
        step(qi, sb_sc, None, True)

    outs = [acc_sc[hh, :HEAD_DIM, :] / acc_sc[hh, HEAD_DIM:HEAD_DIM + 1, :] for hh in range(2)]
    o_ref[0] = jnp.concatenate(outs, axis=0).T.astype(o_ref.dtype)


def _fox_call(r_flat, q_t, k, v_t, g_pairs):
    batch, seq, d = k.shape
    n_pairs = d // LANES
    t = _FOX_T
    nq = seq // t
    return pl.pallas_call(
        _fox_kernel,
        out_shape=jax.ShapeDtypeStruct((batch, seq, d), BF16),
        grid=(batch, n_pairs, nq),
        in_specs=[
            pl.BlockSpec(memory_space=pltpu.SMEM),
            pl.BlockSpec((LANES, t), lambda b, h, i: (h, b * nq + i)),
            pl.BlockSpec((1, seq, LANES), lambda b, h, i: (b, 0, h)),
            pl.BlockSpec((LANES, seq), lambda b, h, i: (h, b)),
            pl.BlockSpec((1, 1, seq, 2), lambda b, h, i: (b, h, 0, 0)),
        ],
        out_specs=pl.BlockSpec((1, t, LANES), lambda b, h, i: (b, i, h)),
        scratch_shapes=[pltpu.VMEM((2, seq, LANES), BF16), pltpu.VMEM((2, 1, t), F32),
                        pltpu.VMEM((2, HEAD_DIM + _ONES_ROWS, t), F32),
                        pltpu.VMEM((2, t, t), F32), pltpu.VMEM((2, t, t), F32)],
        compiler_params=_params(("parallel", "parallel", "arbitrary")),
        name="fox_attention",
    )(r_flat, q_t, k, v_t, g_pairs)


def _out_kernel(*refs, tiles_per_batch, with_router):
    o_ref, w_ref, x_ref, gt_ref = refs[:4]
    b = pl.program_id(0) // tiles_per_batch
    y = jnp.dot(o_ref[...], w_ref[...], preferred_element_type=F32)
    xn = x_ref[...] + gt_ref[pl.ds(b, 1), :] * y
    if not with_router:
        refs[4][...] = xn
        return
    g_ref, sh_ref, sc_ref, wr_ref, br_ref, xo_ref, h_ref, route_ref, cnt_ref = refs[4:]
    xo_ref[...] = xn
    h = _rms_mod(xn, g_ref[...], sh_ref[pl.ds(b, 1), :], sc_ref[pl.ds(b, 1), :])
    h_ref[...] = h
    logits = _dot_split(*_split_bf16(h), wr_ref) + br_ref[...]
    tm = logits.shape[0]
    lane = lax.broadcasted_iota(jnp.int32, logits.shape, 1).astype(F32)
    logits = jnp.where(lane < N_EXPERTS, logits, -jnp.inf)
    m1 = jnp.max(logits, axis=1, keepdims=True)
    i1 = jnp.min(jnp.where(logits == m1, lane, float(LANES)), axis=1, keepdims=True)
    rest = jnp.where(lane == i1, -jnp.inf, logits)
    m2 = jnp.max(rest, axis=1, keepdims=True)
    i2 = jnp.min(jnp.where(rest == m2, lane, float(LANES)), axis=1, keepdims=True)
    e2 = jnp.exp(m2 - m1)
    den = 1.0 + e2

    @pl.when(pl.program_id(0) == 0)
    def _():
        cnt_ref[...] = jnp.zeros_like(cnt_ref)

    sel1 = lane == i1
    sel2 = lane == i2
    onehot = jnp.where(sel1 | sel2, 1.0, 0.0)
    r = lax.broadcasted_iota(jnp.int32, (tm, tm), 0)
    c = lax.broadcasted_iota(jnp.int32, (tm, tm), 1)
    lower = jnp.where(c < r, 1.0, 0.0).astype(BF16)
    before = jnp.dot(lower, onehot.astype(BF16), preferred_element_type=F32) + cnt_ref[0:1, :]
    rank1 = jnp.sum(jnp.where(sel1, before, 0.0), axis=1, keepdims=True)
    rank2 = jnp.sum(jnp.where(sel2, before, 0.0), axis=1, keepdims=True)
    cnt_ref[0:1, :] = cnt_ref[0:1, :] + jnp.sum(onehot, axis=0, keepdims=True)
    route = jnp.where(lane == 0.0, i1, 0.0)
    for k, val in enumerate((i2, 1.0 / den, e2 / den, rank1, rank2), start=1):
        route = jnp.where(lane == float(k), val, route)
    route_ref[...] = route


def _out_call(o, w_out, x, mod, gate_col, seq, router=None, tm=512):
    n, d = x.shape
    with_router = router is not None
    in_specs = [
        pl.BlockSpec((tm, d), lambda i: (i, 0)),
        pl.BlockSpec((d, d), lambda i: (0, 0)),
        pl.BlockSpec((tm, d), lambda i: (i, 0)),
        pl.BlockSpec((SUBLANES, d), lambda i: (0, gate_col)),
    ]
    args = [o, w_out, x, mod]
    out_shape = [jax.ShapeDtypeStruct((n, d), F32)]
    out_specs = [pl.BlockSpec((tm, d), lambda i: (i, 0))]
    if with_router:
        g, sh_col, sc_col, w_r, b_r = router
        in_specs += [
            pl.BlockSpec((1, d), lambda i: (0, 0)),
            pl.BlockSpec((SUBLANES, d), lambda i: (0, sh_col)),
            pl.BlockSpec((SUBLANES, d), lambda i: (0, sc_col)),
            pl.BlockSpec((2, d, LANES), lambda i: (0, 0, 0)),
            pl.BlockSpec((1, LANES), lambda i: (0, 0)),
        ]
        args += [g.reshape(1, d), mod, mod, w_r, b_r]
        out_shape += [jax.ShapeDtypeStruct((n, d), F32),
                      jax.ShapeDtypeStruct((n, LANES), F32),
                      jax.ShapeDtypeStruct((SUBLANES, LANES), F32)]
        out_specs += [pl.BlockSpec((tm, d), lambda i: (i, 0)),
                      pl.BlockSpec((tm, LANES), lambda i: (i, 0)),
                      pl.BlockSpec((SUBLANES, LANES), lambda i: (0, 0))]
    res = pl.pallas_call(
        functools.partial(_out_kernel, tiles_per_batch=seq // tm, with_router=with_router),
        out_shape=out_shape,
        grid=(n // tm,),
        in_specs=in_specs,
        out_specs=out_specs,
        compiler_params=_params(("arbitrary",) if with_router else ("parallel",)),
        name="out_proj_router" if with_router else "out_proj",
    )(*args)
    return res if with_router else res[0]


_FF_TILE = 1792


def _snake(i, j, nj):
    return jnp.where(i % 2 == 0, j, nj - 1 - j)


def _ffn_kernel(x_ref, g_ref, sh_ref, sc_ref, gt_ref, wg_ref, wu_ref, wd_ref,
                o_ref, h_sc, *, tiles_per_batch):
    b = pl.program_id(0) // tiles_per_batch

    @pl.when(pl.program_id(1) == 0)
    def _():
        x = x_ref[...]
        h = _rms_mod(x, g_ref[...], sh_ref[pl.ds(b, 1), :], sc_ref[pl.ds(b, 1), :])
        h_sc[...] = h.astype(BF16)
        o_ref[...] = x

    h = h_sc[...]
    g = jnp.dot(h, wg_ref[...], preferred_element_type=F32)
    u = jnp.dot(h, wu_ref[...], preferred_element_type=F32)
    a = (_silu(g) * u).astype(BF16)
    o_ref[...] += gt_ref[pl.ds(b, 1), :] * jnp.dot(a, wd_ref[...], preferred_element_type=F32)


def _ffn_call(x, g, mod, w_gu, w_down, seq, tm=512, tf=_FF_TILE):
    n, d = x.shape
    f = w_down.shape[0]
    nj = f // tf
    return pl.pallas_call(
        functools.partial(_ffn_kernel, tiles_per_batch=seq // tm),
        out_shape=jax.ShapeDtypeStruct((n, d), F32),
        grid=(n // tm, nj),
        in_specs=[
            pl.BlockSpec((tm, d), lambda i, j: (i, 0)),
            pl.BlockSpec((1, d), lambda i, j: (0, 0)),
            pl.BlockSpec((SUBLANES, d), lambda i, j: (0, 3)),
            pl.BlockSpec((SUBLANES, d), lambda i, j: (0, 4)),
            pl.BlockSpec((SUBLANES, d), lambda i, j: (0, 5)),
            pl.BlockSpec((d, tf), lambda i, j: (0, _snake(i, j, nj))),
            pl.BlockSpec((d, tf), lambda i, j: (0, _snake(i, j, nj) + nj)),
            pl.BlockSpec((tf, d), lambda i, j: (_snake(i, j, nj), 0)),
        ],
        out_specs=pl.BlockSpec((tm, d), lambda i, j: (i, 0)),
        scratch_shapes=[pltpu.VMEM((tm, d), BF16)],
        compiler_params=_params(("parallel", "arbitrary")),
        name="ffn_swiglu",
    )(x, g.reshape(1, d), mod, mod, mod, w_gu, w_gu, w_down)


_MOE_TM = 512


def _dispatch_kernel(d1_ref, d2_ref, h_ref, xs_in_ref, xs_ref, sem):
    del xs_in_ref
    tm = h_ref.shape[0]
    base = pl.program_id(0) * tm

    def issue(r, carry):
        src = h_ref.at[pl.ds(r, 1)]
        pltpu.make_async_copy(src, xs_ref.at[pl.ds(d1_ref[base + r], 1)], sem.at[0]).start()
        pltpu.make_async_copy(src, xs_ref.at[pl.ds(d2_ref[base + r], 1)], sem.at[1]).start()
        return carry

    lax.fori_loop(0, tm, issue, 0, unroll=8)
    pltpu.make_async_copy(h_ref, xs_ref.at[pl.ds(0, tm)], sem.at[0]).wait()
    pltpu.make_async_copy(h_ref, xs_ref.at[pl.ds(0, tm)], sem.at[1]).wait()


def _dispatch_call(dest1, dest2, h, n_rows, tm=512):
    n, d = h.shape
    zeros = jnp.zeros((n_rows, d), h.dtype)
    return pl.pallas_call(
        _dispatch_kernel,
        out_shape=jax.ShapeDtypeStruct((n_rows, d), h.dtype),
        grid_spec=pltpu.PrefetchScalarGridSpec(
            num_scalar_prefetch=2,
            grid=(n // tm,),
            in_specs=[
                pl.BlockSpec((tm, d), lambda i, d1, d2: (i, 0)),
                pl.BlockSpec(memory_space=pl.ANY),
            ],
            out_specs=pl.BlockSpec(memory_space=pl.ANY),
            scratch_shapes=[pltpu.SemaphoreType.DMA((2,))],
        ),
        input_output_aliases={3: 0},
        compiler_params=_params(("arbitrary",)),
        name="moe_dispatch",
    )(dest1, dest2, h, zeros)


def _experts_kernel(te_ref, nt_ref, xs_ref, wg_ref, wu_ref, wd_ref, ye_ref):
    del te_ref
    t = pl.program_id(0)

    @pl.when(pl.program_id(1) == 0)
    def _():
        ye_ref[...] = jnp.zeros_like(ye_ref)

    @pl.when(t < nt_ref[0])
    def _():
        h = xs_ref[...].astype(BF16)
        g = jnp.dot(h, wg_ref[0], preferred_element_type=F32)
        u = jnp.dot(h, wu_ref[0], preferred_element_type=F32)
        a = (_silu(g) * u).astype(BF16)
        ye_ref[...] += jnp.dot(a, wd_ref[0], preferred_element_type=F32)


def _experts_call(tile_expert, n_tiles, xs, w_gu, w_down, tf=_FF_TILE):
    n_rows, d = xs.shape
    _, f, _ = w_down.shape
    nj = f // tf
    tm = _MOE_TM
    tile = lambda t, nt: jnp.maximum(jnp.minimum(t, nt[0] - 1), 0)
    jj = lambda t, j, nt: _snake(tile(t, nt), jnp.where(t < nt[0], j, nj - 1), nj)
    return pl.pallas_call(
        _experts_kernel,
        out_shape=jax.ShapeDtypeStruct((n_rows, d), F32),
        grid_spec=pltpu.PrefetchScalarGridSpec(
            num_scalar_prefetch=2,
            grid=(n_rows // tm, nj),
            in_specs=[
                pl.BlockSpec((tm, d), lambda t, j, te, nt: (tile(t, nt), 0)),
                pl.BlockSpec((1, d, tf), lambda t, j, te, nt: (te[tile(t, nt)], 0, jj(t, j, nt))),
                pl.BlockSpec((1, d, tf), lambda t, j, te, nt: (te[tile(t, nt)], 0, jj(t, j, nt) + nj)),
                pl.BlockSpec((1, tf, d), lambda t, j, te, nt: (te[tile(t, nt)], jj(t, j, nt), 0)),
            ],
            out_specs=pl.BlockSpec((tm, d), lambda t, j, te, nt: (t, 0)),
        ),
        compiler_params=_params(("arbitrary", "arbitrary")),
        name="moe_experts",
    )(tile_expert, n_tiles, xs, w_gu, w_gu, w_down)


def _combine_kernel(d1_ref, d2_ref, ye_ref, x_ref, route_ref, gt_ref, gf_ref, o_ref,
                    y1_sc, y2_sc, sem, *, tiles_per_batch):
    tm = x_ref.shape[0]
    i = pl.program_id(0)
    b = i // tiles_per_batch
    base = i * tm

    def issue(r, carry):
        pltpu.make_async_copy(ye_ref.at[pl.ds(d1_ref[base + r], 1)], y1_sc.at[pl.ds(r, 1)],
                              sem.at[0]).start()
        pltpu.make_async_copy(ye_ref.at[pl.ds(d2_ref[base + r], 1)], y2_sc.at[pl.ds(r, 1)],
                              sem.at[1]).start()
        return carry

    lax.fori_loop(0, tm, issue, 0, unroll=8)
    pltpu.make_async_copy(ye_ref.at[pl.ds(0, tm)], y1_sc, sem.at[0]).wait()
    pltpu.make_async_copy(ye_ref.at[pl.ds(0, tm)], y2_sc, sem.at[1]).wait()
    route = route_ref[...]
    y = route[:, 2:3] * y1_sc[...] + route[:, 3:4] * y2_sc[...]
    xn = x_ref[...] + gt_ref[pl.ds(b, 1), :] * y
    ms = jnp.mean(xn * xn, axis=-1, keepdims=True)
    o_ref[...] = xn * lax.rsqrt(ms + EPS) * gf_ref[...]


def _combine_call(dest1, dest2, ye, x, route, mod, g_final, seq, tm=512):
    n, d = x.shape
    return pl.pallas_call(
        functools.partial(_combine_kernel, tiles_per_batch=seq // tm),
        out_shape=jax.ShapeDtypeStruct((n, d), F32),
        grid_spec=pltpu.PrefetchScalarGridSpec(
            num_scalar_prefetch=2,
            grid=(n // tm,),
            in_specs=[
                pl.BlockSpec(memory_space=pl.ANY),
                pl.BlockSpec((tm, d), lambda i, d1, d2: (i, 0)),
                pl.BlockSpec((tm, LANES), lambda i, d1, d2: (i, 0)),
                pl.BlockSpec((SUBLANES, d), lambda i, d1, d2: (0, 5)),
                pl.BlockSpec((1, d), lambda i, d1, d2: (0, 0)),
            ],
            out_specs=pl.BlockSpec((tm, d), lambda i, d1, d2: (i, 0)),
            scratch_shapes=[pltpu.VMEM((tm, d), F32), pltpu.VMEM((tm, d), F32),
                            pltpu.SemaphoreType.DMA((2,))],
        ),
        compiler_params=_params(("arbitrary",)),
        name="moe_combine",
    )(dest1, dest2, ye, x, route, mod, g_final.reshape(1, d))


def _moe_call(h, route, counts, x, mod, g_final, w_gu, w_down, seq):
    n, d = x.shape
    ne = w_down.shape[0]
    tm = _MOE_TM
    max_tiles = (2 * n) // tm + ne
    e1 = route[:, 0].astype(jnp.int32)
    e2 = route[:, 1].astype(jnp.int32)
    cnt = counts[0, :ne].astype(jnp.int32)
    tiles_e = (cnt + tm - 1) // tm
    tile_end = jnp.cumsum(tiles_e)
    row_start = (tile_end - tiles_e) * tm
    dest1 = row_start[e1] + route[:, 4].astype(jnp.int32)
    dest2 = row_start[e2] + route[:, 5].astype(jnp.int32)
    n_tiles = tile_end[-1:]
    tile_expert = jnp.minimum(
        jnp.searchsorted(tile_end, jnp.arange(max_tiles, dtype=jnp.int32), side="right"),
        ne - 1).astype(jnp.int32)
    xs = _dispatch_call(dest1, dest2, h, max_tiles * tm)
    ye = _experts_call(tile_expert, n_tiles, xs, w_gu, w_down)
    return _combine_call(dest1, dest2, ye, x, route, mod, g_final, seq)


_SWA_TQ = 2 * CHUNK
_SWA_BAND = 2 * _SWA_TQ


def _swa_bucket_tiles():
    cc = np.arange(_SWA_BAND)[:, None]
    r = np.arange(_SWA_TQ)[None, :]
    rel = cc - _SWA_TQ - r
    nb = REL_BUCKETS // 2
    max_exact = nb // 2
    ret = (rel > 0).astype(np.int32) * nb
    n = np.abs(rel)
    large = max_exact + (np.log(np.maximum(n, 1) / max_exact)
                         / np.log(REL_MAX_DIST / max_exact) * (nb - max_exact)).astype(np.int32)
    large = np.minimum(large, nb - 1)
    bucket = (ret + np.where(n < max_exact, n, large)).astype(np.int32)
    q_chunk = r // CHUNK
    k_chunk = cc // CHUNK
    visible = (k_chunk >= q_chunk) & (k_chunk <= q_chunk + WINDOW_CHUNKS)
    later = np.where(visible, bucket, -1)
    first = np.where(cc >= _SWA_TQ, later, -1)
    return np.stack([first, later]).astype(np.int32)


def _swa_bias_kernel(tbl_ref, bkt_ref, o_ref):
    n_heads = o_ref.shape[1]
    for v in range(2):
        bkt = bkt_ref[v]
        for head in range(n_heads):
            tile = jnp.full(bkt.shape, NEG_BIG, F32)
            for bk in range(REL_BUCKETS):
                tile = jnp.where(bkt == bk, tbl_ref[head, bk] * LOG2E, tile)
            o_ref[v, head] = tile


def _swa_bias_call(rel_bias):
    bkt = jnp.asarray(_swa_bucket_tiles())
    n_heads = rel_bias.shape[1]
    return pl.pallas_call(
        _swa_bias_kernel,
        out_shape=jax.ShapeDtypeStruct((2, n_heads, _SWA_BAND, _SWA_TQ), F32),
        in_specs=[
            pl.BlockSpec(memory_space=pltpu.SMEM),
            pl.BlockSpec(memory_space=pltpu.VMEM),
        ],
        out_specs=pl.BlockSpec(memory_space=pltpu.VMEM),
        name="swa_bias",
    )(rel_bias.T, bkt)


def _swa_kernel(qt_ref, kp_ref, kc_ref, vtp_ref, vtc_ref, bias_ref, sink_ref, o_ref):
    tq = _SWA_TQ
    lane = lax.broadcasted_iota(jnp.int32, (1, LANES), 1)
    ones = jnp.ones((_ONES_ROWS, _SWA_BAND), BF16)
    outs = [None] * (SWA_KV_HEADS * SWA_GROUP)
    units = [(hk, par) for hk in range(SWA_KV_HEADS) for par in range(2)]

    def scores(hk, par):
        ksl = slice(hk * LANES, (hk + 1) * LANES)
        kb = jnp.concatenate([kp_ref[0, :, ksl], kc_ref[0, :, ksl]], axis=0)
        f0 = hk * SWA_GROUP * HEAD_DIM
        wq = jnp.concatenate([qt_ref[f0:f0 + LANES, :], qt_ref[f0 + LANES:f0 + 2 * LANES, :]],
                             axis=1)
        head_lanes = (lane < HEAD_DIM) if par == 0 else (lane >= HEAD_DIM)
        return jnp.dot(jnp.where(head_lanes, kb, 0), wq, preferred_element_type=F32)

    sts = [scores(hk, par) for hk, par in units]
    for (hk, par), st in zip(units, sts):
        vsl = slice(hk * HEAD_DIM, (hk + 1) * HEAD_DIM)
        vt1 = jnp.concatenate(
            [jnp.concatenate([vtp_ref[vsl, :], vtc_ref[vsl, :]], axis=1), ones], axis=0)
        heads = (hk * SWA_GROUP + par, hk * SWA_GROUP + par + 2)
        ps, ms = [], []
        for i, head in enumerate(heads):
            s = st[:, i * tq:(i + 1) * tq] + bias_ref[0, head]
            m = jnp.maximum(jnp.max(s, axis=0, keepdims=True), sink_ref[head])
            ps.append(jnp.exp2(s - m).astype(BF16))
            ms.append(m)
        acc = jnp.dot(vt1, jnp.concatenate(ps, axis=1), preferred_element_type=F32)
        for i, head in enumerate(heads):
            a = acc[:, i * tq:(i + 1) * tq]
            den = a[HEAD_DIM:HEAD_DIM + 1] + jnp.exp2(sink_ref[head] - ms[i])
            outs[head] = a[:HEAD_DIM] / den
    o_ref[0] = jnp.concatenate(outs, axis=0).T.astype(o_ref.dtype)


def _swa_call(q_t, k, v_t, bias, sink, batch, seq):
    d = q_t.shape[0]
    kw = k.shape[2]
    vw = v_t.shape[0]
    tq = _SWA_TQ
    nq = seq // tq
    return pl.pallas_call(
        _swa_kernel,
        out_shape=jax.ShapeDtypeStruct((batch, seq, d), BF16),
        grid=(batch, nq),
        in_specs=[
            pl.BlockSpec((d, tq), lambda b, i: (0, b * nq + i)),
            pl.BlockSpec((1, tq, kw), lambda b, i: (b, jnp.maximum(i - 1, 0), 0)),
            pl.BlockSpec((1, tq, kw), lambda b, i: (b, i, 0)),
            pl.BlockSpec((vw, tq), lambda b, i: (0, b * nq + jnp.maximum(i - 1, 0))),
            pl.BlockSpec((vw, tq), lambda b, i: (0, b * nq + i)),
            pl.BlockSpec((1,) + bias.shape[1:], lambda b, i: (jnp.minimum(i, 1), 0, 0, 0)),
            pl.BlockSpec(sink.shape, lambda b, i: (0, 0, 0)),
        ],
        out_specs=pl.BlockSpec((1, tq, d), lambda b, i: (b, i, 0)),
        compiler_params=_params(("parallel", "arbitrary")),
        name="swa_attention",
    )(q_t, k, k, v_t, v_t, bias, sink)


def kernel(x, c, w_ada, b_ada, g_norm_mix, g_norm_ffn, g_final, fox_w_in, fox_b_f, fox_w_out, swa_w_in, swa_sinks, swa_w_out, rel_bias, ffn_w_gu, ffn_w_down, moe_w_router, moe_b_router, moe_w_gu, moe_w_down):
    batch, seq, d = x.shape
    n = batch * seq
    q_scale = HEAD_DIM ** -0.5
    xf = x.reshape(n, d)

    c_pad = jnp.zeros((SUBLANES, d), F32).at[:batch].set(c)
    mod = _ada_call(c_pad, w_ada, b_ada)
    mod0, mod1 = mod[0], mod[1]

    w_in = fox_w_in[0]
    n_heads = d // HEAD_DIM
    k, q_t, v_t, f = _attn_proj_call(
        xf, g_norm_mix[0], mod0, w_in[:, d:2 * d].astype(BF16),
        (w_in[:, :d] * (q_scale * LOG2E)).T.astype(BF16), w_in[:, 2 * d:3 * d].T.astype(BF16),
        seq, w_f=jnp.stack(_split_bf16(w_in[:, 3 * d:])))
    g_cum, r_cum = _cum_call(f, fox_b_f[0], batch, seq)
    g_pairs = g_cum.reshape(batch, seq, n_heads // 2, 2).transpose(0, 2, 1, 3)
    r_flat = r_cum.transpose(0, 2, 1).reshape(-1)
    o = _fox_call(r_flat, q_t, k.reshape(batch, seq, d), v_t, g_pairs)
    x1 = _out_call(o.reshape(n, d), fox_w_out[0].astype(BF16), xf, mod0, 2, seq)
    x2 = _ffn_call(x1, g_norm_ffn[0], mod0, ffn_w_gu[0].astype(BF16), ffn_w_down[0].astype(BF16), seq)

    w_in = swa_w_in[0]
    kvw = SWA_KV_HEADS * HEAD_DIM
    dup = lambda w: jnp.repeat(w.reshape(d, SWA_KV_HEADS, 1, HEAD_DIM), 2, axis=2).reshape(d, 2 * kvw)
    k, q_t, v_t = _attn_proj_call(
        x2, g_norm_mix[1], mod1, dup(w_in[:, d:d + kvw]).astype(BF16),
        (w_in[:, :d] * (q_scale * LOG2E)).T.astype(BF16), w_in[:, d + kvw:].T.astype(BF16), seq)
    bias = _swa_bias_call(rel_bias)
    sink = jnp.broadcast_to((swa_sinks[0] * LOG2E)[:, None, None], (n_heads, 1, _SWA_TQ))
    o = _swa_call(q_t, k.reshape(batch, seq, 2 * kvw), v_t, bias, sink, batch, seq)
    w_r = jnp.stack(_split_bf16(jnp.zeros((d, LANES), F32).at[:, :N_EXPERTS].set(moe_w_router[0])))
    b_r = jnp.zeros((1, LANES), F32).at[0, :N_EXPERTS].set(moe_b_router[0])
    x3, h4, route, counts = _out_call(o.reshape(n, d), swa_w_out[0].astype(BF16), x2, mod1, 2, seq,
                                      router=(g_norm_ffn[1], 3, 4, w_r, b_r))
    out = _moe_call(h4, route, counts, x3, mod1, g_final, moe_w_gu[0].astype(BF16),
                    moe_w_down[0].astype(BF16), seq)
    return out.reshape(batch, seq, d)
```

```python
import functools

import numpy as np
import jax
import jax.numpy as jnp
from jax import lax
from jax.experimental import pallas as pl
from jax.experimental.pallas import tpu as pltpu

F32 = jnp.float32
BF16 = jnp.bfloat16
HIGHEST = lax.Precision.HIGHEST

HEAD_DIM = 64
CHUNK = 64
WINDOW_CHUNKS = 2
REL_BUCKETS = 32
REL_MAX_DIST = 128
SWA_KV_HEADS = 4
SWA_GROUP = 4
N_EXPERTS = 8
EPS = 1e-6

LANES = 128
SUBLANES = 8
VMEM_LIMIT = 56 * 1024 * 1024

NEG_BIG = -1e30


def _params(sem, vmem=VMEM_LIMIT):
    return pltpu.CompilerParams(dimension_semantics=sem, vmem_limit_bytes=vmem)


def _rms_mod(x, g, shift, scale):
    ms = jnp.mean(x * x, axis=-1, keepdims=True)
    y = x * lax.rsqrt(ms + EPS) * g
    return y * (1.0 + scale) + shift


def _silu(x):
    return x / (1.0 + jnp.exp(-x))


def _split_bf16(x):
    hi = x.astype(BF16)
    return hi, (x - hi.astype(F32)).astype(BF16)


def _dot_split(x_hi, x_lo, w_ref):
    return (jnp.dot(x_hi, w_ref[0], preferred_element_type=F32)
            + jnp.dot(x_lo, w_ref[0], preferred_element_type=F32)
            + jnp.dot(x_hi, w_ref[1], preferred_element_type=F32))


_BF16_SUBLANES = 16


def _ride_along(arrays, n_steps, step_index):
    views, in_specs, out_specs, out_shapes = [], [], [], []
    for a in arrays:
        v = a.reshape(-1, a.shape[-1])
        rows, rem = divmod(v.shape[0], n_steps)
        assert rem == 0 and rows % _BF16_SUBLANES == 0, v.shape
        spec = pl.BlockSpec((rows, v.shape[1]), lambda *g: (step_index(*g), 0))
        views.append(v)
        in_specs.append(spec)
        out_specs.append(spec)
        out_shapes.append(jax.ShapeDtypeStruct(v.shape, BF16))
    return views, in_specs, out_specs, out_shapes


def _ride_along_cast(in_refs, out_refs):
    for src, dst in zip(in_refs, out_refs):
        dst[...] = src[...].astype(BF16)


def _ada_kernel(c_ref, w_ref, b_ref, o_ref):
    cond = _silu(c_ref[...])
    o_ref[0] = jnp.dot(cond, w_ref[0], preferred_element_type=F32,
                       precision=HIGHEST) + b_ref[0]


def _ada_call(c_pad, w_ada, b_ada):
    depth, d, n = w_ada.shape
    tn = 1536
    return pl.pallas_call(
        _ada_kernel,
        out_shape=jax.ShapeDtypeStruct((depth, SUBLANES, n), F32),
        grid=(depth, n // tn),
        in_specs=[
            pl.BlockSpec((SUBLANES, d), lambda l, j: (0, 0)),
            pl.BlockSpec((1, d, tn), lambda l, j: (l, 0, j)),
            pl.BlockSpec((1, 1, tn), lambda l, j: (l, 0, j)),
        ],
        out_specs=pl.BlockSpec((1, SUBLANES, tn), lambda l, j: (l, 0, j)),
        compiler_params=_params(("parallel", "parallel")),
        name="ada_mod",
    )(c_pad, w_ada, b_ada.reshape(depth, 1, n))


_NT = (((1,), (1,)), ((), ()))


def _attn_proj_kernel(*refs, tiles_per_batch, with_gate, n_cast):
    x_ref, g_ref, sh_ref, sc_ref, wk_ref, wqt_ref, wvt_ref = refs[:7]
    n_in = 7 + with_gate + n_cast
    k_ref, qt_ref, vt_ref = refs[n_in:n_in + 3]
    b = pl.program_id(0) // tiles_per_batch
    h = _rms_mod(x_ref[...], g_ref[...], sh_ref[pl.ds(b, 1), :], sc_ref[pl.ds(b, 1), :])
    hb = h.astype(BF16)
    k_ref[...] = jnp.dot(hb, wk_ref[...], preferred_element_type=F32).astype(BF16)
    qt_ref[...] = lax.dot_general(wqt_ref[...], hb, _NT, preferred_element_type=F32).astype(BF16)
    vt_ref[...] = lax.dot_general(wvt_ref[...], hb, _NT, preferred_element_type=F32).astype(BF16)
    if with_gate:
        refs[n_in + 3][...] = _dot_split(hb, (h - hb.astype(F32)).astype(BF16), refs[7])
    _ride_along_cast(refs[n_in - n_cast:n_in], refs[len(refs) - n_cast:])


def _attn_proj_call(x, g, mod, w_k, w_qt, w_vt, seq, w_f=None, cast=(), tm=512):
    n, d = x.shape
    with_gate = w_f is not None
    full = lambda a: pl.BlockSpec(a.shape, lambda i: (0, 0))
    c_views, c_in, c_out, c_shapes = _ride_along(cast, n // tm, lambda i: i)
    in_specs = [
        pl.BlockSpec((tm, d), lambda i: (i, 0)),
        pl.BlockSpec((1, d), lambda i: (0, 0)),
        pl.BlockSpec((SUBLANES, d), lambda i: (0, 0)),
        pl.BlockSpec((SUBLANES, d), lambda i: (0, 1)),
        full(w_k), full(w_qt), full(w_vt),
    ]
    args = [x, g.reshape(1, d), mod, mod, w_k, w_qt, w_vt]
    out_shape = [jax.ShapeDtypeStruct((n, w_k.shape[1]), BF16),
                 jax.ShapeDtypeStruct((w_qt.shape[0], n), BF16),
                 jax.ShapeDtypeStruct((w_vt.shape[0], n), BF16)]
    out_specs = [pl.BlockSpec((tm, w_k.shape[1]), lambda i: (i, 0)),
                 pl.BlockSpec((w_qt.shape[0], tm), lambda i: (0, i)),
                 pl.BlockSpec((w_vt.shape[0], tm), lambda i: (0, i))]
    if with_gate:
        in_specs.append(pl.BlockSpec(w_f.shape, lambda i: (0, 0, 0)))
        args.append(w_f)
        out_shape.append(jax.ShapeDtypeStruct((n, w_f.shape[2]), F32))
        out_specs.append(pl.BlockSpec((tm, w_f.shape[2]), lambda i: (i, 0)))
    return pl.pallas_call(
        functools.partial(_attn_proj_kernel, tiles_per_batch=seq // tm, with_gate=with_gate,
                          n_cast=len(cast)),
        out_shape=out_shape + c_shapes,
        grid=(n // tm,),
        in_specs=in_specs + c_in,
        out_specs=out_specs + c_out,
        compiler_params=_params(("parallel",)),
        name="attn_proj_gate" if with_gate else "attn_proj",
    )(*args, *c_views)


_FOX_T = 512
LOG2E = 1.4426950408889634


def _cum_kernel(f_ref, bf_ref, g_ref, r_ref):
    x = f_ref[...] + bf_ref[...]
    logf = (jnp.minimum(x, 0.0) - jnp.log(1.0 + jnp.exp(-jnp.abs(x)))) * LOG2E
    seq, nh = logf.shape
    r = lax.broadcasted_iota(jnp.int32, (_FOX_T, _FOX_T), 0)
    c = lax.broadcasted_iota(jnp.int32, (_FOX_T, _FOX_T), 1)
    lower = (c <= r).astype(F32)
    carry = jnp.zeros((1, nh), F32)
    for ch in range(seq // _FOX_T):
        rows = slice(ch * _FOX_T, (ch + 1) * _FOX_T)
        cs = jnp.dot(lower, logf[rows, :], preferred_element_type=F32, precision=HIGHEST)
        g_ref[0, rows, :] = cs
        r_ref[0, ch:ch + 1, :] = carry
        carry = carry + cs[_FOX_T - 1:_FOX_T, :]


def _cum_call(f, b_f, batch, seq):
    nh = f.shape[1]
    return pl.pallas_call(
        _cum_kernel,
        out_shape=[jax.ShapeDtypeStruct((batch, seq, nh), F32),
                   jax.ShapeDtypeStruct((batch, seq // _FOX_T, nh), F32)],
        grid=(batch,),
        in_specs=[
            pl.BlockSpec((seq, nh), lambda b: (b, 0)),
            pl.BlockSpec((1, nh), lambda b: (0, 0)),
        ],
        out_specs=[pl.BlockSpec((1, seq, nh), lambda b: (b, 0, 0)),
                   pl.BlockSpec((1, seq // _FOX_T, nh), lambda b: (b, 0, 0))],
        compiler_params=_params(("parallel",)),
        name="forget_cumsum",
    )(f, b_f.reshape(1, nh))


_ONES_ROWS = 16


def _fox_kernel(*refs, n_cast):
    r_ref, qt_ref, k_ref, vt_ref, g_ref = refs[:5]
    o_ref = refs[5 + n_cast]
    ka_sc, m_sc, acc_sc, sa_sc, sb_sc = refs[6 + 2 * n_cast:]
    _ride_along_cast(refs[5:5 + n_cast], refs[6 + n_cast:6 + 2 * n_cast])
    t = _FOX_T
    b, hp, qi = pl.program_id(0), pl.program_id(1), pl.program_id(2)
    seq = k_ref.shape[1]
    n_blocks = seq // t
    n_heads = 2 * pl.num_programs(1)

    lane = lax.broadcasted_iota(jnp.int32, (1, LANES), 1)
    feat = lax.broadcasted_iota(jnp.int32, (LANES, 1), 0)
    aug0 = [HEAD_DIM, 0]

    @pl.when(qi == 0)
    def _():
        g_all = g_ref[0]
        head_col = lax.broadcasted_iota(jnp.int32, g_all.shape, 1)
        for hh in range(2):
            g_head = jnp.sum(jnp.where(head_col == 2 * hp + hh, g_all, 0.0), axis=1, keepdims=True)
            gb = jnp.broadcast_to(g_head, (seq, LANES))
            hi = gb.astype(BF16).astype(F32)
            mid = (gb - hi).astype(BF16).astype(F32)
            lo = gb - hi - mid
            aug = jnp.where(lane == aug0[hh], hi,
                            jnp.where(lane == aug0[hh] + 1, mid,
                                      jnp.where(lane == aug0[hh] + 2, lo, 0.0)))
            own = (lane >= hh * HEAD_DIM) & (lane < (hh + 1) * HEAD_DIM)
            ka_sc[hh] = jnp.where(own, k_ref[0].astype(F32), aug).astype(BF16)

    qt2 = qt_ref[...]
    key = lax.broadcasted_iota(jnp.int32, (t, t), 0)
    qry = lax.broadcasted_iota(jnp.int32, (t, t), 1)
    ones = jnp.ones((_ONES_ROWS, t), BF16)
    feat_t = lax.broadcasted_iota(jnp.int32, (LANES, t), 0)
    qh = [jnp.where((feat >= hh * HEAD_DIM) & (feat < (hh + 1) * HEAD_DIM), qt2, 0)
          + jnp.where((feat_t >= aug0[hh]) & (feat_t < aug0[hh] + 3), -1.0, 0.0).astype(BF16)
          for hh in range(2)]
    r_base = [(b * n_heads + 2 * hp + hh) * n_blocks for hh in range(2)]
    r_q = [r_ref[r_base[hh] + qi] for hh in range(2)]
    m_sc[...] = jnp.full_like(m_sc, -jnp.inf)
    acc_sc[...] = jnp.zeros_like(acc_sc)

    def scores(kb, dst):
        k0 = pl.multiple_of(kb * t, t)
        for hh in range(2):
            dst[hh] = jnp.dot(ka_sc[hh, pl.ds(k0, t), :], qh[hh],
                              preferred_element_type=F32)

    def step(kb, cur, nxt, causal):
        if nxt is not None:
            scores(kb + 1, nxt)
        k0 = pl.multiple_of(kb * t, t)
        for hh in range(2):
            st = cur[hh]
            if causal:
                st = jnp.where(key <= qry, st, -jnp.inf)
            c = r_ref[r_base[hh] + kb] - r_q[hh]
            m_old = m_sc[hh]
            m_new = jnp.maximum(m_old, jnp.max(st, axis=0, keepdims=True) - c)
            alpha = jnp.exp2(m_old - m_new)
            p = jnp.exp2(st - (m_new + c)).astype(BF16)
            vt1 = jnp.concatenate(
                [vt_ref[hh * HEAD_DIM:(hh + 1) * HEAD_DIM, pl.ds(k0, t)], ones], axis=0)
            acc_sc[hh] = alpha * acc_sc[hh] + jnp.dot(vt1, p, preferred_element_type=F32)
            m_sc[hh] = m_new

    scores(0, sa_sc)

    def pair(i, carry):
        step(2 * i, sa_sc, sb_sc, False)
        step(2 * i + 1, sb_sc, sa_sc, False)
        return carry

    lax.fori_loop(0, qi // 2, pair, 0)

    @pl.when(qi % 2 == 0)
    def _():
        step(qi, sa_sc, None, True)

    @pl.when(qi % 2 == 1)
    def _():
        step(qi - 1, sa_sc, sb_sc, False)
        step(qi, sb_sc, None, True)

    outs = [acc_sc[hh, :HEAD_DIM, :] / acc_sc[hh, HEAD_DIM:HEAD_DIM + 1, :] for hh in range(2)]
    o_ref[0] = jnp.concatenate(outs, axis=0).T.astype(o_ref.dtype)


def _fox_call(r_flat, q_t, k, v_t, g_cum, cast=()):
    batch, seq, d = k.shape
    n_pairs = d // LANES
    t = _FOX_T
    nq = seq // t
    c_views, c_in, c_out, c_shapes = _ride_along(
        cast, batch * n_pairs * nq, lambda b, h, i: (b * n_pairs + h) * nq + i)
    return pl.pallas_call(
        functools.partial(_fox_kernel, n_cast=len(cast)),
        out_shape=[jax.ShapeDtypeStruct((batch, seq, d), BF16)] + c_shapes,
        grid=(batch, n_pairs, nq),
        in_specs=[
            pl.BlockSpec(memory_space=pltpu.SMEM),
            pl.BlockSpec((LANES, t), lambda b, h, i: (h, b * nq + i)),
            pl.BlockSpec((1, seq, LANES), lambda b, h, i: (b, 0, h)),
            pl.BlockSpec((LANES, seq), lambda b, h, i: (h, b)),
            pl.BlockSpec((1, seq, g_cum.shape[2]), lambda b, h, i: (b, 0, 0)),
        ] + c_in,
        out_specs=[pl.BlockSpec((1, t, LANES), lambda b, h, i: (b, i, h))] + c_out,
        scratch_shapes=[pltpu.VMEM((2, seq, LANES), BF16), pltpu.VMEM((2, 1, t), F32),
                        pltpu.VMEM((2, HEAD_DIM + _ONES_ROWS, t), F32),
                        pltpu.VMEM((2, t, t), F32), pltpu.VMEM((2, t, t), F32)],
        compiler_params=_params(("parallel", "parallel", "arbitrary")),
        name="fox_attention",
    )(r_flat, q_t, k, v_t, g_cum, *c_views)


def _out_kernel(*refs, tiles_per_batch, with_router):
    o_ref, w_ref, x_ref, gt_ref = refs[:4]
    b = pl.program_id(0) // tiles_per_batch
    y = jnp.dot(o_ref[...], w_ref[...], preferred_element_type=F32)
    xn = x_ref[...] + gt_ref[pl.ds(b, 1), :] * y
    if not with_router:
        refs[4][...] = xn
        return
    g_ref, sh_ref, sc_ref, wr_ref, br_ref, xo_ref, h_ref, route_ref, cnt_ref = refs[4:]
    xo_ref[...] = xn
    h = _rms_mod(xn, g_ref[...], sh_ref[pl.ds(b, 1), :], sc_ref[pl.ds(b, 1), :])
    h_ref[...] = h
    logits = _dot_split(*_split_bf16(h), wr_ref) + br_ref[...]
    tm = logits.shape[0]
    lane = lax.broadcasted_iota(jnp.int32, logits.shape, 1).astype(F32)
    logits = jnp.where(lane < N_EXPERTS, logits, -jnp.inf)
    m1 = jnp.max(logits, axis=1, keepdims=True)
    i1 = jnp.min(jnp.where(logits == m1, lane, float(LANES)), axis=1, keepdims=True)
    rest = jnp.where(lane == i1, -jnp.inf, logits)
    m2 = jnp.max(rest, axis=1, keepdims=True)
    i2 = jnp.min(jnp.where(rest == m2, lane, float(LANES)), axis=1, keepdims=True)
    e2 = jnp.exp(m2 - m1)
    den = 1.0 + e2

    @pl.when(pl.program_id(0) == 0)
    def _():
        cnt_ref[...] = jnp.zeros_like(cnt_ref)

    sel1 = lane == i1
    sel2 = lane == i2
    onehot = jnp.where(sel1 | sel2, 1.0, 0.0)
    r = lax.broadcasted_iota(jnp.int32, (tm, tm), 0)
    c = lax.broadcasted_iota(jnp.int32, (tm, tm), 1)
    lower = jnp.where(c < r, 1.0, 0.0).astype(BF16)
    before = jnp.dot(lower, onehot.astype(BF16), preferred_element_type=F32) + cnt_ref[0:1, :]
    rank1 = jnp.sum(jnp.where(sel1, before, 0.0), axis=1, keepdims=True)
    rank2 = jnp.sum(jnp.where(sel2, before, 0.0), axis=1, keepdims=True)
    cnt_ref[0:1, :] = cnt_ref[0:1, :] + jnp.sum(onehot, axis=0, keepdims=True)
    route = jnp.where(lane == 0.0, i1, 0.0)
    for k, val in enumerate((i2, 1.0 / den, e2 / den, rank1, rank2), start=1):
        route = jnp.where(lane == float(k), val, route)
    route_ref[...] = route


def _out_call(o, w_out, x, mod, gate_col, seq, router=None, tm=512):
    n, d = x.shape
    with_router = router is not None
    in_specs = [
        pl.BlockSpec((tm, d), lambda i: (i, 0)),
        pl.BlockSpec((d, d), lambda i: (0, 0)),
        pl.BlockSpec((tm, d), lambda i: (i, 0)),
        pl.BlockSpec((SUBLANES, d), lambda i: (0, gate_col)),
    ]
    args = [o, w_out, x, mod]
    out_shape = [jax.ShapeDtypeStruct((n, d), F32)]
    out_specs = [pl.BlockSpec((tm, d), lambda i: (i, 0))]
    if with_router:
        g, sh_col, sc_col, w_r, b_r = router
        in_specs += [
            pl.BlockSpec((1, d), lambda i: (0, 0)),
            pl.BlockSpec((SUBLANES, d), lambda i: (0, sh_col)),
            pl.BlockSpec((SUBLANES, d), lambda i: (0, sc_col)),
            pl.BlockSpec((2, d, LANES), lambda i: (0, 0, 0)),
            pl.BlockSpec((1, LANES), lambda i: (0, 0)),
        ]
        args += [g.reshape(1, d), mod, mod, w_r, b_r]
        out_shape += [jax.ShapeDtypeStruct((n, d), F32),
                      jax.ShapeDtypeStruct((n, LANES), F32),
                      jax.ShapeDtypeStruct((SUBLANES, LANES), F32)]
        out_specs += [pl.BlockSpec((tm, d), lambda i: (i, 0)),
                      pl.BlockSpec((tm, LANES), lambda i: (i, 0)),
                      pl.BlockSpec((SUBLANES, LANES), lambda i: (0, 0))]
    res = pl.pallas_call(
        functools.partial(_out_kernel, tiles_per_batch=seq // tm, with_router=with_router),
        out_shape=out_shape,
        grid=(n // tm,),
        in_specs=in_specs,
        out_specs=out_specs,
        compiler_params=_params(("arbitrary",) if with_router else ("parallel",)),
        name="out_proj_router" if with_router else "out_proj",
    )(*args)
    return res if with_router else res[0]


_FF_TILE = 1792


def _snake(i, j, nj):
    return jnp.where(i % 2 == 0, j, nj - 1 - j)


def _ffn_kernel(x_ref, g_ref, sh_ref, sc_ref, gt_ref, wg_ref, wu_ref, wd_ref,
                o_ref, h_sc, *, tiles_per_batch):
    b = pl.program_id(0) // tiles_per_batch

    @pl.when(pl.program_id(1) == 0)
    def _():
        x = x_ref[...]
        h = _rms_mod(x, g_ref[...], sh_ref[pl.ds(b, 1), :], sc_ref[pl.ds(b, 1), :])
        h_sc[...] = h.astype(BF16)
        o_ref[...] = x

    h = h_sc[...]
    g = jnp.dot(h, wg_ref[...], preferred_element_type=F32)
    u = jnp.dot(h, wu_ref[...], preferred_element_type=F32)
    a = (_silu(g) * u).astype(BF16)
    o_ref[...] += gt_ref[pl.ds(b, 1), :] * jnp.dot(a, wd_ref[...], preferred_element_type=F32)


def _ffn_call(x, g, mod, w_gu, w_down, seq, tm=512, tf=_FF_TILE):
    n, d = x.shape
    f = w_down.shape[0]
    nj = f // tf
    return pl.pallas_call(
        functools.partial(_ffn_kernel, tiles_per_batch=seq // tm),
        out_shape=jax.ShapeDtypeStruct((n, d), F32),
        grid=(n // tm, nj),
        in_specs=[
            pl.BlockSpec((tm, d), lambda i, j: (i, 0)),
            pl.BlockSpec((1, d), lambda i, j: (0, 0)),
            pl.BlockSpec((SUBLANES, d), lambda i, j: (0, 3)),
            pl.BlockSpec((SUBLANES, d), lambda i, j: (0, 4)),
            pl.BlockSpec((SUBLANES, d), lambda i, j: (0, 5)),
            pl.BlockSpec((d, tf), lambda i, j: (0, _snake(i, j, nj))),
            pl.BlockSpec((d, tf), lambda i, j: (0, _snake(i, j, nj) + nj)),
            pl.BlockSpec((tf, d), lambda i, j: (_snake(i, j, nj), 0)),
        ],
        out_specs=pl.BlockSpec((tm, d), lambda i, j: (i, 0)),
        scratch_shapes=[pltpu.VMEM((tm, d), BF16)],
        compiler_params=_params(("parallel", "arbitrary")),
        name="ffn_swiglu",
    )(x, g.reshape(1, d), mod, mod, mod, w_gu, w_gu, w_down)


_MOE_TM = 512


def _dispatch_kernel(d1_ref, d2_ref, zero_ref, h_ref, xs_ref, z_sc, sem):
    tm = h_ref.shape[0]
    base = pl.program_id(0) * tm

    @pl.when(pl.program_id(0) == 0)
    def _():
        z_sc[...] = jnp.zeros_like(z_sc)

        def zero_copy(t):
            return pltpu.make_async_copy(z_sc, xs_ref.at[pl.ds(pl.multiple_of(t * tm, tm), tm)],
                                         sem.at[0])

        def start(t, carry):
            @pl.when(zero_ref[t] != 0)
            def _():
                zero_copy(t).start()
            return carry

        def wait(t, carry):
            @pl.when(zero_ref[t] != 0)
            def _():
                zero_copy(t).wait()
            return carry

        lax.fori_loop(0, zero_ref.shape[0], start, 0)
        lax.fori_loop(0, zero_ref.shape[0], wait, 0)

    def issue(r, carry):
        src = h_ref.at[pl.ds(r, 1)]
        pltpu.make_async_copy(src, xs_ref.at[pl.ds(d1_ref[base + r], 1)], sem.at[0]).start()
        pltpu.make_async_copy(src, xs_ref.at[pl.ds(d2_ref[base + r], 1)], sem.at[1]).start()
        return carry

    lax.fori_loop(0, tm, issue, 0, unroll=8)
    pltpu.make_async_copy(h_ref, xs_ref.at[pl.ds(0, tm)], sem.at[0]).wait()
    pltpu.make_async_copy(h_ref, xs_ref.at[pl.ds(0, tm)], sem.at[1]).wait()


def _dispatch_call(dest1, dest2, zero_tile, h, tm=_MOE_TM):
    n, d = h.shape
    n_rows = zero_tile.shape[0] * tm
    return pl.pallas_call(
        _dispatch_kernel,
        out_shape=jax.ShapeDtypeStruct((n_rows, d), h.dtype),
        grid_spec=pltpu.PrefetchScalarGridSpec(
            num_scalar_prefetch=3,
            grid=(n // tm,),
            in_specs=[pl.BlockSpec((tm, d), lambda i, d1, d2, zt: (i, 0))],
            out_specs=pl.BlockSpec(memory_space=pl.ANY),
            scratch_shapes=[pltpu.VMEM((tm, d), h.dtype), pltpu.SemaphoreType.DMA((2,))],
        ),
        compiler_params=_params(("arbitrary",)),
        name="moe_dispatch",
    )(dest1, dest2, zero_tile, h)


def _experts_kernel(te_ref, nt_ref, xs_ref, wg_ref, wu_ref, wd_ref, ye_ref):
    del te_ref
    t = pl.program_id(0)

    @pl.when(pl.program_id(1) == 0)
    def _():
        ye_ref[...] = jnp.zeros_like(ye_ref)

    @pl.when(t < nt_ref[0])
    def _():
        h = xs_ref[...].astype(BF16)
        g = jnp.dot(h, wg_ref[0], preferred_element_type=F32)
        u = jnp.dot(h, wu_ref[0], preferred_element_type=F32)
        a = (_silu(g) * u).astype(BF16)
        ye_ref[...] += jnp.dot(a, wd_ref[0], preferred_element_type=F32)


def _experts_call(tile_expert, n_tiles, xs, w_gu, w_down, tf=_FF_TILE):
    n_rows, d = xs.shape
    _, f, _ = w_down.shape
    nj = f // tf
    tm = _MOE_TM
    tile = lambda t, nt: jnp.maximum(jnp.minimum(t, nt[0] - 1), 0)
    jj = lambda t, j, nt: _snake(tile(t, nt), jnp.where(t < nt[0], j, nj - 1), nj)
    return pl.pallas_call(
        _experts_kernel,
        out_shape=jax.ShapeDtypeStruct((n_rows, d), F32),
        grid_spec=pltpu.PrefetchScalarGridSpec(
            num_scalar_prefetch=2,
            grid=(n_rows // tm, nj),
            in_specs=[
                pl.BlockSpec((tm, d), lambda t, j, te, nt: (tile(t, nt), 0)),
                pl.BlockSpec((1, d, tf), lambda t, j, te, nt: (te[tile(t, nt)], 0, jj(t, j, nt))),
                pl.BlockSpec((1, d, tf), lambda t, j, te, nt: (te[tile(t, nt)], 0, jj(t, j, nt) + nj)),
                pl.BlockSpec((1, tf, d), lambda t, j, te, nt: (te[tile(t, nt)], jj(t, j, nt), 0)),
            ],
            out_specs=pl.BlockSpec((tm, d), lambda t, j, te, nt: (t, 0)),
        ),
        compiler_params=_params(("arbitrary", "arbitrary")),
        name="moe_experts",
    )(tile_expert, n_tiles, xs, w_gu, w_gu, w_down)


def _combine_kernel(d1_ref, d2_ref, ye_ref, x_ref, route_ref, gt_ref, gf_ref, o_ref,
                    y1_sc, y2_sc, sem, *, tiles_per_batch):
    tm = x_ref.shape[0]
    i = pl.program_id(0)
    b = i // tiles_per_batch
    base = i * tm

    def issue(r, carry):
        pltpu.make_async_copy(ye_ref.at[pl.ds(d1_ref[base + r], 1)], y1_sc.at[pl.ds(r, 1)],
                              sem.at[0]).start()
        pltpu.make_async_copy(ye_ref.at[pl.ds(d2_ref[base + r], 1)], y2_sc.at[pl.ds(r, 1)],
                              sem.at[1]).start()
        return carry

    lax.fori_loop(0, tm, issue, 0, unroll=8)
    pltpu.make_async_copy(ye_ref.at[pl.ds(0, tm)], y1_sc, sem.at[0]).wait()
    pltpu.make_async_copy(ye_ref.at[pl.ds(0, tm)], y2_sc, sem.at[1]).wait()
    route = route_ref[...]
    y = route[:, 2:3] * y1_sc[...] + route[:, 3:4] * y2_sc[...]
    xn = x_ref[...] + gt_ref[pl.ds(b, 1), :] * y
    ms = jnp.mean(xn * xn, axis=-1, keepdims=True)
    o_ref[...] = xn * lax.rsqrt(ms + EPS) * gf_ref[...]


def _combine_call(dest1, dest2, ye, x, route, mod, g_final, seq, tm=512):
    n, d = x.shape
    return pl.pallas_call(
        functools.partial(_combine_kernel, tiles_per_batch=seq // tm),
        out_shape=jax.ShapeDtypeStruct((n, d), F32),
        grid_spec=pltpu.PrefetchScalarGridSpec(
            num_scalar_prefetch=2,
            grid=(n // tm,),
            in_specs=[
                pl.BlockSpec(memory_space=pl.ANY),
                pl.BlockSpec((tm, d), lambda i, d1, d2: (i, 0)),
                pl.BlockSpec((tm, LANES), lambda i, d1, d2: (i, 0)),
                pl.BlockSpec((SUBLANES, d), lambda i, d1, d2: (0, 5)),
                pl.BlockSpec((1, d), lambda i, d1, d2: (0, 0)),
            ],
            out_specs=pl.BlockSpec((tm, d), lambda i, d1, d2: (i, 0)),
            scratch_shapes=[pltpu.VMEM((tm, d), F32), pltpu.VMEM((tm, d), F32),
                            pltpu.SemaphoreType.DMA((2,))],
        ),
        compiler_params=_params(("arbitrary",)),
        name="moe_combine",
    )(dest1, dest2, ye, x, route, mod, g_final.reshape(1, d))


def _moe_call(h, route, counts, x, mod, g_final, w_gu, w_down, seq):
    n, d = x.shape
    ne = w_down.shape[0]
    tm = _MOE_TM
    max_tiles = (2 * n) // tm + ne
    e1 = route[:, 0].astype(jnp.int32)
    e2 = route[:, 1].astype(jnp.int32)
    cnt = counts[0, :ne].astype(jnp.int32)
    tiles_e = (cnt + tm - 1) // tm
    tile_end = jnp.cumsum(tiles_e)
    row_start = (tile_end - tiles_e) * tm
    dest1 = row_start[e1] + route[:, 4].astype(jnp.int32)
    dest2 = row_start[e2] + route[:, 5].astype(jnp.int32)
    n_tiles = tile_end[-1:]
    tile_ids = jnp.arange(max_tiles, dtype=jnp.int32)
    tile_expert = jnp.minimum(
        jnp.sum((tile_ids[:, None] >= tile_end[None, :]).astype(jnp.int32), axis=1), ne - 1)
    is_last = jnp.any((tile_ids[:, None] == tile_end[None, :] - 1) & (tiles_e[None, :] > 0), axis=1)
    zero_tile = (is_last | (tile_ids >= n_tiles[0])).astype(jnp.int32)
    xs = _dispatch_call(dest1, dest2, zero_tile, h)
    ye = _experts_call(tile_expert, n_tiles, xs, w_gu, w_down)
    return _combine_call(dest1, dest2, ye, x, route, mod, g_final, seq)


_SWA_TQ = 2 * CHUNK
_SWA_BAND = 2 * _SWA_TQ


def _swa_bucket_tiles():
    cc = np.arange(_SWA_BAND)[:, None]
    r = np.arange(_SWA_TQ)[None, :]
    rel = cc - _SWA_TQ - r
    nb = REL_BUCKETS // 2
    max_exact = nb // 2
    ret = (rel > 0).astype(np.int32) * nb
    n = np.abs(rel)
    large = max_exact + (np.log(np.maximum(n, 1) / max_exact)
                         / np.log(REL_MAX_DIST / max_exact) * (nb - max_exact)).astype(np.int32)
    large = np.minimum(large, nb - 1)
    bucket = (ret + np.where(n < max_exact, n, large)).astype(np.int32)
    q_chunk = r // CHUNK
    k_chunk = cc // CHUNK
    visible = (k_chunk >= q_chunk) & (k_chunk <= q_chunk + WINDOW_CHUNKS)
    later = np.where(visible, bucket, -1)
    first = np.where(cc >= _SWA_TQ, later, -1)
    return np.stack([first, later]).astype(np.int32)


def _swa_bias_kernel(tbl_ref, bkt_ref, o_ref):
    n_heads = o_ref.shape[1]
    for v in range(2):
        bkt = bkt_ref[v]
        for head in range(n_heads):
            tile = jnp.full(bkt.shape, NEG_BIG, F32)
            for bk in range(REL_BUCKETS):
                tile = jnp.where(bkt == bk, tbl_ref[head, bk] * LOG2E, tile)
            o_ref[v, head] = tile


def _swa_bias_call(rel_bias):
    bkt = jnp.asarray(_swa_bucket_tiles())
    n_heads = rel_bias.shape[1]
    return pl.pallas_call(
        _swa_bias_kernel,
        out_shape=jax.ShapeDtypeStruct((2, n_heads, _SWA_BAND, _SWA_TQ), F32),
        in_specs=[
            pl.BlockSpec(memory_space=pltpu.SMEM),
            pl.BlockSpec(memory_space=pltpu.VMEM),
        ],
        out_specs=pl.BlockSpec(memory_space=pltpu.VMEM),
        name="swa_bias",
    )(rel_bias.T, bkt)


def _swa_kernel(qt_ref, kp_ref, kc_ref, vtp_ref, vtc_ref, bias_ref, sink_ref, o_ref):
    tq = _SWA_TQ
    lane = lax.broadcasted_iota(jnp.int32, (1, LANES), 1)
    ones = jnp.ones((_ONES_ROWS, _SWA_BAND), BF16)
    outs = [None] * (SWA_KV_HEADS * SWA_GROUP)
    units = [(hk, par) for hk in range(SWA_KV_HEADS) for par in range(2)]

    def scores(hk, par):
        ksl = slice(hk * LANES, (hk + 1) * LANES)
        kb = jnp.concatenate([kp_ref[0, :, ksl], kc_ref[0, :, ksl]], axis=0)
        f0 = hk * SWA_GROUP * HEAD_DIM
        wq = jnp.concatenate([qt_ref[f0:f0 + LANES, :], qt_ref[f0 + LANES:f0 + 2 * LANES, :]],
                             axis=1)
        head_lanes = (lane < HEAD_DIM) if par == 0 else (lane >= HEAD_DIM)
        return jnp.dot(jnp.where(head_lanes, kb, 0), wq, preferred_element_type=F32)

    sts = [scores(hk, par) for hk, par in units]
    for (hk, par), st in zip(units, sts):
        vsl = slice(hk * HEAD_DIM, (hk + 1) * HEAD_DIM)
        vt1 = jnp.concatenate(
            [jnp.concatenate([vtp_ref[vsl, :], vtc_ref[vsl, :]], axis=1), ones], axis=0)
        heads = (hk * SWA_GROUP + par, hk * SWA_GROUP + par + 2)
        ps, ms = [], []
        for i, head in enumerate(heads):
            s = st[:, i * tq:(i + 1) * tq] + bias_ref[0, head]
            m = jnp.maximum(jnp.max(s, axis=0, keepdims=True), sink_ref[head])
            ps.append(jnp.exp2(s - m).astype(BF16))
            ms.append(m)
        acc = jnp.dot(vt1, jnp.concatenate(ps, axis=1), preferred_element_type=F32)
        for i, head in enumerate(heads):
            a = acc[:, i * tq:(i + 1) * tq]
            den = a[HEAD_DIM:HEAD_DIM + 1] + jnp.exp2(sink_ref[head] - ms[i])
            outs[head] = a[:HEAD_DIM] / den
    o_ref[0] = jnp.concatenate(outs, axis=0).T.astype(o_ref.dtype)


def _swa_call(q_t, k, v_t, bias, sink, batch, seq):
    d = q_t.shape[0]
    kw = k.shape[2]
    vw = v_t.shape[0]
    tq = _SWA_TQ
    nq = seq // tq
    return pl.pallas_call(
        _swa_kernel,
        out_shape=jax.ShapeDtypeStruct((batch, seq, d), BF16),
        grid=(batch, nq),
        in_specs=[
            pl.BlockSpec((d, tq), lambda b, i: (0, b * nq + i)),
            pl.BlockSpec((1, tq, kw), lambda b, i: (b, jnp.maximum(i - 1, 0), 0)),
            pl.BlockSpec((1, tq, kw), lambda b, i: (b, i, 0)),
            pl.BlockSpec((vw, tq), lambda b, i: (0, b * nq + jnp.maximum(i - 1, 0))),
            pl.BlockSpec((vw, tq), lambda b, i: (0, b * nq + i)),
            pl.BlockSpec((1,) + bias.shape[1:], lambda b, i: (jnp.minimum(i, 1), 0, 0, 0)),
            pl.BlockSpec(sink.shape, lambda b, i: (0, 0, 0)),
        ],
        out_specs=pl.BlockSpec((1, tq, d), lambda b, i: (b, i, 0)),
        compiler_params=_params(("parallel", "arbitrary")),
        name="swa_attention",
    )(q_t, k, k, v_t, v_t, bias, sink)


def kernel(x, c, w_ada, b_ada, g_norm_mix, g_norm_ffn, g_final, fox_w_in, fox_b_f, fox_w_out, swa_w_in, swa_sinks, swa_w_out, rel_bias, ffn_w_gu, ffn_w_down, moe_w_router, moe_b_router, moe_w_gu, moe_w_down):
    batch, seq, d = x.shape
    n = batch * seq
    q_scale = HEAD_DIM ** -0.5
    xf = x.reshape(n, d)

    c_pad = jnp.zeros((SUBLANES, d), F32).at[:batch].set(c)
    mod = _ada_call(c_pad, w_ada, b_ada)
    mod0, mod1 = mod[0], mod[1]

    w_in = fox_w_in[0]
    n_heads = d // HEAD_DIM
    k, q_t, v_t, f, w_gu0, w_down0, w_out0, w_out1 = _attn_proj_call(
        xf, g_norm_mix[0], mod0, w_in[:, d:2 * d].astype(BF16),
        (w_in[:, :d] * (q_scale * LOG2E)).T.astype(BF16), w_in[:, 2 * d:3 * d].T.astype(BF16),
        seq, w_f=jnp.stack(_split_bf16(w_in[:, 3 * d:])),
        cast=(ffn_w_gu[0], ffn_w_down[0], fox_w_out[0], swa_w_out[0]))
    g_cum, r_cum = _cum_call(f, fox_b_f[0], batch, seq)
    r_flat = r_cum.transpose(0, 2, 1).reshape(-1)
    o, w_gu1, w_down1 = _fox_call(r_flat, q_t, k.reshape(batch, seq, d), v_t, g_cum,
                                  cast=(moe_w_gu[0], moe_w_down[0]))
    w_gu1 = w_gu1.reshape(moe_w_gu.shape[1:])
    w_down1 = w_down1.reshape(moe_w_down.shape[1:])
    x1 = _out_call(o.reshape(n, d), w_out0, xf, mod0, 2, seq)
    x2 = _ffn_call(x1, g_norm_ffn[0], mod0, w_gu0, w_down0, seq)

    w_in = swa_w_in[0]
    kvw = SWA_KV_HEADS * HEAD_DIM
    dup = lambda w: jnp.repeat(w.reshape(d, SWA_KV_HEADS, 1, HEAD_DIM), 2, axis=2).reshape(d, 2 * kvw)
    k, q_t, v_t = _attn_proj_call(
        x2, g_norm_mix[1], mod1, dup(w_in[:, d:d + kvw]).astype(BF16),
        (w_in[:, :d] * (q_scale * LOG2E)).T.astype(BF16), w_in[:, d + kvw:].T.astype(BF16), seq)
    bias = _swa_bias_call(rel_bias)
    sink = jnp.broadcast_to((swa_sinks[0] * LOG2E)[:, None, None], (n_heads, 1, _SWA_TQ))
    o = _swa_call(q_t, k.reshape(batch, seq, 2 * kvw), v_t, bias, sink, batch, seq)
    w_r = jnp.stack(_split_bf16(jnp.zeros((d, LANES), F32).at[:, :N_EXPERTS].set(moe_w_router[0])))
    b_r = jnp.zeros((1, LANES), F32).at[0, :N_EXPERTS].set(moe_b_router[0])
    x3, h4, route, counts = _out_call(o.reshape(n, d), w_out1, x2, mod1, 2, seq,
                                      router=(g_norm_ffn[1], 3, 4, w_r, b_r))
    out = _moe_call(h4, route, counts, x3, mod1, g_final, w_gu1, w_down1, seq)
    return out.reshape(batch, seq, d)
```

```python
import functools

import numpy as np
import jax
import jax.numpy as jnp
from jax import lax
from jax.experimental import pallas as pl
from jax.experimental.pallas import tpu as pltpu

F32 = jnp.float32
BF16 = jnp.bfloat16
HIGHEST = lax.Precision.HIGHEST

HEAD_DIM = 64
CHUNK = 64
WINDOW_CHUNKS = 2
REL_BUCKETS = 32
REL_MAX_DIST = 128
SWA_KV_HEADS = 4
SWA_GROUP = 4
N_EXPERTS = 8
EPS = 1e-6

LANES = 128
SUBLANES = 8
VMEM_LIMIT = 56 * 1024 * 1024

NEG_BIG = -1e30


def _params(sem, vmem=VMEM_LIMIT):
    return pltpu.CompilerParams(dimension_semantics=sem, vmem_limit_bytes=vmem)


def _rms_mod(x, g, shift, scale):
    ms = jnp.mean(x * x, axis=-1, keepdims=True)
    y = x * lax.rsqrt(ms + EPS) * g
    return y * (1.0 + scale) + shift


def _silu(x):
    return x / (1.0 + jnp.exp(-x))


def _split_bf16(x):
    hi = x.astype(BF16)
    return hi, (x - hi.astype(F32)).astype(BF16)


def _dot_split(x_hi, x_lo, w_ref):
    return (jnp.dot(x_hi, w_ref[0], preferred_element_type=F32)
            + jnp.dot(x_lo, w_ref[0], preferred_element_type=F32)
            + jnp.dot(x_hi, w_ref[1], preferred_element_type=F32))


_BF16_SUBLANES = 16


def _ride_along(arrays, n_steps, step_index):
    views, in_specs, out_specs, out_shapes = [], [], [], []
    for a in arrays:
        v = a.reshape(-1, a.shape[-1])
        rows, rem = divmod(v.shape[0], n_steps)
        assert rem == 0 and rows % _BF16_SUBLANES == 0, v.shape
        spec = pl.BlockSpec((rows, v.shape[1]), lambda *g: (step_index(*g), 0))
        views.append(v)
        in_specs.append(spec)
        out_specs.append(spec)
        out_shapes.append(jax.ShapeDtypeStruct(v.shape, BF16))
    return views, in_specs, out_specs, out_shapes


def _ride_along_cast(in_refs, out_refs):
    for src, dst in zip(in_refs, out_refs):
        dst[...] = src[...].astype(BF16)


def _ada_kernel(c_ref, w_ref, b_ref, o_ref):
    cond = _silu(c_ref[...])
    o_ref[0] = jnp.dot(cond, w_ref[0], preferred_element_type=F32,
                       precision=HIGHEST) + b_ref[0]


def _ada_call(c_pad, w_ada, b_ada):
    depth, d, n = w_ada.shape
    tn = 1536
    return pl.pallas_call(
        _ada_kernel,
        out_shape=jax.ShapeDtypeStruct((depth, SUBLANES, n), F32),
        grid=(depth, n // tn),
        in_specs=[
            pl.BlockSpec((SUBLANES, d), lambda l, j: (0, 0)),
            pl.BlockSpec((1, d, tn), lambda l, j: (l, 0, j)),
            pl.BlockSpec((1, 1, tn), lambda l, j: (l, 0, j)),
        ],
        out_specs=pl.BlockSpec((1, SUBLANES, tn), lambda l, j: (l, 0, j)),
        compiler_params=_params(("parallel", "parallel")),
        name="ada_mod",
    )(c_pad, w_ada, b_ada.reshape(depth, 1, n))


_NT = (((1,), (1,)), ((), ()))


def _attn_proj_kernel(*refs, tiles_per_batch, with_gate, n_cast):
    x_ref, g_ref, sh_ref, sc_ref, wk_ref, wqt_ref, wvt_ref = refs[:7]
    n_in = 7 + with_gate + n_cast
    k_ref, qt_ref, vt_ref = refs[n_in:n_in + 3]
    b = pl.program_id(0) // tiles_per_batch
    h = _rms_mod(x_ref[...], g_ref[...], sh_ref[pl.ds(b, 1), :], sc_ref[pl.ds(b, 1), :])
    hb = h.astype(BF16)
    k_ref[...] = jnp.dot(hb, wk_ref[...], preferred_element_type=F32).astype(BF16)
    qt_ref[...] = lax.dot_general(wqt_ref[...], hb, _NT, preferred_element_type=F32).astype(BF16)
    vt_ref[...] = lax.dot_general(wvt_ref[...], hb, _NT, preferred_element_type=F32).astype(BF16)
    if with_gate:
        refs[n_in + 3][...] = _dot_split(hb, (h - hb.astype(F32)).astype(BF16), refs[7])
    _ride_along_cast(refs[n_in - n_cast:n_in], refs[len(refs) - n_cast:])


def _attn_proj_call(x, g, mod, w_k, w_qt, w_vt, seq, w_f=None, cast=(), tm=512):
    n, d = x.shape
    with_gate = w_f is not None
    full = lambda a: pl.BlockSpec(a.shape, lambda i: (0, 0))
    c_views, c_in, c_out, c_shapes = _ride_along(cast, n // tm, lambda i: i)
    in_specs = [
        pl.BlockSpec((tm, d), lambda i: (i, 0)),
        pl.BlockSpec((1, d), lambda i: (0, 0)),
        pl.BlockSpec((SUBLANES, d), lambda i: (0, 0)),
        pl.BlockSpec((SUBLANES, d), lambda i: (0, 1)),
        full(w_k), full(w_qt), full(w_vt),
    ]
    args = [x, g.reshape(1, d), mod, mod, w_k, w_qt, w_vt]
    out_shape = [jax.ShapeDtypeStruct((n, w_k.shape[1]), BF16),
                 jax.ShapeDtypeStruct((w_qt.shape[0], n), BF16),
                 jax.ShapeDtypeStruct((w_vt.shape[0], n), BF16)]
    out_specs = [pl.BlockSpec((tm, w_k.shape[1]), lambda i: (i, 0)),
                 pl.BlockSpec((w_qt.shape[0], tm), lambda i: (0, i)),
                 pl.BlockSpec((w_vt.shape[0], tm), lambda i: (0, i))]
    if with_gate:
        in_specs.append(pl.BlockSpec(w_f.shape, lambda i: (0, 0, 0)))
        args.append(w_f)
        out_shape.append(jax.ShapeDtypeStruct((n, w_f.shape[2]), F32))
        out_specs.append(pl.BlockSpec((tm, w_f.shape[2]), lambda i: (i, 0)))
    return pl.pallas_call(
        functools.partial(_attn_proj_kernel, tiles_per_batch=seq // tm, with_gate=with_gate,
                          n_cast=len(cast)),
        out_shape=out_shape + c_shapes,
        grid=(n // tm,),
        in_specs=in_specs + c_in,
        out_specs=out_specs + c_out,
        compiler_params=_params(("parallel",)),
        name="attn_proj_gate" if with_gate else "attn_proj",
    )(*args, *c_views)


_FOX_T = 512
LOG2E = 1.4426950408889634


def _cum_kernel(f_ref, bf_ref, g_ref, r_ref):
    x = f_ref[...] + bf_ref[...]
    logf = (jnp.minimum(x, 0.0) - jnp.log(1.0 + jnp.exp(-jnp.abs(x)))) * LOG2E
    seq, nh = logf.shape
    r = lax.broadcasted_iota(jnp.int32, (_FOX_T, _FOX_T), 0)
    c = lax.broadcasted_iota(jnp.int32, (_FOX_T, _FOX_T), 1)
    lower = (c <= r).astype(F32)
    carry = jnp.zeros((1, nh), F32)
    for ch in range(seq // _FOX_T):
        rows = slice(ch * _FOX_T, (ch + 1) * _FOX_T)
        cs = jnp.dot(lower, logf[rows, :], preferred_element_type=F32, precision=HIGHEST)
        g_ref[0, rows, :] = cs
        r_ref[0, ch:ch + 1, :] = carry
        carry = carry + cs[_FOX_T - 1:_FOX_T, :]


def _cum_call(f, b_f, batch, seq):
    nh = f.shape[1]
    return pl.pallas_call(
        _cum_kernel,
        out_shape=[jax.ShapeDtypeStruct((batch, seq, nh), F32),
                   jax.ShapeDtypeStruct((batch, seq // _FOX_T, nh), F32)],
        grid=(batch,),
        in_specs=[
            pl.BlockSpec((seq, nh), lambda b: (b, 0)),
            pl.BlockSpec((1, nh), lambda b: (0, 0)),
        ],
        out_specs=[pl.BlockSpec((1, seq, nh), lambda b: (b, 0, 0)),
                   pl.BlockSpec((1, seq // _FOX_T, nh), lambda b: (b, 0, 0))],
        compiler_params=_params(("parallel",)),
        name="forget_cumsum",
    )(f, b_f.reshape(1, nh))


_ONES_ROWS = 16


def _fox_kernel(*refs, n_cast):
    r_ref, qt_ref, k_ref, vt_ref, g_ref = refs[:5]
    o_ref = refs[5 + n_cast]
    ka_sc, m_sc, acc_sc, sa_sc, sb_sc, xa_sc, xb_sc = refs[6 + 2 * n_cast:]
    _ride_along_cast(refs[5:5 + n_cast], refs[6 + n_cast:6 + 2 * n_cast])
    t = _FOX_T
    b, hp = pl.program_id(0), pl.program_id(1)
    seq = k_ref.shape[1]
    n_blocks = seq // t
    n_heads = 2 * pl.num_programs(1)

    lane = lax.broadcasted_iota(jnp.int32, (1, LANES), 1)
    feat = lax.broadcasted_iota(jnp.int32, (LANES, 1), 0)
    aug0 = [HEAD_DIM, 0]

    g_all = g_ref[0]
    head_col = lax.broadcasted_iota(jnp.int32, g_all.shape, 1)
    for hh in range(2):
        g_head = jnp.sum(jnp.where(head_col == 2 * hp + hh, g_all, 0.0), axis=1, keepdims=True)
        gb = jnp.broadcast_to(g_head, (seq, LANES))
        hi = gb.astype(BF16).astype(F32)
        mid = (gb - hi).astype(BF16).astype(F32)
        lo = gb - hi - mid
        aug = jnp.where(lane == aug0[hh], hi,
                        jnp.where(lane == aug0[hh] + 1, mid,
                                  jnp.where(lane == aug0[hh] + 2, lo, 0.0)))
        own = (lane >= hh * HEAD_DIM) & (lane < (hh + 1) * HEAD_DIM)
        ka_sc[hh] = jnp.where(own, k_ref[0].astype(F32), aug).astype(BF16)

    key = lax.broadcasted_iota(jnp.int32, (t, t), 0)
    qry = lax.broadcasted_iota(jnp.int32, (t, t), 1)
    ones = jnp.ones((_ONES_ROWS, t), BF16)
    feat_t = lax.broadcasted_iota(jnp.int32, (LANES, t), 0)
    r_base = [(b * n_heads + 2 * hp + hh) * n_blocks for hh in range(2)]
    bufs = ((sa_sc, xa_sc), (sb_sc, xb_sc))

    def queries(qi):
        qt2 = qt_ref[:, qi * t:(qi + 1) * t]
        return [jnp.where((feat >= hh * HEAD_DIM) & (feat < (hh + 1) * HEAD_DIM), qt2, 0)
                + jnp.where((feat_t >= aug0[hh]) & (feat_t < aug0[hh] + 3), -1.0, 0.0).astype(BF16)
                for hh in range(2)]

    def scores(qh, kb, dst):
        for hh in range(2):
            st = jnp.dot(ka_sc[hh, kb * t:(kb + 1) * t, :], qh[hh],
                         preferred_element_type=F32)
            dst[0][hh] = st
            dst[1][hh] = jnp.max(st, axis=0, keepdims=True)

    def softmax_pv(qi, kb, cur):
        for hh in range(2):
            st = cur[0][hh]
            if kb == qi:
                st = jnp.where(key <= qry, st, -jnp.inf)
                st_max = jnp.max(st, axis=0, keepdims=True)
            else:
                st_max = cur[1][hh]
            c = r_ref[r_base[hh] + kb] - r_ref[r_base[hh] + qi]
            m_old = m_sc[hh]
            m_new = jnp.maximum(m_old, st_max - c)
            alpha = jnp.exp2(m_old - m_new)
            p = jnp.exp2(st - (m_new + c)).astype(BF16)
            vt1 = jnp.concatenate(
                [vt_ref[hh * HEAD_DIM:(hh + 1) * HEAD_DIM, kb * t:(kb + 1) * t], ones], axis=0)
            acc_sc[hh] = alpha * acc_sc[hh] + jnp.dot(vt1, p, preferred_element_type=F32)
            m_sc[hh] = m_new

    pairs = [(qi, kb) for qi in range(n_blocks) for kb in range(qi + 1)]
    qh = queries(0)
    scores(qh, 0, bufs[0])
    for s, (qi, kb) in enumerate(pairs):
        if kb == 0:
            m_sc[...] = jnp.full_like(m_sc, -jnp.inf)
            acc_sc[...] = jnp.zeros_like(acc_sc)
        if s + 1 < len(pairs):
            qi_n, kb_n = pairs[s + 1]
            if qi_n != qi:
                qh = queries(qi_n)
            scores(qh, kb_n, bufs[(s + 1) % 2])
        softmax_pv(qi, kb, bufs[s % 2])
        if kb == qi:
            outs = [acc_sc[hh, :HEAD_DIM, :] / acc_sc[hh, HEAD_DIM:HEAD_DIM + 1, :]
                    for hh in range(2)]
            o_ref[0, qi * t:(qi + 1) * t, :] = jnp.concatenate(outs, axis=0).T.astype(o_ref.dtype)


def _fox_call(r_flat, q_t, k, v_t, g_cum, cast=()):
    batch, seq, d = k.shape
    n_pairs = d // LANES
    t = _FOX_T
    c_views, c_in, c_out, c_shapes = _ride_along(
        cast, batch * n_pairs, lambda b, h: b * n_pairs + h)
    return pl.pallas_call(
        functools.partial(_fox_kernel, n_cast=len(cast)),
        out_shape=[jax.ShapeDtypeStruct((batch, seq, d), BF16)] + c_shapes,
        grid=(batch, n_pairs),
        in_specs=[
            pl.BlockSpec(memory_space=pltpu.SMEM),
            pl.BlockSpec((LANES, seq), lambda b, h: (h, b)),
            pl.BlockSpec((1, seq, LANES), lambda b, h: (b, 0, h)),
            pl.BlockSpec((LANES, seq), lambda b, h: (h, b)),
            pl.BlockSpec((1, seq, g_cum.shape[2]), lambda b, h: (b, 0, 0)),
        ] + c_in,
        out_specs=[pl.BlockSpec((1, seq, LANES), lambda b, h: (b, 0, h))] + c_out,
        scratch_shapes=[pltpu.VMEM((2, seq, LANES), BF16), pltpu.VMEM((2, 1, t), F32),
                        pltpu.VMEM((2, HEAD_DIM + _ONES_ROWS, t), F32),
                        pltpu.VMEM((2, t, t), F32), pltpu.VMEM((2, t, t), F32),
                        pltpu.VMEM((2, 1, t), F32), pltpu.VMEM((2, 1, t), F32)],
        compiler_params=_params(("parallel", "parallel")),
        name="fox_attention",
    )(r_flat, q_t, k, v_t, g_cum, *c_views)


def _out_kernel(*refs, tiles_per_batch, with_router):
    o_ref, w_ref, x_ref, gt_ref = refs[:4]
    b = pl.program_id(0) // tiles_per_batch
    y = jnp.dot(o_ref[...], w_ref[...], preferred_element_type=F32)
    xn = x_ref[...] + gt_ref[pl.ds(b, 1), :] * y
    if not with_router:
        refs[4][...] = xn
        return
    g_ref, sh_ref, sc_ref, wr_ref, br_ref, xo_ref, h_ref, route_ref, cnt_ref = refs[4:]
    xo_ref[...] = xn
    h = _rms_mod(xn, g_ref[...], sh_ref[pl.ds(b, 1), :], sc_ref[pl.ds(b, 1), :])
    h_ref[...] = h
    logits = _dot_split(*_split_bf16(h), wr_ref) + br_ref[...]
    tm = logits.shape[0]
    lane = lax.broadcasted_iota(jnp.int32, logits.shape, 1).astype(F32)
    logits = jnp.where(lane < N_EXPERTS, logits, -jnp.inf)
    m1 = jnp.max(logits, axis=1, keepdims=True)
    i1 = jnp.min(jnp.where(logits == m1, lane, float(LANES)), axis=1, keepdims=True)
    rest = jnp.where(lane == i1, -jnp.inf, logits)
    m2 = jnp.max(rest, axis=1, keepdims=True)
    i2 = jnp.min(jnp.where(rest == m2, lane, float(LANES)), axis=1, keepdims=True)
    e2 = jnp.exp(m2 - m1)
    den = 1.0 + e2

    @pl.when(pl.program_id(0) == 0)
    def _():
        cnt_ref[...] = jnp.zeros_like(cnt_ref)

    sel1 = lane == i1
    sel2 = lane == i2
    onehot = jnp.where(sel1 | sel2, 1.0, 0.0)
    r = lax.broadcasted_iota(jnp.int32, (tm, tm), 0)
    c = lax.broadcasted_iota(jnp.int32, (tm, tm), 1)
    lower = jnp.where(c < r, 1.0, 0.0).astype(BF16)
    before = jnp.dot(lower, onehot.astype(BF16), preferred_element_type=F32) + cnt_ref[0:1, :]
    rank1 = jnp.sum(jnp.where(sel1, before, 0.0), axis=1, keepdims=True)
    rank2 = jnp.sum(jnp.where(sel2, before, 0.0), axis=1, keepdims=True)
    cnt_ref[0:1, :] = cnt_ref[0:1, :] + jnp.sum(onehot, axis=0, keepdims=True)
    route = jnp.where(lane == 0.0, i1, 0.0)
    for k, val in enumerate((i2, 1.0 / den, e2 / den, rank1, rank2), start=1):
        route = jnp.where(lane == float(k), val, route)
    route_ref[...] = route


def _out_call(o, w_out, x, mod, gate_col, seq, router=None, tm=512):
    n, d = x.shape
    with_router = router is not None
    in_specs = [
        pl.BlockSpec((tm, d), lambda i: (i, 0)),
        pl.BlockSpec((d, d), lambda i: (0, 0)),
        pl.BlockSpec((tm, d), lambda i: (i, 0)),
        pl.BlockSpec((SUBLANES, d), lambda i: (0, gate_col)),
    ]
    args = [o, w_out, x, mod]
    out_shape = [jax.ShapeDtypeStruct((n, d), F32)]
    out_specs = [pl.BlockSpec((tm, d), lambda i: (i, 0))]
    if with_router:
        g, sh_col, sc_col, w_r, b_r = router
        in_specs += [
            pl.BlockSpec((1, d), lambda i: (0, 0)),
            pl.BlockSpec((SUBLANES, d), lambda i: (0, sh_col)),
            pl.BlockSpec((SUBLANES, d), lambda i: (0, sc_col)),
            pl.BlockSpec((2, d, LANES), lambda i: (0, 0, 0)),
            pl.BlockSpec((1, LANES), lambda i: (0, 0)),
        ]
        args += [g.reshape(1, d), mod, mod, w_r, b_r]
        out_shape += [jax.ShapeDtypeStruct((n, d), F32),
                      jax.ShapeDtypeStruct((n, LANES), F32),
                      jax.ShapeDtypeStruct((SUBLANES, LANES), F32)]
        out_specs += [pl.BlockSpec((tm, d), lambda i: (i, 0)),
                      pl.BlockSpec((tm, LANES), lambda i: (i, 0)),
                      pl.BlockSpec((SUBLANES, LANES), lambda i: (0, 0))]
    res = pl.pallas_call(
        functools.partial(_out_kernel, tiles_per_batch=seq // tm, with_router=with_router),
        out_shape=out_shape,
        grid=(n // tm,),
        in_specs=in_specs,
        out_specs=out_specs,
        compiler_params=_params(("arbitrary",) if with_router else ("parallel",)),
        name="out_proj_router" if with_router else "out_proj",
    )(*args)
    return res if with_router else res[0]


_FF_TILE = 1792


def _snake(i, j, nj):
    return jnp.where(i % 2 == 0, j, nj - 1 - j)


def _ffn_kernel(*refs, tiles_per_batch, n_cast):
    x_ref, g_ref, sh_ref, sc_ref, gt_ref, wg_ref, wu_ref, wd_ref = refs[:8]
    o_ref = refs[8 + n_cast]
    h_sc = refs[-1]
    _ride_along_cast(refs[8:8 + n_cast], refs[9 + n_cast:9 + 2 * n_cast])
    b = pl.program_id(0) // tiles_per_batch

    @pl.when(pl.program_id(1) == 0)
    def _():
        x = x_ref[...]
        h = _rms_mod(x, g_ref[...], sh_ref[pl.ds(b, 1), :], sc_ref[pl.ds(b, 1), :])
        h_sc[...] = h.astype(BF16)
        o_ref[...] = x

    h = h_sc[...]
    g = jnp.dot(h, wg_ref[...], preferred_element_type=F32)
    u = jnp.dot(h, wu_ref[...], preferred_element_type=F32)
    a = (_silu(g) * u).astype(BF16)
    o_ref[...] += gt_ref[pl.ds(b, 1), :] * jnp.dot(a, wd_ref[...], preferred_element_type=F32)


def _ffn_call(x, g, mod, w_gu, w_down, seq, cast=(), tm=512, tf=_FF_TILE):
    n, d = x.shape
    f = w_down.shape[0]
    nj = f // tf
    c_views, c_in, c_out, c_shapes = _ride_along(cast, (n // tm) * nj, lambda i, j: i * nj + j)
    return pl.pallas_call(
        functools.partial(_ffn_kernel, tiles_per_batch=seq // tm, n_cast=len(cast)),
        out_shape=[jax.ShapeDtypeStruct((n, d), F32)] + c_shapes,
        grid=(n // tm, nj),
        in_specs=[
            pl.BlockSpec((tm, d), lambda i, j: (i, 0)),
            pl.BlockSpec((1, d), lambda i, j: (0, 0)),
            pl.BlockSpec((SUBLANES, d), lambda i, j: (0, 3)),
            pl.BlockSpec((SUBLANES, d), lambda i, j: (0, 4)),
            pl.BlockSpec((SUBLANES, d), lambda i, j: (0, 5)),
            pl.BlockSpec((d, tf), lambda i, j: (0, _snake(i, j, nj))),
            pl.BlockSpec((d, tf), lambda i, j: (0, _snake(i, j, nj) + nj)),
            pl.BlockSpec((tf, d), lambda i, j: (_snake(i, j, nj), 0)),
        ] + c_in,
        out_specs=[pl.BlockSpec((tm, d), lambda i, j: (i, 0))] + c_out,
        scratch_shapes=[pltpu.VMEM((tm, d), BF16)],
        compiler_params=_params(("parallel", "arbitrary")),
        name="ffn_swiglu",
    )(x, g.reshape(1, d), mod, mod, mod, w_gu, w_gu, w_down, *c_views)


_MOE_TM = 512


def _dispatch_kernel(d1_ref, d2_ref, zero_ref, h_ref, xs_ref, z_sc, sem):
    tm = h_ref.shape[0]
    base = pl.program_id(0) * tm

    @pl.when(pl.program_id(0) == 0)
    def _():
        z_sc[...] = jnp.zeros_like(z_sc)

        def zero_copy(t):
            return pltpu.make_async_copy(z_sc, xs_ref.at[pl.ds(pl.multiple_of(t * tm, tm), tm)],
                                         sem.at[0])

        def start(t, carry):
            @pl.when(zero_ref[t] != 0)
            def _():
                zero_copy(t).start()
            return carry

        def wait(t, carry):
            @pl.when(zero_ref[t] != 0)
            def _():
                zero_copy(t).wait()
            return carry

        lax.fori_loop(0, zero_ref.shape[0], start, 0)
        lax.fori_loop(0, zero_ref.shape[0], wait, 0)

    def issue(r, carry):
        src = h_ref.at[pl.ds(r, 1)]
        pltpu.make_async_copy(src, xs_ref.at[pl.ds(d1_ref[base + r], 1)], sem.at[0]).start()
        pltpu.make_async_copy(src, xs_ref.at[pl.ds(d2_ref[base + r], 1)], sem.at[1]).start()
        return carry

    lax.fori_loop(0, tm, issue, 0, unroll=8)
    pltpu.make_async_copy(h_ref, xs_ref.at[pl.ds(0, tm)], sem.at[0]).wait()
    pltpu.make_async_copy(h_ref, xs_ref.at[pl.ds(0, tm)], sem.at[1]).wait()


def _dispatch_call(dest1, dest2, zero_tile, h, tm=_MOE_TM):
    n, d = h.shape
    n_rows = zero_tile.shape[0] * tm
    return pl.pallas_call(
        _dispatch_kernel,
        out_shape=jax.ShapeDtypeStruct((n_rows, d), h.dtype),
        grid_spec=pltpu.PrefetchScalarGridSpec(
            num_scalar_prefetch=3,
            grid=(n // tm,),
            in_specs=[pl.BlockSpec((tm, d), lambda i, d1, d2, zt: (i, 0))],
            out_specs=pl.BlockSpec(memory_space=pl.ANY),
            scratch_shapes=[pltpu.VMEM((tm, d), h.dtype), pltpu.SemaphoreType.DMA((2,))],
        ),
        compiler_params=_params(("arbitrary",)),
        name="moe_dispatch",
    )(dest1, dest2, zero_tile, h)


def _experts_kernel(te_ref, nt_ref, xs_ref, wg_ref, wu_ref, wd_ref, ye_ref):
    del te_ref
    t = pl.program_id(0)

    @pl.when(pl.program_id(1) == 0)
    def _():
        ye_ref[...] = jnp.zeros_like(ye_ref)

    @pl.when(t < nt_ref[0])
    def _():
        h = xs_ref[...].astype(BF16)
        g = jnp.dot(h, wg_ref[0], preferred_element_type=F32)
        u = jnp.dot(h, wu_ref[0], preferred_element_type=F32)
        a = (_silu(g) * u).astype(BF16)
        ye_ref[...] += jnp.dot(a, wd_ref[0], preferred_element_type=F32)


def _experts_call(tile_expert, n_tiles, xs, w_gu, w_down, tf=_FF_TILE):
    n_rows, d = xs.shape
    _, f, _ = w_down.shape
    nj = f // tf
    tm = _MOE_TM
    tile = lambda t, nt: jnp.maximum(jnp.minimum(t, nt[0] - 1), 0)
    jj = lambda t, j, nt: _snake(tile(t, nt), jnp.where(t < nt[0], j, nj - 1), nj)
    return pl.pallas_call(
        _experts_kernel,
        out_shape=jax.ShapeDtypeStruct((n_rows, d), F32),
        grid_spec=pltpu.PrefetchScalarGridSpec(
            num_scalar_prefetch=2,
            grid=(n_rows // tm, nj),
            in_specs=[
                pl.BlockSpec((tm, d), lambda t, j, te, nt: (tile(t, nt), 0)),
                pl.BlockSpec((1, d, tf), lambda t, j, te, nt: (te[tile(t, nt)], 0, jj(t, j, nt))),
                pl.BlockSpec((1, d, tf), lambda t, j, te, nt: (te[tile(t, nt)], 0, jj(t, j, nt) + nj)),
                pl.BlockSpec((1, tf, d), lambda t, j, te, nt: (te[tile(t, nt)], jj(t, j, nt), 0)),
            ],
            out_specs=pl.BlockSpec((tm, d), lambda t, j, te, nt: (t, 0)),
        ),
        compiler_params=_params(("arbitrary", "arbitrary")),
        name="moe_experts",
    )(tile_expert, n_tiles, xs, w_gu, w_gu, w_down)


def _combine_kernel(d1_ref, d2_ref, ye_ref, x_ref, route_ref, gt_ref, gf_ref, o_ref,
                    y1_sc, y2_sc, sem, *, tiles_per_batch):
    tm = x_ref.shape[0]
    i = pl.program_id(0)
    b = i // tiles_per_batch
    base = i * tm

    def issue(r, carry):
        pltpu.make_async_copy(ye_ref.at[pl.ds(d1_ref[base + r], 1)], y1_sc.at[pl.ds(r, 1)],
                              sem.at[0]).start()
        pltpu.make_async_copy(ye_ref.at[pl.ds(d2_ref[base + r], 1)], y2_sc.at[pl.ds(r, 1)],
                              sem.at[1]).start()
        return carry

    lax.fori_loop(0, tm, issue, 0, unroll=8)
    pltpu.make_async_copy(ye_ref.at[pl.ds(0, tm)], y1_sc, sem.at[0]).wait()
    pltpu.make_async_copy(ye_ref.at[pl.ds(0, tm)], y2_sc, sem.at[1]).wait()
    route = route_ref[...]
    y = route[:, 2:3] * y1_sc[...] + route[:, 3:4] * y2_sc[...]
    xn = x_ref[...] + gt_ref[pl.ds(b, 1), :] * y
    ms = jnp.mean(xn * xn, axis=-1, keepdims=True)
    o_ref[...] = xn * lax.rsqrt(ms + EPS) * gf_ref[...]


def _combine_call(dest1, dest2, ye, x, route, mod, g_final, seq, tm=512):
    n, d = x.shape
    return pl.pallas_call(
        functools.partial(_combine_kernel, tiles_per_batch=seq // tm),
        out_shape=jax.ShapeDtypeStruct((n, d), F32),
        grid_spec=pltpu.PrefetchScalarGridSpec(
            num_scalar_prefetch=2,
            grid=(n // tm,),
            in_specs=[
                pl.BlockSpec(memory_space=pl.ANY),
                pl.BlockSpec((tm, d), lambda i, d1, d2: (i, 0)),
                pl.BlockSpec((tm, LANES), lambda i, d1, d2: (i, 0)),
                pl.BlockSpec((SUBLANES, d), lambda i, d1, d2: (0, 5)),
                pl.BlockSpec((1, d), lambda i, d1, d2: (0, 0)),
            ],
            out_specs=pl.BlockSpec((tm, d), lambda i, d1, d2: (i, 0)),
            scratch_shapes=[pltpu.VMEM((tm, d), F32), pltpu.VMEM((tm, d), F32),
                            pltpu.SemaphoreType.DMA((2,))],
        ),
        compiler_params=_params(("arbitrary",)),
        name="moe_combine",
    )(dest1, dest2, ye, x, route, mod, g_final.reshape(1, d))


def _moe_call(h, route, counts, x, mod, g_final, w_gu, w_down, seq):
    n, d = x.shape
    ne = w_down.shape[0]
    tm = _MOE_TM
    max_tiles = (2 * n) // tm + ne
    e1 = route[:, 0].astype(jnp.int32)
    e2 = route[:, 1].astype(jnp.int32)
    cnt = counts[0, :ne].astype(jnp.int32)
    tiles_e = (cnt + tm - 1) // tm
    tile_end = jnp.cumsum(tiles_e)
    row_start = (tile_end - tiles_e) * tm
    dest1 = row_start[e1] + route[:, 4].astype(jnp.int32)
    dest2 = row_start[e2] + route[:, 5].astype(jnp.int32)
    n_tiles = tile_end[-1:]
    tile_ids = jnp.arange(max_tiles, dtype=jnp.int32)
    tile_expert = jnp.minimum(
        jnp.sum((tile_ids[:, None] >= tile_end[None, :]).astype(jnp.int32), axis=1), ne - 1)
    is_last = jnp.any((tile_ids[:, None] == tile_end[None, :] - 1) & (tiles_e[None, :] > 0), axis=1)
    zero_tile = (is_last | (tile_ids >= n_tiles[0])).astype(jnp.int32)
    xs = _dispatch_call(dest1, dest2, zero_tile, h)
    ye = _experts_call(tile_expert, n_tiles, xs, w_gu, w_down)
    return _combine_call(dest1, dest2, ye, x, route, mod, g_final, seq)


_SWA_TQ = 2 * CHUNK
_SWA_BAND = 2 * _SWA_TQ


def _swa_bucket_tiles():
    cc = np.arange(_SWA_BAND)[:, None]
    r = np.arange(_SWA_TQ)[None, :]
    rel = cc - _SWA_TQ - r
    nb = REL_BUCKETS // 2
    max_exact = nb // 2
    ret = (rel > 0).astype(np.int32) * nb
    n = np.abs(rel)
    large = max_exact + (np.log(np.maximum(n, 1) / max_exact)
                         / np.log(REL_MAX_DIST / max_exact) * (nb - max_exact)).astype(np.int32)
    large = np.minimum(large, nb - 1)
    bucket = (ret + np.where(n < max_exact, n, large)).astype(np.int32)
    q_chunk = r // CHUNK
    k_chunk = cc // CHUNK
    visible = (k_chunk >= q_chunk) & (k_chunk <= q_chunk + WINDOW_CHUNKS)
    later = np.where(visible, bucket, -1)
    first = np.where(cc >= _SWA_TQ, later, -1)
    return np.stack([first, later]).astype(np.int32)


def _swa_bias_kernel(tbl_ref, bkt_ref, o_ref):
    n_heads = o_ref.shape[1]
    for v in range(2):
        bkt = bkt_ref[v]
        for head in range(n_heads):
            tile = jnp.full(bkt.shape, NEG_BIG, F32)
            for bk in range(REL_BUCKETS):
                tile = jnp.where(bkt == bk, tbl_ref[head, bk] * LOG2E, tile)
            o_ref[v, head] = tile


def _swa_bias_call(rel_bias):
    bkt = jnp.asarray(_swa_bucket_tiles())
    n_heads = rel_bias.shape[1]
    return pl.pallas_call(
        _swa_bias_kernel,
        out_shape=jax.ShapeDtypeStruct((2, n_heads, _SWA_BAND, _SWA_TQ), F32),
        in_specs=[
            pl.BlockSpec(memory_space=pltpu.SMEM),
            pl.BlockSpec(memory_space=pltpu.VMEM),
        ],
        out_specs=pl.BlockSpec(memory_space=pltpu.VMEM),
        name="swa_bias",
    )(rel_bias.T, bkt)


def _swa_kernel(qt_ref, kp_ref, kc_ref, vtp_ref, vtc_ref, bias_ref, sink_ref, o_ref):
    tq = _SWA_TQ
    lane = lax.broadcasted_iota(jnp.int32, (1, LANES), 1)
    ones = jnp.ones((_ONES_ROWS, _SWA_BAND), BF16)
    outs = [None] * (SWA_KV_HEADS * SWA_GROUP)
    units = [(hk, par) for hk in range(SWA_KV_HEADS) for par in range(2)]

    def scores(hk, par):
        ksl = slice(hk * LANES, (hk + 1) * LANES)
        kb = jnp.concatenate([kp_ref[0, :, ksl], kc_ref[0, :, ksl]], axis=0)
        f0 = hk * SWA_GROUP * HEAD_DIM
        wq = jnp.concatenate([qt_ref[f0:f0 + LANES, :], qt_ref[f0 + LANES:f0 + 2 * LANES, :]],
                             axis=1)
        head_lanes = (lane < HEAD_DIM) if par == 0 else (lane >= HEAD_DIM)
        return jnp.dot(jnp.where(head_lanes, kb, 0), wq, preferred_element_type=F32)

    sts = [scores(hk, par) for hk, par in units]
    for (hk, par), st in zip(units, sts):
        vsl = slice(hk * HEAD_DIM, (hk + 1) * HEAD_DIM)
        vt1 = jnp.concatenate(
            [jnp.concatenate([vtp_ref[vsl, :], vtc_ref[vsl, :]], axis=1), ones], axis=0)
        heads = (hk * SWA_GROUP + par, hk * SWA_GROUP + par + 2)
        ps, ms = [], []
        for i, head in enumerate(heads):
            s = st[:, i * tq:(i + 1) * tq] + bias_ref[0, head]
            m = jnp.maximum(jnp.max(s, axis=0, keepdims=True), sink_ref[head])
            ps.append(jnp.exp2(s - m).astype(BF16))
            ms.append(m)
        acc = jnp.dot(vt1, jnp.concatenate(ps, axis=1), preferred_element_type=F32)
        for i, head in enumerate(heads):
            a = acc[:, i * tq:(i + 1) * tq]
            den = a[HEAD_DIM:HEAD_DIM + 1] + jnp.exp2(sink_ref[head] - ms[i])
            outs[head] = a[:HEAD_DIM] / den
    o_ref[0] = jnp.concatenate(outs, axis=0).T.astype(o_ref.dtype)


def _swa_call(q_t, k, v_t, bias, sink, batch, seq):
    d = q_t.shape[0]
    kw = k.shape[2]
    vw = v_t.shape[0]
    tq = _SWA_TQ
    nq = seq // tq
    return pl.pallas_call(
        _swa_kernel,
        out_shape=jax.ShapeDtypeStruct((batch, seq, d), BF16),
        grid=(batch, nq),
        in_specs=[
            pl.BlockSpec((d, tq), lambda b, i: (0, b * nq + i)),
            pl.BlockSpec((1, tq, kw), lambda b, i: (b, jnp.maximum(i - 1, 0), 0)),
            pl.BlockSpec((1, tq, kw), lambda b, i: (b, i, 0)),
            pl.BlockSpec((vw, tq), lambda b, i: (0, b * nq + jnp.maximum(i - 1, 0))),
            pl.BlockSpec((vw, tq), lambda b, i: (0, b * nq + i)),
            pl.BlockSpec((1,) + bias.shape[1:], lambda b, i: (jnp.minimum(i, 1), 0, 0, 0)),
            pl.BlockSpec(sink.shape, lambda b, i: (0, 0, 0)),
        ],
        out_specs=pl.BlockSpec((1, tq, d), lambda b, i: (b, i, 0)),
        compiler_params=_params(("parallel", "arbitrary")),
        name="swa_attention",
    )(q_t, k, k, v_t, v_t, bias, sink)


def kernel(x, c, w_ada, b_ada, g_norm_mix, g_norm_ffn, g_final, fox_w_in, fox_b_f, fox_w_out, swa_w_in, swa_sinks, swa_w_out, rel_bias, ffn_w_gu, ffn_w_down, moe_w_router, moe_b_router, moe_w_gu, moe_w_down):
    batch, seq, d = x.shape
    n = batch * seq
    q_scale = HEAD_DIM ** -0.5
    xf = x.reshape(n, d)

    c_pad = jnp.zeros((SUBLANES, d), F32).at[:batch].set(c)
    mod = _ada_call(c_pad, w_ada, b_ada)
    mod0, mod1 = mod[0], mod[1]

    w_in = fox_w_in[0]
    n_heads = d // HEAD_DIM
    k, q_t, v_t, f, w_gu0, w_down0, w_out0, w_out1 = _attn_proj_call(
        xf, g_norm_mix[0], mod0, w_in[:, d:2 * d].astype(BF16),
        (w_in[:, :d] * (q_scale * LOG2E)).T.astype(BF16), w_in[:, 2 * d:3 * d].T.astype(BF16),
        seq, w_f=jnp.stack(_split_bf16(w_in[:, 3 * d:])),
        cast=(ffn_w_gu[0], ffn_w_down[0], fox_w_out[0], swa_w_out[0]))
    g_cum, r_cum = _cum_call(f, fox_b_f[0], batch, seq)
    r_flat = r_cum.transpose(0, 2, 1).reshape(-1)
    o, w_down1 = _fox_call(r_flat, q_t, k.reshape(batch, seq, d), v_t, g_cum,
                           cast=(moe_w_down[0],))
    w_down1 = w_down1.reshape(moe_w_down.shape[1:])
    x1 = _out_call(o.reshape(n, d), w_out0, xf, mod0, 2, seq)
    x2, w_gu1 = _ffn_call(x1, g_norm_ffn[0], mod0, w_gu0, w_down0, seq, cast=(moe_w_gu[0],))
    w_gu1 = w_gu1.reshape(moe_w_gu.shape[1:])

    w_in = swa_w_in[0]
    kvw = SWA_KV_HEADS * HEAD_DIM
    dup = lambda w: jnp.repeat(w.reshape(d, SWA_KV_HEADS, 1, HEAD_DIM), 2, axis=2).reshape(d, 2 * kvw)
    k, q_t, v_t = _attn_proj_call(
        x2, g_norm_mix[1], mod1, dup(w_in[:, d:d + kvw]).astype(BF16),
        (w_in[:, :d] * (q_scale * LOG2E)).T.astype(BF16), w_in[:, d + kvw:].T.astype(BF16), seq)
    bias = _swa_bias_call(rel_bias)
    sink = jnp.broadcast_to((swa_sinks[0] * LOG2E)[:, None, None], (n_heads, 1, _SWA_TQ))
    o = _swa_call(q_t, k.reshape(batch, seq, 2 * kvw), v_t, bias, sink, batch, seq)
    w_r = jnp.stack(_split_bf16(jnp.zeros((d, LANES), F32).at[:, :N_EXPERTS].set(moe_w_router[0])))
    b_r = jnp.zeros((1, LANES), F32).at[0, :N_EXPERTS].set(moe_b_router[0])
    x3, h4, route, counts = _out_call(o.reshape(n, d), w_out1, x2, mod1, 2, seq,
                                      router=(g_norm_ffn[1], 3, 4, w_r, b_r))
    out = _moe_call(h4, route, counts, x3, mod1, g_final, w_gu1, w_down1, seq)
    return out.reshape(batch, seq, d)
```

```python
import functools

import numpy as np
import jax
import jax.numpy as jnp
from jax import lax
from jax.experimental import pallas as pl
from jax.experimental.pallas import tpu as pltpu

F32 = jnp.float32
BF16 = jnp.bfloat16
HIGHEST = lax.Precision.HIGHEST

HEAD_DIM = 64
CHUNK = 64
WINDOW_CHUNKS = 2
REL_BUCKETS = 32
REL_MAX_DIST = 128
SWA_KV_HEADS = 4
SWA_GROUP = 4
N_EXPERTS = 8
EPS = 1e-6

LANES = 128
SUBLANES = 8
VMEM_LIMIT = 56 * 1024 * 1024

NEG_BIG = -1e30


def _params(sem, vmem=VMEM_LIMIT):
    return pltpu.CompilerParams(dimension_semantics=sem, vmem_limit_bytes=vmem)


def _rms_mod(x, g, shift, scale):
    ms = jnp.mean(x * x, axis=-1, keepdims=True)
    y = x * lax.rsqrt(ms + EPS) * g
    return y * (1.0 + scale) + shift


def _silu(x):
    return x / (1.0 + jnp.exp(-x))


def _split_bf16(x):
    hi = x.astype(BF16)
    return hi, (x - hi.astype(F32)).astype(BF16)


def _dot_split(x_hi, x_lo, w_ref):
    return (jnp.dot(x_hi, w_ref[0], preferred_element_type=F32)
            + jnp.dot(x_lo, w_ref[0], preferred_element_type=F32)
            + jnp.dot(x_hi, w_ref[1], preferred_element_type=F32))


_BF16_SUBLANES = 16


def _ride_along(arrays, n_steps, step_index):
    views, in_specs, out_specs, out_shapes = [], [], [], []
    for a in arrays:
        v = a.reshape(-1, a.shape[-1])
        rows, rem = divmod(v.shape[0], n_steps)
        assert rem == 0 and rows % _BF16_SUBLANES == 0, v.shape
        spec = pl.BlockSpec((rows, v.shape[1]), lambda *g: (step_index(*g), 0))
        views.append(v)
        in_specs.append(spec)
        out_specs.append(spec)
        out_shapes.append(jax.ShapeDtypeStruct(v.shape, BF16))
    return views, in_specs, out_specs, out_shapes


def _ride_along_cast(in_refs, out_refs):
    for src, dst in zip(in_refs, out_refs):
        dst[...] = src[...].astype(BF16)


def _ada_kernel(c_ref, w_ref, b_ref, o_ref):
    cond = _silu(c_ref[...])
    o_ref[0] = jnp.dot(cond, w_ref[0], preferred_element_type=F32,
                       precision=HIGHEST) + b_ref[0]


def _ada_call(c_pad, w_ada, b_ada):
    depth, d, n = w_ada.shape
    tn = 1536
    return pl.pallas_call(
        _ada_kernel,
        out_shape=jax.ShapeDtypeStruct((depth, SUBLANES, n), F32),
        grid=(depth, n // tn),
        in_specs=[
            pl.BlockSpec((SUBLANES, d), lambda l, j: (0, 0)),
            pl.BlockSpec((1, d, tn), lambda l, j: (l, 0, j)),
            pl.BlockSpec((1, 1, tn), lambda l, j: (l, 0, j)),
        ],
        out_specs=pl.BlockSpec((1, SUBLANES, tn), lambda l, j: (l, 0, j)),
        compiler_params=_params(("parallel", "parallel")),
        name="ada_mod",
    )(c_pad, w_ada, b_ada.reshape(depth, 1, n))


_NT = (((1,), (1,)), ((), ()))


def _attn_proj_kernel(*refs, tiles_per_batch, with_gate, n_cast):
    x_ref, g_ref, sh_ref, sc_ref, wk_ref, wqt_ref, wvt_ref = refs[:7]
    n_in = 7 + with_gate + n_cast
    k_ref, qt_ref, vt_ref = refs[n_in:n_in + 3]
    b = pl.program_id(0) // tiles_per_batch
    h = _rms_mod(x_ref[...], g_ref[...], sh_ref[pl.ds(b, 1), :], sc_ref[pl.ds(b, 1), :])
    hb = h.astype(BF16)
    k_ref[...] = jnp.dot(hb, wk_ref[...], preferred_element_type=F32).astype(BF16)
    qt_ref[...] = lax.dot_general(wqt_ref[...], hb, _NT, preferred_element_type=F32).astype(BF16)
    vt_ref[...] = lax.dot_general(wvt_ref[...], hb, _NT, preferred_element_type=F32).astype(BF16)
    if with_gate:
        refs[n_in + 3][...] = _dot_split(hb, (h - hb.astype(F32)).astype(BF16), refs[7])
    _ride_along_cast(refs[n_in - n_cast:n_in], refs[len(refs) - n_cast:])


def _attn_proj_call(x, g, mod, w_k, w_qt, w_vt, seq, w_f=None, cast=(), tm=512):
    n, d = x.shape
    with_gate = w_f is not None
    full = lambda a: pl.BlockSpec(a.shape, lambda i: (0, 0))
    c_views, c_in, c_out, c_shapes = _ride_along(cast, n // tm, lambda i: i)
    in_specs = [
        pl.BlockSpec((tm, d), lambda i: (i, 0)),
        pl.BlockSpec((1, d), lambda i: (0, 0)),
        pl.BlockSpec((SUBLANES, d), lambda i: (0, 0)),
        pl.BlockSpec((SUBLANES, d), lambda i: (0, 1)),
        full(w_k), full(w_qt), full(w_vt),
    ]
    args = [x, g.reshape(1, d), mod, mod, w_k, w_qt, w_vt]
    out_shape = [jax.ShapeDtypeStruct((n, w_k.shape[1]), BF16),
                 jax.ShapeDtypeStruct((w_qt.shape[0], n), BF16),
                 jax.ShapeDtypeStruct((w_vt.shape[0], n), BF16)]
    out_specs = [pl.BlockSpec((tm, w_k.shape[1]), lambda i: (i, 0)),
                 pl.BlockSpec((w_qt.shape[0], tm), lambda i: (0, i)),
                 pl.BlockSpec((w_vt.shape[0], tm), lambda i: (0, i))]
    if with_gate:
        in_specs.append(pl.BlockSpec(w_f.shape, lambda i: (0, 0, 0)))
        args.append(w_f)
        out_shape.append(jax.ShapeDtypeStruct((n, w_f.shape[2]), F32))
        out_specs.append(pl.BlockSpec((tm, w_f.shape[2]), lambda i: (i, 0)))
    return pl.pallas_call(
        functools.partial(_attn_proj_kernel, tiles_per_batch=seq // tm, with_gate=with_gate,
                          n_cast=len(cast)),
        out_shape=out_shape + c_shapes,
        grid=(n // tm,),
        in_specs=in_specs + c_in,
        out_specs=out_specs + c_out,
        compiler_params=_params(("parallel",)),
        name="attn_proj_gate" if with_gate else "attn_proj",
    )(*args, *c_views)


_FOX_T = 512
LOG2E = 1.4426950408889634


def _cum_kernel(f_ref, bf_ref, g_ref, r_ref):
    x = f_ref[...] + bf_ref[...]
    logf = (jnp.minimum(x, 0.0) - jnp.log(1.0 + jnp.exp(-jnp.abs(x)))) * LOG2E
    seq, nh = logf.shape
    r = lax.broadcasted_iota(jnp.int32, (_FOX_T, _FOX_T), 0)
    c = lax.broadcasted_iota(jnp.int32, (_FOX_T, _FOX_T), 1)
    lower = (c <= r).astype(F32)
    carry = jnp.zeros((1, nh), F32)
    for ch in range(seq // _FOX_T):
        rows = slice(ch * _FOX_T, (ch + 1) * _FOX_T)
        cs = jnp.dot(lower, logf[rows, :], preferred_element_type=F32, precision=HIGHEST)
        g_ref[0, rows, :] = cs
        r_ref[0, ch:ch + 1, :] = carry
        carry = carry + cs[_FOX_T - 1:_FOX_T, :]


def _cum_call(f, b_f, batch, seq):
    nh = f.shape[1]
    return pl.pallas_call(
        _cum_kernel,
        out_shape=[jax.ShapeDtypeStruct((batch, seq, nh), F32),
                   jax.ShapeDtypeStruct((batch, seq // _FOX_T, nh), F32)],
        grid=(batch,),
        in_specs=[
            pl.BlockSpec((seq, nh), lambda b: (b, 0)),
            pl.BlockSpec((1, nh), lambda b: (0, 0)),
        ],
        out_specs=[pl.BlockSpec((1, seq, nh), lambda b: (b, 0, 0)),
                   pl.BlockSpec((1, seq // _FOX_T, nh), lambda b: (b, 0, 0))],
        compiler_params=_params(("parallel",)),
        name="forget_cumsum",
    )(f, b_f.reshape(1, nh))


_ONES_ROWS = 16


def _fox_kernel(*refs, n_cast):
    r_ref, qt_ref, k_ref, vt_ref, g_ref = refs[:5]
    o_ref = refs[5 + n_cast]
    ka_sc, m_sc, acc_sc, sa_sc, sb_sc, xa_sc, xb_sc = refs[6 + 2 * n_cast:]
    _ride_along_cast(refs[5:5 + n_cast], refs[6 + n_cast:6 + 2 * n_cast])
    t = _FOX_T
    b, hp = pl.program_id(0), pl.program_id(1)
    seq = k_ref.shape[1]
    n_blocks = seq // t
    n_heads = 2 * pl.num_programs(1)

    lane = lax.broadcasted_iota(jnp.int32, (1, LANES), 1)
    feat = lax.broadcasted_iota(jnp.int32, (LANES, 1), 0)
    aug0 = [HEAD_DIM, 0]

    g_all = g_ref[0]
    head_col = lax.broadcasted_iota(jnp.int32, g_all.shape, 1)
    for hh in range(2):
        g_head = jnp.sum(jnp.where(head_col == 2 * hp + hh, g_all, 0.0), axis=1, keepdims=True)
        gb = jnp.broadcast_to(g_head, (seq, LANES))
        hi = gb.astype(BF16).astype(F32)
        mid = (gb - hi).astype(BF16).astype(F32)
        lo = gb - hi - mid
        aug = jnp.where(lane == aug0[hh], hi,
                        jnp.where(lane == aug0[hh] + 1, mid,
                                  jnp.where(lane == aug0[hh] + 2, lo, 0.0)))
        own = (lane >= hh * HEAD_DIM) & (lane < (hh + 1) * HEAD_DIM)
        ka_sc[hh] = jnp.where(own, k_ref[0].astype(F32), aug).astype(BF16)

    key = lax.broadcasted_iota(jnp.int32, (t, t), 0)
    qry = lax.broadcasted_iota(jnp.int32, (t, t), 1)
    ones = jnp.ones((_ONES_ROWS, t), BF16)
    feat_t = lax.broadcasted_iota(jnp.int32, (LANES, t), 0)
    r_base = [(b * n_heads + 2 * hp + hh) * n_blocks for hh in range(2)]
    bufs = ((sa_sc, xa_sc), (sb_sc, xb_sc))

    def queries(qi):
        qt2 = qt_ref[:, qi * t:(qi + 1) * t]
        return [jnp.where((feat >= hh * HEAD_DIM) & (feat < (hh + 1) * HEAD_DIM), qt2, 0)
                + jnp.where((feat_t >= aug0[hh]) & (feat_t < aug0[hh] + 3), -1.0, 0.0).astype(BF16)
                for hh in range(2)]

    def scores(qh, kb, dst):
        for hh in range(2):
            st = jnp.dot(ka_sc[hh, kb * t:(kb + 1) * t, :], qh[hh],
                         preferred_element_type=F32)
            dst[0][hh] = st
            dst[1][hh] = jnp.max(st, axis=0, keepdims=True)

    def softmax_pv(qi, kb, cur):
        for hh in range(2):
            st = cur[0][hh]
            if kb == qi:
                st = jnp.where(key <= qry, st, -jnp.inf)
                st_max = jnp.max(st, axis=0, keepdims=True)
            else:
                st_max = cur[1][hh]
            c = r_ref[r_base[hh] + kb] - r_ref[r_base[hh] + qi]
            m_old = m_sc[hh]
            m_new = jnp.maximum(m_old, st_max - c)
            alpha = jnp.exp2(m_old - m_new)
            p = jnp.exp2(st - (m_new + c)).astype(BF16)
            vt1 = jnp.concatenate(
                [vt_ref[hh * HEAD_DIM:(hh + 1) * HEAD_DIM, kb * t:(kb + 1) * t], ones], axis=0)
            acc_sc[hh] = alpha * acc_sc[hh] + jnp.dot(vt1, p, preferred_element_type=F32)
            m_sc[hh] = m_new

    pairs = [(qi, kb) for qi in range(n_blocks) for kb in range(qi + 1)]
    qh = queries(0)
    scores(qh, 0, bufs[0])
    for s, (qi, kb) in enumerate(pairs):
        if kb == 0:
            m_sc[...] = jnp.full_like(m_sc, -jnp.inf)
            acc_sc[...] = jnp.zeros_like(acc_sc)
        if s + 1 < len(pairs):
            qi_n, kb_n = pairs[s + 1]
            if qi_n != qi:
                qh = queries(qi_n)
            scores(qh, kb_n, bufs[(s + 1) % 2])
        softmax_pv(qi, kb, bufs[s % 2])
        if kb == qi:
            outs = [acc_sc[hh, :HEAD_DIM, :] / acc_sc[hh, HEAD_DIM:HEAD_DIM + 1, :]
                    for hh in range(2)]
            o_ref[0, qi * t:(qi + 1) * t, :] = jnp.concatenate(outs, axis=0).T.astype(o_ref.dtype)


def _fox_call(r_flat, q_t, k, v_t, g_cum, cast=()):
    batch, seq, d = k.shape
    n_pairs = d // LANES
    t = _FOX_T
    c_views, c_in, c_out, c_shapes = _ride_along(
        cast, batch * n_pairs, lambda b, h: b * n_pairs + h)
    return pl.pallas_call(
        functools.partial(_fox_kernel, n_cast=len(cast)),
        out_shape=[jax.ShapeDtypeStruct((batch, seq, d), BF16)] + c_shapes,
        grid=(batch, n_pairs),
        in_specs=[
            pl.BlockSpec(memory_space=pltpu.SMEM),
            pl.BlockSpec((LANES, seq), lambda b, h: (h, b)),
            pl.BlockSpec((1, seq, LANES), lambda b, h: (b, 0, h)),
            pl.BlockSpec((LANES, seq), lambda b, h: (h, b)),
            pl.BlockSpec((1, seq, g_cum.shape[2]), lambda b, h: (b, 0, 0)),
        ] + c_in,
        out_specs=[pl.BlockSpec((1, seq, LANES), lambda b, h: (b, 0, h))] + c_out,
        scratch_shapes=[pltpu.VMEM((2, seq, LANES), BF16), pltpu.VMEM((2, 1, t), F32),
                        pltpu.VMEM((2, HEAD_DIM + _ONES_ROWS, t), F32),
                        pltpu.VMEM((2, t, t), F32), pltpu.VMEM((2, t, t), F32),
                        pltpu.VMEM((2, 1, t), F32), pltpu.VMEM((2, 1, t), F32)],
        compiler_params=_params(("parallel", "parallel")),
        name="fox_attention",
    )(r_flat, q_t, k, v_t, g_cum, *c_views)


def _out_kernel(*refs, tiles_per_batch, with_router):
    o_ref, w_ref, x_ref, gt_ref = refs[:4]
    b = pl.program_id(0) // tiles_per_batch
    y = jnp.dot(o_ref[...], w_ref[...], preferred_element_type=F32)
    xn = x_ref[...] + gt_ref[pl.ds(b, 1), :] * y
    if not with_router:
        refs[4][...] = xn
        return
    g_ref, sh_ref, sc_ref, wr_ref, br_ref, xo_ref, h_ref, route_ref, cnt_ref = refs[4:]
    xo_ref[...] = xn
    h = _rms_mod(xn, g_ref[...], sh_ref[pl.ds(b, 1), :], sc_ref[pl.ds(b, 1), :])
    h_ref[...] = h
    logits = _dot_split(*_split_bf16(h), wr_ref) + br_ref[...]
    tm = logits.shape[0]
    lane = lax.broadcasted_iota(jnp.int32, logits.shape, 1).astype(F32)
    logits = jnp.where(lane < N_EXPERTS, logits, -jnp.inf)
    m1 = jnp.max(logits, axis=1, keepdims=True)
    i1 = jnp.min(jnp.where(logits == m1, lane, float(LANES)), axis=1, keepdims=True)
    rest = jnp.where(lane == i1, -jnp.inf, logits)
    m2 = jnp.max(rest, axis=1, keepdims=True)
    i2 = jnp.min(jnp.where(rest == m2, lane, float(LANES)), axis=1, keepdims=True)
    e2 = jnp.exp(m2 - m1)
    den = 1.0 + e2

    @pl.when(pl.program_id(0) == 0)
    def _():
        cnt_ref[...] = jnp.zeros_like(cnt_ref)

    sel1 = lane == i1
    sel2 = lane == i2
    onehot = jnp.where(sel1 | sel2, 1.0, 0.0)
    r = lax.broadcasted_iota(jnp.int32, (tm, tm), 0)
    c = lax.broadcasted_iota(jnp.int32, (tm, tm), 1)
    lower = jnp.where(c < r, 1.0, 0.0).astype(BF16)
    before = jnp.dot(lower, onehot.astype(BF16), preferred_element_type=F32) + cnt_ref[0:1, :]
    rank1 = jnp.sum(jnp.where(sel1, before, 0.0), axis=1, keepdims=True)
    rank2 = jnp.sum(jnp.where(sel2, before, 0.0), axis=1, keepdims=True)
    cnt_ref[0:1, :] = cnt_ref[0:1, :] + jnp.sum(onehot, axis=0, keepdims=True)
    route = jnp.where(lane == 0.0, i1, 0.0)
    for k, val in enumerate((i2, 1.0 / den, e2 / den, rank1, rank2), start=1):
        route = jnp.where(lane == float(k), val, route)
    route_ref[...] = route


def _out_call(o, w_out, x, mod, gate_col, seq, router=None, tm=512):
    n, d = x.shape
    with_router = router is not None
    in_specs = [
        pl.BlockSpec((tm, d), lambda i: (i, 0)),
        pl.BlockSpec((d, d), lambda i: (0, 0)),
        pl.BlockSpec((tm, d), lambda i: (i, 0)),
        pl.BlockSpec((SUBLANES, d), lambda i: (0, gate_col)),
    ]
    args = [o, w_out, x, mod]
    out_shape = [jax.ShapeDtypeStruct((n, d), F32)]
    out_specs = [pl.BlockSpec((tm, d), lambda i: (i, 0))]
    if with_router:
        g, sh_col, sc_col, w_r, b_r = router
        in_specs += [
            pl.BlockSpec((1, d), lambda i: (0, 0)),
            pl.BlockSpec((SUBLANES, d), lambda i: (0, sh_col)),
            pl.BlockSpec((SUBLANES, d), lambda i: (0, sc_col)),
            pl.BlockSpec((2, d, LANES), lambda i: (0, 0, 0)),
            pl.BlockSpec((1, LANES), lambda i: (0, 0)),
        ]
        args += [g.reshape(1, d), mod, mod, w_r, b_r]
        out_shape += [jax.ShapeDtypeStruct((n, d), F32),
                      jax.ShapeDtypeStruct((n, LANES), F32),
                      jax.ShapeDtypeStruct((SUBLANES, LANES), F32)]
        out_specs += [pl.BlockSpec((tm, d), lambda i: (i, 0)),
                      pl.BlockSpec((tm, LANES), lambda i: (i, 0)),
                      pl.BlockSpec((SUBLANES, LANES), lambda i: (0, 0))]
    res = pl.pallas_call(
        functools.partial(_out_kernel, tiles_per_batch=seq // tm, with_router=with_router),
        out_shape=out_shape,
        grid=(n // tm,),
        in_specs=in_specs,
        out_specs=out_specs,
        compiler_params=_params(("arbitrary",) if with_router else ("parallel",)),
        name="out_proj_router" if with_router else "out_proj",
    )(*args)
    return res if with_router else res[0]


_FF_TILE = 1792


def _snake(i, j, nj):
    return jnp.where(i % 2 == 0, j, nj - 1 - j)


def _ffn_kernel(*refs, tiles_per_batch, n_cast):
    x_ref, g_ref, sh_ref, sc_ref, gt_ref, wg_ref, wu_ref, wd_ref = refs[:8]
    o_ref = refs[8 + n_cast]
    h_sc = refs[-1]
    _ride_along_cast(refs[8:8 + n_cast], refs[9 + n_cast:9 + 2 * n_cast])
    b = pl.program_id(0) // tiles_per_batch

    @pl.when(pl.program_id(1) == 0)
    def _():
        x = x_ref[...]
        h = _rms_mod(x, g_ref[...], sh_ref[pl.ds(b, 1), :], sc_ref[pl.ds(b, 1), :])
        h_sc[...] = h.astype(BF16)
        o_ref[...] = x

    h = h_sc[...]
    g = jnp.dot(h, wg_ref[...], preferred_element_type=F32)
    u = jnp.dot(h, wu_ref[...], preferred_element_type=F32)
    a = (_silu(g) * u).astype(BF16)
    o_ref[...] += gt_ref[pl.ds(b, 1), :] * jnp.dot(a, wd_ref[...], preferred_element_type=F32)


def _ffn_call(x, g, mod, w_gu, w_down, seq, cast=(), tm=512, tf=_FF_TILE):
    n, d = x.shape
    f = w_down.shape[0]
    nj = f // tf
    c_views, c_in, c_out, c_shapes = _ride_along(cast, (n // tm) * nj, lambda i, j: i * nj + j)
    return pl.pallas_call(
        functools.partial(_ffn_kernel, tiles_per_batch=seq // tm, n_cast=len(cast)),
        out_shape=[jax.ShapeDtypeStruct((n, d), F32)] + c_shapes,
        grid=(n // tm, nj),
        in_specs=[
            pl.BlockSpec((tm, d), lambda i, j: (i, 0)),
            pl.BlockSpec((1, d), lambda i, j: (0, 0)),
            pl.BlockSpec((SUBLANES, d), lambda i, j: (0, 3)),
            pl.BlockSpec((SUBLANES, d), lambda i, j: (0, 4)),
            pl.BlockSpec((SUBLANES, d), lambda i, j: (0, 5)),
            pl.BlockSpec((d, tf), lambda i, j: (0, _snake(i, j, nj))),
            pl.BlockSpec((d, tf), lambda i, j: (0, _snake(i, j, nj) + nj)),
            pl.BlockSpec((tf, d), lambda i, j: (_snake(i, j, nj), 0)),
        ] + c_in,
        out_specs=[pl.BlockSpec((tm, d), lambda i, j: (i, 0))] + c_out,
        scratch_shapes=[pltpu.VMEM((tm, d), BF16)],
        compiler_params=_params(("parallel", "arbitrary")),
        name="ffn_swiglu",
    )(x, g.reshape(1, d), mod, mod, mod, w_gu, w_gu, w_down, *c_views)


_MOE_TM = 512


def _dispatch_kernel(d1_ref, d2_ref, zero_ref, h_ref, xs_ref, z_sc, sem):
    tm = h_ref.shape[0]
    base = pl.program_id(0) * tm

    @pl.when(pl.program_id(0) == 0)
    def _():
        z_sc[...] = jnp.zeros_like(z_sc)

        def zero_copy(t):
            return pltpu.make_async_copy(z_sc, xs_ref.at[pl.ds(pl.multiple_of(t * tm, tm), tm)],
                                         sem.at[0])

        def start(t, carry):
            @pl.when(zero_ref[t] != 0)
            def _():
                zero_copy(t).start()
            return carry

        def wait(t, carry):
            @pl.when(zero_ref[t] != 0)
            def _():
                zero_copy(t).wait()
            return carry

        lax.fori_loop(0, zero_ref.shape[0], start, 0)
        lax.fori_loop(0, zero_ref.shape[0], wait, 0)

    def issue(r, carry):
        src = h_ref.at[pl.ds(r, 1)]
        pltpu.make_async_copy(src, xs_ref.at[pl.ds(d1_ref[base + r], 1)], sem.at[0]).start()
        pltpu.make_async_copy(src, xs_ref.at[pl.ds(d2_ref[base + r], 1)],
                              sem.at[1]).start(priority=1)
        return carry

    lax.fori_loop(0, tm, issue, 0, unroll=8)
    pltpu.make_async_copy(h_ref, xs_ref.at[pl.ds(0, tm)], sem.at[0]).wait()
    pltpu.make_async_copy(h_ref, xs_ref.at[pl.ds(0, tm)], sem.at[1]).wait()


def _dispatch_call(dest1, dest2, zero_tile, h, tm=_MOE_TM):
    n, d = h.shape
    n_rows = zero_tile.shape[0] * tm
    return pl.pallas_call(
        _dispatch_kernel,
        out_shape=jax.ShapeDtypeStruct((n_rows, d), h.dtype),
        grid_spec=pltpu.PrefetchScalarGridSpec(
            num_scalar_prefetch=3,
            grid=(n // tm,),
            in_specs=[pl.BlockSpec((tm, d), lambda i, d1, d2, zt: (i, 0))],
            out_specs=pl.BlockSpec(memory_space=pl.ANY),
            scratch_shapes=[pltpu.VMEM((tm, d), h.dtype), pltpu.SemaphoreType.DMA((2,))],
        ),
        compiler_params=_params(("arbitrary",)),
        name="moe_dispatch",
    )(dest1, dest2, zero_tile, h)


def _experts_kernel(te_ref, nt_ref, xs_ref, wg_ref, wu_ref, wd_ref, ye_ref):
    del te_ref
    t = pl.program_id(0)

    @pl.when(pl.program_id(1) == 0)
    def _():
        ye_ref[...] = jnp.zeros_like(ye_ref)

    @pl.when(t < nt_ref[0])
    def _():
        h = xs_ref[...].astype(BF16)
        g = jnp.dot(h, wg_ref[0], preferred_element_type=F32)
        u = jnp.dot(h, wu_ref[0], preferred_element_type=F32)
        a = (_silu(g) * u).astype(BF16)
        ye_ref[...] += jnp.dot(a, wd_ref[0], preferred_element_type=F32)


def _experts_call(tile_expert, n_tiles, xs, w_gu, w_down, tf=_FF_TILE):
    n_rows, d = xs.shape
    _, f, _ = w_down.shape
    nj = f // tf
    tm = _MOE_TM
    tile = lambda t, nt: jnp.maximum(jnp.minimum(t, nt[0] - 1), 0)
    jj = lambda t, j, nt: _snake(tile(t, nt), jnp.where(t < nt[0], j, nj - 1), nj)
    return pl.pallas_call(
        _experts_kernel,
        out_shape=jax.ShapeDtypeStruct((n_rows, d), F32),
        grid_spec=pltpu.PrefetchScalarGridSpec(
            num_scalar_prefetch=2,
            grid=(n_rows // tm, nj),
            in_specs=[
                pl.BlockSpec((tm, d), lambda t, j, te, nt: (tile(t, nt), 0)),
                pl.BlockSpec((1, d, tf), lambda t, j, te, nt: (te[tile(t, nt)], 0, jj(t, j, nt))),
                pl.BlockSpec((1, d, tf), lambda t, j, te, nt: (te[tile(t, nt)], 0, jj(t, j, nt) + nj)),
                pl.BlockSpec((1, tf, d), lambda t, j, te, nt: (te[tile(t, nt)], jj(t, j, nt), 0)),
            ],
            out_specs=pl.BlockSpec((tm, d), lambda t, j, te, nt: (t, 0)),
        ),
        compiler_params=_params(("arbitrary", "arbitrary")),
        name="moe_experts",
    )(tile_expert, n_tiles, xs, w_gu, w_gu, w_down)


def _combine_kernel(d1_ref, d2_ref, ye_ref, x_ref, route_ref, gt_ref, gf_ref, o_ref,
                    y1_sc, y2_sc, sem, *, tiles_per_batch):
    tm = x_ref.shape[0]
    i = pl.program_id(0)
    b = i // tiles_per_batch
    base = i * tm

    def issue(r, carry):
        pltpu.make_async_copy(ye_ref.at[pl.ds(d1_ref[base + r], 1)], y1_sc.at[pl.ds(r, 1)],
                              sem.at[0]).start()
        pltpu.make_async_copy(ye_ref.at[pl.ds(d2_ref[base + r], 1)], y2_sc.at[pl.ds(r, 1)],
                              sem.at[1]).start(priority=1)
        return carry

    lax.fori_loop(0, tm, issue, 0, unroll=8)
    pltpu.make_async_copy(ye_ref.at[pl.ds(0, tm)], y1_sc, sem.at[0]).wait()
    pltpu.make_async_copy(ye_ref.at[pl.ds(0, tm)], y2_sc, sem.at[1]).wait()
    route = route_ref[...]
    y = route[:, 2:3] * y1_sc[...] + route[:, 3:4] * y2_sc[...]
    xn = x_ref[...] + gt_ref[pl.ds(b, 1), :] * y
    ms = jnp.mean(xn * xn, axis=-1, keepdims=True)
    o_ref[...] = xn * lax.rsqrt(ms + EPS) * gf_ref[...]


def _combine_call(dest1, dest2, ye, x, route, mod, g_final, seq, tm=512):
    n, d = x.shape
    return pl.pallas_call(
        functools.partial(_combine_kernel, tiles_per_batch=seq // tm),
        out_shape=jax.ShapeDtypeStruct((n, d), F32),
        grid_spec=pltpu.PrefetchScalarGridSpec(
            num_scalar_prefetch=2,
            grid=(n // tm,),
            in_specs=[
                pl.BlockSpec(memory_space=pl.ANY),
                pl.BlockSpec((tm, d), lambda i, d1, d2: (i, 0)),
                pl.BlockSpec((tm, LANES), lambda i, d1, d2: (i, 0)),
                pl.BlockSpec((SUBLANES, d), lambda i, d1, d2: (0, 5)),
                pl.BlockSpec((1, d), lambda i, d1, d2: (0, 0)),
            ],
            out_specs=pl.BlockSpec((tm, d), lambda i, d1, d2: (i, 0)),
            scratch_shapes=[pltpu.VMEM((tm, d), F32), pltpu.VMEM((tm, d), F32),
                            pltpu.SemaphoreType.DMA((2,))],
        ),
        compiler_params=_params(("arbitrary",)),
        name="moe_combine",
    )(dest1, dest2, ye, x, route, mod, g_final.reshape(1, d))


def _moe_call(h, route, counts, x, mod, g_final, w_gu, w_down, seq):
    n, d = x.shape
    ne = w_down.shape[0]
    tm = _MOE_TM
    max_tiles = (2 * n) // tm + ne
    e1 = route[:, 0].astype(jnp.int32)
    e2 = route[:, 1].astype(jnp.int32)
    cnt = counts[0, :ne].astype(jnp.int32)
    tiles_e = (cnt + tm - 1) // tm
    tile_end = jnp.cumsum(tiles_e)
    row_start = (tile_end - tiles_e) * tm
    dest1 = row_start[e1] + route[:, 4].astype(jnp.int32)
    dest2 = row_start[e2] + route[:, 5].astype(jnp.int32)
    n_tiles = tile_end[-1:]
    tile_ids = jnp.arange(max_tiles, dtype=jnp.int32)
    tile_expert = jnp.minimum(
        jnp.sum((tile_ids[:, None] >= tile_end[None, :]).astype(jnp.int32), axis=1), ne - 1)
    is_last = jnp.any((tile_ids[:, None] == tile_end[None, :] - 1) & (tiles_e[None, :] > 0), axis=1)
    zero_tile = (is_last | (tile_ids >= n_tiles[0])).astype(jnp.int32)
    xs = _dispatch_call(dest1, dest2, zero_tile, h)
    ye = _experts_call(tile_expert, n_tiles, xs, w_gu, w_down)
    return _combine_call(dest1, dest2, ye, x, route, mod, g_final, seq)


_SWA_TQ = 2 * CHUNK
_SWA_BAND = 2 * _SWA_TQ


def _swa_bucket_tiles():
    cc = np.arange(_SWA_BAND)[:, None]
    r = np.arange(_SWA_TQ)[None, :]
    rel = cc - _SWA_TQ - r
    nb = REL_BUCKETS // 2
    max_exact = nb // 2
    ret = (rel > 0).astype(np.int32) * nb
    n = np.abs(rel)
    large = max_exact + (np.log(np.maximum(n, 1) / max_exact)
                         / np.log(REL_MAX_DIST / max_exact) * (nb - max_exact)).astype(np.int32)
    large = np.minimum(large, nb - 1)
    bucket = (ret + np.where(n < max_exact, n, large)).astype(np.int32)
    q_chunk = r // CHUNK
    k_chunk = cc // CHUNK
    visible = (k_chunk >= q_chunk) & (k_chunk <= q_chunk + WINDOW_CHUNKS)
    later = np.where(visible, bucket, -1)
    first = np.where(cc >= _SWA_TQ, later, -1)
    return np.stack([first, later]).astype(np.int32)


def _swa_bias_kernel(tbl_ref, bkt_ref, o_ref):
    n_heads = o_ref.shape[1]
    for v in range(2):
        bkt = bkt_ref[v]
        for head in range(n_heads):
            tile = jnp.full(bkt.shape, NEG_BIG, F32)
            for bk in range(REL_BUCKETS):
                tile = jnp.where(bkt == bk, tbl_ref[head, bk] * LOG2E, tile)
            o_ref[v, head] = tile


def _swa_bias_call(rel_bias):
    bkt = jnp.asarray(_swa_bucket_tiles())
    n_heads = rel_bias.shape[1]
    return pl.pallas_call(
        _swa_bias_kernel,
        out_shape=jax.ShapeDtypeStruct((2, n_heads, _SWA_BAND, _SWA_TQ), F32),
        in_specs=[
            pl.BlockSpec(memory_space=pltpu.SMEM),
            pl.BlockSpec(memory_space=pltpu.VMEM),
        ],
        out_specs=pl.BlockSpec(memory_space=pltpu.VMEM),
        name="swa_bias",
    )(rel_bias.T, bkt)


def _swa_kernel(qt_ref, kp_ref, kc_ref, vtp_ref, vtc_ref, bias_ref, sink_ref, o_ref):
    tq = _SWA_TQ
    lane = lax.broadcasted_iota(jnp.int32, (1, LANES), 1)
    ones = jnp.ones((_ONES_ROWS, _SWA_BAND), BF16)
    outs = [None] * (SWA_KV_HEADS * SWA_GROUP)
    units = [(hk, par) for hk in range(SWA_KV_HEADS) for par in range(2)]

    def scores(hk, par):
        ksl = slice(hk * LANES, (hk + 1) * LANES)
        kb = jnp.concatenate([kp_ref[0, :, ksl], kc_ref[0, :, ksl]], axis=0)
        f0 = hk * SWA_GROUP * HEAD_DIM
        wq = jnp.concatenate([qt_ref[f0:f0 + LANES, :], qt_ref[f0 + LANES:f0 + 2 * LANES, :]],
                             axis=1)
        head_lanes = (lane < HEAD_DIM) if par == 0 else (lane >= HEAD_DIM)
        return jnp.dot(jnp.where(head_lanes, kb, 0), wq, preferred_element_type=F32)

    sts = [scores(hk, par) for hk, par in units]
    for (hk, par), st in zip(units, sts):
        vsl = slice(hk * HEAD_DIM, (hk + 1) * HEAD_DIM)
        vt1 = jnp.concatenate(
            [jnp.concatenate([vtp_ref[vsl, :], vtc_ref[vsl, :]], axis=1), ones], axis=0)
        heads = (hk * SWA_GROUP + par, hk * SWA_GROUP + par + 2)
        ps, ms = [], []
        for i, head in enumerate(heads):
            s = st[:, i * tq:(i + 1) * tq] + bias_ref[0, head]
            m = jnp.maximum(jnp.max(s, axis=0, keepdims=True), sink_ref[head])
            ps.append(jnp.exp2(s - m).astype(BF16))
            ms.append(m)
        acc = jnp.dot(vt1, jnp.concatenate(ps, axis=1), preferred_element_type=F32)
        for i, head in enumerate(heads):
            a = acc[:, i * tq:(i + 1) * tq]
            den = a[HEAD_DIM:HEAD_DIM + 1] + jnp.exp2(sink_ref[head] - ms[i])
            outs[head] = a[:HEAD_DIM] / den
    o_ref[0] = jnp.concatenate(outs, axis=0).T.astype(o_ref.dtype)


def _swa_call(q_t, k, v_t, bias, sink, batch, seq):
    d = q_t.shape[0]
    kw = k.shape[2]
    vw = v_t.shape[0]
    tq = _SWA_TQ
    nq = seq // tq
    return pl.pallas_call(
        _swa_kernel,
        out_shape=jax.ShapeDtypeStruct((batch, seq, d), BF16),
        grid=(batch, nq),
        in_specs=[
            pl.BlockSpec((d, tq), lambda b, i: (0, b * nq + i)),
            pl.BlockSpec((1, tq, kw), lambda b, i: (b, jnp.maximum(i - 1, 0), 0)),
            pl.BlockSpec((1, tq, kw), lambda b, i: (b, i, 0)),
            pl.BlockSpec((vw, tq), lambda b, i: (0, b * nq + jnp.maximum(i - 1, 0))),
            pl.BlockSpec((vw, tq), lambda b, i: (0, b * nq + i)),
            pl.BlockSpec((1,) + bias.shape[1:], lambda b, i: (jnp.minimum(i, 1), 0, 0, 0)),
            pl.BlockSpec(sink.shape, lambda b, i: (0, 0, 0)),
        ],
        out_specs=pl.BlockSpec((1, tq, d), lambda b, i: (b, i, 0)),
        compiler_params=_params(("parallel", "arbitrary")),
        name="swa_attention",
    )(q_t, k, k, v_t, v_t, bias, sink)


def kernel(x, c, w_ada, b_ada, g_norm_mix, g_norm_ffn, g_final, fox_w_in, fox_b_f, fox_w_out, swa_w_in, swa_sinks, swa_w_out, rel_bias, ffn_w_gu, ffn_w_down, moe_w_router, moe_b_router, moe_w_gu, moe_w_down):
    batch, seq, d = x.shape
    n = batch * seq
    q_scale = HEAD_DIM ** -0.5
    xf = x.reshape(n, d)

    c_pad = jnp.zeros((SUBLANES, d), F32).at[:batch].set(c)
    mod = _ada_call(c_pad, w_ada, b_ada)
    mod0, mod1 = mod[0], mod[1]

    w_in = fox_w_in[0]
    n_heads = d // HEAD_DIM
    k, q_t, v_t, f, w_gu0, w_down0, w_out0, w_out1 = _attn_proj_call(
        xf, g_norm_mix[0], mod0, w_in[:, d:2 * d].astype(BF16),
        (w_in[:, :d] * (q_scale * LOG2E)).T.astype(BF16), w_in[:, 2 * d:3 * d].T.astype(BF16),
        seq, w_f=jnp.stack(_split_bf16(w_in[:, 3 * d:])),
        cast=(ffn_w_gu[0], ffn_w_down[0], fox_w_out[0], swa_w_out[0]))
    g_cum, r_cum = _cum_call(f, fox_b_f[0], batch, seq)
    r_flat = r_cum.transpose(0, 2, 1).reshape(-1)
    o, w_down1 = _fox_call(r_flat, q_t, k.reshape(batch, seq, d), v_t, g_cum,
                           cast=(moe_w_down[0],))
    w_down1 = w_down1.reshape(moe_w_down.shape[1:])
    x1 = _out_call(o.reshape(n, d), w_out0, xf, mod0, 2, seq)
    x2, w_gu1 = _ffn_call(x1, g_norm_ffn[0], mod0, w_gu0, w_down0, seq, cast=(moe_w_gu[0],))
    w_gu1 = w_gu1.reshape(moe_w_gu.shape[1:])

    w_in = swa_w_in[0]
    kvw = SWA_KV_HEADS * HEAD_DIM
    dup = lambda w: jnp.repeat(w.reshape(d, SWA_KV_HEADS, 1, HEAD_DIM), 2, axis=2).reshape(d, 2 * kvw)
    k, q_t, v_t = _attn_proj_call(
        x2, g_norm_mix[1], mod1, dup(w_in[:, d:d + kvw]).astype(BF16),
        (w_in[:, :d] * (q_scale * LOG2E)).T.astype(BF16), w_in[:, d + kvw:].T.astype(BF16), seq)
    bias = _swa_bias_call(rel_bias)
    sink = jnp.broadcast_to((swa_sinks[0] * LOG2E)[:, None, None], (n_heads, 1, _SWA_TQ))
    o = _swa_call(q_t, k.reshape(batch, seq, 2 * kvw), v_t, bias, sink, batch, seq)
    w_r = jnp.stack(_split_bf16(jnp.zeros((d, LANES), F32).at[:, :N_EXPERTS].set(moe_w_router[0])))
    b_r = jnp.zeros((1, LANES), F32).at[0, :N_EXPERTS].set(moe_b_router[0])
    x3, h4, route, counts = _out_call(o.reshape(n, d), w_out1, x2, mod1, 2, seq,
                                      router=(g_norm_ffn[1], 3, 4, w_r, b_r))
    out = _moe_call(h4, route, counts, x3, mod1, g_final, w_gu1, w_down1, seq)
    return out.reshape(batch, seq, d)
```

```python
import functools

import numpy as np
import jax
import jax.numpy as jnp
from jax import lax
from jax.experimental import pallas as pl
from jax.experimental.pallas import tpu as pltpu

F32 = jnp.float32
BF16 = jnp.bfloat16
HIGHEST = lax.Precision.HIGHEST

HEAD_DIM = 64
CHUNK = 64
WINDOW_CHUNKS = 2
REL_BUCKETS = 32
REL_MAX_DIST = 128
SWA_KV_HEADS = 4
SWA_GROUP = 4
N_EXPERTS = 8
EPS = 1e-6

LANES = 128
SUBLANES = 8
VMEM_LIMIT = 56 * 1024 * 1024

NEG_BIG = -1e30


def _params(sem, vmem=VMEM_LIMIT):
    return pltpu.CompilerParams(dimension_semantics=sem, vmem_limit_bytes=vmem)


def _rms_mod(x, g, shift, scale):
    ms = jnp.mean(x * x, axis=-1, keepdims=True)
    y = x * lax.rsqrt(ms + EPS) * g
    return y * (1.0 + scale) + shift


def _silu(x):
    return x / (1.0 + jnp.exp(-x))


def _split_bf16(x):
    hi = x.astype(BF16)
    return hi, (x - hi.astype(F32)).astype(BF16)


def _dot_split(x_hi, x_lo, w_ref):
    return (jnp.dot(x_hi, w_ref[0], preferred_element_type=F32)
            + jnp.dot(x_lo, w_ref[0], preferred_element_type=F32)
            + jnp.dot(x_hi, w_ref[1], preferred_element_type=F32))


_BF16_SUBLANES = 16


def _ride_along(arrays, n_steps, step_index):
    views, in_specs, out_specs, out_shapes = [], [], [], []
    for a in arrays:
        v = a.reshape(-1, a.shape[-1])
        rows, rem = divmod(v.shape[0], n_steps)
        assert rem == 0 and rows % _BF16_SUBLANES == 0, v.shape
        spec = pl.BlockSpec((rows, v.shape[1]), lambda *g: (step_index(*g), 0))
        views.append(v)
        in_specs.append(spec)
        out_specs.append(spec)
        out_shapes.append(jax.ShapeDtypeStruct(v.shape, BF16))
    return views, in_specs, out_specs, out_shapes


def _ride_along_cast(in_refs, out_refs):
    for src, dst in zip(in_refs, out_refs):
        dst[...] = src[...].astype(BF16)


def _ada_kernel(c_ref, w_ref, b_ref, o_ref):
    cond = _silu(c_ref[...])
    o_ref[0] = jnp.dot(cond, w_ref[0], preferred_element_type=F32,
                       precision=HIGHEST) + b_ref[0]


def _ada_call(c_pad, w_ada, b_ada):
    depth, d, n = w_ada.shape
    tn = 1536
    return pl.pallas_call(
        _ada_kernel,
        out_shape=jax.ShapeDtypeStruct((depth, SUBLANES, n), F32),
        grid=(depth, n // tn),
        in_specs=[
            pl.BlockSpec((SUBLANES, d), lambda l, j: (0, 0)),
            pl.BlockSpec((1, d, tn), lambda l, j: (l, 0, j)),
            pl.BlockSpec((1, 1, tn), lambda l, j: (l, 0, j)),
        ],
        out_specs=pl.BlockSpec((1, SUBLANES, tn), lambda l, j: (l, 0, j)),
        compiler_params=_params(("parallel", "parallel")),
        name="ada_mod",
    )(c_pad, w_ada, b_ada.reshape(depth, 1, n))


_NT = (((1,), (1,)), ((), ()))


def _attn_proj_kernel(*refs, tiles_per_batch, with_gate, n_cast):
    x_ref, g_ref, sh_ref, sc_ref, wk_ref, wqt_ref, wvt_ref = refs[:7]
    n_in = 7 + with_gate + n_cast
    k_ref, qt_ref, vt_ref = refs[n_in:n_in + 3]
    b = pl.program_id(0) // tiles_per_batch
    h = _rms_mod(x_ref[...], g_ref[...], sh_ref[pl.ds(b, 1), :], sc_ref[pl.ds(b, 1), :])
    hb = h.astype(BF16)
    k_ref[...] = jnp.dot(hb, wk_ref[...], preferred_element_type=F32).astype(BF16)
    qt_ref[...] = lax.dot_general(wqt_ref[...], hb, _NT, preferred_element_type=F32).astype(BF16)
    vt_ref[...] = lax.dot_general(wvt_ref[...], hb, _NT, preferred_element_type=F32).astype(BF16)
    if with_gate:
        refs[n_in + 3][...] = _dot_split(hb, (h - hb.astype(F32)).astype(BF16), refs[7])
    _ride_along_cast(refs[n_in - n_cast:n_in], refs[len(refs) - n_cast:])


def _attn_proj_call(x, g, mod, w_k, w_qt, w_vt, seq, w_f=None, cast=(), tm=512):
    n, d = x.shape
    with_gate = w_f is not None
    full = lambda a: pl.BlockSpec(a.shape, lambda i: (0, 0))
    c_views, c_in, c_out, c_shapes = _ride_along(cast, n // tm, lambda i: i)
    in_specs = [
        pl.BlockSpec((tm, d), lambda i: (i, 0)),
        pl.BlockSpec((1, d), lambda i: (0, 0)),
        pl.BlockSpec((SUBLANES, d), lambda i: (0, 0)),
        pl.BlockSpec((SUBLANES, d), lambda i: (0, 1)),
        full(w_k), full(w_qt), full(w_vt),
    ]
    args = [x, g.reshape(1, d), mod, mod, w_k, w_qt, w_vt]
    out_shape = [jax.ShapeDtypeStruct((n, w_k.shape[1]), BF16),
                 jax.ShapeDtypeStruct((w_qt.shape[0], n), BF16),
                 jax.ShapeDtypeStruct((w_vt.shape[0], n), BF16)]
    out_specs = [pl.BlockSpec((tm, w_k.shape[1]), lambda i: (i, 0)),
                 pl.BlockSpec((w_qt.shape[0], tm), lambda i: (0, i)),
                 pl.BlockSpec((w_vt.shape[0], tm), lambda i: (0, i))]
    if with_gate:
        in_specs.append(pl.BlockSpec(w_f.shape, lambda i: (0, 0, 0)))
        args.append(w_f)
        out_shape.append(jax.ShapeDtypeStruct((n, w_f.shape[2]), F32))
        out_specs.append(pl.BlockSpec((tm, w_f.shape[2]), lambda i: (i, 0)))
    return pl.pallas_call(
        functools.partial(_attn_proj_kernel, tiles_per_batch=seq // tm, with_gate=with_gate,
                          n_cast=len(cast)),
        out_shape=out_shape + c_shapes,
        grid=(n // tm,),
        in_specs=in_specs + c_in,
        out_specs=out_specs + c_out,
        compiler_params=_params(("parallel",)),
        name="attn_proj_gate" if with_gate else "attn_proj",
    )(*args, *c_views)


_FOX_T = 512
LOG2E = 1.4426950408889634


def _cum_kernel(f_ref, bf_ref, g_ref, r_ref):
    x = f_ref[...] + bf_ref[...]
    logf = (jnp.minimum(x, 0.0) - jnp.log(1.0 + jnp.exp(-jnp.abs(x)))) * LOG2E
    seq, nh = logf.shape
    r = lax.broadcasted_iota(jnp.int32, (_FOX_T, _FOX_T), 0)
    c = lax.broadcasted_iota(jnp.int32, (_FOX_T, _FOX_T), 1)
    lower = (c <= r).astype(F32)
    carry = jnp.zeros((1, nh), F32)
    for ch in range(seq // _FOX_T):
        rows = slice(ch * _FOX_T, (ch + 1) * _FOX_T)
        cs = jnp.dot(lower, logf[rows, :], preferred_element_type=F32, precision=HIGHEST)
        g_ref[0, rows, :] = cs
        r_ref[0, ch:ch + 1, :] = carry
        carry = carry + cs[_FOX_T - 1:_FOX_T, :]


def _cum_call(f, b_f, batch, seq):
    nh = f.shape[1]
    return pl.pallas_call(
        _cum_kernel,
        out_shape=[jax.ShapeDtypeStruct((batch, seq, nh), F32),
                   jax.ShapeDtypeStruct((batch, seq // _FOX_T, nh), F32)],
        grid=(batch,),
        in_specs=[
            pl.BlockSpec((seq, nh), lambda b: (b, 0)),
            pl.BlockSpec((1, nh), lambda b: (0, 0)),
        ],
        out_specs=[pl.BlockSpec((1, seq, nh), lambda b: (b, 0, 0)),
                   pl.BlockSpec((1, seq // _FOX_T, nh), lambda b: (b, 0, 0))],
        compiler_params=_params(("parallel",)),
        name="forget_cumsum",
    )(f, b_f.reshape(1, nh))


_ONES_ROWS = 16


def _fox_kernel(*refs, n_cast):
    r_ref, qt_ref, k_ref, vt_ref, g_ref = refs[:5]
    o_ref = refs[5 + n_cast]
    ka_sc, m_sc, acc_sc, sa_sc, sb_sc, xa_sc, xb_sc = refs[6 + 2 * n_cast:]
    _ride_along_cast(refs[5:5 + n_cast], refs[6 + n_cast:6 + 2 * n_cast])
    t = _FOX_T
    b, hp = pl.program_id(0), pl.program_id(1)
    seq = k_ref.shape[1]
    n_blocks = seq // t
    n_heads = 2 * pl.num_programs(1)

    lane = lax.broadcasted_iota(jnp.int32, (1, LANES), 1)
    feat = lax.broadcasted_iota(jnp.int32, (LANES, 1), 0)
    aug0 = [HEAD_DIM, 0]

    g_all = g_ref[0]
    head_col = lax.broadcasted_iota(jnp.int32, g_all.shape, 1)
    for hh in range(2):
        g_head = jnp.sum(jnp.where(head_col == 2 * hp + hh, g_all, 0.0), axis=1, keepdims=True)
        gb = jnp.broadcast_to(g_head, (seq, LANES))
        hi = gb.astype(BF16).astype(F32)
        mid = (gb - hi).astype(BF16).astype(F32)
        lo = gb - hi - mid
        aug = jnp.where(lane == aug0[hh], hi,
                        jnp.where(lane == aug0[hh] + 1, mid,
                                  jnp.where(lane == aug0[hh] + 2, lo, 0.0)))
        own = (lane >= hh * HEAD_DIM) & (lane < (hh + 1) * HEAD_DIM)
        ka_sc[hh] = jnp.where(own, k_ref[0].astype(F32), aug).astype(BF16)

    key = lax.broadcasted_iota(jnp.int32, (t, t), 0)
    qry = lax.broadcasted_iota(jnp.int32, (t, t), 1)
    ones = jnp.ones((_ONES_ROWS, t), BF16)
    feat_t = lax.broadcasted_iota(jnp.int32, (LANES, t), 0)
    r_base = [(b * n_heads + 2 * hp + hh) * n_blocks for hh in range(2)]
    bufs = ((sa_sc, xa_sc), (sb_sc, xb_sc))

    def queries(qi):
        qt2 = qt_ref[:, qi * t:(qi + 1) * t]
        return [jnp.where((feat >= hh * HEAD_DIM) & (feat < (hh + 1) * HEAD_DIM), qt2, 0)
                + jnp.where((feat_t >= aug0[hh]) & (feat_t < aug0[hh] + 3), -1.0, 0.0).astype(BF16)
                for hh in range(2)]

    def scores(qh, kb, dst):
        for hh in range(2):
            st = jnp.dot(ka_sc[hh, kb * t:(kb + 1) * t, :], qh[hh],
                         preferred_element_type=F32)
            dst[0][hh] = st
            dst[1][hh] = jnp.max(st, axis=0, keepdims=True)

    def softmax_pv(qi, kb, cur):
        for hh in range(2):
            st = cur[0][hh]
            if kb == qi:
                st = jnp.where(key <= qry, st, -jnp.inf)
                st_max = jnp.max(st, axis=0, keepdims=True)
            else:
                st_max = cur[1][hh]
            c = r_ref[r_base[hh] + kb] - r_ref[r_base[hh] + qi]
            m_old = m_sc[hh]
            m_new = jnp.maximum(m_old, st_max - c)
            alpha = jnp.exp2(m_old - m_new)
            p = jnp.exp2(st - (m_new + c)).astype(BF16)
            vt1 = jnp.concatenate(
                [vt_ref[hh * HEAD_DIM:(hh + 1) * HEAD_DIM, kb * t:(kb + 1) * t], ones], axis=0)
            acc_sc[hh] = alpha * acc_sc[hh] + jnp.dot(vt1, p, preferred_element_type=F32)
            m_sc[hh] = m_new

    pairs = [(qi, kb) for qi in range(n_blocks) for kb in range(qi + 1)]
    qh = queries(0)
    scores(qh, 0, bufs[0])
    for s, (qi, kb) in enumerate(pairs):
        if kb == 0:
            m_sc[...] = jnp.full_like(m_sc, -jnp.inf)
            acc_sc[...] = jnp.zeros_like(acc_sc)
        if s + 1 < len(pairs):
            qi_n, kb_n = pairs[s + 1]
            if qi_n != qi:
                qh = queries(qi_n)
            scores(qh, kb_n, bufs[(s + 1) % 2])
        softmax_pv(qi, kb, bufs[s % 2])
        if kb == qi:
            outs = [acc_sc[hh, :HEAD_DIM, :] / acc_sc[hh, HEAD_DIM:HEAD_DIM + 1, :]
                    for hh in range(2)]
            o_ref[0, qi * t:(qi + 1) * t, :] = jnp.concatenate(outs, axis=0).T.astype(o_ref.dtype)


def _fox_call(r_flat, q_t, k, v_t, g_cum, cast=()):
    batch, seq, d = k.shape
    n_pairs = d // LANES
    t = _FOX_T
    c_views, c_in, c_out, c_shapes = _ride_along(
        cast, batch * n_pairs, lambda b, h: b * n_pairs + h)
    return pl.pallas_call(
        functools.partial(_fox_kernel, n_cast=len(cast)),
        out_shape=[jax.ShapeDtypeStruct((batch, seq, d), BF16)] + c_shapes,
        grid=(batch, n_pairs),
        in_specs=[
            pl.BlockSpec(memory_space=pltpu.SMEM),
            pl.BlockSpec((LANES, seq), lambda b, h: (h, b)),
            pl.BlockSpec((1, seq, LANES), lambda b, h: (b, 0, h)),
            pl.BlockSpec((LANES, seq), lambda b, h: (h, b)),
            pl.BlockSpec((1, seq, g_cum.shape[2]), lambda b, h: (b, 0, 0)),
        ] + c_in,
        out_specs=[pl.BlockSpec((1, seq, LANES), lambda b, h: (b, 0, h))] + c_out,
        scratch_shapes=[pltpu.VMEM((2, seq, LANES), BF16), pltpu.VMEM((2, 1, t), F32),
                        pltpu.VMEM((2, HEAD_DIM + _ONES_ROWS, t), F32),
                        pltpu.VMEM((2, t, t), F32), pltpu.VMEM((2, t, t), F32),
                        pltpu.VMEM((2, 1, t), F32), pltpu.VMEM((2, 1, t), F32)],
        compiler_params=_params(("parallel", "parallel")),
        name="fox_attention",
    )(r_flat, q_t, k, v_t, g_cum, *c_views)


def _out_kernel(o_ref, w_ref, x_ref, gt_ref, g_ref, sh_ref, sc_ref, wr_ref, br_ref,
                xo_ref, h_ref, route_ref, route_t_ref, cnt_ref, *, tiles_per_batch):
    b = pl.program_id(0) // tiles_per_batch
    y = jnp.dot(o_ref[...], w_ref[...], preferred_element_type=F32)
    xn = x_ref[...] + gt_ref[pl.ds(b, 1), :] * y
    xo_ref[...] = xn
    h = _rms_mod(xn, g_ref[...], sh_ref[pl.ds(b, 1), :], sc_ref[pl.ds(b, 1), :])
    h_ref[...] = h
    logits = _dot_split(*_split_bf16(h), wr_ref) + br_ref[...]
    tm = logits.shape[0]
    lane = lax.broadcasted_iota(jnp.int32, logits.shape, 1).astype(F32)
    logits = jnp.where(lane < N_EXPERTS, logits, -jnp.inf)
    m1 = jnp.max(logits, axis=1, keepdims=True)
    i1 = jnp.min(jnp.where(logits == m1, lane, float(LANES)), axis=1, keepdims=True)
    rest = jnp.where(lane == i1, -jnp.inf, logits)
    m2 = jnp.max(rest, axis=1, keepdims=True)
    i2 = jnp.min(jnp.where(rest == m2, lane, float(LANES)), axis=1, keepdims=True)
    e2 = jnp.exp(m2 - m1)
    den = 1.0 + e2

    @pl.when(pl.program_id(0) == 0)
    def _():
        cnt_ref[...] = jnp.zeros_like(cnt_ref)

    sel1 = lane == i1
    sel2 = lane == i2
    onehot = jnp.where(sel1 | sel2, 1.0, 0.0)
    r = lax.broadcasted_iota(jnp.int32, (tm, tm), 0)
    c = lax.broadcasted_iota(jnp.int32, (tm, tm), 1)
    lower = jnp.where(c < r, 1.0, 0.0).astype(BF16)
    before = jnp.dot(lower, onehot.astype(BF16), preferred_element_type=F32) + cnt_ref[0:1, :]
    rank1 = jnp.sum(jnp.where(sel1, before, 0.0), axis=1, keepdims=True)
    rank2 = jnp.sum(jnp.where(sel2, before, 0.0), axis=1, keepdims=True)
    cnt_ref[0:1, :] = cnt_ref[0:1, :] + jnp.sum(onehot, axis=0, keepdims=True)
    route = jnp.where(lane == 0.0, i1, 0.0)
    for k, val in enumerate((i2, 1.0 / den, e2 / den, rank1, rank2), start=1):
        route = jnp.where(lane == float(k), val, route)
    route_ref[...] = route
    route_t_ref[...] = route.T[:SUBLANES, :]


def _out_call(o, w_out, x, mod, g, w_r, b_r, seq, tm=512):
    n, d = x.shape
    row = lambda w: pl.BlockSpec((tm, w), lambda i: (i, 0))
    col = lambda c: pl.BlockSpec((SUBLANES, d), lambda i: (0, c))
    return pl.pallas_call(
        functools.partial(_out_kernel, tiles_per_batch=seq // tm),
        out_shape=[jax.ShapeDtypeStruct((n, d), F32), jax.ShapeDtypeStruct((n, d), F32),
                   jax.ShapeDtypeStruct((n, LANES), F32), jax.ShapeDtypeStruct((SUBLANES, n), F32),
                   jax.ShapeDtypeStruct((SUBLANES, LANES), F32)],
        grid=(n // tm,),
        in_specs=[
            row(d),
            pl.BlockSpec((d, d), lambda i: (0, 0)),
            row(d),
            col(2),
            pl.BlockSpec((1, d), lambda i: (0, 0)),
            col(3),
            col(4),
            pl.BlockSpec((2, d, LANES), lambda i: (0, 0, 0)),
            pl.BlockSpec((1, LANES), lambda i: (0, 0)),
        ],
        out_specs=[row(d), row(d), row(LANES), pl.BlockSpec((SUBLANES, tm), lambda i: (0, i)),
                   pl.BlockSpec((SUBLANES, LANES), lambda i: (0, 0))],
        compiler_params=_params(("arbitrary",)),
        name="out_proj_router",
    )(o, w_out, x, mod, g.reshape(1, d), mod, mod, w_r, b_r)


_FF_TILE = 1792


def _snake(i, j, nj):
    return jnp.where(i % 2 == 0, j, nj - 1 - j)


def _ffn_kernel(*refs, tiles_per_batch, n_cast):
    (x_ref, attn_ref, wo_ref, gta_ref, g_ref, sh_ref, sc_ref, gt_ref,
     wg_ref, wu_ref, wd_ref) = refs[:11]
    o_ref = refs[11 + n_cast]
    h_sc = refs[-1]
    _ride_along_cast(refs[11:11 + n_cast], refs[12 + n_cast:12 + 2 * n_cast])
    b = pl.program_id(0) // tiles_per_batch

    @pl.when(pl.program_id(1) == 0)
    def _():
        x = x_ref[...] + gta_ref[pl.ds(b, 1), :] * jnp.dot(
            attn_ref[...], wo_ref[...], preferred_element_type=F32)
        h = _rms_mod(x, g_ref[...], sh_ref[pl.ds(b, 1), :], sc_ref[pl.ds(b, 1), :])
        h_sc[...] = h.astype(BF16)
        o_ref[...] = x

    h = h_sc[...]
    g = jnp.dot(h, wg_ref[...], preferred_element_type=F32)
    u = jnp.dot(h, wu_ref[...], preferred_element_type=F32)
    a = (_silu(g) * u).astype(BF16)
    o_ref[...] += gt_ref[pl.ds(b, 1), :] * jnp.dot(a, wd_ref[...], preferred_element_type=F32)


def _ffn_call(x, attn, w_out, g, mod, w_gu, w_down, seq, cast=(), tm=512, tf=_FF_TILE):
    n, d = x.shape
    f = w_down.shape[0]
    nj = f // tf
    c_views, c_in, c_out, c_shapes = _ride_along(cast, (n // tm) * nj, lambda i, j: i * nj + j)
    return pl.pallas_call(
        functools.partial(_ffn_kernel, tiles_per_batch=seq // tm, n_cast=len(cast)),
        out_shape=[jax.ShapeDtypeStruct((n, d), F32)] + c_shapes,
        grid=(n // tm, nj),
        in_specs=[
            pl.BlockSpec((tm, d), lambda i, j: (i, 0)),
            pl.BlockSpec((tm, d), lambda i, j: (i, 0)),
            pl.BlockSpec((d, d), lambda i, j: (0, 0)),
            pl.BlockSpec((SUBLANES, d), lambda i, j: (0, 2)),
            pl.BlockSpec((1, d), lambda i, j: (0, 0)),
            pl.BlockSpec((SUBLANES, d), lambda i, j: (0, 3)),
            pl.BlockSpec((SUBLANES, d), lambda i, j: (0, 4)),
            pl.BlockSpec((SUBLANES, d), lambda i, j: (0, 5)),
            pl.BlockSpec((d, tf), lambda i, j: (0, _snake(i, j, nj))),
            pl.BlockSpec((d, tf), lambda i, j: (0, _snake(i, j, nj) + nj)),
            pl.BlockSpec((tf, d), lambda i, j: (_snake(i, j, nj), 0)),
        ] + c_in,
        out_specs=[pl.BlockSpec((tm, d), lambda i, j: (i, 0))] + c_out,
        scratch_shapes=[pltpu.VMEM((tm, d), BF16)],
        compiler_params=_params(("parallel", "arbitrary")),
        name="ffn_swiglu",
    )(x, attn, w_out, mod, g.reshape(1, d), mod, mod, mod, w_gu, w_gu, w_down, *c_views)


_MOE_TM = 512


def _dispatch_kernel(d1_ref, d2_ref, zero_ref, h_ref, xs_ref, z_sc, sem):
    tm = h_ref.shape[0]
    base = pl.program_id(0) * tm

    @pl.when(pl.program_id(0) == 0)
    def _():
        z_sc[...] = jnp.zeros_like(z_sc)

        def zero_copy(t):
            return pltpu.make_async_copy(z_sc, xs_ref.at[pl.ds(pl.multiple_of(t * tm, tm), tm)],
                                         sem.at[0])

        def start(t, carry):
            @pl.when(zero_ref[t] != 0)
            def _():
                zero_copy(t).start()
            return carry

        def wait(t, carry):
            @pl.when(zero_ref[t] != 0)
            def _():
                zero_copy(t).wait()
            return carry

        lax.fori_loop(0, zero_ref.shape[0], start, 0)
        lax.fori_loop(0, zero_ref.shape[0], wait, 0)

    def issue(r, carry):
        src = h_ref.at[pl.ds(r, 1)]
        pltpu.make_async_copy(src, xs_ref.at[pl.ds(d1_ref[base + r], 1)], sem.at[0]).start()
        pltpu.make_async_copy(src, xs_ref.at[pl.ds(d2_ref[base + r], 1)],
                              sem.at[1]).start(priority=1)
        return carry

    lax.fori_loop(0, tm, issue, 0, unroll=8)
    pltpu.make_async_copy(h_ref, xs_ref.at[pl.ds(0, tm)], sem.at[0]).wait()
    pltpu.make_async_copy(h_ref, xs_ref.at[pl.ds(0, tm)], sem.at[1]).wait()


def _dispatch_call(dest1, dest2, zero_tile, h, tm=_MOE_TM):
    n, d = h.shape
    n_rows = zero_tile.shape[0] * tm
    return pl.pallas_call(
        _dispatch_kernel,
        out_shape=jax.ShapeDtypeStruct((n_rows, d), h.dtype),
        grid_spec=pltpu.PrefetchScalarGridSpec(
            num_scalar_prefetch=3,
            grid=(n // tm,),
            in_specs=[pl.BlockSpec((tm, d), lambda i, d1, d2, zt: (i, 0))],
            out_specs=pl.BlockSpec(memory_space=pl.ANY),
            scratch_shapes=[pltpu.VMEM((tm, d), h.dtype), pltpu.SemaphoreType.DMA((2,))],
        ),
        compiler_params=_params(("arbitrary",)),
        name="moe_dispatch",
    )(dest1, dest2, zero_tile, h)


def _experts_kernel(te_ref, nt_ref, xs_ref, wg_ref, wu_ref, wd_ref, ye_ref):
    del te_ref
    t = pl.program_id(0)

    @pl.when(pl.program_id(1) == 0)
    def _():
        ye_ref[...] = jnp.zeros_like(ye_ref)

    @pl.when(t < nt_ref[0])
    def _():
        h = xs_ref[...].astype(BF16)
        g = jnp.dot(h, wg_ref[0], preferred_element_type=F32)
        u = jnp.dot(h, wu_ref[0], preferred_element_type=F32)
        a = (_silu(g) * u).astype(BF16)
        ye_ref[...] += jnp.dot(a, wd_ref[0], preferred_element_type=F32)


def _experts_call(tile_expert, n_tiles, xs, w_gu, w_down, tf=_FF_TILE):
    n_rows, d = xs.shape
    _, f, _ = w_down.shape
    nj = f // tf
    tm = _MOE_TM
    tile = lambda t, nt: jnp.maximum(jnp.minimum(t, nt[0] - 1), 0)
    jj = lambda t, j, nt: _snake(tile(t, nt), jnp.where(t < nt[0], j, nj - 1), nj)
    return pl.pallas_call(
        _experts_kernel,
        out_shape=jax.ShapeDtypeStruct((n_rows, d), F32),
        grid_spec=pltpu.PrefetchScalarGridSpec(
            num_scalar_prefetch=2,
            grid=(n_rows // tm, nj),
            in_specs=[
                pl.BlockSpec((tm, d), lambda t, j, te, nt: (tile(t, nt), 0)),
                pl.BlockSpec((1, d, tf), lambda t, j, te, nt: (te[tile(t, nt)], 0, jj(t, j, nt))),
                pl.BlockSpec((1, d, tf), lambda t, j, te, nt: (te[tile(t, nt)], 0, jj(t, j, nt) + nj)),
                pl.BlockSpec((1, tf, d), lambda t, j, te, nt: (te[tile(t, nt)], jj(t, j, nt), 0)),
            ],
            out_specs=pl.BlockSpec((tm, d), lambda t, j, te, nt: (t, 0)),
        ),
        compiler_params=_params(("arbitrary", "arbitrary")),
        name="moe_experts",
    )(tile_expert, n_tiles, xs, w_gu, w_gu, w_down)


def _combine_kernel(d1_ref, d2_ref, ye_ref, x_ref, route_ref, gt_ref, gf_ref, o_ref,
                    y1_sc, y2_sc, sem, *, tiles_per_batch):
    tm = x_ref.shape[0]
    i = pl.program_id(0)
    b = i // tiles_per_batch
    slot = i % 2

    def gather(tile, dst_slot):
        base = tile * tm

        def issue(r, carry):
            pltpu.make_async_copy(ye_ref.at[pl.ds(d1_ref[base + r], 1)],
                                  y1_sc.at[dst_slot, pl.ds(r, 1)], sem.at[0, dst_slot]).start()
            pltpu.make_async_copy(ye_ref.at[pl.ds(d2_ref[base + r], 1)],
                                  y2_sc.at[dst_slot, pl.ds(r, 1)],
                                  sem.at[1, dst_slot]).start(priority=1)
            return carry

        lax.fori_loop(0, tm, issue, 0, unroll=8)

    @pl.when(i == 0)
    def _():
        gather(0, 0)

    @pl.when(i + 1 < pl.num_programs(0))
    def _():
        gather(i + 1, 1 - slot)

    pltpu.make_async_copy(ye_ref.at[pl.ds(0, tm)], y1_sc.at[slot], sem.at[0, slot]).wait()
    pltpu.make_async_copy(ye_ref.at[pl.ds(0, tm)], y2_sc.at[slot], sem.at[1, slot]).wait()
    route = route_ref[...]
    y = route[:, 2:3] * y1_sc[slot] + route[:, 3:4] * y2_sc[slot]
    xn = x_ref[...] + gt_ref[pl.ds(b, 1), :] * y
    ms = jnp.mean(xn * xn, axis=-1, keepdims=True)
    o_ref[...] = xn * lax.rsqrt(ms + EPS) * gf_ref[...]


def _combine_call(dest1, dest2, ye, x, route, mod, g_final, seq, tm=512):
    n, d = x.shape
    return pl.pallas_call(
        functools.partial(_combine_kernel, tiles_per_batch=seq // tm),
        out_shape=jax.ShapeDtypeStruct((n, d), F32),
        grid_spec=pltpu.PrefetchScalarGridSpec(
            num_scalar_prefetch=2,
            grid=(n // tm,),
            in_specs=[
                pl.BlockSpec(memory_space=pl.ANY),
                pl.BlockSpec((tm, d), lambda i, d1, d2: (i, 0)),
                pl.BlockSpec((tm, LANES), lambda i, d1, d2: (i, 0)),
                pl.BlockSpec((SUBLANES, d), lambda i, d1, d2: (0, 5)),
                pl.BlockSpec((1, d), lambda i, d1, d2: (0, 0)),
            ],
            out_specs=pl.BlockSpec((tm, d), lambda i, d1, d2: (i, 0)),
            scratch_shapes=[pltpu.VMEM((2, tm, d), F32), pltpu.VMEM((2, tm, d), F32),
                            pltpu.SemaphoreType.DMA((2, 2))],
        ),
        compiler_params=_params(("arbitrary",)),
        name="moe_combine",
    )(dest1, dest2, ye, x, route, mod, g_final.reshape(1, d))


def _moe_call(h, route, route_t, counts, x, mod, g_final, w_gu, w_down, seq):
    n, d = x.shape
    ne = w_down.shape[0]
    tm = _MOE_TM
    max_tiles = (2 * n) // tm + ne
    e1, e2, _, _, rank1, rank2 = (route_t[k].astype(jnp.int32) for k in range(6))
    cnt = counts[0, :ne].astype(jnp.int32)
    tiles_e = (cnt + tm - 1) // tm
    tile_end = jnp.cumsum(tiles_e)
    row_start = (tile_end - tiles_e) * tm
    dest1 = row_start[e1] + rank1
    dest2 = row_start[e2] + rank2
    n_tiles = tile_end[-1:]
    tile_ids = jnp.arange(max_tiles, dtype=jnp.int32)
    tile_expert = jnp.minimum(
        jnp.sum((tile_ids[:, None] >= tile_end[None, :]).astype(jnp.int32), axis=1), ne - 1)
    is_last = jnp.any((tile_ids[:, None] == tile_end[None, :] - 1) & (tiles_e[None, :] > 0), axis=1)
    zero_tile = (is_last | (tile_ids >= n_tiles[0])).astype(jnp.int32)
    xs = _dispatch_call(dest1, dest2, zero_tile, h)
    ye = _experts_call(tile_expert, n_tiles, xs, w_gu, w_down)
    return _combine_call(dest1, dest2, ye, x, route, mod, g_final, seq)


_SWA_TQ = 2 * CHUNK
_SWA_BAND = 2 * _SWA_TQ


def _swa_bucket_tiles():
    cc = np.arange(_SWA_BAND)[:, None]
    r = np.arange(_SWA_TQ)[None, :]
    rel = cc - _SWA_TQ - r
    nb = REL_BUCKETS // 2
    max_exact = nb // 2
    ret = (rel > 0).astype(np.int32) * nb
    n = np.abs(rel)
    large = max_exact + (np.log(np.maximum(n, 1) / max_exact)
                         / np.log(REL_MAX_DIST / max_exact) * (nb - max_exact)).astype(np.int32)
    large = np.minimum(large, nb - 1)
    bucket = (ret + np.where(n < max_exact, n, large)).astype(np.int32)
    q_chunk = r // CHUNK
    k_chunk = cc // CHUNK
    visible = (k_chunk >= q_chunk) & (k_chunk <= q_chunk + WINDOW_CHUNKS)
    later = np.where(visible, bucket, -1)
    first = np.where(cc >= _SWA_TQ, later, -1)
    return np.stack([first, later]).astype(np.int32)


def _swa_bias_kernel(tbl_ref, bkt_ref, o_ref):
    n_heads = o_ref.shape[1]
    for v in range(2):
        bkt = bkt_ref[v]
        for head in range(n_heads):
            tile = jnp.full(bkt.shape, NEG_BIG, F32)
            for bk in range(REL_BUCKETS):
                tile = jnp.where(bkt == bk, tbl_ref[head, bk] * LOG2E, tile)
            o_ref[v, head] = tile


def _swa_bias_call(rel_bias):
    bkt = jnp.asarray(_swa_bucket_tiles())
    n_heads = rel_bias.shape[1]
    return pl.pallas_call(
        _swa_bias_kernel,
        out_shape=jax.ShapeDtypeStruct((2, n_heads, _SWA_BAND, _SWA_TQ), F32),
        in_specs=[
            pl.BlockSpec(memory_space=pltpu.SMEM),
            pl.BlockSpec(memory_space=pltpu.VMEM),
        ],
        out_specs=pl.BlockSpec(memory_space=pltpu.VMEM),
        name="swa_bias",
    )(rel_bias.T, bkt)


def _swa_kernel(qt_ref, kp_ref, kc_ref, vtp_ref, vtc_ref, bias_ref, sink_ref, o_ref):
    tq = _SWA_TQ
    lane = lax.broadcasted_iota(jnp.int32, (1, LANES), 1)
    ones = jnp.ones((_ONES_ROWS, _SWA_BAND), BF16)
    outs = [None] * (SWA_KV_HEADS * SWA_GROUP)
    units = [(hk, par) for hk in range(SWA_KV_HEADS) for par in range(2)]

    def scores(hk, par):
        ksl = slice(hk * LANES, (hk + 1) * LANES)
        kb = jnp.concatenate([kp_ref[0, :, ksl], kc_ref[0, :, ksl]], axis=0)
        f0 = hk * SWA_GROUP * HEAD_DIM
        wq = jnp.concatenate([qt_ref[f0:f0 + LANES, :], qt_ref[f0 + LANES:f0 + 2 * LANES, :]],
                             axis=1)
        head_lanes = (lane < HEAD_DIM) if par == 0 else (lane >= HEAD_DIM)
        return jnp.dot(jnp.where(head_lanes, kb, 0), wq, preferred_element_type=F32)

    sts = [scores(hk, par) for hk, par in units]
    for (hk, par), st in zip(units, sts):
        vsl = slice(hk * HEAD_DIM, (hk + 1) * HEAD_DIM)
        vt1 = jnp.concatenate(
            [jnp.concatenate([vtp_ref[vsl, :], vtc_ref[vsl, :]], axis=1), ones], axis=0)
        heads = (hk * SWA_GROUP + par, hk * SWA_GROUP + par + 2)
        ps, ms = [], []
        for i, head in enumerate(heads):
            s = st[:, i * tq:(i + 1) * tq] + bias_ref[0, head]
            m = jnp.maximum(jnp.max(s, axis=0, keepdims=True), sink_ref[head])
            ps.append(jnp.exp2(s - m).astype(BF16))
            ms.append(m)
        acc = jnp.dot(vt1, jnp.concatenate(ps, axis=1), preferred_element_type=F32)
        for i, head in enumerate(heads):
            a = acc[:, i * tq:(i + 1) * tq]
            den = a[HEAD_DIM:HEAD_DIM + 1] + jnp.exp2(sink_ref[head] - ms[i])
            outs[head] = a[:HEAD_DIM] / den
    o_ref[0] = jnp.concatenate(outs, axis=0).T.astype(o_ref.dtype)


def _swa_call(q_t, k, v_t, bias, sink, batch, seq):
    d = q_t.shape[0]
    kw = k.shape[2]
    vw = v_t.shape[0]
    tq = _SWA_TQ
    nq = seq // tq
    return pl.pallas_call(
        _swa_kernel,
        out_shape=jax.ShapeDtypeStruct((batch, seq, d), BF16),
        grid=(batch, nq),
        in_specs=[
            pl.BlockSpec((d, tq), lambda b, i: (0, b * nq + i)),
            pl.BlockSpec((1, tq, kw), lambda b, i: (b, jnp.maximum(i - 1, 0), 0)),
            pl.BlockSpec((1, tq, kw), lambda b, i: (b, i, 0)),
            pl.BlockSpec((vw, tq), lambda b, i: (0, b * nq + jnp.maximum(i - 1, 0))),
            pl.BlockSpec((vw, tq), lambda b, i: (0, b * nq + i)),
            pl.BlockSpec((1,) + bias.shape[1:], lambda b, i: (jnp.minimum(i, 1), 0, 0, 0)),
            pl.BlockSpec(sink.shape, lambda b, i: (0, 0, 0)),
        ],
        out_specs=pl.BlockSpec((1, tq, d), lambda b, i: (b, i, 0)),
        compiler_params=_params(("parallel", "arbitrary")),
        name="swa_attention",
    )(q_t, k, k, v_t, v_t, bias, sink)


def kernel(x, c, w_ada, b_ada, g_norm_mix, g_norm_ffn, g_final, fox_w_in, fox_b_f, fox_w_out, swa_w_in, swa_sinks, swa_w_out, rel_bias, ffn_w_gu, ffn_w_down, moe_w_router, moe_b_router, moe_w_gu, moe_w_down):
    batch, seq, d = x.shape
    n = batch * seq
    q_scale = HEAD_DIM ** -0.5
    xf = x.reshape(n, d)

    c_pad = jnp.zeros((SUBLANES, d), F32).at[:batch].set(c)
    mod = _ada_call(c_pad, w_ada, b_ada)
    mod0, mod1 = mod[0], mod[1]

    w_in = fox_w_in[0]
    n_heads = d // HEAD_DIM
    k, q_t, v_t, f, w_gu0, w_down0, w_out0, w_out1 = _attn_proj_call(
        xf, g_norm_mix[0], mod0, w_in[:, d:2 * d].astype(BF16),
        (w_in[:, :d] * (q_scale * LOG2E)).T.astype(BF16), w_in[:, 2 * d:3 * d].T.astype(BF16),
        seq, w_f=jnp.stack(_split_bf16(w_in[:, 3 * d:])),
        cast=(ffn_w_gu[0], ffn_w_down[0], fox_w_out[0], swa_w_out[0]))
    g_cum, r_cum = _cum_call(f, fox_b_f[0], batch, seq)
    r_flat = r_cum.transpose(0, 2, 1).reshape(-1)
    o, w_down1 = _fox_call(r_flat, q_t, k.reshape(batch, seq, d), v_t, g_cum,
                           cast=(moe_w_down[0],))
    w_down1 = w_down1.reshape(moe_w_down.shape[1:])
    x2, w_gu1 = _ffn_call(xf, o.reshape(n, d), w_out0, g_norm_ffn[0], mod0, w_gu0, w_down0, seq,
                          cast=(moe_w_gu[0],))
    w_gu1 = w_gu1.reshape(moe_w_gu.shape[1:])

    w_in = swa_w_in[0]
    kvw = SWA_KV_HEADS * HEAD_DIM
    dup = lambda w: jnp.repeat(w.reshape(d, SWA_KV_HEADS, 1, HEAD_DIM), 2, axis=2).reshape(d, 2 * kvw)
    k, q_t, v_t = _attn_proj_call(
        x2, g_norm_mix[1], mod1, dup(w_in[:, d:d + kvw]).astype(BF16),
        (w_in[:, :d] * (q_scale * LOG2E)).T.astype(BF16), w_in[:, d + kvw:].T.astype(BF16), seq)
    bias = _swa_bias_call(rel_bias)
    sink = jnp.broadcast_to((swa_sinks[0] * LOG2E)[:, None, None], (n_heads, 1, _SWA_TQ))
    o = _swa_call(q_t, k.reshape(batch, seq, 2 * kvw), v_t, bias, sink, batch, seq)
    w_r = jnp.stack(_split_bf16(jnp.zeros((d, LANES), F32).at[:, :N_EXPERTS].set(moe_w_router[0])))
    b_r = jnp.zeros((1, LANES), F32).at[0, :N_EXPERTS].set(moe_b_router[0])
    x3, h4, route, route_t, counts = _out_call(o.reshape(n, d), w_out1, x2, mod1, g_norm_ffn[1],
                                               w_r, b_r, seq)
    out = _moe_call(h4, route, route_t, counts, x3, mod1, g_final, w_gu1, w_down1, seq)
    return out.reshape(batch, seq, d)
```

```python
import functools

import numpy as np
import jax
import jax.numpy as jnp
from jax import lax
from jax.experimental import pallas as pl
from jax.experimental.pallas import tpu as pltpu

F32 = jnp.float32
BF16 = jnp.bfloat16

HEAD_DIM = 64
CHUNK = 64
WINDOW_CHUNKS = 2
REL_BUCKETS = 32
REL_MAX_DIST = 128
SWA_KV_HEADS = 4
SWA_GROUP = 4
N_EXPERTS = 8
EPS = 1e-6

LANES = 128
SUBLANES = 8
VMEM_LIMIT = 56 * 1024 * 1024

NEG_BIG = -1e30


def _params(sem, vmem=VMEM_LIMIT):
    return pltpu.CompilerParams(dimension_semantics=sem, vmem_limit_bytes=vmem)


def _rms_mod(x, g, shift, scale):
    ms = jnp.mean(x * x, axis=-1, keepdims=True)
    y = x * lax.rsqrt(ms + EPS) * g
    return y * (1.0 + scale) + shift


def _silu(x):
    return x / (1.0 + jnp.exp(-x))


def _split_bf16(x):
    hi = x.astype(BF16)
    return hi, (x - hi.astype(F32)).astype(BF16)


def _split_cols(w):
    return jnp.concatenate(_split_bf16(w), axis=1)


def _dot_split(x_hi, x_lo, w_ref):
    n = w_ref.shape[1] // 2
    y = jnp.dot(x_hi, w_ref[...], preferred_element_type=F32)
    return y[:, :n] + y[:, n:] + jnp.dot(x_lo, w_ref[:, :n], preferred_element_type=F32)


_BF16_SUBLANES = 16


def _ride_along(arrays, n_steps, step_index):
    views, in_specs, out_specs, out_shapes = [], [], [], []
    for a in arrays:
        v = a.reshape(-1, a.shape[-1])
        rows, rem = divmod(v.shape[0], n_steps)
        assert rem == 0 and rows % _BF16_SUBLANES == 0, v.shape
        spec = pl.BlockSpec((rows, v.shape[1]), lambda *g: (step_index(*g), 0))
        views.append(v)
        in_specs.append(spec)
        out_specs.append(spec)
        out_shapes.append(jax.ShapeDtypeStruct(v.shape, BF16))
    return views, in_specs, out_specs, out_shapes


def _ride_along_cast(in_refs, out_refs):
    for src, dst in zip(in_refs, out_refs):
        dst[...] = src[...].astype(BF16)


def _ada_kernel(c_ref, w_ref, b_ref, o_ref):
    c_hi, c_lo = _split_bf16(_silu(c_ref[...]))
    w_hi, w_lo = _split_bf16(w_ref[0])
    o_ref[0] = (jnp.dot(c_hi, w_hi, preferred_element_type=F32)
                + jnp.dot(c_lo, w_hi, preferred_element_type=F32)
                + jnp.dot(c_hi, w_lo, preferred_element_type=F32) + b_ref[0])


def _ada_call(c_pad, w_ada, b_ada):
    depth, d, n = w_ada.shape
    tn = 1536
    return pl.pallas_call(
        _ada_kernel,
        out_shape=jax.ShapeDtypeStruct((depth, SUBLANES, n), F32),
        grid=(depth, n // tn),
        in_specs=[
            pl.BlockSpec((SUBLANES, d), lambda l, j: (0, 0)),
            pl.BlockSpec((1, d, tn), lambda l, j: (l, 0, j)),
            pl.BlockSpec((1, 1, tn), lambda l, j: (l, 0, j)),
        ],
        out_specs=pl.BlockSpec((1, SUBLANES, tn), lambda l, j: (l, 0, j)),
        compiler_params=_params(("parallel", "parallel")),
        name="ada_mod",
    )(c_pad, w_ada, b_ada.reshape(depth, 1, n))


_NT = (((1,), (1,)), ((), ()))


def _attn_proj_kernel(*refs, tiles_per_batch, with_gate, n_cast):
    x_ref, g_ref, sh_ref, sc_ref, wk_ref, wqt_ref, wvt_ref = refs[:7]
    n_in = 7 + with_gate + n_cast
    k_ref, qt_ref, vt_ref = refs[n_in:n_in + 3]
    b = pl.program_id(0) // tiles_per_batch
    h = _rms_mod(x_ref[...], g_ref[...], sh_ref[pl.ds(b, 1), :], sc_ref[pl.ds(b, 1), :])
    hb = h.astype(BF16)
    k_ref[...] = jnp.dot(hb, wk_ref[...], preferred_element_type=F32).astype(BF16)
    qt_ref[...] = lax.dot_general(wqt_ref[...], hb, _NT, preferred_element_type=F32).astype(BF16)
    vt_ref[...] = lax.dot_general(wvt_ref[...], hb, _NT, preferred_element_type=F32).astype(BF16)
    if with_gate:
        refs[n_in + 3][...] = _dot_split(hb, (h - hb.astype(F32)).astype(BF16), refs[7])
    _ride_along_cast(refs[n_in - n_cast:n_in], refs[len(refs) - n_cast:])


def _attn_proj_call(x, g, mod, w_k, w_qt, w_vt, seq, w_f=None, cast=(), tm=512):
    n, d = x.shape
    with_gate = w_f is not None
    full = lambda a: pl.BlockSpec(a.shape, lambda i: (0, 0))
    c_views, c_in, c_out, c_shapes = _ride_along(cast, n // tm, lambda i: i)
    in_specs = [
        pl.BlockSpec((tm, d), lambda i: (i, 0)),
        pl.BlockSpec((1, d), lambda i: (0, 0)),
        pl.BlockSpec((SUBLANES, d), lambda i: (0, 0)),
        pl.BlockSpec((SUBLANES, d), lambda i: (0, 1)),
        full(w_k), full(w_qt), full(w_vt),
    ]
    args = [x, g.reshape(1, d), mod, mod, w_k, w_qt, w_vt]
    out_shape = [jax.ShapeDtypeStruct((n, w_k.shape[1]), BF16),
                 jax.ShapeDtypeStruct((w_qt.shape[0], n), BF16),
                 jax.ShapeDtypeStruct((w_vt.shape[0], n), BF16)]
    out_specs = [pl.BlockSpec((tm, w_k.shape[1]), lambda i: (i, 0)),
                 pl.BlockSpec((w_qt.shape[0], tm), lambda i: (0, i)),
                 pl.BlockSpec((w_vt.shape[0], tm), lambda i: (0, i))]
    if with_gate:
        in_specs.append(full(w_f))
        args.append(w_f)
        out_shape.append(jax.ShapeDtypeStruct((n, w_f.shape[1] // 2), F32))
        out_specs.append(pl.BlockSpec((tm, w_f.shape[1] // 2), lambda i: (i, 0)))
    return pl.pallas_call(
        functools.partial(_attn_proj_kernel, tiles_per_batch=seq // tm, with_gate=with_gate,
                          n_cast=len(cast)),
        out_shape=out_shape + c_shapes,
        grid=(n // tm,),
        in_specs=in_specs + c_in,
        out_specs=out_specs + c_out,
        compiler_params=_params(("parallel",)),
        name="attn_proj_gate" if with_gate else "attn_proj",
    )(*args, *c_views)


_FOX_T = 512
LOG2E = 1.4426950408889634


def _cum_kernel(f_ref, bf_ref, g_ref, r_ref):
    x = f_ref[...] + bf_ref[...]
    logf = (jnp.minimum(x, 0.0) - jnp.log(1.0 + jnp.exp(-jnp.abs(x)))) * LOG2E
    seq, nh = logf.shape
    r = lax.broadcasted_iota(jnp.int32, (_FOX_T, _FOX_T), 0)
    c = lax.broadcasted_iota(jnp.int32, (_FOX_T, _FOX_T), 1)
    lower = jnp.where(c <= r, 1.0, 0.0).astype(BF16)
    hi = logf.astype(BF16).astype(F32)
    mid = (logf - hi).astype(BF16).astype(F32)
    parts = jnp.concatenate([hi, mid, logf - hi - mid], axis=1).astype(BF16)
    carry = jnp.zeros((1, nh), F32)
    for ch in range(seq // _FOX_T):
        rows = slice(ch * _FOX_T, (ch + 1) * _FOX_T)
        y = jnp.dot(lower, parts[rows, :], preferred_element_type=F32)
        cs = y[:, :nh] + y[:, nh:2 * nh] + y[:, 2 * nh:]
        g_ref[0, rows, :] = cs
        r_ref[0, ch:ch + 1, :] = carry
        carry = carry + cs[_FOX_T - 1:_FOX_T, :]


def _cum_call(f, b_f, batch, seq):
    nh = f.shape[1]
    return pl.pallas_call(
        _cum_kernel,
        out_shape=[jax.ShapeDtypeStruct((batch, seq, nh), F32),
                   jax.ShapeDtypeStruct((batch, seq // _FOX_T, nh), F32)],
        grid=(batch,),
        in_specs=[
            pl.BlockSpec((seq, nh), lambda b: (b, 0)),
            pl.BlockSpec((1, nh), lambda b: (0, 0)),
        ],
        out_specs=[pl.BlockSpec((1, seq, nh), lambda b: (b, 0, 0)),
                   pl.BlockSpec((1, seq // _FOX_T, nh), lambda b: (b, 0, 0))],
        compiler_params=_params(("parallel",)),
        name="forget_cumsum",
    )(f, b_f.reshape(1, nh))


_ONES_ROWS = 16


def _fox_kernel(*refs, n_cast):
    r_ref, qt_ref, k_ref, vt_ref, g_ref = refs[:5]
    o_ref = refs[5 + n_cast]
    ka_sc, m_sc, acc_sc, sa_sc, sb_sc, xa_sc, xb_sc = refs[6 + 2 * n_cast:]
    _ride_along_cast(refs[5:5 + n_cast], refs[6 + n_cast:6 + 2 * n_cast])
    t = _FOX_T
    b, hp = pl.program_id(0), pl.program_id(1)
    seq = k_ref.shape[1]
    n_blocks = seq // t
    n_heads = 2 * pl.num_programs(1)

    lane = lax.broadcasted_iota(jnp.int32, (1, LANES), 1)
    feat = lax.broadcasted_iota(jnp.int32, (LANES, 1), 0)
    aug0 = [HEAD_DIM, 0]

    g_all = g_ref[0]
    head_col = lax.broadcasted_iota(jnp.int32, g_all.shape, 1)
    for hh in range(2):
        g_head = jnp.sum(jnp.where(head_col == 2 * hp + hh, g_all, 0.0), axis=1, keepdims=True)
        gb = jnp.broadcast_to(g_head, (seq, LANES))
        hi = gb.astype(BF16).astype(F32)
        mid = (gb - hi).astype(BF16).astype(F32)
        lo = gb - hi - mid
        aug = jnp.where(lane == aug0[hh], hi,
                        jnp.where(lane == aug0[hh] + 1, mid,
                                  jnp.where(lane == aug0[hh] + 2, lo, 0.0)))
        own = (lane >= hh * HEAD_DIM) & (lane < (hh + 1) * HEAD_DIM)
        ka_sc[hh] = jnp.where(own, k_ref[0].astype(F32), aug).astype(BF16)

    key = lax.broadcasted_iota(jnp.int32, (t, t), 0)
    qry = lax.broadcasted_iota(jnp.int32, (t, t), 1)
    ones = jnp.ones((_ONES_ROWS, t), BF16)
    feat_t = lax.broadcasted_iota(jnp.int32, (LANES, t), 0)
    r_base = [(b * n_heads + 2 * hp + hh) * n_blocks for hh in range(2)]
    bufs = ((sa_sc, xa_sc), (sb_sc, xb_sc))

    def queries(qi):
        qt2 = qt_ref[:, qi * t:(qi + 1) * t]
        return [jnp.where((feat >= hh * HEAD_DIM) & (feat < (hh + 1) * HEAD_DIM), qt2, 0)
                + jnp.where((feat_t >= aug0[hh]) & (feat_t < aug0[hh] + 3), -1.0, 0.0).astype(BF16)
                for hh in range(2)]

    def scores(qh, kb, dst):
        for hh in range(2):
            st = jnp.dot(ka_sc[hh, kb * t:(kb + 1) * t, :], qh[hh],
                         preferred_element_type=F32)
            dst[0][hh] = st
            dst[1][hh] = jnp.max(st, axis=0, keepdims=True)

    def softmax_pv(qi, kb, cur):
        for hh in range(2):
            st = cur[0][hh]
            if kb == qi:
                st = jnp.where(key <= qry, st, -jnp.inf)
                st_max = jnp.max(st, axis=0, keepdims=True)
            else:
                st_max = cur[1][hh]
            c = r_ref[r_base[hh] + kb] - r_ref[r_base[hh] + qi]
            m_old = m_sc[hh]
            m_new = jnp.maximum(m_old, st_max - c)
            alpha = jnp.exp2(m_old - m_new)
            p = jnp.exp2(st - (m_new + c)).astype(BF16)
            vt1 = jnp.concatenate(
                [vt_ref[hh * HEAD_DIM:(hh + 1) * HEAD_DIM, kb * t:(kb + 1) * t], ones], axis=0)
            acc_sc[hh] = alpha * acc_sc[hh] + jnp.dot(vt1, p, preferred_element_type=F32)
            m_sc[hh] = m_new

    pairs = [(qi, kb) for qi in range(n_blocks) for kb in range(qi + 1)]
    qh = queries(0)
    scores(qh, 0, bufs[0])
    for s, (qi, kb) in enumerate(pairs):
        if kb == 0:
            m_sc[...] = jnp.full_like(m_sc, -jnp.inf)
            acc_sc[...] = jnp.zeros_like(acc_sc)
        if s + 1 < len(pairs):
            qi_n, kb_n = pairs[s + 1]
            if qi_n != qi:
                qh = queries(qi_n)
            scores(qh, kb_n, bufs[(s + 1) % 2])
        softmax_pv(qi, kb, bufs[s % 2])
        if kb == qi:
            outs = [acc_sc[hh, :HEAD_DIM, :] / acc_sc[hh, HEAD_DIM:HEAD_DIM + 1, :]
                    for hh in range(2)]
            o_ref[0, qi * t:(qi + 1) * t, :] = jnp.concatenate(outs, axis=0).T.astype(o_ref.dtype)


def _fox_call(r_flat, q_t, k, v_t, g_cum, cast=()):
    batch, seq, d = k.shape
    n_pairs = d // LANES
    t = _FOX_T
    c_views, c_in, c_out, c_shapes = _ride_along(
        cast, batch * n_pairs, lambda b, h: b * n_pairs + h)
    return pl.pallas_call(
        functools.partial(_fox_kernel, n_cast=len(cast)),
        out_shape=[jax.ShapeDtypeStruct((batch, seq, d), BF16)] + c_shapes,
        grid=(batch, n_pairs),
        in_specs=[
            pl.BlockSpec(memory_space=pltpu.SMEM),
            pl.BlockSpec((LANES, seq), lambda b, h: (h, b)),
            pl.BlockSpec((1, seq, LANES), lambda b, h: (b, 0, h)),
            pl.BlockSpec((LANES, seq), lambda b, h: (h, b)),
            pl.BlockSpec((1, seq, g_cum.shape[2]), lambda b, h: (b, 0, 0)),
        ] + c_in,
        out_specs=[pl.BlockSpec((1, seq, LANES), lambda b, h: (b, 0, h))] + c_out,
        scratch_shapes=[pltpu.VMEM((2, seq, LANES), BF16), pltpu.VMEM((2, 1, t), F32),
                        pltpu.VMEM((2, HEAD_DIM + _ONES_ROWS, t), F32),
                        pltpu.VMEM((2, t, t), F32), pltpu.VMEM((2, t, t), F32),
                        pltpu.VMEM((2, 1, t), F32), pltpu.VMEM((2, 1, t), F32)],
        compiler_params=_params(("parallel", "parallel")),
        name="fox_attention",
    )(r_flat, q_t, k, v_t, g_cum, *c_views)


def _out_kernel(o_ref, w_ref, x_ref, gt_ref, g_ref, sh_ref, sc_ref, wr_ref, br_ref,
                xo_ref, h_ref, route_ref, route_t_ref, cnt_ref, *, tiles_per_batch):
    b = pl.program_id(0) // tiles_per_batch
    y = jnp.dot(o_ref[...], w_ref[...], preferred_element_type=F32)
    xn = x_ref[...] + gt_ref[pl.ds(b, 1), :] * y
    xo_ref[...] = xn
    h = _rms_mod(xn, g_ref[...], sh_ref[pl.ds(b, 1), :], sc_ref[pl.ds(b, 1), :])
    h_ref[...] = h
    logits = _dot_split(*_split_bf16(h), wr_ref) + br_ref[...]
    tm = logits.shape[0]
    lane = lax.broadcasted_iota(jnp.int32, logits.shape, 1).astype(F32)
    logits = jnp.where(lane < N_EXPERTS, logits, -jnp.inf)
    m1 = jnp.max(logits, axis=1, keepdims=True)
    i1 = jnp.min(jnp.where(logits == m1, lane, float(LANES)), axis=1, keepdims=True)
    rest = jnp.where(lane == i1, -jnp.inf, logits)
    m2 = jnp.max(rest, axis=1, keepdims=True)
    i2 = jnp.min(jnp.where(rest == m2, lane, float(LANES)), axis=1, keepdims=True)
    e2 = jnp.exp(m2 - m1)
    den = 1.0 + e2

    @pl.when(pl.program_id(0) == 0)
    def _():
        cnt_ref[...] = jnp.zeros_like(cnt_ref)

    sel1 = lane == i1
    sel2 = lane == i2
    onehot = jnp.where(sel1 | sel2, 1.0, 0.0)
    r = lax.broadcasted_iota(jnp.int32, (tm, tm), 0)
    c = lax.broadcasted_iota(jnp.int32, (tm, tm), 1)
    lower = jnp.where(c < r, 1.0, 0.0).astype(BF16)
    before = jnp.dot(lower, onehot.astype(BF16), preferred_element_type=F32) + cnt_ref[0:1, :]
    rank1 = jnp.sum(jnp.where(sel1, before, 0.0), axis=1, keepdims=True)
    rank2 = jnp.sum(jnp.where(sel2, before, 0.0), axis=1, keepdims=True)
    cnt_ref[0:1, :] = cnt_ref[0:1, :] + jnp.sum(onehot, axis=0, keepdims=True)
    route = jnp.where(lane == 0.0, i1, 0.0)
    for k, val in enumerate((i2, 1.0 / den, e2 / den, rank1, rank2), start=1):
        route = jnp.where(lane == float(k), val, route)
    route_ref[...] = route
    route_t_ref[...] = route.T[:SUBLANES, :]


def _out_call(o, w_out, x, mod, g, w_r, b_r, seq, tm=512):
    n, d = x.shape
    row = lambda w: pl.BlockSpec((tm, w), lambda i: (i, 0))
    col = lambda c: pl.BlockSpec((SUBLANES, d), lambda i: (0, c))
    return pl.pallas_call(
        functools.partial(_out_kernel, tiles_per_batch=seq // tm),
        out_shape=[jax.ShapeDtypeStruct((n, d), F32), jax.ShapeDtypeStruct((n, d), F32),
                   jax.ShapeDtypeStruct((n, LANES), F32), jax.ShapeDtypeStruct((SUBLANES, n), F32),
                   jax.ShapeDtypeStruct((SUBLANES, LANES), F32)],
        grid=(n // tm,),
        in_specs=[
            row(d),
            pl.BlockSpec((d, d), lambda i: (0, 0)),
            row(d),
            col(2),
            pl.BlockSpec((1, d), lambda i: (0, 0)),
            col(3),
            col(4),
            pl.BlockSpec((d, 2 * LANES), lambda i: (0, 0)),
            pl.BlockSpec((1, LANES), lambda i: (0, 0)),
        ],
        out_specs=[row(d), row(d), row(LANES), pl.BlockSpec((SUBLANES, tm), lambda i: (0, i)),
                   pl.BlockSpec((SUBLANES, LANES), lambda i: (0, 0))],
        compiler_params=_params(("arbitrary",)),
        name="out_proj_router",
    )(o, w_out, x, mod, g.reshape(1, d), mod, mod, w_r, b_r)


_FF_TILE = 1792


def _snake(i, j, nj):
    return jnp.where(i % 2 == 0, j, nj - 1 - j)


def _ffn_kernel(*refs, tiles_per_batch, n_cast):
    (x_ref, attn_ref, wo_ref, gta_ref, g_ref, sh_ref, sc_ref, gt_ref,
     wg_ref, wu_ref, wd_ref) = refs[:11]
    o_ref = refs[11 + n_cast]
    h_sc = refs[-1]
    _ride_along_cast(refs[11:11 + n_cast], refs[12 + n_cast:12 + 2 * n_cast])
    b = pl.program_id(0) // tiles_per_batch

    @pl.when(pl.program_id(1) == 0)
    def _():
        x = x_ref[...] + gta_ref[pl.ds(b, 1), :] * jnp.dot(
            attn_ref[...], wo_ref[...], preferred_element_type=F32)
        h = _rms_mod(x, g_ref[...], sh_ref[pl.ds(b, 1), :], sc_ref[pl.ds(b, 1), :])
        h_sc[...] = h.astype(BF16)
        o_ref[...] = x

    h = h_sc[...]
    g = jnp.dot(h, wg_ref[...], preferred_element_type=F32)
    u = jnp.dot(h, wu_ref[...], preferred_element_type=F32)
    a = (_silu(g) * u).astype(BF16)
    o_ref[...] += gt_ref[pl.ds(b, 1), :] * jnp.dot(a, wd_ref[...], preferred_element_type=F32)


def _ffn_call(x, attn, w_out, g, mod, w_gu, w_down, seq, cast=(), tm=512, tf=_FF_TILE):
    n, d = x.shape
    f = w_down.shape[0]
    nj = f // tf
    c_views, c_in, c_out, c_shapes = _ride_along(cast, (n // tm) * nj, lambda i, j: i * nj + j)
    return pl.pallas_call(
        functools.partial(_ffn_kernel, tiles_per_batch=seq // tm, n_cast=len(cast)),
        out_shape=[jax.ShapeDtypeStruct((n, d), F32)] + c_shapes,
        grid=(n // tm, nj),
        in_specs=[
            pl.BlockSpec((tm, d), lambda i, j: (i, 0)),
            pl.BlockSpec((tm, d), lambda i, j: (i, 0)),
            pl.BlockSpec((d, d), lambda i, j: (0, 0)),
            pl.BlockSpec((SUBLANES, d), lambda i, j: (0, 2)),
            pl.BlockSpec((1, d), lambda i, j: (0, 0)),
            pl.BlockSpec((SUBLANES, d), lambda i, j: (0, 3)),
            pl.BlockSpec((SUBLANES, d), lambda i, j: (0, 4)),
            pl.BlockSpec((SUBLANES, d), lambda i, j: (0, 5)),
            pl.BlockSpec((d, tf), lambda i, j: (0, _snake(i, j, nj))),
            pl.BlockSpec((d, tf), lambda i, j: (0, _snake(i, j, nj) + nj)),
            pl.BlockSpec((tf, d), lambda i, j: (_snake(i, j, nj), 0)),
        ] + c_in,
        out_specs=[pl.BlockSpec((tm, d), lambda i, j: (i, 0))] + c_out,
        scratch_shapes=[pltpu.VMEM((tm, d), BF16)],
        compiler_params=_params(("parallel", "arbitrary")),
        name="ffn_swiglu",
    )(x, attn, w_out, mod, g.reshape(1, d), mod, mod, mod, w_gu, w_gu, w_down, *c_views)


_MOE_TM = 512


def _dispatch_kernel(d1_ref, d2_ref, zero_ref, h_ref, xs_ref, z_sc, sem):
    tm = h_ref.shape[0]
    base = pl.program_id(0) * tm

    @pl.when(pl.program_id(0) == 0)
    def _():
        z_sc[...] = jnp.zeros_like(z_sc)

        def zero_copy(t):
            return pltpu.make_async_copy(z_sc, xs_ref.at[pl.ds(pl.multiple_of(t * tm, tm), tm)],
                                         sem.at[0])

        def start(t, carry):
            @pl.when(zero_ref[t] != 0)
            def _():
                zero_copy(t).start()
            return carry

        def wait(t, carry):
            @pl.when(zero_ref[t] != 0)
            def _():
                zero_copy(t).wait()
            return carry

        lax.fori_loop(0, zero_ref.shape[0], start, 0)
        lax.fori_loop(0, zero_ref.shape[0], wait, 0)

    def issue(r, carry):
        src = h_ref.at[pl.ds(r, 1)]
        pltpu.make_async_copy(src, xs_ref.at[pl.ds(d1_ref[base + r], 1)], sem.at[0]).start()
        pltpu.make_async_copy(src, xs_ref.at[pl.ds(d2_ref[base + r], 1)],
                              sem.at[1]).start(priority=1)
        return carry

    lax.fori_loop(0, tm, issue, 0, unroll=8)
    pltpu.make_async_copy(h_ref, xs_ref.at[pl.ds(0, tm)], sem.at[0]).wait()
    pltpu.make_async_copy(h_ref, xs_ref.at[pl.ds(0, tm)], sem.at[1]).wait()


def _dispatch_call(dest1, dest2, zero_tile, h, tm=_MOE_TM):
    n, d = h.shape
    n_rows = zero_tile.shape[0] * tm
    return pl.pallas_call(
        _dispatch_kernel,
        out_shape=jax.ShapeDtypeStruct((n_rows, d), h.dtype),
        grid_spec=pltpu.PrefetchScalarGridSpec(
            num_scalar_prefetch=3,
            grid=(n // tm,),
            in_specs=[pl.BlockSpec((tm, d), lambda i, d1, d2, zt: (i, 0))],
            out_specs=pl.BlockSpec(memory_space=pl.ANY),
            scratch_shapes=[pltpu.VMEM((tm, d), h.dtype), pltpu.SemaphoreType.DMA((2,))],
        ),
        compiler_params=_params(("arbitrary",)),
        name="moe_dispatch",
    )(dest1, dest2, zero_tile, h)


def _experts_kernel(te_ref, nt_ref, xs_ref, wg_ref, wu_ref, wd_ref, ye_ref):
    del te_ref
    t = pl.program_id(0)

    @pl.when(pl.program_id(1) == 0)
    def _():
        ye_ref[...] = jnp.zeros_like(ye_ref)

    @pl.when(t < nt_ref[0])
    def _():
        h = xs_ref[...].astype(BF16)
        g = jnp.dot(h, wg_ref[0], preferred_element_type=F32)
        u = jnp.dot(h, wu_ref[0], preferred_element_type=F32)
        a = (_silu(g) * u).astype(BF16)
        ye_ref[...] += jnp.dot(a, wd_ref[0], preferred_element_type=F32)


def _experts_call(tile_expert, n_tiles, xs, w_gu, w_down, tf=_FF_TILE):
    n_rows, d = xs.shape
    _, f, _ = w_down.shape
    nj = f // tf
    tm = _MOE_TM
    tile = lambda t, nt: jnp.maximum(jnp.minimum(t, nt[0] - 1), 0)
    jj = lambda t, j, nt: _snake(tile(t, nt), jnp.where(t < nt[0], j, nj - 1), nj)
    return pl.pallas_call(
        _experts_kernel,
        out_shape=jax.ShapeDtypeStruct((n_rows, d), F32),
        grid_spec=pltpu.PrefetchScalarGridSpec(
            num_scalar_prefetch=2,
            grid=(n_rows // tm, nj),
            in_specs=[
                pl.BlockSpec((tm, d), lambda t, j, te, nt: (tile(t, nt), 0)),
                pl.BlockSpec((1, d, tf), lambda t, j, te, nt: (te[tile(t, nt)], 0, jj(t, j, nt))),
                pl.BlockSpec((1, d, tf), lambda t, j, te, nt: (te[tile(t, nt)], 0, jj(t, j, nt) + nj)),
                pl.BlockSpec((1, tf, d), lambda t, j, te, nt: (te[tile(t, nt)], jj(t, j, nt), 0)),
            ],
            out_specs=pl.BlockSpec((tm, d), lambda t, j, te, nt: (t, 0)),
        ),
        compiler_params=_params(("arbitrary", "arbitrary")),
        name="moe_experts",
    )(tile_expert, n_tiles, xs, w_gu, w_gu, w_down)


def _combine_kernel(d1_ref, d2_ref, ye_ref, x_ref, route_ref, gt_ref, gf_ref, o_ref,
                    y1_sc, y2_sc, sem, *, tiles_per_batch):
    tm = x_ref.shape[0]
    i = pl.program_id(0)
    b = i // tiles_per_batch
    slot = i % 2

    def gather(tile, dst_slot):
        base = tile * tm

        def issue(r, carry):
            pltpu.make_async_copy(ye_ref.at[pl.ds(d1_ref[base + r], 1)],
                                  y1_sc.at[dst_slot, pl.ds(r, 1)], sem.at[0, dst_slot]).start()
            pltpu.make_async_copy(ye_ref.at[pl.ds(d2_ref[base + r], 1)],
                                  y2_sc.at[dst_slot, pl.ds(r, 1)],
                                  sem.at[1, dst_slot]).start(priority=1)
            return carry

        lax.fori_loop(0, tm, issue, 0, unroll=8)

    @pl.when(i == 0)
    def _():
        gather(0, 0)

    @pl.when(i + 1 < pl.num_programs(0))
    def _():
        gather(i + 1, 1 - slot)

    pltpu.make_async_copy(ye_ref.at[pl.ds(0, tm)], y1_sc.at[slot], sem.at[0, slot]).wait()
    pltpu.make_async_copy(ye_ref.at[pl.ds(0, tm)], y2_sc.at[slot], sem.at[1, slot]).wait()
    route = route_ref[...]
    y = route[:, 2:3] * y1_sc[slot] + route[:, 3:4] * y2_sc[slot]
    xn = x_ref[...] + gt_ref[pl.ds(b, 1), :] * y
    ms = jnp.mean(xn * xn, axis=-1, keepdims=True)
    o_ref[...] = xn * lax.rsqrt(ms + EPS) * gf_ref[...]


def _combine_call(dest1, dest2, ye, x, route, mod, g_final, seq, tm=512):
    n, d = x.shape
    return pl.pallas_call(
        functools.partial(_combine_kernel, tiles_per_batch=seq // tm),
        out_shape=jax.ShapeDtypeStruct((n, d), F32),
        grid_spec=pltpu.PrefetchScalarGridSpec(
            num_scalar_prefetch=2,
            grid=(n // tm,),
            in_specs=[
                pl.BlockSpec(memory_space=pl.ANY),
                pl.BlockSpec((tm, d), lambda i, d1, d2: (i, 0)),
                pl.BlockSpec((tm, LANES), lambda i, d1, d2: (i, 0)),
                pl.BlockSpec((SUBLANES, d), lambda i, d1, d2: (0, 5)),
                pl.BlockSpec((1, d), lambda i, d1, d2: (0, 0)),
            ],
            out_specs=pl.BlockSpec((tm, d), lambda i, d1, d2: (i, 0)),
            scratch_shapes=[pltpu.VMEM((2, tm, d), F32), pltpu.VMEM((2, tm, d), F32),
                            pltpu.SemaphoreType.DMA((2, 2))],
        ),
        compiler_params=_params(("arbitrary",)),
        name="moe_combine",
    )(dest1, dest2, ye, x, route, mod, g_final.reshape(1, d))


def _moe_call(h, route, route_t, counts, x, mod, g_final, w_gu, w_down, seq):
    n, d = x.shape
    ne = w_down.shape[0]
    tm = _MOE_TM
    max_tiles = (2 * n) // tm + ne
    e1, e2, _, _, rank1, rank2 = (route_t[k].astype(jnp.int32) for k in range(6))
    cnt = counts[0, :ne].astype(jnp.int32)
    tiles_e = (cnt + tm - 1) // tm
    tile_end = jnp.cumsum(tiles_e)
    row_start = (tile_end - tiles_e) * tm
    dest1 = row_start[e1] + rank1
    dest2 = row_start[e2] + rank2
    n_tiles = tile_end[-1:]
    tile_ids = jnp.arange(max_tiles, dtype=jnp.int32)
    tile_expert = jnp.minimum(
        jnp.sum((tile_ids[:, None] >= tile_end[None, :]).astype(jnp.int32), axis=1), ne - 1)
    is_last = jnp.any((tile_ids[:, None] == tile_end[None, :] - 1) & (tiles_e[None, :] > 0), axis=1)
    zero_tile = (is_last | (tile_ids >= n_tiles[0])).astype(jnp.int32)
    xs = _dispatch_call(dest1, dest2, zero_tile, h)
    ye = _experts_call(tile_expert, n_tiles, xs, w_gu, w_down)
    return _combine_call(dest1, dest2, ye, x, route, mod, g_final, seq)


_SWA_TQ = 2 * CHUNK
_SWA_BAND = 2 * _SWA_TQ


def _swa_bucket_tiles():
    cc = np.arange(_SWA_BAND)[:, None]
    r = np.arange(_SWA_TQ)[None, :]
    rel = cc - _SWA_TQ - r
    nb = REL_BUCKETS // 2
    max_exact = nb // 2
    ret = (rel > 0).astype(np.int32) * nb
    n = np.abs(rel)
    large = max_exact + (np.log(np.maximum(n, 1) / max_exact)
                         / np.log(REL_MAX_DIST / max_exact) * (nb - max_exact)).astype(np.int32)
    large = np.minimum(large, nb - 1)
    bucket = (ret + np.where(n < max_exact, n, large)).astype(np.int32)
    q_chunk = r // CHUNK
    k_chunk = cc // CHUNK
    visible = (k_chunk >= q_chunk) & (k_chunk <= q_chunk + WINDOW_CHUNKS)
    later = np.where(visible, bucket, -1)
    first = np.where(cc >= _SWA_TQ, later, -1)
    return np.stack([first, later]).astype(np.int32)


def _swa_bias_kernel(tbl_ref, bkt_ref, o_ref):
    n_heads = o_ref.shape[1]
    for v in range(2):
        bkt = bkt_ref[v]
        for head in range(n_heads):
            tile = jnp.full(bkt.shape, NEG_BIG, F32)
            for bk in range(REL_BUCKETS):
                tile = jnp.where(bkt == bk, tbl_ref[head, bk] * LOG2E, tile)
            o_ref[v, head] = tile


def _swa_bias_call(rel_bias):
    bkt = jnp.asarray(_swa_bucket_tiles())
    n_heads = rel_bias.shape[1]
    return pl.pallas_call(
        _swa_bias_kernel,
        out_shape=jax.ShapeDtypeStruct((2, n_heads, _SWA_BAND, _SWA_TQ), F32),
        in_specs=[
            pl.BlockSpec(memory_space=pltpu.SMEM),
            pl.BlockSpec(memory_space=pltpu.VMEM),
        ],
        out_specs=pl.BlockSpec(memory_space=pltpu.VMEM),
        name="swa_bias",
    )(rel_bias.T, bkt)


def _swa_kernel(qt_ref, kp_ref, kc_ref, vtp_ref, vtc_ref, bias_ref, sink_ref, o_ref):
    tq = _SWA_TQ
    lane = lax.broadcasted_iota(jnp.int32, (1, LANES), 1)
    ones = jnp.ones((_ONES_ROWS, _SWA_BAND), BF16)
    outs = [None] * (SWA_KV_HEADS * SWA_GROUP)
    units = [(hk, par) for hk in range(SWA_KV_HEADS) for par in range(2)]

    def scores(hk, par):
        ksl = slice(hk * LANES, (hk + 1) * LANES)
        kb = jnp.concatenate([kp_ref[0, :, ksl], kc_ref[0, :, ksl]], axis=0)
        f0 = hk * SWA_GROUP * HEAD_DIM
        wq = jnp.concatenate([qt_ref[f0:f0 + LANES, :], qt_ref[f0 + LANES:f0 + 2 * LANES, :]],
                             axis=1)
        head_lanes = (lane < HEAD_DIM) if par == 0 else (lane >= HEAD_DIM)
        return jnp.dot(jnp.where(head_lanes, kb, 0), wq, preferred_element_type=F32)

    sts = [scores(hk, par) for hk, par in units]
    for (hk, par), st in zip(units, sts):
        vsl = slice(hk * HEAD_DIM, (hk + 1) * HEAD_DIM)
        vt1 = jnp.concatenate(
            [jnp.concatenate([vtp_ref[vsl, :], vtc_ref[vsl, :]], axis=1), ones], axis=0)
        heads = (hk * SWA_GROUP + par, hk * SWA_GROUP + par + 2)
        ps, ms = [], []
        for i, head in enumerate(heads):
            s = st[:, i * tq:(i + 1) * tq] + bias_ref[0, head]
            m = jnp.maximum(jnp.max(s, axis=0, keepdims=True), sink_ref[head])
            ps.append(jnp.exp2(s - m).astype(BF16))
            ms.append(m)
        acc = jnp.dot(vt1, jnp.concatenate(ps, axis=1), preferred_element_type=F32)
        for i, head in enumerate(heads):
            a = acc[:, i * tq:(i + 1) * tq]
            den = a[HEAD_DIM:HEAD_DIM + 1] + jnp.exp2(sink_ref[head] - ms[i])
            outs[head] = a[:HEAD_DIM] / den
    o_ref[0] = jnp.concatenate(outs, axis=0).T.astype(o_ref.dtype)


def _swa_call(q_t, k, v_t, bias, sink, batch, seq):
    d = q_t.shape[0]
    kw = k.shape[2]
    vw = v_t.shape[0]
    tq = _SWA_TQ
    nq = seq // tq
    return pl.pallas_call(
        _swa_kernel,
        out_shape=jax.ShapeDtypeStruct((batch, seq, d), BF16),
        grid=(batch, nq),
        in_specs=[
            pl.BlockSpec((d, tq), lambda b, i: (0, b * nq + i)),
            pl.BlockSpec((1, tq, kw), lambda b, i: (b, jnp.maximum(i - 1, 0), 0)),
            pl.BlockSpec((1, tq, kw), lambda b, i: (b, i, 0)),
            pl.BlockSpec((vw, tq), lambda b, i: (0, b * nq + jnp.maximum(i - 1, 0))),
            pl.BlockSpec((vw, tq), lambda b, i: (0, b * nq + i)),
            pl.BlockSpec((1,) + bias.shape[1:], lambda b, i: (jnp.minimum(i, 1), 0, 0, 0)),
            pl.BlockSpec(sink.shape, lambda b, i: (0, 0, 0)),
        ],
        out_specs=pl.BlockSpec((1, tq, d), lambda b, i: (b, i, 0)),
        compiler_params=_params(("parallel", "arbitrary")),
        name="swa_attention",
    )(q_t, k, k, v_t, v_t, bias, sink)


def kernel(x, c, w_ada, b_ada, g_norm_mix, g_norm_ffn, g_final, fox_w_in, fox_b_f, fox_w_out, swa_w_in, swa_sinks, swa_w_out, rel_bias, ffn_w_gu, ffn_w_down, moe_w_router, moe_b_router, moe_w_gu, moe_w_down):
    batch, seq, d = x.shape
    n = batch * seq
    q_scale = HEAD_DIM ** -0.5
    xf = x.reshape(n, d)

    c_pad = jnp.zeros((SUBLANES, d), F32).at[:batch].set(c)
    mod = _ada_call(c_pad, w_ada, b_ada)
    mod0, mod1 = mod[0], mod[1]

    w_in = fox_w_in[0]
    n_heads = d // HEAD_DIM
    k, q_t, v_t, f, w_gu0, w_down0, w_out0, w_out1 = _attn_proj_call(
        xf, g_norm_mix[0], mod0, w_in[:, d:2 * d].astype(BF16),
        (w_in[:, :d] * (q_scale * LOG2E)).T.astype(BF16), w_in[:, 2 * d:3 * d].T.astype(BF16),
        seq, w_f=_split_cols(w_in[:, 3 * d:]),
        cast=(ffn_w_gu[0], ffn_w_down[0], fox_w_out[0], swa_w_out[0]))
    g_cum, r_cum = _cum_call(f, fox_b_f[0], batch, seq)
    r_flat = r_cum.transpose(0, 2, 1).reshape(-1)
    o, w_down1 = _fox_call(r_flat, q_t, k.reshape(batch, seq, d), v_t, g_cum,
                           cast=(moe_w_down[0],))
    w_down1 = w_down1.reshape(moe_w_down.shape[1:])
    x2, w_gu1 = _ffn_call(xf, o.reshape(n, d), w_out0, g_norm_ffn[0], mod0, w_gu0, w_down0, seq,
                          cast=(moe_w_gu[0],))
    w_gu1 = w_gu1.reshape(moe_w_gu.shape[1:])

    w_in = swa_w_in[0]
    kvw = SWA_KV_HEADS * HEAD_DIM
    dup = lambda w: jnp.repeat(w.reshape(d, SWA_KV_HEADS, 1, HEAD_DIM), 2, axis=2).reshape(d, 2 * kvw)
    k, q_t, v_t = _attn_proj_call(
        x2, g_norm_mix[1], mod1, dup(w_in[:, d:d + kvw]).astype(BF16),
        (w_in[:, :d] * (q_scale * LOG2E)).T.astype(BF16), w_in[:, d + kvw:].T.astype(BF16), seq)
    bias = _swa_bias_call(rel_bias)
    sink = jnp.broadcast_to((swa_sinks[0] * LOG2E)[:, None, None], (n_heads, 1, _SWA_TQ))
    o = _swa_call(q_t, k.reshape(batch, seq, 2 * kvw), v_t, bias, sink, batch, seq)
    w_r = _split_cols(jnp.zeros((d, LANES), F32).at[:, :N_EXPERTS].set(moe_w_router[0]))
    b_r = jnp.zeros((1, LANES), F32).at[0, :N_EXPERTS].set(moe_b_router[0])
    x3, h4, route, route_t, counts = _out_call(o.reshape(n, d), w_out1, x2, mod1, g_norm_ffn[1],
                                               w_r, b_r, seq)
    out = _moe_call(h4, route, route_t, counts, x3, mod1, g_final, w_gu1, w_down1, seq)
    return out.reshape(batch, seq, d)
```

```python
import functools

import numpy as np
import jax
import jax.numpy as jnp
from jax import lax
from jax.experimental import pallas as pl
from jax.experimental.pallas import tpu as pltpu

F32 = jnp.float32
BF16 = jnp.bfloat16

HEAD_DIM = 64
CHUNK = 64
WINDOW_CHUNKS = 2
REL_BUCKETS = 32
REL_MAX_DIST = 128
SWA_KV_HEADS = 4
SWA_GROUP = 4
N_EXPERTS = 8
EPS = 1e-6

LANES = 128
SUBLANES = 8
VMEM_LIMIT = 56 * 1024 * 1024

NEG_BIG = -1e30


def _params(sem, vmem=VMEM_LIMIT):
    return pltpu.CompilerParams(dimension_semantics=sem, vmem_limit_bytes=vmem)


def _rms_mod(x, g, shift, scale):
    ms = jnp.mean(x * x, axis=-1, keepdims=True)
    y = x * lax.rsqrt(ms + EPS) * g
    return y * (1.0 + scale) + shift


def _silu(x):
    return x / (1.0 + jnp.exp(-x))


def _split_bf16(x):
    hi = x.astype(BF16)
    return hi, (x - hi.astype(F32)).astype(BF16)


def _split_cols(w):
    return jnp.concatenate(_split_bf16(w), axis=1)


def _dot_split(x_hi, x_lo, w_ref):
    n = w_ref.shape[1] // 2
    y = jnp.dot(x_hi, w_ref[...], preferred_element_type=F32)
    return y[:, :n] + y[:, n:] + jnp.dot(x_lo, w_ref[:, :n], preferred_element_type=F32)


_BF16_SUBLANES = 16


def _ride_along(arrays, n_steps, step_index):
    views, in_specs, out_specs, out_shapes = [], [], [], []
    for a in arrays:
        v = a.reshape(-1, a.shape[-1])
        rows, rem = divmod(v.shape[0], n_steps)
        assert rem == 0 and rows % _BF16_SUBLANES == 0, v.shape
        spec = pl.BlockSpec((rows, v.shape[1]), lambda *g: (step_index(*g), 0))
        views.append(v)
        in_specs.append(spec)
        out_specs.append(spec)
        out_shapes.append(jax.ShapeDtypeStruct(v.shape, BF16))
    return views, in_specs, out_specs, out_shapes


def _ride_along_cast(in_refs, out_refs):
    for src, dst in zip(in_refs, out_refs):
        dst[...] = src[...].astype(BF16)


def _ada_kernel(c_ref, w_ref, b_ref, o_ref):
    c_hi, c_lo = _split_bf16(_silu(c_ref[...]))
    w_hi, w_lo = _split_bf16(w_ref[0])
    o_ref[0] = (jnp.dot(c_hi, w_hi, preferred_element_type=F32)
                + jnp.dot(c_lo, w_hi, preferred_element_type=F32)
                + jnp.dot(c_hi, w_lo, preferred_element_type=F32) + b_ref[0])


def _ada_call(c_pad, w_ada, b_ada):
    depth, d, n = w_ada.shape
    tn = 1536
    return pl.pallas_call(
        _ada_kernel,
        out_shape=jax.ShapeDtypeStruct((depth, SUBLANES, n), F32),
        grid=(depth, n // tn),
        in_specs=[
            pl.BlockSpec((SUBLANES, d), lambda l, j: (0, 0)),
            pl.BlockSpec((1, d, tn), lambda l, j: (l, 0, j)),
            pl.BlockSpec((1, 1, tn), lambda l, j: (l, 0, j)),
        ],
        out_specs=pl.BlockSpec((1, SUBLANES, tn), lambda l, j: (l, 0, j)),
        compiler_params=_params(("parallel", "parallel")),
        name="ada_mod",
    )(c_pad, w_ada, b_ada.reshape(depth, 1, n))


_NT = (((1,), (1,)), ((), ()))


def _attn_proj_kernel(*refs, tiles_per_batch, with_gate, n_cast):
    x_ref, g_ref, sh_ref, sc_ref, wk_ref, wqt_ref, wvt_ref = refs[:7]
    n_in = 7 + with_gate + n_cast
    k_ref, qt_ref, vt_ref = refs[n_in:n_in + 3]
    b = pl.program_id(0) // tiles_per_batch
    h = _rms_mod(x_ref[...], g_ref[...], sh_ref[pl.ds(b, 1), :], sc_ref[pl.ds(b, 1), :])
    hb = h.astype(BF16)
    k_ref[...] = jnp.dot(hb, wk_ref[...], preferred_element_type=F32).astype(BF16)
    qt_ref[...] = lax.dot_general(wqt_ref[...], hb, _NT, preferred_element_type=F32).astype(BF16)
    vt_ref[...] = lax.dot_general(wvt_ref[...], hb, _NT, preferred_element_type=F32).astype(BF16)
    if with_gate:
        refs[n_in + 3][...] = _dot_split(hb, (h - hb.astype(F32)).astype(BF16), refs[7])
    _ride_along_cast(refs[n_in - n_cast:n_in], refs[len(refs) - n_cast:])


def _attn_proj_call(x, g, mod, w_k, w_qt, w_vt, seq, w_f=None, cast=(), tm=512):
    n, d = x.shape
    with_gate = w_f is not None
    full = lambda a: pl.BlockSpec(a.shape, lambda i: (0, 0))
    c_views, c_in, c_out, c_shapes = _ride_along(cast, n // tm, lambda i: i)
    in_specs = [
        pl.BlockSpec((tm, d), lambda i: (i, 0)),
        pl.BlockSpec((1, d), lambda i: (0, 0)),
        pl.BlockSpec((SUBLANES, d), lambda i: (0, 0)),
        pl.BlockSpec((SUBLANES, d), lambda i: (0, 1)),
        full(w_k), full(w_qt), full(w_vt),
    ]
    args = [x, g.reshape(1, d), mod, mod, w_k, w_qt, w_vt]
    out_shape = [jax.ShapeDtypeStruct((n, w_k.shape[1]), BF16),
                 jax.ShapeDtypeStruct((w_qt.shape[0], n), BF16),
                 jax.ShapeDtypeStruct((w_vt.shape[0], n), BF16)]
    out_specs = [pl.BlockSpec((tm, w_k.shape[1]), lambda i: (i, 0)),
                 pl.BlockSpec((w_qt.shape[0], tm), lambda i: (0, i)),
                 pl.BlockSpec((w_vt.shape[0], tm), lambda i: (0, i))]
    if with_gate:
        in_specs.append(full(w_f))
        args.append(w_f)
        out_shape.append(jax.ShapeDtypeStruct((n, w_f.shape[1] // 2), F32))
        out_specs.append(pl.BlockSpec((tm, w_f.shape[1] // 2), lambda i: (i, 0)))
    return pl.pallas_call(
        functools.partial(_attn_proj_kernel, tiles_per_batch=seq // tm, with_gate=with_gate,
                          n_cast=len(cast)),
        out_shape=out_shape + c_shapes,
        grid=(n // tm,),
        in_specs=in_specs + c_in,
        out_specs=out_specs + c_out,
        compiler_params=_params(("parallel",)),
        name="attn_proj_gate" if with_gate else "attn_proj",
    )(*args, *c_views)


_FOX_T = 512
LOG2E = 1.4426950408889634


def _cum_kernel(f_ref, bf_ref, g_ref, r_ref):
    x = f_ref[...] + bf_ref[...]
    logf = (jnp.minimum(x, 0.0) - jnp.log(1.0 + jnp.exp(-jnp.abs(x)))) * LOG2E
    seq, nh = logf.shape
    r = lax.broadcasted_iota(jnp.int32, (_FOX_T, _FOX_T), 0)
    c = lax.broadcasted_iota(jnp.int32, (_FOX_T, _FOX_T), 1)
    lower = jnp.where(c <= r, 1.0, 0.0).astype(BF16)
    hi = logf.astype(BF16).astype(F32)
    mid = (logf - hi).astype(BF16).astype(F32)
    parts = jnp.concatenate([hi, mid, logf - hi - mid], axis=1).astype(BF16)
    carry = jnp.zeros((1, nh), F32)
    for ch in range(seq // _FOX_T):
        rows = slice(ch * _FOX_T, (ch + 1) * _FOX_T)
        y = jnp.dot(lower, parts[rows, :], preferred_element_type=F32)
        cs = y[:, :nh] + y[:, nh:2 * nh] + y[:, 2 * nh:]
        g_ref[0, rows, :] = cs
        r_ref[0, ch:ch + 1, :] = carry
        carry = carry + cs[_FOX_T - 1:_FOX_T, :]


def _cum_call(f, b_f, batch, seq):
    nh = f.shape[1]
    return pl.pallas_call(
        _cum_kernel,
        out_shape=[jax.ShapeDtypeStruct((batch, seq, nh), F32),
                   jax.ShapeDtypeStruct((batch, seq // _FOX_T, nh), F32)],
        grid=(batch,),
        in_specs=[
            pl.BlockSpec((seq, nh), lambda b: (b, 0)),
            pl.BlockSpec((1, nh), lambda b: (0, 0)),
        ],
        out_specs=[pl.BlockSpec((1, seq, nh), lambda b: (b, 0, 0)),
                   pl.BlockSpec((1, seq // _FOX_T, nh), lambda b: (b, 0, 0))],
        compiler_params=_params(("parallel",)),
        name="forget_cumsum",
    )(f, b_f.reshape(1, nh))


_ONES_ROWS = 16


def _fox_kernel(*refs, n_cast):
    r_ref, qt_ref, k_ref, vt_ref, g_ref = refs[:5]
    o_ref = refs[5 + n_cast]
    ka_sc, m_sc, acc_sc, sa_sc, sb_sc, xa_sc, xb_sc = refs[6 + 2 * n_cast:]
    _ride_along_cast(refs[5:5 + n_cast], refs[6 + n_cast:6 + 2 * n_cast])
    t = _FOX_T
    b, hp = pl.program_id(0), pl.program_id(1)
    seq = k_ref.shape[1]
    n_blocks = seq // t
    n_heads = 2 * pl.num_programs(1)

    lane = lax.broadcasted_iota(jnp.int32, (1, LANES), 1)
    feat = lax.broadcasted_iota(jnp.int32, (LANES, 1), 0)
    aug0 = [HEAD_DIM, 0]

    g_all = g_ref[0]
    head_col = lax.broadcasted_iota(jnp.int32, g_all.shape, 1)
    for hh in range(2):
        g_head = jnp.sum(jnp.where(head_col == 2 * hp + hh, g_all, 0.0), axis=1, keepdims=True)
        gb = jnp.broadcast_to(g_head, (seq, LANES))
        hi = gb.astype(BF16).astype(F32)
        mid = (gb - hi).astype(BF16).astype(F32)
        lo = gb - hi - mid
        aug = jnp.where(lane == aug0[hh], hi,
                        jnp.where(lane == aug0[hh] + 1, mid,
                                  jnp.where(lane == aug0[hh] + 2, lo, 0.0)))
        own = (lane >= hh * HEAD_DIM) & (lane < (hh + 1) * HEAD_DIM)
        ka_sc[hh] = jnp.where(own, k_ref[0].astype(F32), aug).astype(BF16)

    key = lax.broadcasted_iota(jnp.int32, (t, t), 0)
    qry = lax.broadcasted_iota(jnp.int32, (t, t), 1)
    ones = jnp.ones((_ONES_ROWS, t), BF16)
    feat_t = lax.broadcasted_iota(jnp.int32, (LANES, t), 0)
    r_base = [(b * n_heads + 2 * hp + hh) * n_blocks for hh in range(2)]
    bufs = ((sa_sc, xa_sc), (sb_sc, xb_sc))

    def queries(qi):
        qt2 = qt_ref[:, qi * t:(qi + 1) * t]
        return [jnp.where((feat >= hh * HEAD_DIM) & (feat < (hh + 1) * HEAD_DIM), qt2, 0)
                + jnp.where((feat_t >= aug0[hh]) & (feat_t < aug0[hh] + 3), -1.0, 0.0).astype(BF16)
                for hh in range(2)]

    def scores(qh, kb, dst):
        for hh in range(2):
            st = jnp.dot(ka_sc[hh, kb * t:(kb + 1) * t, :], qh[hh],
                         preferred_element_type=F32)
            dst[0][hh] = st
            dst[1][hh] = jnp.max(st, axis=0, keepdims=True)

    def softmax_pv(qi, kb, cur):
        for hh in range(2):
            st = cur[0][hh]
            if kb == qi:
                st = jnp.where(key <= qry, st, -jnp.inf)
                st_max = jnp.max(st, axis=0, keepdims=True)
            else:
                st_max = cur[1][hh]
            c = r_ref[r_base[hh] + kb] - r_ref[r_base[hh] + qi]
            m_old = m_sc[hh]
            m_new = jnp.maximum(m_old, st_max - c)
            alpha = jnp.exp2(m_old - m_new)
            p = jnp.exp2(st - (m_new + c)).astype(BF16)
            vt1 = jnp.concatenate(
                [vt_ref[hh * HEAD_DIM:(hh + 1) * HEAD_DIM, kb * t:(kb + 1) * t], ones], axis=0)
            acc_sc[hh] = alpha * acc_sc[hh] + jnp.dot(vt1, p, preferred_element_type=F32)
            m_sc[hh] = m_new

    pairs = [(qi, kb) for qi in range(n_blocks) for kb in range(qi + 1)]
    qh = queries(0)
    scores(qh, 0, bufs[0])
    for s, (qi, kb) in enumerate(pairs):
        if kb == 0:
            m_sc[...] = jnp.full_like(m_sc, -jnp.inf)
            acc_sc[...] = jnp.zeros_like(acc_sc)
        if s + 1 < len(pairs):
            qi_n, kb_n = pairs[s + 1]
            if qi_n != qi:
                qh = queries(qi_n)
            scores(qh, kb_n, bufs[(s + 1) % 2])
        softmax_pv(qi, kb, bufs[s % 2])
        if kb == qi:
            outs = [acc_sc[hh, :HEAD_DIM, :] / acc_sc[hh, HEAD_DIM:HEAD_DIM + 1, :]
                    for hh in range(2)]
            o_ref[0, qi * t:(qi + 1) * t, :] = jnp.concatenate(outs, axis=0).T.astype(o_ref.dtype)


def _fox_call(r_flat, q_t, k, v_t, g_cum, cast=()):
    batch, seq, d = k.shape
    n_pairs = d // LANES
    t = _FOX_T
    c_views, c_in, c_out, c_shapes = _ride_along(
        cast, batch * n_pairs, lambda b, h: b * n_pairs + h)
    return pl.pallas_call(
        functools.partial(_fox_kernel, n_cast=len(cast)),
        out_shape=[jax.ShapeDtypeStruct((batch, seq, d), BF16)] + c_shapes,
        grid=(batch, n_pairs),
        in_specs=[
            pl.BlockSpec(memory_space=pltpu.SMEM),
            pl.BlockSpec((LANES, seq), lambda b, h: (h, b)),
            pl.BlockSpec((1, seq, LANES), lambda b, h: (b, 0, h)),
            pl.BlockSpec((LANES, seq), lambda b, h: (h, b)),
            pl.BlockSpec((1, seq, g_cum.shape[2]), lambda b, h: (b, 0, 0)),
        ] + c_in,
        out_specs=[pl.BlockSpec((1, seq, LANES), lambda b, h: (b, 0, h))] + c_out,
        scratch_shapes=[pltpu.VMEM((2, seq, LANES), BF16), pltpu.VMEM((2, 1, t), F32),
                        pltpu.VMEM((2, HEAD_DIM + _ONES_ROWS, t), F32),
                        pltpu.VMEM((2, t, t), F32), pltpu.VMEM((2, t, t), F32),
                        pltpu.VMEM((2, 1, t), F32), pltpu.VMEM((2, 1, t), F32)],
        compiler_params=_params(("parallel", "parallel")),
        name="fox_attention",
    )(r_flat, q_t, k, v_t, g_cum, *c_views)


def _out_kernel(o_ref, w_ref, x_ref, gt_ref, g_ref, sh_ref, sc_ref, wr_ref, br_ref,
                xo_ref, h_ref, route_ref, route_t_ref, cnt_ref, *, tiles_per_batch):
    b = pl.program_id(0) // tiles_per_batch
    y = jnp.dot(o_ref[...], w_ref[...], preferred_element_type=F32)
    xn = x_ref[...] + gt_ref[pl.ds(b, 1), :] * y
    xo_ref[...] = xn
    h = _rms_mod(xn, g_ref[...], sh_ref[pl.ds(b, 1), :], sc_ref[pl.ds(b, 1), :])
    h_ref[...] = h
    logits = _dot_split(*_split_bf16(h), wr_ref) + br_ref[...]
    tm = logits.shape[0]
    lane = lax.broadcasted_iota(jnp.int32, logits.shape, 1).astype(F32)
    logits = jnp.where(lane < N_EXPERTS, logits, -jnp.inf)
    m1 = jnp.max(logits, axis=1, keepdims=True)
    i1 = jnp.min(jnp.where(logits == m1, lane, float(LANES)), axis=1, keepdims=True)
    rest = jnp.where(lane == i1, -jnp.inf, logits)
    m2 = jnp.max(rest, axis=1, keepdims=True)
    i2 = jnp.min(jnp.where(rest == m2, lane, float(LANES)), axis=1, keepdims=True)
    e2 = jnp.exp(m2 - m1)
    den = 1.0 + e2

    @pl.when(pl.program_id(0) == 0)
    def _():
        cnt_ref[...] = jnp.zeros_like(cnt_ref)

    sel1 = lane == i1
    sel2 = lane == i2
    onehot = jnp.where(sel1 | sel2, 1.0, 0.0)
    r = lax.broadcasted_iota(jnp.int32, (tm, tm), 0)
    c = lax.broadcasted_iota(jnp.int32, (tm, tm), 1)
    lower = jnp.where(c < r, 1.0, 0.0).astype(BF16)
    before = jnp.dot(lower, onehot.astype(BF16), preferred_element_type=F32) + cnt_ref[0:1, :]
    rank1 = jnp.sum(jnp.where(sel1, before, 0.0), axis=1, keepdims=True)
    rank2 = jnp.sum(jnp.where(sel2, before, 0.0), axis=1, keepdims=True)
    cnt_ref[0:1, :] = cnt_ref[0:1, :] + jnp.sum(onehot, axis=0, keepdims=True)
    route = jnp.where(lane == 0.0, i1, 0.0)
    for k, val in enumerate((i2, 1.0 / den, e2 / den, rank1, rank2), start=1):
        route = jnp.where(lane == float(k), val, route)
    route_ref[...] = route
    route_t_ref[...] = route.T[:SUBLANES, :]


def _out_call(o, w_out, x, mod, g, w_r, b_r, seq, tm=512):
    n, d = x.shape
    row = lambda w: pl.BlockSpec((tm, w), lambda i: (i, 0))
    col = lambda c: pl.BlockSpec((SUBLANES, d), lambda i: (0, c))
    return pl.pallas_call(
        functools.partial(_out_kernel, tiles_per_batch=seq // tm),
        out_shape=[jax.ShapeDtypeStruct((n, d), F32), jax.ShapeDtypeStruct((n, d), F32),
                   jax.ShapeDtypeStruct((n, LANES), F32), jax.ShapeDtypeStruct((SUBLANES, n), F32),
                   jax.ShapeDtypeStruct((SUBLANES, LANES), F32)],
        grid=(n // tm,),
        in_specs=[
            row(d),
            pl.BlockSpec((d, d), lambda i: (0, 0)),
            row(d),
            col(2),
            pl.BlockSpec((1, d), lambda i: (0, 0)),
            col(3),
            col(4),
            pl.BlockSpec((d, 2 * LANES), lambda i: (0, 0)),
            pl.BlockSpec((1, LANES), lambda i: (0, 0)),
        ],
        out_specs=[row(d), row(d), row(LANES), pl.BlockSpec((SUBLANES, tm), lambda i: (0, i)),
                   pl.BlockSpec((SUBLANES, LANES), lambda i: (0, 0))],
        compiler_params=_params(("arbitrary",)),
        name="out_proj_router",
    )(o, w_out, x, mod, g.reshape(1, d), mod, mod, w_r, b_r)


_FF_TILE = 1792


def _snake(i, j, nj):
    return jnp.where(i % 2 == 0, j, nj - 1 - j)


def _ffn_kernel(*refs, tiles_per_batch, n_cast):
    (x_ref, attn_ref, wo_ref, gta_ref, g_ref, sh_ref, sc_ref, gt_ref,
     wg_ref, wu_ref, wd_ref) = refs[:11]
    o_ref = refs[11 + n_cast]
    h_sc = refs[-1]
    _ride_along_cast(refs[11:11 + n_cast], refs[12 + n_cast:12 + 2 * n_cast])
    b = pl.program_id(0) // tiles_per_batch

    @pl.when(pl.program_id(1) == 0)
    def _():
        x = x_ref[...] + gta_ref[pl.ds(b, 1), :] * jnp.dot(
            attn_ref[...], wo_ref[...], preferred_element_type=F32)
        h = _rms_mod(x, g_ref[...], sh_ref[pl.ds(b, 1), :], sc_ref[pl.ds(b, 1), :])
        h_sc[...] = h.astype(BF16)
        o_ref[...] = x

    h = h_sc[...]
    g = jnp.dot(h, wg_ref[...], preferred_element_type=F32)
    u = jnp.dot(h, wu_ref[...], preferred_element_type=F32)
    a = (_silu(g) * u).astype(BF16)
    o_ref[...] += gt_ref[pl.ds(b, 1), :] * jnp.dot(a, wd_ref[...], preferred_element_type=F32)


def _ffn_call(x, attn, w_out, g, mod, w_gu, w_down, seq, cast=(), tm=512, tf=_FF_TILE):
    n, d = x.shape
    f = w_down.shape[0]
    nj = f // tf
    c_views, c_in, c_out, c_shapes = _ride_along(cast, (n // tm) * nj, lambda i, j: i * nj + j)
    return pl.pallas_call(
        functools.partial(_ffn_kernel, tiles_per_batch=seq // tm, n_cast=len(cast)),
        out_shape=[jax.ShapeDtypeStruct((n, d), F32)] + c_shapes,
        grid=(n // tm, nj),
        in_specs=[
            pl.BlockSpec((tm, d), lambda i, j: (i, 0)),
            pl.BlockSpec((tm, d), lambda i, j: (i, 0)),
            pl.BlockSpec((d, d), lambda i, j: (0, 0)),
            pl.BlockSpec((SUBLANES, d), lambda i, j: (0, 2)),
            pl.BlockSpec((1, d), lambda i, j: (0, 0)),
            pl.BlockSpec((SUBLANES, d), lambda i, j: (0, 3)),
            pl.BlockSpec((SUBLANES, d), lambda i, j: (0, 4)),
            pl.BlockSpec((SUBLANES, d), lambda i, j: (0, 5)),
            pl.BlockSpec((d, tf), lambda i, j: (0, _snake(i, j, nj))),
            pl.BlockSpec((d, tf), lambda i, j: (0, _snake(i, j, nj) + nj)),
            pl.BlockSpec((tf, d), lambda i, j: (_snake(i, j, nj), 0)),
        ] + c_in,
        out_specs=[pl.BlockSpec((tm, d), lambda i, j: (i, 0))] + c_out,
        scratch_shapes=[pltpu.VMEM((tm, d), BF16)],
        compiler_params=_params(("parallel", "arbitrary")),
        name="ffn_swiglu",
    )(x, attn, w_out, mod, g.reshape(1, d), mod, mod, mod, w_gu, w_gu, w_down, *c_views)


_MOE_TM = 512


def _dispatch_kernel(d1_ref, d2_ref, zero_ref, h_ref, xs_ref, z_sc, sem):
    tm = h_ref.shape[0]
    base = pl.program_id(0) * tm

    @pl.when(pl.program_id(0) == 0)
    def _():
        z_sc[...] = jnp.zeros_like(z_sc)

        def zero_copy(t):
            return pltpu.make_async_copy(z_sc, xs_ref.at[pl.ds(pl.multiple_of(t * tm, tm), tm)],
                                         sem.at[0])

        def start(t, carry):
            @pl.when(zero_ref[t] != 0)
            def _():
                zero_copy(t).start()
            return carry

        def wait(t, carry):
            @pl.when(zero_ref[t] != 0)
            def _():
                zero_copy(t).wait()
            return carry

        lax.fori_loop(0, zero_ref.shape[0], start, 0)
        lax.fori_loop(0, zero_ref.shape[0], wait, 0)

    def issue(r, carry):
        src = h_ref.at[pl.ds(r, 1)]
        pltpu.make_async_copy(src, xs_ref.at[pl.ds(d1_ref[base + r], 1)], sem.at[0]).start()
        pltpu.make_async_copy(src, xs_ref.at[pl.ds(d2_ref[base + r], 1)],
                              sem.at[1]).start(priority=1)
        return carry

    lax.fori_loop(0, tm, issue, 0, unroll=8)
    pltpu.make_async_copy(h_ref, xs_ref.at[pl.ds(0, tm)], sem.at[0]).wait()
    pltpu.make_async_copy(h_ref, xs_ref.at[pl.ds(0, tm)], sem.at[1]).wait()


def _dispatch_call(dest1, dest2, zero_tile, h, tm=_MOE_TM):
    n, d = h.shape
    n_rows = zero_tile.shape[0] * tm
    return pl.pallas_call(
        _dispatch_kernel,
        out_shape=jax.ShapeDtypeStruct((n_rows, d), h.dtype),
        grid_spec=pltpu.PrefetchScalarGridSpec(
            num_scalar_prefetch=3,
            grid=(n // tm,),
            in_specs=[pl.BlockSpec((tm, d), lambda i, d1, d2, zt: (i, 0))],
            out_specs=pl.BlockSpec(memory_space=pl.ANY),
            scratch_shapes=[pltpu.VMEM((tm, d), h.dtype), pltpu.SemaphoreType.DMA((2,))],
        ),
        compiler_params=_params(("arbitrary",)),
        name="moe_dispatch",
    )(dest1, dest2, zero_tile, h)


def _experts_kernel(te_ref, nt_ref, xs_ref, wg_ref, wu_ref, wd_ref, ye_ref):
    del te_ref
    t = pl.program_id(0)

    @pl.when(pl.program_id(1) == 0)
    def _():
        ye_ref[...] = jnp.zeros_like(ye_ref)

    @pl.when(t < nt_ref[0])
    def _():
        h = xs_ref[...].astype(BF16)
        g = jnp.dot(h, wg_ref[0], preferred_element_type=F32)
        u = jnp.dot(h, wu_ref[0], preferred_element_type=F32)
        a = (_silu(g) * u).astype(BF16)
        ye_ref[...] += jnp.dot(a, wd_ref[0], preferred_element_type=F32)


def _experts_call(tile_expert, n_tiles, xs, w_gu, w_down, tf=_FF_TILE):
    n_rows, d = xs.shape
    _, f, _ = w_down.shape
    nj = f // tf
    tm = _MOE_TM
    tile = lambda t, nt: jnp.maximum(jnp.minimum(t, nt[0] - 1), 0)
    jj = lambda t, j, nt: _snake(tile(t, nt), jnp.where(t < nt[0], j, nj - 1), nj)
    return pl.pallas_call(
        _experts_kernel,
        out_shape=jax.ShapeDtypeStruct((n_rows, d), F32),
        grid_spec=pltpu.PrefetchScalarGridSpec(
            num_scalar_prefetch=2,
            grid=(n_rows // tm, nj),
            in_specs=[
                pl.BlockSpec((tm, d), lambda t, j, te, nt: (tile(t, nt), 0)),
                pl.BlockSpec((1, d, tf), lambda t, j, te, nt: (te[tile(t, nt)], 0, jj(t, j, nt))),
                pl.BlockSpec((1, d, tf), lambda t, j, te, nt: (te[tile(t, nt)], 0, jj(t, j, nt) + nj)),
                pl.BlockSpec((1, tf, d), lambda t, j, te, nt: (te[tile(t, nt)], jj(t, j, nt), 0)),
            ],
            out_specs=pl.BlockSpec((tm, d), lambda t, j, te, nt: (t, 0)),
        ),
        compiler_params=_params(("arbitrary", "arbitrary")),
        name="moe_experts",
    )(tile_expert, n_tiles, xs, w_gu, w_gu, w_down)


def _combine_kernel(d1_ref, d2_ref, ye_ref, x_ref, route_ref, gt_ref, gf_ref, o_ref,
                    y1_sc, y2_sc, sem, *, tiles_per_batch):
    tm = x_ref.shape[0]
    i = pl.program_id(0)
    b = i // tiles_per_batch
    slot = i % 2

    def gather(tile, dst_slot):
        base = tile * tm

        def issue(r, carry):
            pltpu.make_async_copy(ye_ref.at[pl.ds(d1_ref[base + r], 1)],
                                  y1_sc.at[dst_slot, pl.ds(r, 1)], sem.at[0, dst_slot]).start()
            pltpu.make_async_copy(ye_ref.at[pl.ds(d2_ref[base + r], 1)],
                                  y2_sc.at[dst_slot, pl.ds(r, 1)],
                                  sem.at[1, dst_slot]).start(priority=1)
            return carry

        lax.fori_loop(0, tm, issue, 0, unroll=8)

    @pl.when(i == 0)
    def _():
        gather(0, 0)

    @pl.when(i + 1 < pl.num_programs(0))
    def _():
        gather(i + 1, 1 - slot)

    pltpu.make_async_copy(ye_ref.at[pl.ds(0, tm)], y1_sc.at[slot], sem.at[0, slot]).wait()
    pltpu.make_async_copy(ye_ref.at[pl.ds(0, tm)], y2_sc.at[slot], sem.at[1, slot]).wait()
    route = route_ref[...]
    y = route[:, 2:3] * y1_sc[slot] + route[:, 3:4] * y2_sc[slot]
    xn = x_ref[...] + gt_ref[pl.ds(b, 1), :] * y
    ms = jnp.mean(xn * xn, axis=-1, keepdims=True)
    o_ref[...] = xn * lax.rsqrt(ms + EPS) * gf_ref[...]


def _combine_call(dest1, dest2, ye, x, route, mod, g_final, seq, tm=512):
    n, d = x.shape
    return pl.pallas_call(
        functools.partial(_combine_kernel, tiles_per_batch=seq // tm),
        out_shape=jax.ShapeDtypeStruct((n, d), F32),
        grid_spec=pltpu.PrefetchScalarGridSpec(
            num_scalar_prefetch=2,
            grid=(n // tm,),
            in_specs=[
                pl.BlockSpec(memory_space=pl.ANY),
                pl.BlockSpec((tm, d), lambda i, d1, d2: (i, 0)),
                pl.BlockSpec((tm, LANES), lambda i, d1, d2: (i, 0)),
                pl.BlockSpec((SUBLANES, d), lambda i, d1, d2: (0, 5)),
                pl.BlockSpec((1, d), lambda i, d1, d2: (0, 0)),
            ],
            out_specs=pl.BlockSpec((tm, d), lambda i, d1, d2: (i, 0)),
            scratch_shapes=[pltpu.VMEM((2, tm, d), F32), pltpu.VMEM((2, tm, d), F32),
                            pltpu.SemaphoreType.DMA((2, 2))],
        ),
        compiler_params=_params(("arbitrary",)),
        name="moe_combine",
    )(dest1, dest2, ye, x, route, mod, g_final.reshape(1, d))


def _moe_call(h, route, route_t, counts, x, mod, g_final, w_gu, w_down, seq):
    n, d = x.shape
    ne = w_down.shape[0]
    tm = _MOE_TM
    max_tiles = (2 * n) // tm + ne
    e1, e2, _, _, rank1, rank2 = (route_t[k].astype(jnp.int32) for k in range(6))
    cnt = counts[0, :ne].astype(jnp.int32)
    tiles_e = (cnt + tm - 1) // tm
    tile_end = jnp.cumsum(tiles_e)
    row_start = (tile_end - tiles_e) * tm
    dest1 = row_start[e1] + rank1
    dest2 = row_start[e2] + rank2
    n_tiles = tile_end[-1:]
    tile_ids = jnp.arange(max_tiles, dtype=jnp.int32)
    tile_expert = jnp.minimum(
        jnp.sum((tile_ids[:, None] >= tile_end[None, :]).astype(jnp.int32), axis=1), ne - 1)
    is_last = jnp.any((tile_ids[:, None] == tile_end[None, :] - 1) & (tiles_e[None, :] > 0), axis=1)
    zero_tile = (is_last | (tile_ids >= n_tiles[0])).astype(jnp.int32)
    xs = _dispatch_call(dest1, dest2, zero_tile, h)
    ye = _experts_call(tile_expert, n_tiles, xs, w_gu, w_down)
    return _combine_call(dest1, dest2, ye, x, route, mod, g_final, seq)


_SWA_TQ = 2 * CHUNK
_SWA_BAND = 2 * _SWA_TQ
_SWA_SUB = 4


def _swa_bucket_tiles():
    cc = np.arange(_SWA_BAND)[:, None]
    r = np.arange(_SWA_TQ)[None, :]
    rel = cc - _SWA_TQ - r
    nb = REL_BUCKETS // 2
    max_exact = nb // 2
    ret = (rel > 0).astype(np.int32) * nb
    n = np.abs(rel)
    large = max_exact + (np.log(np.maximum(n, 1) / max_exact)
                         / np.log(REL_MAX_DIST / max_exact) * (nb - max_exact)).astype(np.int32)
    large = np.minimum(large, nb - 1)
    bucket = (ret + np.where(n < max_exact, n, large)).astype(np.int32)
    q_chunk = r // CHUNK
    k_chunk = cc // CHUNK
    visible = (k_chunk >= q_chunk) & (k_chunk <= q_chunk + WINDOW_CHUNKS)
    later = np.where(visible, bucket, -1)
    first = np.where(cc >= _SWA_TQ, later, -1)
    return np.stack([first, later]).astype(np.int32)


def _swa_bias_kernel(tbl_ref, bkt_ref, o_ref):
    n_heads = o_ref.shape[1]
    for v in range(2):
        bkt = bkt_ref[v]
        for head in range(n_heads):
            tile = jnp.full(bkt.shape, NEG_BIG, F32)
            for bk in range(REL_BUCKETS):
                tile = jnp.where(bkt == bk, tbl_ref[head, bk] * LOG2E, tile)
            o_ref[v, head] = tile


def _swa_bias_call(rel_bias):
    bkt = jnp.asarray(_swa_bucket_tiles())
    n_heads = rel_bias.shape[1]
    return pl.pallas_call(
        _swa_bias_kernel,
        out_shape=jax.ShapeDtypeStruct((2, n_heads, _SWA_BAND, _SWA_TQ), F32),
        in_specs=[
            pl.BlockSpec(memory_space=pltpu.SMEM),
            pl.BlockSpec(memory_space=pltpu.VMEM),
        ],
        out_specs=pl.BlockSpec(memory_space=pltpu.VMEM),
        name="swa_bias",
    )(rel_bias.T, bkt)


def _swa_kernel(qt_ref, kp_ref, kc_ref, vtp_ref, vtc_ref, bias_ref, sink_ref, o_ref):
    tq = _SWA_TQ
    first_block = jnp.minimum(pl.program_id(1), 1)
    lane = lax.broadcasted_iota(jnp.int32, (1, LANES), 1)
    ones = jnp.ones((_ONES_ROWS, _SWA_BAND), BF16)
    units = [(sub, hk, par) for sub in range(_SWA_SUB)
             for hk in range(SWA_KV_HEADS) for par in range(2)]

    def band_keys(sub, ksl):
        if sub == 0:
            return jnp.concatenate([kp_ref[0, :, ksl], kc_ref[0, :tq, ksl]], axis=0)
        return kc_ref[0, (sub - 1) * tq:(sub + 1) * tq, ksl]

    def band_values(sub, vsl):
        if sub == 0:
            return jnp.concatenate([vtp_ref[vsl, :], vtc_ref[vsl, :tq]], axis=1)
        return vtc_ref[vsl, (sub - 1) * tq:(sub + 1) * tq]

    def scores(sub, hk, par):
        kb = band_keys(sub, slice(hk * LANES, (hk + 1) * LANES))
        f0 = hk * SWA_GROUP * HEAD_DIM
        qs = slice(sub * tq, (sub + 1) * tq)
        wq = jnp.concatenate([qt_ref[f0:f0 + LANES, qs], qt_ref[f0 + LANES:f0 + 2 * LANES, qs]],
                             axis=1)
        head_lanes = (lane < HEAD_DIM) if par == 0 else (lane >= HEAD_DIM)
        return jnp.dot(jnp.where(head_lanes, kb, 0), wq, preferred_element_type=F32)

    sts = [scores(*u) for u in units]
    outs = {}
    for (sub, hk, par), st in zip(units, sts):
        vt1 = jnp.concatenate(
            [band_values(sub, slice(hk * HEAD_DIM, (hk + 1) * HEAD_DIM)), ones], axis=0)
        variant = first_block if sub == 0 else 1
        heads = (hk * SWA_GROUP + par, hk * SWA_GROUP + par + 2)
        ps, ms = [], []
        for i, head in enumerate(heads):
            s = st[:, i * tq:(i + 1) * tq] + bias_ref[variant, head]
            m = jnp.maximum(jnp.max(s, axis=0, keepdims=True), sink_ref[head])
            ps.append(jnp.exp2(s - m).astype(BF16))
            ms.append(m)
        acc = jnp.dot(vt1, jnp.concatenate(ps, axis=1), preferred_element_type=F32)
        for i, head in enumerate(heads):
            a = acc[:, i * tq:(i + 1) * tq]
            den = a[HEAD_DIM:HEAD_DIM + 1] + jnp.exp2(sink_ref[head] - ms[i])
            outs[sub, head] = a[:HEAD_DIM] / den
    n_heads = SWA_KV_HEADS * SWA_GROUP
    for sub in range(_SWA_SUB):
        o_t = jnp.concatenate([outs[sub, head] for head in range(n_heads)], axis=0)
        o_ref[0, sub * tq:(sub + 1) * tq, :] = o_t.T.astype(o_ref.dtype)


def _swa_call(q_t, k, v_t, bias, sink, batch, seq):
    d = q_t.shape[0]
    kw = k.shape[2]
    vw = v_t.shape[0]
    tq = _SWA_TQ
    ts = _SWA_SUB * tq
    ns = seq // ts
    prev = lambda i: jnp.maximum(_SWA_SUB * i - 1, 0)
    return pl.pallas_call(
        _swa_kernel,
        out_shape=jax.ShapeDtypeStruct((batch, seq, d), BF16),
        grid=(batch, ns),
        in_specs=[
            pl.BlockSpec((d, ts), lambda b, i: (0, b * ns + i)),
            pl.BlockSpec((1, tq, kw), lambda b, i: (b, prev(i), 0)),
            pl.BlockSpec((1, ts, kw), lambda b, i: (b, i, 0)),
            pl.BlockSpec((vw, tq), lambda b, i: (0, b * (seq // tq) + prev(i))),
            pl.BlockSpec((vw, ts), lambda b, i: (0, b * ns + i)),
            pl.BlockSpec(bias.shape, lambda b, i: (0, 0, 0, 0)),
            pl.BlockSpec(sink.shape, lambda b, i: (0, 0, 0)),
        ],
        out_specs=pl.BlockSpec((1, ts, d), lambda b, i: (b, i, 0)),
        compiler_params=_params(("parallel", "arbitrary")),
        name="swa_attention",
    )(q_t, k, k, v_t, v_t, bias, sink)


def kernel(x, c, w_ada, b_ada, g_norm_mix, g_norm_ffn, g_final, fox_w_in, fox_b_f, fox_w_out, swa_w_in, swa_sinks, swa_w_out, rel_bias, ffn_w_gu, ffn_w_down, moe_w_router, moe_b_router, moe_w_gu, moe_w_down):
    batch, seq, d = x.shape
    n = batch * seq
    q_scale = HEAD_DIM ** -0.5
    xf = x.reshape(n, d)

    c_pad = jnp.zeros((SUBLANES, d), F32).at[:batch].set(c)
    mod = _ada_call(c_pad, w_ada, b_ada)
    mod0, mod1 = mod[0], mod[1]

    w_in = fox_w_in[0]
    n_heads = d // HEAD_DIM
    k, q_t, v_t, f, w_gu0, w_down0, w_out0, w_out1 = _attn_proj_call(
        xf, g_norm_mix[0], mod0, w_in[:, d:2 * d].astype(BF16),
        (w_in[:, :d] * (q_scale * LOG2E)).T.astype(BF16), w_in[:, 2 * d:3 * d].T.astype(BF16),
        seq, w_f=_split_cols(w_in[:, 3 * d:]),
        cast=(ffn_w_gu[0], ffn_w_down[0], fox_w_out[0], swa_w_out[0]))
    g_cum, r_cum = _cum_call(f, fox_b_f[0], batch, seq)
    r_flat = r_cum.transpose(0, 2, 1).reshape(-1)
    o, w_down1 = _fox_call(r_flat, q_t, k.reshape(batch, seq, d), v_t, g_cum,
                           cast=(moe_w_down[0],))
    w_down1 = w_down1.reshape(moe_w_down.shape[1:])
    x2, w_gu1 = _ffn_call(xf, o.reshape(n, d), w_out0, g_norm_ffn[0], mod0, w_gu0, w_down0, seq,
                          cast=(moe_w_gu[0],))
    w_gu1 = w_gu1.reshape(moe_w_gu.shape[1:])

    w_in = swa_w_in[0]
    kvw = SWA_KV_HEADS * HEAD_DIM
    dup = lambda w: jnp.repeat(w.reshape(d, SWA_KV_HEADS, 1, HEAD_DIM), 2, axis=2).reshape(d, 2 * kvw)
    k, q_t, v_t = _attn_proj_call(
        x2, g_norm_mix[1], mod1, dup(w_in[:, d:d + kvw]).astype(BF16),
        (w_in[:, :d] * (q_scale * LOG2E)).T.astype(BF16), w_in[:, d + kvw:].T.astype(BF16), seq)
    bias = _swa_bias_call(rel_bias)
    sink = jnp.broadcast_to((swa_sinks[0] * LOG2E)[:, None, None], (n_heads, 1, _SWA_TQ))
    o = _swa_call(q_t, k.reshape(batch, seq, 2 * kvw), v_t, bias, sink, batch, seq)
    w_r = _split_cols(jnp.zeros((d, LANES), F32).at[:, :N_EXPERTS].set(moe_w_router[0]))
    b_r = jnp.zeros((1, LANES), F32).at[0, :N_EXPERTS].set(moe_b_router[0])
    x3, h4, route, route_t, counts = _out_call(o.reshape(n, d), w_out1, x2, mod1, g_norm_ffn[1],
                                               w_r, b_r, seq)
    out = _moe_call(h4, route, route_t, counts, x3, mod1, g_final, w_gu1, w_down1, seq)
    return out.reshape(batch, seq, d)
```

```python
import functools

import numpy as np
import jax
import jax.numpy as jnp
from jax import lax
from jax.experimental import pallas as pl
from jax.experimental.pallas import tpu as pltpu

F32 = jnp.float32
BF16 = jnp.bfloat16

HEAD_DIM = 64
CHUNK = 64
WINDOW_CHUNKS = 2
REL_BUCKETS = 32
REL_MAX_DIST = 128
SWA_KV_HEADS = 4
SWA_GROUP = 4
N_EXPERTS = 8
EPS = 1e-6

LANES = 128
SUBLANES = 8
VMEM_LIMIT = 56 * 1024 * 1024

NEG_BIG = -1e30


def _params(sem, vmem=VMEM_LIMIT):
    return pltpu.CompilerParams(dimension_semantics=sem, vmem_limit_bytes=vmem)


def _rms_mod(x, g, shift, scale):
    ms = jnp.mean(x * x, axis=-1, keepdims=True)
    y = x * lax.rsqrt(ms + EPS) * g
    return y * (1.0 + scale) + shift


def _silu(x):
    return x / (1.0 + jnp.exp(-x))


def _split_bf16(x):
    hi = x.astype(BF16)
    return hi, (x - hi.astype(F32)).astype(BF16)


def _split_cols(w):
    return jnp.concatenate(_split_bf16(w), axis=1)


def _dot_split(x_hi, x_lo, w_ref):
    n = w_ref.shape[1] // 2
    y = jnp.dot(x_hi, w_ref[...], preferred_element_type=F32)
    return y[:, :n] + y[:, n:] + jnp.dot(x_lo, w_ref[:, :n], preferred_element_type=F32)


_BF16_SUBLANES = 16


def _ride_along(arrays, n_steps, step_index):
    views, in_specs, out_specs, out_shapes = [], [], [], []
    for a in arrays:
        v = a.reshape(-1, a.shape[-1])
        rows, rem = divmod(v.shape[0], n_steps)
        assert rem == 0 and rows % _BF16_SUBLANES == 0, v.shape
        spec = pl.BlockSpec((rows, v.shape[1]), lambda *g: (step_index(*g), 0))
        views.append(v)
        in_specs.append(spec)
        out_specs.append(spec)
        out_shapes.append(jax.ShapeDtypeStruct(v.shape, BF16))
    return views, in_specs, out_specs, out_shapes


def _ride_along_cast(in_refs, out_refs):
    for src, dst in zip(in_refs, out_refs):
        dst[...] = src[...].astype(BF16)


def _ada_kernel(c_ref, w_ref, b_ref, o_ref):
    c_hi, c_lo = _split_bf16(_silu(c_ref[...]))
    w_hi, w_lo = _split_bf16(w_ref[0])
    o_ref[0] = (jnp.dot(c_hi, w_hi, preferred_element_type=F32)
                + jnp.dot(c_lo, w_hi, preferred_element_type=F32)
                + jnp.dot(c_hi, w_lo, preferred_element_type=F32) + b_ref[0])


def _ada_call(c_pad, w_ada, b_ada):
    depth, d, n = w_ada.shape
    tn = 1536
    return pl.pallas_call(
        _ada_kernel,
        out_shape=jax.ShapeDtypeStruct((depth, SUBLANES, n), F32),
        grid=(depth, n // tn),
        in_specs=[
            pl.BlockSpec((SUBLANES, d), lambda l, j: (0, 0)),
            pl.BlockSpec((1, d, tn), lambda l, j: (l, 0, j)),
            pl.BlockSpec((1, 1, tn), lambda l, j: (l, 0, j)),
        ],
        out_specs=pl.BlockSpec((1, SUBLANES, tn), lambda l, j: (l, 0, j)),
        compiler_params=_params(("parallel", "parallel")),
        name="ada_mod",
    )(c_pad, w_ada, b_ada.reshape(depth, 1, n))


_NT = (((1,), (1,)), ((), ()))


def _attn_proj_kernel(*refs, tiles_per_batch, with_gate, n_cast):
    x_ref, g_ref, sh_ref, sc_ref, wk_ref, wqt_ref, wvt_ref = refs[:7]
    n_in = 7 + with_gate + n_cast
    k_ref, qt_ref, vt_ref = refs[n_in:n_in + 3]
    b = pl.program_id(0) // tiles_per_batch
    h = _rms_mod(x_ref[...], g_ref[...], sh_ref[pl.ds(b, 1), :], sc_ref[pl.ds(b, 1), :])
    hb = h.astype(BF16)
    k_ref[...] = jnp.dot(hb, wk_ref[...], preferred_element_type=F32).astype(BF16)
    qt_ref[...] = lax.dot_general(wqt_ref[...], hb, _NT, preferred_element_type=F32).astype(BF16)
    vt_ref[...] = lax.dot_general(wvt_ref[...], hb, _NT, preferred_element_type=F32).astype(BF16)
    if with_gate:
        refs[n_in + 3][...] = _dot_split(hb, (h - hb.astype(F32)).astype(BF16), refs[7])
    _ride_along_cast(refs[n_in - n_cast:n_in], refs[len(refs) - n_cast:])


def _attn_proj_call(x, g, mod, w_k, w_qt, w_vt, seq, w_f=None, cast=(), tm=512):
    n, d = x.shape
    with_gate = w_f is not None
    full = lambda a: pl.BlockSpec(a.shape, lambda i: (0, 0))
    c_views, c_in, c_out, c_shapes = _ride_along(cast, n // tm, lambda i: i)
    in_specs = [
        pl.BlockSpec((tm, d), lambda i: (i, 0)),
        pl.BlockSpec((1, d), lambda i: (0, 0)),
        pl.BlockSpec((SUBLANES, d), lambda i: (0, 0)),
        pl.BlockSpec((SUBLANES, d), lambda i: (0, 1)),
        full(w_k), full(w_qt), full(w_vt),
    ]
    args = [x, g.reshape(1, d), mod, mod, w_k, w_qt, w_vt]
    out_shape = [jax.ShapeDtypeStruct((n, w_k.shape[1]), BF16),
                 jax.ShapeDtypeStruct((w_qt.shape[0], n), BF16),
                 jax.ShapeDtypeStruct((w_vt.shape[0], n), BF16)]
    out_specs = [pl.BlockSpec((tm, w_k.shape[1]), lambda i: (i, 0)),
                 pl.BlockSpec((w_qt.shape[0], tm), lambda i: (0, i)),
                 pl.BlockSpec((w_vt.shape[0], tm), lambda i: (0, i))]
    if with_gate:
        in_specs.append(full(w_f))
        args.append(w_f)
        out_shape.append(jax.ShapeDtypeStruct((n, w_f.shape[1] // 2), F32))
        out_specs.append(pl.BlockSpec((tm, w_f.shape[1] // 2), lambda i: (i, 0)))
    return pl.pallas_call(
        functools.partial(_attn_proj_kernel, tiles_per_batch=seq // tm, with_gate=with_gate,
                          n_cast=len(cast)),
        out_shape=out_shape + c_shapes,
        grid=(n // tm,),
        in_specs=in_specs + c_in,
        out_specs=out_specs + c_out,
        compiler_params=_params(("parallel",)),
        name="attn_proj_gate" if with_gate else "attn_proj",
    )(*args, *c_views)


_FOX_T = 512
LOG2E = 1.4426950408889634


def _cum_kernel(f_ref, bf_ref, g_ref, r_ref):
    x = f_ref[...] + bf_ref[...]
    logf = (jnp.minimum(x, 0.0) - jnp.log(1.0 + jnp.exp(-jnp.abs(x)))) * LOG2E
    seq, nh = logf.shape
    r = lax.broadcasted_iota(jnp.int32, (_FOX_T, _FOX_T), 0)
    c = lax.broadcasted_iota(jnp.int32, (_FOX_T, _FOX_T), 1)
    lower = jnp.where(c <= r, 1.0, 0.0).astype(BF16)
    hi = logf.astype(BF16).astype(F32)
    mid = (logf - hi).astype(BF16).astype(F32)
    parts = jnp.concatenate([hi, mid, logf - hi - mid], axis=1).astype(BF16)
    carry = jnp.zeros((1, nh), F32)
    for ch in range(seq // _FOX_T):
        rows = slice(ch * _FOX_T, (ch + 1) * _FOX_T)
        y = jnp.dot(lower, parts[rows, :], preferred_element_type=F32)
        cs = y[:, :nh] + y[:, nh:2 * nh] + y[:, 2 * nh:]
        g_ref[0, rows, :] = cs
        r_ref[0, ch:ch + 1, :] = carry
        carry = carry + cs[_FOX_T - 1:_FOX_T, :]


def _cum_call(f, b_f, batch, seq):
    nh = f.shape[1]
    return pl.pallas_call(
        _cum_kernel,
        out_shape=[jax.ShapeDtypeStruct((batch, seq, nh), F32),
                   jax.ShapeDtypeStruct((batch, seq // _FOX_T, nh), F32)],
        grid=(batch,),
        in_specs=[
            pl.BlockSpec((seq, nh), lambda b: (b, 0)),
            pl.BlockSpec((1, nh), lambda b: (0, 0)),
        ],
        out_specs=[pl.BlockSpec((1, seq, nh), lambda b: (b, 0, 0)),
                   pl.BlockSpec((1, seq // _FOX_T, nh), lambda b: (b, 0, 0))],
        compiler_params=_params(("parallel",)),
        name="forget_cumsum",
    )(f, b_f.reshape(1, nh))


_ONES_ROWS = 16


def _fox_kernel(*refs, n_cast):
    r_ref, qt_ref, k_ref, vt_ref, g_ref = refs[:5]
    o_ref = refs[5 + n_cast]
    ka_sc, m_sc, acc_sc, sa_sc, sb_sc, xa_sc, xb_sc = refs[6 + 2 * n_cast:]
    _ride_along_cast(refs[5:5 + n_cast], refs[6 + n_cast:6 + 2 * n_cast])
    t = _FOX_T
    b, hp = pl.program_id(0), pl.program_id(1)
    seq = k_ref.shape[1]
    n_blocks = seq // t
    n_heads = 2 * pl.num_programs(1)

    lane = lax.broadcasted_iota(jnp.int32, (1, LANES), 1)
    feat = lax.broadcasted_iota(jnp.int32, (LANES, 1), 0)
    aug0 = [HEAD_DIM, 0]

    g_all = g_ref[0]
    head_col = lax.broadcasted_iota(jnp.int32, g_all.shape, 1)
    for hh in range(2):
        g_head = jnp.sum(jnp.where(head_col == 2 * hp + hh, g_all, 0.0), axis=1, keepdims=True)
        gb = jnp.broadcast_to(g_head, (seq, LANES))
        hi = gb.astype(BF16).astype(F32)
        mid = (gb - hi).astype(BF16).astype(F32)
        lo = gb - hi - mid
        aug = jnp.where(lane == aug0[hh], hi,
                        jnp.where(lane == aug0[hh] + 1, mid,
                                  jnp.where(lane == aug0[hh] + 2, lo, 0.0)))
        own = (lane >= hh * HEAD_DIM) & (lane < (hh + 1) * HEAD_DIM)
        ka_sc[hh] = jnp.where(own, k_ref[0].astype(F32), aug).astype(BF16)

    feat_t = lax.broadcasted_iota(jnp.int32, (LANES, t), 0)
    r_base = [(b * n_heads + 2 * hp + hh) * n_blocks for hh in range(2)]
    bufs = ((sa_sc, xa_sc), (sb_sc, xb_sc))

    def queries(qi):
        qt2 = qt_ref[:, qi * t:(qi + 1) * t]
        return [jnp.where((feat >= hh * HEAD_DIM) & (feat < (hh + 1) * HEAD_DIM), qt2, 0)
                + jnp.where((feat_t >= aug0[hh]) & (feat_t < aug0[hh] + 3), -1.0, 0.0).astype(BF16)
                for hh in range(2)]

    half = t // 2
    lo, hi = slice(0, half), slice(half, t)
    below_diag = (lax.broadcasted_iota(jnp.int32, (half, half), 0)
                  <= lax.broadcasted_iota(jnp.int32, (half, half), 1))

    def scores(qh, qi, kb, dst):
        k0 = kb * t
        for hh in range(2):
            if kb == qi:
                dst[0][hh, lo, :] = jnp.dot(ka_sc[hh, k0:k0 + half, :], qh[hh],
                                            preferred_element_type=F32)
                dst[0][hh, hi, hi] = jnp.dot(ka_sc[hh, k0 + half:k0 + t, :], qh[hh][:, hi],
                                             preferred_element_type=F32)
            else:
                st = jnp.dot(ka_sc[hh, k0:k0 + t, :], qh[hh],
                             preferred_element_type=F32)
                dst[0][hh] = st
                dst[1][hh] = jnp.max(st, axis=0, keepdims=True)

    def values(hh, kb, keys):
        k0, k1 = kb * t + keys.start, kb * t + keys.stop
        return jnp.concatenate(
            [vt_ref[hh * HEAD_DIM:(hh + 1) * HEAD_DIM, k0:k1],
             jnp.ones((_ONES_ROWS, k1 - k0), BF16)], axis=0)

    def softmax_pv(qi, kb, cur):
        for hh in range(2):
            c = r_ref[r_base[hh] + kb] - r_ref[r_base[hh] + qi]
            m_old = m_sc[hh]
            if kb == qi:
                s_ll = jnp.where(below_diag, cur[0][hh, lo, lo], -jnp.inf)
                s_lh = cur[0][hh, lo, hi]
                s_hh = jnp.where(below_diag, cur[0][hh, hi, hi], -jnp.inf)
                st_max = jnp.concatenate(
                    [jnp.max(s_ll, axis=0, keepdims=True),
                     jnp.maximum(jnp.max(s_lh, axis=0, keepdims=True),
                                 jnp.max(s_hh, axis=0, keepdims=True))], axis=1)
                m_new = jnp.maximum(m_old, st_max - c)
                alpha = jnp.exp2(m_old - m_new)
                shift = m_new + c
                p_ll = jnp.exp2(s_ll - shift[:, lo]).astype(BF16)
                p_lh = jnp.exp2(s_lh - shift[:, hi]).astype(BF16)
                p_hh = jnp.exp2(s_hh - shift[:, hi]).astype(BF16)
                v_lo, v_hi = values(hh, kb, lo), values(hh, kb, hi)
                pv = jnp.concatenate(
                    [jnp.dot(v_lo, p_ll, preferred_element_type=F32),
                     jnp.dot(v_lo, p_lh, preferred_element_type=F32)
                     + jnp.dot(v_hi, p_hh, preferred_element_type=F32)], axis=1)
            else:
                st = cur[0][hh]
                m_new = jnp.maximum(m_old, cur[1][hh] - c)
                alpha = jnp.exp2(m_old - m_new)
                p = jnp.exp2(st - (m_new + c)).astype(BF16)
                pv = jnp.dot(values(hh, kb, slice(0, t)), p, preferred_element_type=F32)
            acc_sc[hh] = alpha * acc_sc[hh] + pv
            m_sc[hh] = m_new

    pairs = [(qi, kb) for qi in range(n_blocks) for kb in range(qi + 1)]
    qh = queries(0)
    scores(qh, 0, 0, bufs[0])
    for s, (qi, kb) in enumerate(pairs):
        if kb == 0:
            m_sc[...] = jnp.full_like(m_sc, -jnp.inf)
            acc_sc[...] = jnp.zeros_like(acc_sc)
        if s + 1 < len(pairs):
            qi_n, kb_n = pairs[s + 1]
            if qi_n != qi:
                qh = queries(qi_n)
            scores(qh, qi_n, kb_n, bufs[(s + 1) % 2])
        softmax_pv(qi, kb, bufs[s % 2])
        if kb == qi:
            outs = [acc_sc[hh, :HEAD_DIM, :] / acc_sc[hh, HEAD_DIM:HEAD_DIM + 1, :]
                    for hh in range(2)]
            o_ref[0, qi * t:(qi + 1) * t, :] = jnp.concatenate(outs, axis=0).T.astype(o_ref.dtype)


def _fox_call(r_flat, q_t, k, v_t, g_cum, cast=()):
    batch, seq, d = k.shape
    n_pairs = d // LANES
    t = _FOX_T
    c_views, c_in, c_out, c_shapes = _ride_along(
        cast, batch * n_pairs, lambda b, h: b * n_pairs + h)
    return pl.pallas_call(
        functools.partial(_fox_kernel, n_cast=len(cast)),
        out_shape=[jax.ShapeDtypeStruct((batch, seq, d), BF16)] + c_shapes,
        grid=(batch, n_pairs),
        in_specs=[
            pl.BlockSpec(memory_space=pltpu.SMEM),
            pl.BlockSpec((LANES, seq), lambda b, h: (h, b)),
            pl.BlockSpec((1, seq, LANES), lambda b, h: (b, 0, h)),
            pl.BlockSpec((LANES, seq), lambda b, h: (h, b)),
            pl.BlockSpec((1, seq, g_cum.shape[2]), lambda b, h: (b, 0, 0)),
        ] + c_in,
        out_specs=[pl.BlockSpec((1, seq, LANES), lambda b, h: (b, 0, h))] + c_out,
        scratch_shapes=[pltpu.VMEM((2, seq, LANES), BF16), pltpu.VMEM((2, 1, t), F32),
                        pltpu.VMEM((2, HEAD_DIM + _ONES_ROWS, t), F32),
                        pltpu.VMEM((2, t, t), F32), pltpu.VMEM((2, t, t), F32),
                        pltpu.VMEM((2, 1, t), F32), pltpu.VMEM((2, 1, t), F32)],
        compiler_params=_params(("parallel", "parallel")),
        name="fox_attention",
    )(r_flat, q_t, k, v_t, g_cum, *c_views)


def _out_kernel(o_ref, w_ref, x_ref, gt_ref, g_ref, sh_ref, sc_ref, wr_ref, br_ref,
                xo_ref, h_ref, route_ref, route_t_ref, cnt_ref, *, tiles_per_batch):
    b = pl.program_id(0) // tiles_per_batch
    y = jnp.dot(o_ref[...], w_ref[...], preferred_element_type=F32)
    xn = x_ref[...] + gt_ref[pl.ds(b, 1), :] * y
    xo_ref[...] = xn
    h = _rms_mod(xn, g_ref[...], sh_ref[pl.ds(b, 1), :], sc_ref[pl.ds(b, 1), :])
    h_ref[...] = h
    logits = _dot_split(*_split_bf16(h), wr_ref) + br_ref[...]
    tm = logits.shape[0]
    lane = lax.broadcasted_iota(jnp.int32, logits.shape, 1).astype(F32)
    logits = jnp.where(lane < N_EXPERTS, logits, -jnp.inf)
    m1 = jnp.max(logits, axis=1, keepdims=True)
    i1 = jnp.min(jnp.where(logits == m1, lane, float(LANES)), axis=1, keepdims=True)
    rest = jnp.where(lane == i1, -jnp.inf, logits)
    m2 = jnp.max(rest, axis=1, keepdims=True)
    i2 = jnp.min(jnp.where(rest == m2, lane, float(LANES)), axis=1, keepdims=True)
    e2 = jnp.exp(m2 - m1)
    den = 1.0 + e2

    @pl.when(pl.program_id(0) == 0)
    def _():
        cnt_ref[...] = jnp.zeros_like(cnt_ref)

    sel1 = lane == i1
    sel2 = lane == i2
    onehot = jnp.where(sel1 | sel2, 1.0, 0.0)
    r = lax.broadcasted_iota(jnp.int32, (tm, tm), 0)
    c = lax.broadcasted_iota(jnp.int32, (tm, tm), 1)
    lower = jnp.where(c < r, 1.0, 0.0).astype(BF16)
    before = jnp.dot(lower, onehot.astype(BF16), preferred_element_type=F32) + cnt_ref[0:1, :]
    rank1 = jnp.sum(jnp.where(sel1, before, 0.0), axis=1, keepdims=True)
    rank2 = jnp.sum(jnp.where(sel2, before, 0.0), axis=1, keepdims=True)
    cnt_ref[0:1, :] = cnt_ref[0:1, :] + jnp.sum(onehot, axis=0, keepdims=True)
    route = jnp.where(lane == 0.0, i1, 0.0)
    for k, val in enumerate((i2, 1.0 / den, e2 / den, rank1, rank2), start=1):
        route = jnp.where(lane == float(k), val, route)
    route_ref[...] = route
    route_t_ref[...] = route.T[:SUBLANES, :]


def _out_call(o, w_out, x, mod, g, w_r, b_r, seq, tm=512):
    n, d = x.shape
    row = lambda w: pl.BlockSpec((tm, w), lambda i: (i, 0))
    col = lambda c: pl.BlockSpec((SUBLANES, d), lambda i: (0, c))
    return pl.pallas_call(
        functools.partial(_out_kernel, tiles_per_batch=seq // tm),
        out_shape=[jax.ShapeDtypeStruct((n, d), F32), jax.ShapeDtypeStruct((n, d), F32),
                   jax.ShapeDtypeStruct((n, LANES), F32), jax.ShapeDtypeStruct((SUBLANES, n), F32),
                   jax.ShapeDtypeStruct((SUBLANES, LANES), F32)],
        grid=(n // tm,),
        in_specs=[
            row(d),
            pl.BlockSpec((d, d), lambda i: (0, 0)),
            row(d),
            col(2),
            pl.BlockSpec((1, d), lambda i: (0, 0)),
            col(3),
            col(4),
            pl.BlockSpec((d, 2 * LANES), lambda i: (0, 0)),
            pl.BlockSpec((1, LANES), lambda i: (0, 0)),
        ],
        out_specs=[row(d), row(d), row(LANES), pl.BlockSpec((SUBLANES, tm), lambda i: (0, i)),
                   pl.BlockSpec((SUBLANES, LANES), lambda i: (0, 0))],
        compiler_params=_params(("arbitrary",)),
        name="out_proj_router",
    )(o, w_out, x, mod, g.reshape(1, d), mod, mod, w_r, b_r)


_FF_TILE = 1792


def _snake(i, j, nj):
    return jnp.where(i % 2 == 0, j, nj - 1 - j)


def _ffn_kernel(*refs, tiles_per_batch, n_cast):
    (x_ref, attn_ref, wo_ref, gta_ref, g_ref, sh_ref, sc_ref, gt_ref,
     wg_ref, wu_ref, wd_ref) = refs[:11]
    o_ref = refs[11 + n_cast]
    h_sc = refs[-1]
    _ride_along_cast(refs[11:11 + n_cast], refs[12 + n_cast:12 + 2 * n_cast])
    b = pl.program_id(0) // tiles_per_batch

    @pl.when(pl.program_id(1) == 0)
    def _():
        x = x_ref[...] + gta_ref[pl.ds(b, 1), :] * jnp.dot(
            attn_ref[...], wo_ref[...], preferred_element_type=F32)
        h = _rms_mod(x, g_ref[...], sh_ref[pl.ds(b, 1), :], sc_ref[pl.ds(b, 1), :])
        h_sc[...] = h.astype(BF16)
        o_ref[...] = x

    h = h_sc[...]
    g = jnp.dot(h, wg_ref[...], preferred_element_type=F32)
    u = jnp.dot(h, wu_ref[...], preferred_element_type=F32)
    a = (_silu(g) * u).astype(BF16)
    o_ref[...] += gt_ref[pl.ds(b, 1), :] * jnp.dot(a, wd_ref[...], preferred_element_type=F32)


def _ffn_call(x, attn, w_out, g, mod, w_gu, w_down, seq, cast=(), tm=512, tf=_FF_TILE):
    n, d = x.shape
    f = w_down.shape[0]
    nj = f // tf
    c_views, c_in, c_out, c_shapes = _ride_along(cast, (n // tm) * nj, lambda i, j: i * nj + j)
    return pl.pallas_call(
        functools.partial(_ffn_kernel, tiles_per_batch=seq // tm, n_cast=len(cast)),
        out_shape=[jax.ShapeDtypeStruct((n, d), F32)] + c_shapes,
        grid=(n // tm, nj),
        in_specs=[
            pl.BlockSpec((tm, d), lambda i, j: (i, 0)),
            pl.BlockSpec((tm, d), lambda i, j: (i, 0)),
            pl.BlockSpec((d, d), lambda i, j: (0, 0)),
            pl.BlockSpec((SUBLANES, d), lambda i, j: (0, 2)),
            pl.BlockSpec((1, d), lambda i, j: (0, 0)),
            pl.BlockSpec((SUBLANES, d), lambda i, j: (0, 3)),
            pl.BlockSpec((SUBLANES, d), lambda i, j: (0, 4)),
            pl.BlockSpec((SUBLANES, d), lambda i, j: (0, 5)),
            pl.BlockSpec((d, tf), lambda i, j: (0, _snake(i, j, nj))),
            pl.BlockSpec((d, tf), lambda i, j: (0, _snake(i, j, nj) + nj)),
            pl.BlockSpec((tf, d), lambda i, j: (_snake(i, j, nj), 0)),
        ] + c_in,
        out_specs=[pl.BlockSpec((tm, d), lambda i, j: (i, 0))] + c_out,
        scratch_shapes=[pltpu.VMEM((tm, d), BF16)],
        compiler_params=_params(("parallel", "arbitrary")),
        name="ffn_swiglu",
    )(x, attn, w_out, mod, g.reshape(1, d), mod, mod, mod, w_gu, w_gu, w_down, *c_views)


_MOE_TM = 512


def _dispatch_kernel(d1_ref, d2_ref, zero_ref, h_ref, xs_ref, z_sc, sem):
    tm = h_ref.shape[0]
    base = pl.program_id(0) * tm

    @pl.when(pl.program_id(0) == 0)
    def _():
        z_sc[...] = jnp.zeros_like(z_sc)

        def zero_copy(t):
            return pltpu.make_async_copy(z_sc, xs_ref.at[pl.ds(pl.multiple_of(t * tm, tm), tm)],
                                         sem.at[0])

        def start(t, carry):
            @pl.when(zero_ref[t] != 0)
            def _():
                zero_copy(t).start()
            return carry

        def wait(t, carry):
            @pl.when(zero_ref[t] != 0)
            def _():
                zero_copy(t).wait()
            return carry

        lax.fori_loop(0, zero_ref.shape[0], start, 0)
        lax.fori_loop(0, zero_ref.shape[0], wait, 0)

    def issue(r, carry):
        src = h_ref.at[pl.ds(r, 1)]
        pltpu.make_async_copy(src, xs_ref.at[pl.ds(d1_ref[base + r], 1)], sem.at[0]).start()
        pltpu.make_async_copy(src, xs_ref.at[pl.ds(d2_ref[base + r], 1)],
                              sem.at[1]).start(priority=1)
        return carry

    lax.fori_loop(0, tm, issue, 0, unroll=8)
    pltpu.make_async_copy(h_ref, xs_ref.at[pl.ds(0, tm)], sem.at[0]).wait()
    pltpu.make_async_copy(h_ref, xs_ref.at[pl.ds(0, tm)], sem.at[1]).wait()


def _dispatch_call(dest1, dest2, zero_tile, h, tm=_MOE_TM):
    n, d = h.shape
    n_rows = zero_tile.shape[0] * tm
    return pl.pallas_call(
        _dispatch_kernel,
        out_shape=jax.ShapeDtypeStruct((n_rows, d), h.dtype),
        grid_spec=pltpu.PrefetchScalarGridSpec(
            num_scalar_prefetch=3,
            grid=(n // tm,),
            in_specs=[pl.BlockSpec((tm, d), lambda i, d1, d2, zt: (i, 0))],
            out_specs=pl.BlockSpec(memory_space=pl.ANY),
            scratch_shapes=[pltpu.VMEM((tm, d), h.dtype), pltpu.SemaphoreType.DMA((2,))],
        ),
        compiler_params=_params(("arbitrary",)),
        name="moe_dispatch",
    )(dest1, dest2, zero_tile, h)


def _experts_kernel(te_ref, nt_ref, xs_ref, wg_ref, wu_ref, wd_ref, ye_ref):
    del te_ref
    t = pl.program_id(0)

    @pl.when(pl.program_id(1) == 0)
    def _():
        ye_ref[...] = jnp.zeros_like(ye_ref)

    @pl.when(t < nt_ref[0])
    def _():
        h = xs_ref[...].astype(BF16)
        g = jnp.dot(h, wg_ref[0], preferred_element_type=F32)
        u = jnp.dot(h, wu_ref[0], preferred_element_type=F32)
        a = (_silu(g) * u).astype(BF16)
        ye_ref[...] += jnp.dot(a, wd_ref[0], preferred_element_type=F32)


def _experts_call(tile_expert, n_tiles, xs, w_gu, w_down, tf=_FF_TILE):
    n_rows, d = xs.shape
    _, f, _ = w_down.shape
    nj = f // tf
    tm = _MOE_TM
    tile = lambda t, nt: jnp.maximum(jnp.minimum(t, nt[0] - 1), 0)
    jj = lambda t, j, nt: _snake(tile(t, nt), jnp.where(t < nt[0], j, nj - 1), nj)
    return pl.pallas_call(
        _experts_kernel,
        out_shape=jax.ShapeDtypeStruct((n_rows, d), F32),
        grid_spec=pltpu.PrefetchScalarGridSpec(
            num_scalar_prefetch=2,
            grid=(n_rows // tm, nj),
            in_specs=[
                pl.BlockSpec((tm, d), lambda t, j, te, nt: (tile(t, nt), 0)),
                pl.BlockSpec((1, d, tf), lambda t, j, te, nt: (te[tile(t, nt)], 0, jj(t, j, nt))),
                pl.BlockSpec((1, d, tf), lambda t, j, te, nt: (te[tile(t, nt)], 0, jj(t, j, nt) + nj)),
                pl.BlockSpec((1, tf, d), lambda t, j, te, nt: (te[tile(t, nt)], jj(t, j, nt), 0)),
            ],
            out_specs=pl.BlockSpec((tm, d), lambda t, j, te, nt: (t, 0)),
        ),
        compiler_params=_params(("arbitrary", "arbitrary")),
        name="moe_experts",
    )(tile_expert, n_tiles, xs, w_gu, w_gu, w_down)


def _combine_kernel(d1_ref, d2_ref, ye_ref, x_ref, route_ref, gt_ref, gf_ref, o_ref,
                    y1_sc, y2_sc, sem, *, tiles_per_batch):
    tm = x_ref.shape[0]
    i = pl.program_id(0)
    b = i // tiles_per_batch
    slot = i % 2

    def gather(tile, dst_slot):
        base = tile * tm

        def issue(r, carry):
            pltpu.make_async_copy(ye_ref.at[pl.ds(d1_ref[base + r], 1)],
                                  y1_sc.at[dst_slot, pl.ds(r, 1)], sem.at[0, dst_slot]).start()
            pltpu.make_async_copy(ye_ref.at[pl.ds(d2_ref[base + r], 1)],
                                  y2_sc.at[dst_slot, pl.ds(r, 1)],
                                  sem.at[1, dst_slot]).start(priority=1)
            return carry

        lax.fori_loop(0, tm, issue, 0, unroll=8)

    @pl.when(i == 0)
    def _():
        gather(0, 0)

    @pl.when(i + 1 < pl.num_programs(0))
    def _():
        gather(i + 1, 1 - slot)

    pltpu.make_async_copy(ye_ref.at[pl.ds(0, tm)], y1_sc.at[slot], sem.at[0, slot]).wait()
    pltpu.make_async_copy(ye_ref.at[pl.ds(0, tm)], y2_sc.at[slot], sem.at[1, slot]).wait()
    route = route_ref[...]
    y = route[:, 2:3] * y1_sc[slot] + route[:, 3:4] * y2_sc[slot]
    xn = x_ref[...] + gt_ref[pl.ds(b, 1), :] * y
    ms = jnp.mean(xn * xn, axis=-1, keepdims=True)
    o_ref[...] = xn * lax.rsqrt(ms + EPS) * gf_ref[...]


def _combine_call(dest1, dest2, ye, x, route, mod, g_final, seq, tm=512):
    n, d = x.shape
    return pl.pallas_call(
        functools.partial(_combine_kernel, tiles_per_batch=seq // tm),
        out_shape=jax.ShapeDtypeStruct((n, d), F32),
        grid_spec=pltpu.PrefetchScalarGridSpec(
            num_scalar_prefetch=2,
            grid=(n // tm,),
            in_specs=[
                pl.BlockSpec(memory_space=pl.ANY),
                pl.BlockSpec((tm, d), lambda i, d1, d2: (i, 0)),
                pl.BlockSpec((tm, LANES), lambda i, d1, d2: (i, 0)),
                pl.BlockSpec((SUBLANES, d), lambda i, d1, d2: (0, 5)),
                pl.BlockSpec((1, d), lambda i, d1, d2: (0, 0)),
            ],
            out_specs=pl.BlockSpec((tm, d), lambda i, d1, d2: (i, 0)),
            scratch_shapes=[pltpu.VMEM((2, tm, d), F32), pltpu.VMEM((2, tm, d), F32),
                            pltpu.SemaphoreType.DMA((2, 2))],
        ),
        compiler_params=_params(("arbitrary",)),
        name="moe_combine",
    )(dest1, dest2, ye, x, route, mod, g_final.reshape(1, d))


def _moe_call(h, route, route_t, counts, x, mod, g_final, w_gu, w_down, seq):
    n, d = x.shape
    ne = w_down.shape[0]
    tm = _MOE_TM
    max_tiles = (2 * n) // tm + ne
    e1, e2, _, _, rank1, rank2 = (route_t[k].astype(jnp.int32) for k in range(6))
    cnt = counts[0, :ne].astype(jnp.int32)
    tiles_e = (cnt + tm - 1) // tm
    tile_end = jnp.cumsum(tiles_e)
    row_start = (tile_end - tiles_e) * tm
    dest1 = row_start[e1] + rank1
    dest2 = row_start[e2] + rank2
    n_tiles = tile_end[-1:]
    tile_ids = jnp.arange(max_tiles, dtype=jnp.int32)
    tile_expert = jnp.minimum(
        jnp.sum((tile_ids[:, None] >= tile_end[None, :]).astype(jnp.int32), axis=1), ne - 1)
    is_last = jnp.any((tile_ids[:, None] == tile_end[None, :] - 1) & (tiles_e[None, :] > 0), axis=1)
    zero_tile = (is_last | (tile_ids >= n_tiles[0])).astype(jnp.int32)
    xs = _dispatch_call(dest1, dest2, zero_tile, h)
    ye = _experts_call(tile_expert, n_tiles, xs, w_gu, w_down)
    return _combine_call(dest1, dest2, ye, x, route, mod, g_final, seq)


_SWA_TQ = 2 * CHUNK
_SWA_BAND = 2 * _SWA_TQ
_SWA_SUB = 4


def _swa_bucket_tiles():
    cc = np.arange(_SWA_BAND)[:, None]
    r = np.arange(_SWA_TQ)[None, :]
    rel = cc - _SWA_TQ - r
    nb = REL_BUCKETS // 2
    max_exact = nb // 2
    ret = (rel > 0).astype(np.int32) * nb
    n = np.abs(rel)
    large = max_exact + (np.log(np.maximum(n, 1) / max_exact)
                         / np.log(REL_MAX_DIST / max_exact) * (nb - max_exact)).astype(np.int32)
    large = np.minimum(large, nb - 1)
    bucket = (ret + np.where(n < max_exact, n, large)).astype(np.int32)
    q_chunk = r // CHUNK
    k_chunk = cc // CHUNK
    visible = (k_chunk >= q_chunk) & (k_chunk <= q_chunk + WINDOW_CHUNKS)
    later = np.where(visible, bucket, -1)
    first = np.where(cc >= _SWA_TQ, later, -1)
    return np.stack([first, later]).astype(np.int32)


def _swa_bias_kernel(tbl_ref, bkt_ref, o_ref):
    n_heads = o_ref.shape[1]
    for v in range(2):
        bkt = bkt_ref[v]
        for head in range(n_heads):
            tile = jnp.full(bkt.shape, NEG_BIG, F32)
            for bk in range(REL_BUCKETS):
                tile = jnp.where(bkt == bk, tbl_ref[head, bk] * LOG2E, tile)
            o_ref[v, head] = tile


def _swa_bias_call(rel_bias):
    bkt = jnp.asarray(_swa_bucket_tiles())
    n_heads = rel_bias.shape[1]
    return pl.pallas_call(
        _swa_bias_kernel,
        out_shape=jax.ShapeDtypeStruct((2, n_heads, _SWA_BAND, _SWA_TQ), F32),
        in_specs=[
            pl.BlockSpec(memory_space=pltpu.SMEM),
            pl.BlockSpec(memory_space=pltpu.VMEM),
        ],
        out_specs=pl.BlockSpec(memory_space=pltpu.VMEM),
        name="swa_bias",
    )(rel_bias.T, bkt)


def _swa_kernel(qt_ref, kp_ref, kc_ref, vtp_ref, vtc_ref, bias_ref, sink_ref, o_ref):
    tq = _SWA_TQ
    first_block = jnp.minimum(pl.program_id(1), 1)
    lane = lax.broadcasted_iota(jnp.int32, (1, LANES), 1)
    ones = jnp.ones((_ONES_ROWS, _SWA_BAND), BF16)
    units = [(sub, hk, par) for sub in range(_SWA_SUB)
             for hk in range(SWA_KV_HEADS) for par in range(2)]

    def band_keys(sub, ksl):
        if sub == 0:
            return jnp.concatenate([kp_ref[0, :, ksl], kc_ref[0, :tq, ksl]], axis=0)
        return kc_ref[0, (sub - 1) * tq:(sub + 1) * tq, ksl]

    def band_values(sub, vsl):
        if sub == 0:
            return jnp.concatenate([vtp_ref[vsl, :], vtc_ref[vsl, :tq]], axis=1)
        return vtc_ref[vsl, (sub - 1) * tq:(sub + 1) * tq]

    def scores(sub, hk, par):
        kb = band_keys(sub, slice(hk * LANES, (hk + 1) * LANES))
        f0 = hk * SWA_GROUP * HEAD_DIM
        qs = slice(sub * tq, (sub + 1) * tq)
        wq = jnp.concatenate([qt_ref[f0:f0 + LANES, qs], qt_ref[f0 + LANES:f0 + 2 * LANES, qs]],
                             axis=1)
        head_lanes = (lane < HEAD_DIM) if par == 0 else (lane >= HEAD_DIM)
        return jnp.dot(jnp.where(head_lanes, kb, 0), wq, preferred_element_type=F32)

    sts = [scores(*u) for u in units]
    outs = {}
    for (sub, hk, par), st in zip(units, sts):
        vt1 = jnp.concatenate(
            [band_values(sub, slice(hk * HEAD_DIM, (hk + 1) * HEAD_DIM)), ones], axis=0)
        variant = first_block if sub == 0 else 1
        heads = (hk * SWA_GROUP + par, hk * SWA_GROUP + par + 2)
        ps, ms = [], []
        for i, head in enumerate(heads):
            s = st[:, i * tq:(i + 1) * tq] + bias_ref[variant, head]
            m = jnp.maximum(jnp.max(s, axis=0, keepdims=True), sink_ref[head])
            ps.append(jnp.exp2(s - m).astype(BF16))
            ms.append(m)
        acc = jnp.dot(vt1, jnp.concatenate(ps, axis=1), preferred_element_type=F32)
        for i, head in enumerate(heads):
            a = acc[:, i * tq:(i + 1) * tq]
            den = a[HEAD_DIM:HEAD_DIM + 1] + jnp.exp2(sink_ref[head] - ms[i])
            outs[sub, head] = a[:HEAD_DIM] / den
    n_heads = SWA_KV_HEADS * SWA_GROUP
    for sub in range(_SWA_SUB):
        o_t = jnp.concatenate([outs[sub, head] for head in range(n_heads)], axis=0)
        o_ref[0, sub * tq:(sub + 1) * tq, :] = o_t.T.astype(o_ref.dtype)


def _swa_call(q_t, k, v_t, bias, sink, batch, seq):
    d = q_t.shape[0]
    kw = k.shape[2]
    vw = v_t.shape[0]
    tq = _SWA_TQ
    ts = _SWA_SUB * tq
    ns = seq // ts
    prev = lambda i: jnp.maximum(_SWA_SUB * i - 1, 0)
    return pl.pallas_call(
        _swa_kernel,
        out_shape=jax.ShapeDtypeStruct((batch, seq, d), BF16),
        grid=(batch, ns),
        in_specs=[
            pl.BlockSpec((d, ts), lambda b, i: (0, b * ns + i)),
            pl.BlockSpec((1, tq, kw), lambda b, i: (b, prev(i), 0)),
            pl.BlockSpec((1, ts, kw), lambda b, i: (b, i, 0)),
            pl.BlockSpec((vw, tq), lambda b, i: (0, b * (seq // tq) + prev(i))),
            pl.BlockSpec((vw, ts), lambda b, i: (0, b * ns + i)),
            pl.BlockSpec(bias.shape, lambda b, i: (0, 0, 0, 0)),
            pl.BlockSpec(sink.shape, lambda b, i: (0, 0, 0)),
        ],
        out_specs=pl.BlockSpec((1, ts, d), lambda b, i: (b, i, 0)),
        compiler_params=_params(("parallel", "arbitrary")),
        name="swa_attention",
    )(q_t, k, k, v_t, v_t, bias, sink)


def kernel(x, c, w_ada, b_ada, g_norm_mix, g_norm_ffn, g_final, fox_w_in, fox_b_f, fox_w_out, swa_w_in, swa_sinks, swa_w_out, rel_bias, ffn_w_gu, ffn_w_down, moe_w_router, moe_b_router, moe_w_gu, moe_w_down):
    batch, seq, d = x.shape
    n = batch * seq
    q_scale = HEAD_DIM ** -0.5
    xf = x.reshape(n, d)

    c_pad = jnp.zeros((SUBLANES, d), F32).at[:batch].set(c)
    mod = _ada_call(c_pad, w_ada, b_ada)
    mod0, mod1 = mod[0], mod[1]

    w_in = fox_w_in[0]
    n_heads = d // HEAD_DIM
    k, q_t, v_t, f, w_gu0, w_down0, w_out0, w_out1 = _attn_proj_call(
        xf, g_norm_mix[0], mod0, w_in[:, d:2 * d].astype(BF16),
        (w_in[:, :d] * (q_scale * LOG2E)).T.astype(BF16), w_in[:, 2 * d:3 * d].T.astype(BF16),
        seq, w_f=_split_cols(w_in[:, 3 * d:]),
        cast=(ffn_w_gu[0], ffn_w_down[0], fox_w_out[0], swa_w_out[0]))
    g_cum, r_cum = _cum_call(f, fox_b_f[0], batch, seq)
    r_flat = r_cum.transpose(0, 2, 1).reshape(-1)
    o, w_down1 = _fox_call(r_flat, q_t, k.reshape(batch, seq, d), v_t, g_cum,
                           cast=(moe_w_down[0],))
    w_down1 = w_down1.reshape(moe_w_down.shape[1:])
    x2, w_gu1 = _ffn_call(xf, o.reshape(n, d), w_out0, g_norm_ffn[0], mod0, w_gu0, w_down0, seq,
                          cast=(moe_w_gu[0],))
    w_gu1 = w_gu1.reshape(moe_w_gu.shape[1:])

    w_in = swa_w_in[0]
    kvw = SWA_KV_HEADS * HEAD_DIM
    dup = lambda w: jnp.repeat(w.reshape(d, SWA_KV_HEADS, 1, HEAD_DIM), 2, axis=2).reshape(d, 2 * kvw)
    k, q_t, v_t = _attn_proj_call(
        x2, g_norm_mix[1], mod1, dup(w_in[:, d:d + kvw]).astype(BF16),
        (w_in[:, :d] * (q_scale * LOG2E)).T.astype(BF16), w_in[:, d + kvw:].T.astype(BF16), seq)
    bias = _swa_bias_call(rel_bias)
    sink = jnp.broadcast_to((swa_sinks[0] * LOG2E)[:, None, None], (n_heads, 1, _SWA_TQ))
    o = _swa_call(q_t, k.reshape(batch, seq, 2 * kvw), v_t, bias, sink, batch, seq)
    w_r = _split_cols(jnp.zeros((d, LANES), F32).at[:, :N_EXPERTS].set(moe_w_router[0]))
    b_r = jnp.zeros((1, LANES), F32).at[0, :N_EXPERTS].set(moe_b_router[0])
    x3, h4, route, route_t, counts = _out_call(o.reshape(n, d), w_out1, x2, mod1, g_norm_ffn[1],
                                               w_r, b_r, seq)
    out = _moe_call(h4, route, route_t, counts, x3, mod1, g_final, w_gu1, w_down1, seq)
    return out.reshape(batch, seq, d)
```

```python
import functools

import numpy as np
import jax
import jax.numpy as jnp
from jax import lax
from jax.experimental import pallas as pl
from jax.experimental.pallas import tpu as pltpu

F32 = jnp.float32
BF16 = jnp.bfloat16

HEAD_DIM = 64
CHUNK = 64
WINDOW_CHUNKS = 2
REL_BUCKETS = 32
REL_MAX_DIST = 128
SWA_KV_HEADS = 4
SWA_GROUP = 4
N_EXPERTS = 8
EPS = 1e-6

LANES = 128
SUBLANES = 8
VMEM_LIMIT = 56 * 1024 * 1024

NEG_BIG = -1e30


def _params(sem, vmem=VMEM_LIMIT):
    return pltpu.CompilerParams(dimension_semantics=sem, vmem_limit_bytes=vmem)


def _rms_mod(x, g, shift, scale):
    ms = jnp.mean(x * x, axis=-1, keepdims=True)
    y = x * lax.rsqrt(ms + EPS) * g
    return y * (1.0 + scale) + shift


def _silu(x):
    return x / (1.0 + jnp.exp(-x))


def _split_bf16(x):
    hi = x.astype(BF16)
    return hi, (x - hi.astype(F32)).astype(BF16)


def _split_cols(w):
    return jnp.concatenate(_split_bf16(w), axis=1)


def _dot_split(x_hi, x_lo, w_ref):
    n = w_ref.shape[1] // 2
    y = jnp.dot(x_hi, w_ref[...], preferred_element_type=F32)
    return y[:, :n] + y[:, n:] + jnp.dot(x_lo, w_ref[:, :n], preferred_element_type=F32)


_BF16_SUBLANES = 16


def _ride_along(arrays, n_steps, step_index):
    views, in_specs, out_specs, out_shapes = [], [], [], []
    for a in arrays:
        v = a.reshape(-1, a.shape[-1])
        rows, rem = divmod(v.shape[0], n_steps)
        assert rem == 0 and rows % _BF16_SUBLANES == 0, v.shape
        spec = pl.BlockSpec((rows, v.shape[1]), lambda *g: (step_index(*g), 0))
        views.append(v)
        in_specs.append(spec)
        out_specs.append(spec)
        out_shapes.append(jax.ShapeDtypeStruct(v.shape, BF16))
    return views, in_specs, out_specs, out_shapes


def _ride_along_cast(in_refs, out_refs):
    for src, dst in zip(in_refs, out_refs):
        dst[...] = src[...].astype(BF16)


def _ada_kernel(c_ref, w_ref, b_ref, o_ref):
    c_hi, c_lo = _split_bf16(_silu(c_ref[...]))
    w_hi, w_lo = _split_bf16(w_ref[0])
    o_ref[0] = (jnp.dot(c_hi, w_hi, preferred_element_type=F32)
                + jnp.dot(c_lo, w_hi, preferred_element_type=F32)
                + jnp.dot(c_hi, w_lo, preferred_element_type=F32) + b_ref[0])


def _ada_call(c_pad, w_ada, b_ada):
    depth, d, n = w_ada.shape
    tn = 1536
    return pl.pallas_call(
        _ada_kernel,
        out_shape=jax.ShapeDtypeStruct((depth, SUBLANES, n), F32),
        grid=(depth, n // tn),
        in_specs=[
            pl.BlockSpec((SUBLANES, d), lambda l, j: (0, 0)),
            pl.BlockSpec((1, d, tn), lambda l, j: (l, 0, j)),
            pl.BlockSpec((1, 1, tn), lambda l, j: (l, 0, j)),
        ],
        out_specs=pl.BlockSpec((1, SUBLANES, tn), lambda l, j: (l, 0, j)),
        compiler_params=_params(("parallel", "parallel")),
        name="ada_mod",
    )(c_pad, w_ada, b_ada.reshape(depth, 1, n))


_NT = (((1,), (1,)), ((), ()))


def _attn_proj_kernel(*refs, tiles_per_batch, with_gate, n_cast):
    x_ref, g_ref, sh_ref, sc_ref, wk_ref, wqt_ref, wvt_ref = refs[:7]
    n_in = 7 + with_gate + n_cast
    k_ref, qt_ref, vt_ref = refs[n_in:n_in + 3]
    b = pl.program_id(0) // tiles_per_batch
    h = _rms_mod(x_ref[...], g_ref[...], sh_ref[pl.ds(b, 1), :], sc_ref[pl.ds(b, 1), :])
    hb = h.astype(BF16)
    k_ref[...] = jnp.dot(hb, wk_ref[...], preferred_element_type=F32).astype(BF16)
    qt_ref[...] = lax.dot_general(wqt_ref[...], hb, _NT, preferred_element_type=F32).astype(BF16)
    vt_ref[...] = lax.dot_general(wvt_ref[...], hb, _NT, preferred_element_type=F32).astype(BF16)
    if with_gate:
        refs[n_in + 3][...] = _dot_split(hb, (h - hb.astype(F32)).astype(BF16), refs[7])
    _ride_along_cast(refs[n_in - n_cast:n_in], refs[len(refs) - n_cast:])


def _attn_proj_call(x, g, mod, w_k, w_qt, w_vt, seq, w_f=None, cast=(), tm=512):
    n, d = x.shape
    with_gate = w_f is not None
    full = lambda a: pl.BlockSpec(a.shape, lambda i: (0, 0))
    c_views, c_in, c_out, c_shapes = _ride_along(cast, n // tm, lambda i: i)
    in_specs = [
        pl.BlockSpec((tm, d), lambda i: (i, 0)),
        pl.BlockSpec((1, d), lambda i: (0, 0)),
        pl.BlockSpec((SUBLANES, d), lambda i: (0, 0)),
        pl.BlockSpec((SUBLANES, d), lambda i: (0, 1)),
        full(w_k), full(w_qt), full(w_vt),
    ]
    args = [x, g.reshape(1, d), mod, mod, w_k, w_qt, w_vt]
    out_shape = [jax.ShapeDtypeStruct((n, w_k.shape[1]), BF16),
                 jax.ShapeDtypeStruct((w_qt.shape[0], n), BF16),
                 jax.ShapeDtypeStruct((w_vt.shape[0], n), BF16)]
    out_specs = [pl.BlockSpec((tm, w_k.shape[1]), lambda i: (i, 0)),
                 pl.BlockSpec((w_qt.shape[0], tm), lambda i: (0, i)),
                 pl.BlockSpec((w_vt.shape[0], tm), lambda i: (0, i))]
    if with_gate:
        in_specs.append(full(w_f))
        args.append(w_f)
        out_shape.append(jax.ShapeDtypeStruct((n, w_f.shape[1] // 2), F32))
        out_specs.append(pl.BlockSpec((tm, w_f.shape[1] // 2), lambda i: (i, 0)))
    return pl.pallas_call(
        functools.partial(_attn_proj_kernel, tiles_per_batch=seq // tm, with_gate=with_gate,
                          n_cast=len(cast)),
        out_shape=out_shape + c_shapes,
        grid=(n // tm,),
        in_specs=in_specs + c_in,
        out_specs=out_specs + c_out,
        compiler_params=_params(("parallel",)),
        name="attn_proj_gate" if with_gate else "attn_proj",
    )(*args, *c_views)


_FOX_T = 512
LOG2E = 1.4426950408889634


def _cum_kernel(f_ref, bf_ref, g_ref, r_ref):
    x = f_ref[...] + bf_ref[...]
    logf = (jnp.minimum(x, 0.0) - jnp.log(1.0 + jnp.exp(-jnp.abs(x)))) * LOG2E
    seq, nh = logf.shape
    r = lax.broadcasted_iota(jnp.int32, (_FOX_T, _FOX_T), 0)
    c = lax.broadcasted_iota(jnp.int32, (_FOX_T, _FOX_T), 1)
    lower = jnp.where(c <= r, 1.0, 0.0).astype(BF16)
    hi = logf.astype(BF16).astype(F32)
    mid = (logf - hi).astype(BF16).astype(F32)
    parts = jnp.concatenate([hi, mid, logf - hi - mid], axis=1).astype(BF16)
    carry = jnp.zeros((1, nh), F32)
    for ch in range(seq // _FOX_T):
        rows = slice(ch * _FOX_T, (ch + 1) * _FOX_T)
        y = jnp.dot(lower, parts[rows, :], preferred_element_type=F32)
        cs = y[:, :nh] + y[:, nh:2 * nh] + y[:, 2 * nh:]
        g_ref[0, rows, :] = cs
        r_ref[0, ch:ch + 1, :] = carry
        carry = carry + cs[_FOX_T - 1:_FOX_T, :]


def _cum_call(f, b_f, batch, seq):
    nh = f.shape[1]
    return pl.pallas_call(
        _cum_kernel,
        out_shape=[jax.ShapeDtypeStruct((batch, seq, nh), F32),
                   jax.ShapeDtypeStruct((batch, seq // _FOX_T, nh), F32)],
        grid=(batch,),
        in_specs=[
            pl.BlockSpec((seq, nh), lambda b: (b, 0)),
            pl.BlockSpec((1, nh), lambda b: (0, 0)),
        ],
        out_specs=[pl.BlockSpec((1, seq, nh), lambda b: (b, 0, 0)),
                   pl.BlockSpec((1, seq // _FOX_T, nh), lambda b: (b, 0, 0))],
        compiler_params=_params(("parallel",)),
        name="forget_cumsum",
    )(f, b_f.reshape(1, nh))


_ONES_ROWS = 16


def _fox_kernel(*refs, n_cast):
    r_ref, qt_ref, k_ref, vt_ref, g_ref = refs[:5]
    o_ref = refs[5 + n_cast]
    ka_sc, m_sc, acc_sc, sa_sc, sb_sc, xa_sc, xb_sc = refs[6 + 2 * n_cast:]
    _ride_along_cast(refs[5:5 + n_cast], refs[6 + n_cast:6 + 2 * n_cast])
    t = _FOX_T
    b, hp = pl.program_id(0), pl.program_id(1)
    seq = k_ref.shape[1]
    n_blocks = seq // t
    n_heads = 2 * pl.num_programs(1)

    lane = lax.broadcasted_iota(jnp.int32, (1, LANES), 1)
    feat = lax.broadcasted_iota(jnp.int32, (LANES, 1), 0)
    aug0 = [HEAD_DIM, 0]

    g_all = g_ref[0]
    head_col = lax.broadcasted_iota(jnp.int32, g_all.shape, 1)
    for hh in range(2):
        g_head = jnp.sum(jnp.where(head_col == 2 * hp + hh, g_all, 0.0), axis=1, keepdims=True)
        gb = jnp.broadcast_to(g_head, (seq, LANES))
        hi = gb.astype(BF16).astype(F32)
        mid = (gb - hi).astype(BF16).astype(F32)
        lo = gb - hi - mid
        aug = jnp.where(lane == aug0[hh], hi,
                        jnp.where(lane == aug0[hh] + 1, mid,
                                  jnp.where(lane == aug0[hh] + 2, lo, 0.0)))
        own = (lane >= hh * HEAD_DIM) & (lane < (hh + 1) * HEAD_DIM)
        ka_sc[hh] = jnp.where(own, k_ref[0].astype(F32), aug).astype(BF16)

    feat_t = lax.broadcasted_iota(jnp.int32, (LANES, t), 0)
    r_base = [(b * n_heads + 2 * hp + hh) * n_blocks for hh in range(2)]
    bufs = ((sa_sc, xa_sc), (sb_sc, xb_sc))

    def queries(qi):
        qt2 = qt_ref[:, qi * t:(qi + 1) * t]
        return [jnp.where((feat >= hh * HEAD_DIM) & (feat < (hh + 1) * HEAD_DIM), qt2, 0)
                + jnp.where((feat_t >= aug0[hh]) & (feat_t < aug0[hh] + 3), -1.0, 0.0).astype(BF16)
                for hh in range(2)]

    half = t // 2
    lo, hi = slice(0, half), slice(half, t)
    below_diag = (lax.broadcasted_iota(jnp.int32, (half, half), 0)
                  <= lax.broadcasted_iota(jnp.int32, (half, half), 1))

    def scores(qh, qi, kb, dst):
        k0 = kb * t
        for hh in range(2):
            if kb == qi:
                dst[0][hh, lo, :] = jnp.dot(ka_sc[hh, k0:k0 + half, :], qh[hh],
                                            preferred_element_type=F32)
                dst[0][hh, hi, hi] = jnp.dot(ka_sc[hh, k0 + half:k0 + t, :], qh[hh][:, hi],
                                             preferred_element_type=F32)
            else:
                st = jnp.dot(ka_sc[hh, k0:k0 + t, :], qh[hh],
                             preferred_element_type=F32)
                dst[0][hh] = st
                dst[1][hh] = jnp.max(st, axis=0, keepdims=True)

    def values(hh, kb, keys):
        k0, k1 = kb * t + keys.start, kb * t + keys.stop
        return jnp.concatenate(
            [vt_ref[hh * HEAD_DIM:(hh + 1) * HEAD_DIM, k0:k1],
             jnp.ones((_ONES_ROWS, k1 - k0), BF16)], axis=0)

    def softmax_pv(qi, kb, cur):
        for hh in range(2):
            c = r_ref[r_base[hh] + kb] - r_ref[r_base[hh] + qi]
            m_old = m_sc[hh]
            if kb == qi:
                s_ll = jnp.where(below_diag, cur[0][hh, lo, lo], -jnp.inf)
                s_lh = cur[0][hh, lo, hi]
                s_hh = jnp.where(below_diag, cur[0][hh, hi, hi], -jnp.inf)
                st_max = jnp.concatenate(
                    [jnp.max(s_ll, axis=0, keepdims=True),
                     jnp.maximum(jnp.max(s_lh, axis=0, keepdims=True),
                                 jnp.max(s_hh, axis=0, keepdims=True))], axis=1)
                m_new = jnp.maximum(m_old, st_max - c)
                alpha = jnp.exp2(m_old - m_new)
                shift = m_new + c
                p_ll = jnp.exp2(s_ll - shift[:, lo]).astype(BF16)
                p_lh = jnp.exp2(s_lh - shift[:, hi]).astype(BF16)
                p_hh = jnp.exp2(s_hh - shift[:, hi]).astype(BF16)
                v_lo, v_hi = values(hh, kb, lo), values(hh, kb, hi)
                pv = jnp.concatenate(
                    [jnp.dot(v_lo, p_ll, preferred_element_type=F32),
                     jnp.dot(v_lo, p_lh, preferred_element_type=F32)
                     + jnp.dot(v_hi, p_hh, preferred_element_type=F32)], axis=1)
            else:
                st = cur[0][hh]
                m_new = jnp.maximum(m_old, cur[1][hh] - c)
                alpha = jnp.exp2(m_old - m_new)
                p = jnp.exp2(st - (m_new + c)).astype(BF16)
                pv = jnp.dot(values(hh, kb, slice(0, t)), p, preferred_element_type=F32)
            acc_sc[hh] = alpha * acc_sc[hh] + pv
            m_sc[hh] = m_new

    pairs = [(qi, kb) for qi in range(n_blocks) for kb in range(qi + 1)]
    qh = queries(0)
    scores(qh, 0, 0, bufs[0])
    for s, (qi, kb) in enumerate(pairs):
        if kb == 0:
            m_sc[...] = jnp.full_like(m_sc, -jnp.inf)
            acc_sc[...] = jnp.zeros_like(acc_sc)
        if s + 1 < len(pairs):
            qi_n, kb_n = pairs[s + 1]
            if qi_n != qi:
                qh = queries(qi_n)
            scores(qh, qi_n, kb_n, bufs[(s + 1) % 2])
        softmax_pv(qi, kb, bufs[s % 2])
        if kb == qi:
            outs = [acc_sc[hh, :HEAD_DIM, :] / acc_sc[hh, HEAD_DIM:HEAD_DIM + 1, :]
                    for hh in range(2)]
            o_ref[0, qi * t:(qi + 1) * t, :] = jnp.concatenate(outs, axis=0).T.astype(o_ref.dtype)


def _fox_call(r_flat, q_t, k, v_t, g_cum, cast=()):
    batch, seq, d = k.shape
    n_pairs = d // LANES
    t = _FOX_T
    c_views, c_in, c_out, c_shapes = _ride_along(
        cast, batch * n_pairs, lambda b, h: b * n_pairs + h)
    return pl.pallas_call(
        functools.partial(_fox_kernel, n_cast=len(cast)),
        out_shape=[jax.ShapeDtypeStruct((batch, seq, d), BF16)] + c_shapes,
        grid=(batch, n_pairs),
        in_specs=[
            pl.BlockSpec(memory_space=pltpu.SMEM),
            pl.BlockSpec((LANES, seq), lambda b, h: (h, b)),
            pl.BlockSpec((1, seq, LANES), lambda b, h: (b, 0, h)),
            pl.BlockSpec((LANES, seq), lambda b, h: (h, b)),
            pl.BlockSpec((1, seq, g_cum.shape[2]), lambda b, h: (b, 0, 0)),
        ] + c_in,
        out_specs=[pl.BlockSpec((1, seq, LANES), lambda b, h: (b, 0, h))] + c_out,
        scratch_shapes=[pltpu.VMEM((2, seq, LANES), BF16), pltpu.VMEM((2, 1, t), F32),
                        pltpu.VMEM((2, HEAD_DIM + _ONES_ROWS, t), F32),
                        pltpu.VMEM((2, t, t), F32), pltpu.VMEM((2, t, t), F32),
                        pltpu.VMEM((2, 1, t), F32), pltpu.VMEM((2, 1, t), F32)],
        compiler_params=_params(("parallel", "parallel")),
        name="fox_attention",
    )(r_flat, q_t, k, v_t, g_cum, *c_views)


def _out_kernel(o_ref, w_ref, x_ref, gt_ref, g_ref, sh_ref, sc_ref, wr_ref, br_ref,
                xo_ref, h_ref, route_ref, route_t_ref, cnt_ref, *, tiles_per_batch):
    b = pl.program_id(0) // tiles_per_batch
    y = jnp.dot(o_ref[...], w_ref[...], preferred_element_type=F32)
    xn = x_ref[...] + gt_ref[pl.ds(b, 1), :] * y
    xo_ref[...] = xn
    h = _rms_mod(xn, g_ref[...], sh_ref[pl.ds(b, 1), :], sc_ref[pl.ds(b, 1), :])
    h_ref[...] = h
    logits = _dot_split(*_split_bf16(h), wr_ref) + br_ref[...]
    tm = logits.shape[0]
    lane = lax.broadcasted_iota(jnp.int32, logits.shape, 1).astype(F32)
    logits = jnp.where(lane < N_EXPERTS, logits, -jnp.inf)
    m1 = jnp.max(logits, axis=1, keepdims=True)
    i1 = jnp.min(jnp.where(logits == m1, lane, float(LANES)), axis=1, keepdims=True)
    rest = jnp.where(lane == i1, -jnp.inf, logits)
    m2 = jnp.max(rest, axis=1, keepdims=True)
    i2 = jnp.min(jnp.where(rest == m2, lane, float(LANES)), axis=1, keepdims=True)
    e2 = jnp.exp(m2 - m1)
    den = 1.0 + e2

    @pl.when(pl.program_id(0) == 0)
    def _():
        cnt_ref[...] = jnp.zeros_like(cnt_ref)

    sel1 = lane == i1
    sel2 = lane == i2
    onehot = jnp.where(sel1 | sel2, 1.0, 0.0)
    r = lax.broadcasted_iota(jnp.int32, (tm, tm), 0)
    c = lax.broadcasted_iota(jnp.int32, (tm, tm), 1)
    lower = jnp.where(c < r, 1.0, 0.0).astype(BF16)
    before = jnp.dot(lower, onehot.astype(BF16), preferred_element_type=F32) + cnt_ref[0:1, :]
    rank1 = jnp.sum(jnp.where(sel1, before, 0.0), axis=1, keepdims=True)
    rank2 = jnp.sum(jnp.where(sel2, before, 0.0), axis=1, keepdims=True)
    cnt_ref[0:1, :] = cnt_ref[0:1, :] + jnp.sum(onehot, axis=0, keepdims=True)
    route = jnp.where(lane == 0.0, i1, 0.0)
    for k, val in enumerate((i2, 1.0 / den, e2 / den, rank1, rank2), start=1):
        route = jnp.where(lane == float(k), val, route)
    route_ref[...] = route
    route_t_ref[...] = route.T[:SUBLANES, :]


def _out_call(o, w_out, x, mod, g, w_r, b_r, seq, tm=512):
    n, d = x.shape
    row = lambda w: pl.BlockSpec((tm, w), lambda i: (i, 0))
    col = lambda c: pl.BlockSpec((SUBLANES, d), lambda i: (0, c))
    return pl.pallas_call(
        functools.partial(_out_kernel, tiles_per_batch=seq // tm),
        out_shape=[jax.ShapeDtypeStruct((n, d), F32), jax.ShapeDtypeStruct((n, d), F32),
                   jax.ShapeDtypeStruct((n, LANES), F32), jax.ShapeDtypeStruct((SUBLANES, n), F32),
                   jax.ShapeDtypeStruct((SUBLANES, LANES), F32)],
        grid=(n // tm,),
        in_specs=[
            row(d),
            pl.BlockSpec((d, d), lambda i: (0, 0)),
            row(d),
            col(2),
            pl.BlockSpec((1, d), lambda i: (0, 0)),
            col(3),
            col(4),
            pl.BlockSpec((d, 2 * LANES), lambda i: (0, 0)),
            pl.BlockSpec((1, LANES), lambda i: (0, 0)),
        ],
        out_specs=[row(d), row(d), row(LANES), pl.BlockSpec((SUBLANES, tm), lambda i: (0, i)),
                   pl.BlockSpec((SUBLANES, LANES), lambda i: (0, 0))],
        compiler_params=_params(("arbitrary",)),
        name="out_proj_router",
    )(o, w_out, x, mod, g.reshape(1, d), mod, mod, w_r, b_r)


_FF_TILE = 1792


def _snake(i, j, nj):
    return jnp.where(i % 2 == 0, j, nj - 1 - j)


def _ffn_kernel(*refs, tiles_per_batch, n_cast):
    (x_ref, attn_ref, wo_ref, gta_ref, g_ref, sh_ref, sc_ref, gt_ref,
     wg_ref, wu_ref, wd_ref) = refs[:11]
    o_ref = refs[11 + n_cast]
    h_sc = refs[-1]
    _ride_along_cast(refs[11:11 + n_cast], refs[12 + n_cast:12 + 2 * n_cast])
    b = pl.program_id(0) // tiles_per_batch

    @pl.when(pl.program_id(1) == 0)
    def _():
        x = x_ref[...] + gta_ref[pl.ds(b, 1), :] * jnp.dot(
            attn_ref[...], wo_ref[...], preferred_element_type=F32)
        h = _rms_mod(x, g_ref[...], sh_ref[pl.ds(b, 1), :], sc_ref[pl.ds(b, 1), :])
        h_sc[...] = h.astype(BF16)
        o_ref[...] = x

    h = h_sc[...]
    g = jnp.dot(h, wg_ref[...], preferred_element_type=F32)
    u = jnp.dot(h, wu_ref[...], preferred_element_type=F32)
    a = (_silu(g) * u).astype(BF16)
    o_ref[...] += gt_ref[pl.ds(b, 1), :] * jnp.dot(a, wd_ref[...], preferred_element_type=F32)


def _ffn_call(x, attn, w_out, g, mod, w_gu, w_down, seq, cast=(), tm=512, tf=_FF_TILE):
    n, d = x.shape
    f = w_down.shape[0]
    nj = f // tf
    c_views, c_in, c_out, c_shapes = _ride_along(cast, (n // tm) * nj, lambda i, j: i * nj + j)
    return pl.pallas_call(
        functools.partial(_ffn_kernel, tiles_per_batch=seq // tm, n_cast=len(cast)),
        out_shape=[jax.ShapeDtypeStruct((n, d), F32)] + c_shapes,
        grid=(n // tm, nj),
        in_specs=[
            pl.BlockSpec((tm, d), lambda i, j: (i, 0)),
            pl.BlockSpec((tm, d), lambda i, j: (i, 0)),
            pl.BlockSpec((d, d), lambda i, j: (0, 0)),
            pl.BlockSpec((SUBLANES, d), lambda i, j: (0, 2)),
            pl.BlockSpec((1, d), lambda i, j: (0, 0)),
            pl.BlockSpec((SUBLANES, d), lambda i, j: (0, 3)),
            pl.BlockSpec((SUBLANES, d), lambda i, j: (0, 4)),
            pl.BlockSpec((SUBLANES, d), lambda i, j: (0, 5)),
            pl.BlockSpec((d, tf), lambda i, j: (0, _snake(i, j, nj))),
            pl.BlockSpec((d, tf), lambda i, j: (0, _snake(i, j, nj) + nj)),
            pl.BlockSpec((tf, d), lambda i, j: (_snake(i, j, nj), 0)),
        ] + c_in,
        out_specs=[pl.BlockSpec((tm, d), lambda i, j: (i, 0))] + c_out,
        scratch_shapes=[pltpu.VMEM((tm, d), BF16)],
        compiler_params=_params(("parallel", "arbitrary")),
        name="ffn_swiglu",
    )(x, attn, w_out, mod, g.reshape(1, d), mod, mod, mod, w_gu, w_gu, w_down, *c_views)


_MOE_TM = 512


def _dispatch_kernel(d1_ref, d2_ref, zero_ref, h_ref, xs_ref, z_sc, sem):
    tm = h_ref.shape[0]
    base = pl.program_id(0) * tm

    @pl.when(pl.program_id(0) == 0)
    def _():
        z_sc[...] = jnp.zeros_like(z_sc)

        def zero_copy(t):
            return pltpu.make_async_copy(z_sc, xs_ref.at[pl.ds(pl.multiple_of(t * tm, tm), tm)],
                                         sem.at[0])

        def start(t, carry):
            @pl.when(zero_ref[t] != 0)
            def _():
                zero_copy(t).start()
            return carry

        def wait(t, carry):
            @pl.when(zero_ref[t] != 0)
            def _():
                zero_copy(t).wait()
            return carry

        lax.fori_loop(0, zero_ref.shape[0], start, 0)
        lax.fori_loop(0, zero_ref.shape[0], wait, 0)

    def issue(r, carry):
        src = h_ref.at[pl.ds(r, 1)]
        pltpu.make_async_copy(src, xs_ref.at[pl.ds(d1_ref[base + r], 1)], sem.at[0]).start()
        pltpu.make_async_copy(src, xs_ref.at[pl.ds(d2_ref[base + r], 1)],
                              sem.at[1]).start(priority=1)
        return carry

    lax.fori_loop(0, tm, issue, 0, unroll=8)
    pltpu.make_async_copy(h_ref, xs_ref.at[pl.ds(0, tm)], sem.at[0]).wait()
    pltpu.make_async_copy(h_ref, xs_ref.at[pl.ds(0, tm)], sem.at[1]).wait()


def _dispatch_call(dest1, dest2, zero_tile, h, tm=_MOE_TM):
    n, d = h.shape
    n_rows = zero_tile.shape[0] * tm
    return pl.pallas_call(
        _dispatch_kernel,
        out_shape=jax.ShapeDtypeStruct((n_rows, d), h.dtype),
        grid_spec=pltpu.PrefetchScalarGridSpec(
            num_scalar_prefetch=3,
            grid=(n // tm,),
            in_specs=[pl.BlockSpec((tm, d), lambda i, d1, d2, zt: (i, 0))],
            out_specs=pl.BlockSpec(memory_space=pl.ANY),
            scratch_shapes=[pltpu.VMEM((tm, d), h.dtype), pltpu.SemaphoreType.DMA((2,))],
        ),
        compiler_params=_params(("arbitrary",)),
        name="moe_dispatch",
    )(dest1, dest2, zero_tile, h)


def _experts_kernel(te_ref, nt_ref, rows_ref, xs_ref, wg_ref, wu_ref, wd_ref, ye_ref):
    del te_ref, nt_ref
    tm = xs_ref.shape[0]
    rows = rows_ref[pl.program_id(0)]

    @pl.when(pl.program_id(1) == 0)
    def _():
        ye_ref[...] = jnp.zeros_like(ye_ref)

    def swiglu(n):
        h = xs_ref[:n, :].astype(BF16)
        g = jnp.dot(h, wg_ref[0], preferred_element_type=F32)
        u = jnp.dot(h, wu_ref[0], preferred_element_type=F32)
        a = (_silu(g) * u).astype(BF16)
        ye_ref[:n, :] += jnp.dot(a, wd_ref[0], preferred_element_type=F32)

    @pl.when(rows > tm // 2)
    def _():
        swiglu(tm)

    @pl.when((rows > 0) & (rows <= tm // 2))
    def _():
        swiglu(tm // 2)


def _experts_call(tile_expert, n_tiles, tile_rows, xs, w_gu, w_down, tf=_FF_TILE):
    n_rows, d = xs.shape
    _, f, _ = w_down.shape
    nj = f // tf
    tm = _MOE_TM
    tile = lambda t, nt: jnp.maximum(jnp.minimum(t, nt[0] - 1), 0)
    jj = lambda t, j, nt: _snake(tile(t, nt), jnp.where(t < nt[0], j, nj - 1), nj)
    return pl.pallas_call(
        _experts_kernel,
        out_shape=jax.ShapeDtypeStruct((n_rows, d), F32),
        grid_spec=pltpu.PrefetchScalarGridSpec(
            num_scalar_prefetch=3,
            grid=(n_rows // tm, nj),
            in_specs=[
                pl.BlockSpec((tm, d), lambda t, j, te, nt, tr: (tile(t, nt), 0)),
                pl.BlockSpec((1, d, tf), lambda t, j, te, nt, tr: (te[tile(t, nt)], 0, jj(t, j, nt))),
                pl.BlockSpec((1, d, tf),
                             lambda t, j, te, nt, tr: (te[tile(t, nt)], 0, jj(t, j, nt) + nj)),
                pl.BlockSpec((1, tf, d), lambda t, j, te, nt, tr: (te[tile(t, nt)], jj(t, j, nt), 0)),
            ],
            out_specs=pl.BlockSpec((tm, d), lambda t, j, te, nt, tr: (t, 0)),
        ),
        compiler_params=_params(("arbitrary", "arbitrary")),
        name="moe_experts",
    )(tile_expert, n_tiles, tile_rows, xs, w_gu, w_gu, w_down)


def _combine_kernel(d1_ref, d2_ref, ye_ref, x_ref, route_ref, gt_ref, gf_ref, o_ref,
                    y1_sc, y2_sc, sem, *, tiles_per_batch):
    tm = x_ref.shape[0]
    i = pl.program_id(0)
    b = i // tiles_per_batch
    slot = i % 2

    def gather(tile, dst_slot):
        base = tile * tm

        def issue(r, carry):
            pltpu.make_async_copy(ye_ref.at[pl.ds(d1_ref[base + r], 1)],
                                  y1_sc.at[dst_slot, pl.ds(r, 1)], sem.at[0, dst_slot]).start()
            pltpu.make_async_copy(ye_ref.at[pl.ds(d2_ref[base + r], 1)],
                                  y2_sc.at[dst_slot, pl.ds(r, 1)],
                                  sem.at[1, dst_slot]).start(priority=1)
            return carry

        lax.fori_loop(0, tm, issue, 0, unroll=8)

    @pl.when(i == 0)
    def _():
        gather(0, 0)

    @pl.when(i + 1 < pl.num_programs(0))
    def _():
        gather(i + 1, 1 - slot)

    pltpu.make_async_copy(ye_ref.at[pl.ds(0, tm)], y1_sc.at[slot], sem.at[0, slot]).wait()
    pltpu.make_async_copy(ye_ref.at[pl.ds(0, tm)], y2_sc.at[slot], sem.at[1, slot]).wait()
    route = route_ref[...]
    y = route[:, 2:3] * y1_sc[slot] + route[:, 3:4] * y2_sc[slot]
    xn = x_ref[...] + gt_ref[pl.ds(b, 1), :] * y
    ms = jnp.mean(xn * xn, axis=-1, keepdims=True)
    o_ref[...] = xn * lax.rsqrt(ms + EPS) * gf_ref[...]


def _combine_call(dest1, dest2, ye, x, route, mod, g_final, seq, tm=512):
    n, d = x.shape
    return pl.pallas_call(
        functools.partial(_combine_kernel, tiles_per_batch=seq // tm),
        out_shape=jax.ShapeDtypeStruct((n, d), F32),
        grid_spec=pltpu.PrefetchScalarGridSpec(
            num_scalar_prefetch=2,
            grid=(n // tm,),
            in_specs=[
                pl.BlockSpec(memory_space=pl.ANY),
                pl.BlockSpec((tm, d), lambda i, d1, d2: (i, 0)),
                pl.BlockSpec((tm, LANES), lambda i, d1, d2: (i, 0)),
                pl.BlockSpec((SUBLANES, d), lambda i, d1, d2: (0, 5)),
                pl.BlockSpec((1, d), lambda i, d1, d2: (0, 0)),
            ],
            out_specs=pl.BlockSpec((tm, d), lambda i, d1, d2: (i, 0)),
            scratch_shapes=[pltpu.VMEM((2, tm, d), F32), pltpu.VMEM((2, tm, d), F32),
                            pltpu.SemaphoreType.DMA((2, 2))],
        ),
        compiler_params=_params(("arbitrary",)),
        name="moe_combine",
    )(dest1, dest2, ye, x, route, mod, g_final.reshape(1, d))


def _moe_call(h, route, route_t, counts, x, mod, g_final, w_gu, w_down, seq):
    n, d = x.shape
    ne = w_down.shape[0]
    tm = _MOE_TM
    max_tiles = (2 * n) // tm + ne
    e1, e2, _, _, rank1, rank2 = (route_t[k].astype(jnp.int32) for k in range(6))
    cnt = counts[0, :ne].astype(jnp.int32)
    tiles_e = (cnt + tm - 1) // tm
    tile_end = jnp.cumsum(tiles_e)
    row_start = (tile_end - tiles_e) * tm
    dest1 = row_start[e1] + rank1
    dest2 = row_start[e2] + rank2
    n_tiles = tile_end[-1:]
    tile_ids = jnp.arange(max_tiles, dtype=jnp.int32)
    tile_expert = jnp.minimum(
        jnp.sum((tile_ids[:, None] >= tile_end[None, :]).astype(jnp.int32), axis=1), ne - 1)
    is_last = jnp.any((tile_ids[:, None] == tile_end[None, :] - 1) & (tiles_e[None, :] > 0), axis=1)
    zero_tile = (is_last | (tile_ids >= n_tiles[0])).astype(jnp.int32)
    xs = _dispatch_call(dest1, dest2, zero_tile, h)
    tile_rows = jnp.clip(cnt[tile_expert] - (tile_ids - (tile_end - tiles_e)[tile_expert]) * tm, 0, tm)
    tile_rows = jnp.where(tile_ids < n_tiles[0], tile_rows, 0).astype(jnp.int32)
    ye = _experts_call(tile_expert, n_tiles, tile_rows, xs, w_gu, w_down)
    return _combine_call(dest1, dest2, ye, x, route, mod, g_final, seq)


_SWA_TQ = 2 * CHUNK
_SWA_BAND = 2 * _SWA_TQ
_SWA_SUB = 4


def _swa_bucket_tiles():
    cc = np.arange(_SWA_BAND)[:, None]
    r = np.arange(_SWA_TQ)[None, :]
    rel = cc - _SWA_TQ - r
    nb = REL_BUCKETS // 2
    max_exact = nb // 2
    ret = (rel > 0).astype(np.int32) * nb
    n = np.abs(rel)
    large = max_exact + (np.log(np.maximum(n, 1) / max_exact)
                         / np.log(REL_MAX_DIST / max_exact) * (nb - max_exact)).astype(np.int32)
    large = np.minimum(large, nb - 1)
    bucket = (ret + np.where(n < max_exact, n, large)).astype(np.int32)
    q_chunk = r // CHUNK
    k_chunk = cc // CHUNK
    visible = (k_chunk >= q_chunk) & (k_chunk <= q_chunk + WINDOW_CHUNKS)
    later = np.where(visible, bucket, -1)
    first = np.where(cc >= _SWA_TQ, later, -1)
    return np.stack([first, later]).astype(np.int32)


def _swa_bias_kernel(tbl_ref, bkt_ref, o_ref):
    n_heads = o_ref.shape[1]
    for v in range(2):
        bkt = bkt_ref[v]
        for head in range(n_heads):
            tile = jnp.full(bkt.shape, NEG_BIG, F32)
            for bk in range(REL_BUCKETS):
                tile = jnp.where(bkt == bk, tbl_ref[head, bk] * LOG2E, tile)
            o_ref[v, head] = tile


def _swa_bias_call(rel_bias):
    bkt = jnp.asarray(_swa_bucket_tiles())
    n_heads = rel_bias.shape[1]
    return pl.pallas_call(
        _swa_bias_kernel,
        out_shape=jax.ShapeDtypeStruct((2, n_heads, _SWA_BAND, _SWA_TQ), F32),
        in_specs=[
            pl.BlockSpec(memory_space=pltpu.SMEM),
            pl.BlockSpec(memory_space=pltpu.VMEM),
        ],
        out_specs=pl.BlockSpec(memory_space=pltpu.VMEM),
        name="swa_bias",
    )(rel_bias.T, bkt)


def _swa_kernel(qt_ref, kp_ref, kc_ref, vtp_ref, vtc_ref, bias_ref, sink_ref, o_ref):
    tq = _SWA_TQ
    first_block = jnp.minimum(pl.program_id(1), 1)
    lane = lax.broadcasted_iota(jnp.int32, (1, LANES), 1)
    ones = jnp.ones((_ONES_ROWS, _SWA_BAND), BF16)
    units = [(sub, hk, par) for sub in range(_SWA_SUB)
             for hk in range(SWA_KV_HEADS) for par in range(2)]

    def band_keys(sub, ksl):
        if sub == 0:
            return jnp.concatenate([kp_ref[0, :, ksl], kc_ref[0, :tq, ksl]], axis=0)
        return kc_ref[0, (sub - 1) * tq:(sub + 1) * tq, ksl]

    def band_values(sub, vsl):
        if sub == 0:
            return jnp.concatenate([vtp_ref[vsl, :], vtc_ref[vsl, :tq]], axis=1)
        return vtc_ref[vsl, (sub - 1) * tq:(sub + 1) * tq]

    def scores(sub, hk, par):
        kb = band_keys(sub, slice(hk * LANES, (hk + 1) * LANES))
        f0 = hk * SWA_GROUP * HEAD_DIM
        qs = slice(sub * tq, (sub + 1) * tq)
        wq = jnp.concatenate([qt_ref[f0:f0 + LANES, qs], qt_ref[f0 + LANES:f0 + 2 * LANES, qs]],
                             axis=1)
        head_lanes = (lane < HEAD_DIM) if par == 0 else (lane >= HEAD_DIM)
        return jnp.dot(jnp.where(head_lanes, kb, 0), wq, preferred_element_type=F32)

    sts = [scores(*u) for u in units]
    outs = {}
    for (sub, hk, par), st in zip(units, sts):
        vt1 = jnp.concatenate(
            [band_values(sub, slice(hk * HEAD_DIM, (hk + 1) * HEAD_DIM)), ones], axis=0)
        variant = first_block if sub == 0 else 1
        heads = (hk * SWA_GROUP + par, hk * SWA_GROUP + par + 2)
        ps, ms = [], []
        for i, head in enumerate(heads):
            s = st[:, i * tq:(i + 1) * tq] + bias_ref[variant, head]
            m = jnp.maximum(jnp.max(s, axis=0, keepdims=True), sink_ref[head])
            ps.append(jnp.exp2(s - m).astype(BF16))
            ms.append(m)
        acc = jnp.dot(vt1, jnp.concatenate(ps, axis=1), preferred_element_type=F32)
        for i, head in enumerate(heads):
            a = acc[:, i * tq:(i + 1) * tq]
            den = a[HEAD_DIM:HEAD_DIM + 1] + jnp.exp2(sink_ref[head] - ms[i])
            outs[sub, head] = a[:HEAD_DIM] / den
    n_heads = SWA_KV_HEADS * SWA_GROUP
    for sub in range(_SWA_SUB):
        o_t = jnp.concatenate([outs[sub, head] for head in range(n_heads)], axis=0)
        o_ref[0, sub * tq:(sub + 1) * tq, :] = o_t.T.astype(o_ref.dtype)


def _swa_call(q_t, k, v_t, bias, sink, batch, seq):
    d = q_t.shape[0]
    kw = k.shape[2]
    vw = v_t.shape[0]
    tq = _SWA_TQ
    ts = _SWA_SUB * tq
    ns = seq // ts
    prev = lambda i: jnp.maximum(_SWA_SUB * i - 1, 0)
    return pl.pallas_call(
        _swa_kernel,
        out_shape=jax.ShapeDtypeStruct((batch, seq, d), BF16),
        grid=(batch, ns),
        in_specs=[
            pl.BlockSpec((d, ts), lambda b, i: (0, b * ns + i)),
            pl.BlockSpec((1, tq, kw), lambda b, i: (b, prev(i), 0)),
            pl.BlockSpec((1, ts, kw), lambda b, i: (b, i, 0)),
            pl.BlockSpec((vw, tq), lambda b, i: (0, b * (seq // tq) + prev(i))),
            pl.BlockSpec((vw, ts), lambda b, i: (0, b * ns + i)),
            pl.BlockSpec(bias.shape, lambda b, i: (0, 0, 0, 0)),
            pl.BlockSpec(sink.shape, lambda b, i: (0, 0, 0)),
        ],
        out_specs=pl.BlockSpec((1, ts, d), lambda b, i: (b, i, 0)),
        compiler_params=_params(("parallel", "arbitrary")),
        name="swa_attention",
    )(q_t, k, k, v_t, v_t, bias, sink)


def kernel(x, c, w_ada, b_ada, g_norm_mix, g_norm_ffn, g_final, fox_w_in, fox_b_f, fox_w_out, swa_w_in, swa_sinks, swa_w_out, rel_bias, ffn_w_gu, ffn_w_down, moe_w_router, moe_b_router, moe_w_gu, moe_w_down):
    batch, seq, d = x.shape
    n = batch * seq
    q_scale = HEAD_DIM ** -0.5
    xf = x.reshape(n, d)

    c_pad = jnp.zeros((SUBLANES, d), F32).at[:batch].set(c)
    mod = _ada_call(c_pad, w_ada, b_ada)
    mod0, mod1 = mod[0], mod[1]

    w_in = fox_w_in[0]
    n_heads = d // HEAD_DIM
    k, q_t, v_t, f, w_gu0, w_down0, w_out0, w_out1 = _attn_proj_call(
        xf, g_norm_mix[0], mod0, w_in[:, d:2 * d].astype(BF16),
        (w_in[:, :d] * (q_scale * LOG2E)).T.astype(BF16), w_in[:, 2 * d:3 * d].T.astype(BF16),
        seq, w_f=_split_cols(w_in[:, 3 * d:]),
        cast=(ffn_w_gu[0], ffn_w_down[0], fox_w_out[0], swa_w_out[0]))
    g_cum, r_cum = _cum_call(f, fox_b_f[0], batch, seq)
    r_flat = r_cum.transpose(0, 2, 1).reshape(-1)
    o, w_down1 = _fox_call(r_flat, q_t, k.reshape(batch, seq, d), v_t, g_cum,
                           cast=(moe_w_down[0],))
    w_down1 = w_down1.reshape(moe_w_down.shape[1:])
    x2, w_gu1 = _ffn_call(xf, o.reshape(n, d), w_out0, g_norm_ffn[0], mod0, w_gu0, w_down0, seq,
                          cast=(moe_w_gu[0],))
    w_gu1 = w_gu1.reshape(moe_w_gu.shape[1:])

    w_in = swa_w_in[0]
    kvw = SWA_KV_HEADS * HEAD_DIM
    dup = lambda w: jnp.repeat(w.reshape(d, SWA_KV_HEADS, 1, HEAD_DIM), 2, axis=2).reshape(d, 2 * kvw)
    k, q_t, v_t = _attn_proj_call(
        x2, g_norm_mix[1], mod1, dup(w_in[:, d:d + kvw]).astype(BF16),
        (w_in[:, :d] * (q_scale * LOG2E)).T.astype(BF16), w_in[:, d + kvw:].T.astype(BF16), seq)
    bias = _swa_bias_call(rel_bias)
    sink = jnp.broadcast_to((swa_sinks[0] * LOG2E)[:, None, None], (n_heads, 1, _SWA_TQ))
    o = _swa_call(q_t, k.reshape(batch, seq, 2 * kvw), v_t, bias, sink, batch, seq)
    w_r = _split_cols(jnp.zeros((d, LANES), F32).at[:, :N_EXPERTS].set(moe_w_router[0]))
    b_r = jnp.zeros((1, LANES), F32).at[0, :N_EXPERTS].set(moe_b_router[0])
    x3, h4, route, route_t, counts = _out_call(o.reshape(n, d), w_out1, x2, mod1, g_norm_ffn[1],
                                               w_r, b_r, seq)
    out = _moe_call(h4, route, route_t, counts, x3, mod1, g_final, w_gu1, w_down1, seq)
    return out.reshape(batch, seq, d)
```

```python
import functools

import numpy as np
import jax
import jax.numpy as jnp
from jax import lax
from jax.experimental import pallas as pl
from jax.experimental.pallas import tpu as pltpu

F32 = jnp.float32
BF16 = jnp.bfloat16

HEAD_DIM = 64
CHUNK = 64
WINDOW_CHUNKS = 2
REL_BUCKETS = 32
REL_MAX_DIST = 128
SWA_KV_HEADS = 4
SWA_GROUP = 4
N_EXPERTS = 8
EPS = 1e-6

LANES = 128
SUBLANES = 8
VMEM_LIMIT = 56 * 1024 * 1024

NEG_BIG = -1e30


def _params(sem, vmem=VMEM_LIMIT):
    return pltpu.CompilerParams(dimension_semantics=sem, vmem_limit_bytes=vmem)


def _rms_mod(x, g, shift, scale):
    ms = jnp.mean(x * x, axis=-1, keepdims=True)
    y = x * lax.rsqrt(ms + EPS) * g
    return y * (1.0 + scale) + shift


def _silu(x):
    return x / (1.0 + jnp.exp(-x))


def _split_bf16(x):
    hi = x.astype(BF16)
    return hi, (x - hi.astype(F32)).astype(BF16)


def _split_cols(w):
    return jnp.concatenate(_split_bf16(w), axis=1)


def _dot_split(x_hi, x_lo, w_ref):
    n = w_ref.shape[1] // 2
    y = jnp.dot(x_hi, w_ref[...], preferred_element_type=F32)
    return y[:, :n] + y[:, n:] + jnp.dot(x_lo, w_ref[:, :n], preferred_element_type=F32)


_BF16_SUBLANES = 16


def _ride_along(arrays, n_steps, step_index):
    views, in_specs, out_specs, out_shapes = [], [], [], []
    for a in arrays:
        v = a.reshape(-1, a.shape[-1])
        rows, rem = divmod(v.shape[0], n_steps)
        assert rem == 0 and rows % _BF16_SUBLANES == 0, v.shape
        spec = pl.BlockSpec((rows, v.shape[1]), lambda *g: (step_index(*g), 0))
        views.append(v)
        in_specs.append(spec)
        out_specs.append(spec)
        out_shapes.append(jax.ShapeDtypeStruct(v.shape, BF16))
    return views, in_specs, out_specs, out_shapes


def _ride_along_cast(in_refs, out_refs):
    for src, dst in zip(in_refs, out_refs):
        dst[...] = src[...].astype(BF16)


def _ada_kernel(c_ref, w_ref, b_ref, o_ref):
    c_hi, c_lo = _split_bf16(_silu(c_ref[...]))
    w_hi, w_lo = _split_bf16(w_ref[0])
    o_ref[0] = (jnp.dot(c_hi, w_hi, preferred_element_type=F32)
                + jnp.dot(c_lo, w_hi, preferred_element_type=F32)
                + jnp.dot(c_hi, w_lo, preferred_element_type=F32) + b_ref[0])


def _ada_call(c_pad, w_ada, b_ada):
    depth, d, n = w_ada.shape
    tn = 1536
    return pl.pallas_call(
        _ada_kernel,
        out_shape=jax.ShapeDtypeStruct((depth, SUBLANES, n), F32),
        grid=(depth, n // tn),
        in_specs=[
            pl.BlockSpec((SUBLANES, d), lambda l, j: (0, 0)),
            pl.BlockSpec((1, d, tn), lambda l, j: (l, 0, j)),
            pl.BlockSpec((1, 1, tn), lambda l, j: (l, 0, j)),
        ],
        out_specs=pl.BlockSpec((1, SUBLANES, tn), lambda l, j: (l, 0, j)),
        compiler_params=_params(("parallel", "parallel")),
        name="ada_mod",
    )(c_pad, w_ada, b_ada.reshape(depth, 1, n))


_NT = (((1,), (1,)), ((), ()))


def _attn_proj_kernel(*refs, tiles_per_batch, with_gate, n_cast):
    x_ref, g_ref, sh_ref, sc_ref, wk_ref, wqt_ref, wvt_ref = refs[:7]
    n_in = 7 + with_gate + n_cast
    k_ref, qt_ref, vt_ref = refs[n_in:n_in + 3]
    b = pl.program_id(0) // tiles_per_batch
    h = _rms_mod(x_ref[...], g_ref[...], sh_ref[pl.ds(b, 1), :], sc_ref[pl.ds(b, 1), :])
    hb = h.astype(BF16)
    k_ref[...] = jnp.dot(hb, wk_ref[...], preferred_element_type=F32).astype(BF16)
    qt_ref[...] = lax.dot_general(wqt_ref[...], hb, _NT, preferred_element_type=F32).astype(BF16)
    vt_ref[...] = lax.dot_general(wvt_ref[...], hb, _NT, preferred_element_type=F32).astype(BF16)
    if with_gate:
        refs[n_in + 3][...] = _dot_split(hb, (h - hb.astype(F32)).astype(BF16), refs[7])
    _ride_along_cast(refs[n_in - n_cast:n_in], refs[len(refs) - n_cast:])


def _attn_proj_call(x, g, mod, w_k, w_qt, w_vt, seq, w_f=None, cast=(), tm=512):
    n, d = x.shape
    with_gate = w_f is not None
    full = lambda a: pl.BlockSpec(a.shape, lambda i: (0, 0))
    c_views, c_in, c_out, c_shapes = _ride_along(cast, n // tm, lambda i: i)
    in_specs = [
        pl.BlockSpec((tm, d), lambda i: (i, 0)),
        pl.BlockSpec((1, d), lambda i: (0, 0)),
        pl.BlockSpec((SUBLANES, d), lambda i: (0, 0)),
        pl.BlockSpec((SUBLANES, d), lambda i: (0, 1)),
        full(w_k), full(w_qt), full(w_vt),
    ]
    args = [x, g.reshape(1, d), mod, mod, w_k, w_qt, w_vt]
    out_shape = [jax.ShapeDtypeStruct((n, w_k.shape[1]), BF16),
                 jax.ShapeDtypeStruct((w_qt.shape[0], n), BF16),
                 jax.ShapeDtypeStruct((w_vt.shape[0], n), BF16)]
    out_specs = [pl.BlockSpec((tm, w_k.shape[1]), lambda i: (i, 0)),
                 pl.BlockSpec((w_qt.shape[0], tm), lambda i: (0, i)),
                 pl.BlockSpec((w_vt.shape[0], tm), lambda i: (0, i))]
    if with_gate:
        in_specs.append(full(w_f))
        args.append(w_f)
        out_shape.append(jax.ShapeDtypeStruct((n, w_f.shape[1] // 2), F32))
        out_specs.append(pl.BlockSpec((tm, w_f.shape[1] // 2), lambda i: (i, 0)))
    return pl.pallas_call(
        functools.partial(_attn_proj_kernel, tiles_per_batch=seq // tm, with_gate=with_gate,
                          n_cast=len(cast)),
        out_shape=out_shape + c_shapes,
        grid=(n // tm,),
        in_specs=in_specs + c_in,
        out_specs=out_specs + c_out,
        compiler_params=_params(("parallel",)),
        name="attn_proj_gate" if with_gate else "attn_proj",
    )(*args, *c_views)


_FOX_T = 512
LOG2E = 1.4426950408889634


def _cum_kernel(f_ref, bf_ref, g_ref, r_ref):
    x = f_ref[...] + bf_ref[...]
    logf = (jnp.minimum(x, 0.0) - jnp.log(1.0 + jnp.exp(-jnp.abs(x)))) * LOG2E
    seq, nh = logf.shape
    r = lax.broadcasted_iota(jnp.int32, (_FOX_T, _FOX_T), 0)
    c = lax.broadcasted_iota(jnp.int32, (_FOX_T, _FOX_T), 1)
    lower = jnp.where(c <= r, 1.0, 0.0).astype(BF16)
    hi = logf.astype(BF16).astype(F32)
    mid = (logf - hi).astype(BF16).astype(F32)
    parts = jnp.concatenate([hi, mid, logf - hi - mid], axis=1).astype(BF16)
    carry = jnp.zeros((1, nh), F32)
    for ch in range(seq // _FOX_T):
        rows = slice(ch * _FOX_T, (ch + 1) * _FOX_T)
        y = jnp.dot(lower, parts[rows, :], preferred_element_type=F32)
        cs = y[:, :nh] + y[:, nh:2 * nh] + y[:, 2 * nh:]
        g_ref[0, rows, :] = cs
        r_ref[0, ch:ch + 1, :] = carry
        carry = carry + cs[_FOX_T - 1:_FOX_T, :]


def _cum_call(f, b_f, batch, seq):
    nh = f.shape[1]
    return pl.pallas_call(
        _cum_kernel,
        out_shape=[jax.ShapeDtypeStruct((batch, seq, nh), F32),
                   jax.ShapeDtypeStruct((batch, seq // _FOX_T, nh), F32)],
        grid=(batch,),
        in_specs=[
            pl.BlockSpec((seq, nh), lambda b: (b, 0)),
            pl.BlockSpec((1, nh), lambda b: (0, 0)),
        ],
        out_specs=[pl.BlockSpec((1, seq, nh), lambda b: (b, 0, 0)),
                   pl.BlockSpec((1, seq // _FOX_T, nh), lambda b: (b, 0, 0))],
        compiler_params=_params(("parallel",)),
        name="forget_cumsum",
    )(f, b_f.reshape(1, nh))


_ONES_ROWS = 16


def _fox_kernel(*refs, n_cast):
    r_ref, qt_ref, k_ref, vt_ref, g_ref = refs[:5]
    o_ref = refs[5 + n_cast]
    ka_sc, m_sc, acc_sc, sa_sc, sb_sc, xa_sc, xb_sc = refs[6 + 2 * n_cast:]
    _ride_along_cast(refs[5:5 + n_cast], refs[6 + n_cast:6 + 2 * n_cast])
    t = _FOX_T
    b, hp = pl.program_id(0), pl.program_id(1)
    seq = k_ref.shape[1]
    n_blocks = seq // t
    n_heads = 2 * pl.num_programs(1)

    lane = lax.broadcasted_iota(jnp.int32, (1, LANES), 1)
    feat = lax.broadcasted_iota(jnp.int32, (LANES, 1), 0)
    aug0 = [HEAD_DIM, 0]

    g_all = g_ref[0]
    head_col = lax.broadcasted_iota(jnp.int32, g_all.shape, 1)
    for hh in range(2):
        g_head = jnp.sum(jnp.where(head_col == 2 * hp + hh, g_all, 0.0), axis=1, keepdims=True)
        gb = jnp.broadcast_to(g_head, (seq, LANES))
        hi = gb.astype(BF16).astype(F32)
        mid = (gb - hi).astype(BF16).astype(F32)
        lo = gb - hi - mid
        aug = jnp.where(lane == aug0[hh], hi,
                        jnp.where(lane == aug0[hh] + 1, mid,
                                  jnp.where(lane == aug0[hh] + 2, lo, 0.0)))
        own = (lane >= hh * HEAD_DIM) & (lane < (hh + 1) * HEAD_DIM)
        ka_sc[hh] = jnp.where(own, k_ref[0].astype(F32), aug).astype(BF16)

    feat_t = lax.broadcasted_iota(jnp.int32, (LANES, t), 0)
    r_base = [(b * n_heads + 2 * hp + hh) * n_blocks for hh in range(2)]
    bufs = ((sa_sc, xa_sc), (sb_sc, xb_sc))

    def queries(qi):
        qt2 = qt_ref[:, qi * t:(qi + 1) * t]
        return [jnp.where((feat >= hh * HEAD_DIM) & (feat < (hh + 1) * HEAD_DIM), qt2, 0)
                + jnp.where((feat_t >= aug0[hh]) & (feat_t < aug0[hh] + 3), -1.0, 0.0).astype(BF16)
                for hh in range(2)]

    half = t // 2
    lo, hi = slice(0, half), slice(half, t)
    below_diag = (lax.broadcasted_iota(jnp.int32, (half, half), 0)
                  <= lax.broadcasted_iota(jnp.int32, (half, half), 1))

    def scores(qh, qi, kb, dst):
        k0 = kb * t
        for hh in range(2):
            if kb == qi:
                dst[0][hh, lo, :] = jnp.dot(ka_sc[hh, k0:k0 + half, :], qh[hh],
                                            preferred_element_type=F32)
                dst[0][hh, hi, hi] = jnp.dot(ka_sc[hh, k0 + half:k0 + t, :], qh[hh][:, hi],
                                             preferred_element_type=F32)
            else:
                st = jnp.dot(ka_sc[hh, k0:k0 + t, :], qh[hh],
                             preferred_element_type=F32)
                dst[0][hh] = st
                dst[1][hh] = jnp.max(st, axis=0, keepdims=True)

    def values(hh, kb, keys):
        k0, k1 = kb * t + keys.start, kb * t + keys.stop
        return jnp.concatenate(
            [vt_ref[hh * HEAD_DIM:(hh + 1) * HEAD_DIM, k0:k1],
             jnp.ones((_ONES_ROWS, k1 - k0), BF16)], axis=0)

    def softmax_pv(qi, kb, cur):
        for hh in range(2):
            c = r_ref[r_base[hh] + kb] - r_ref[r_base[hh] + qi]
            m_old = m_sc[hh]
            if kb == qi:
                s_ll = jnp.where(below_diag, cur[0][hh, lo, lo], -jnp.inf)
                s_lh = cur[0][hh, lo, hi]
                s_hh = jnp.where(below_diag, cur[0][hh, hi, hi], -jnp.inf)
                st_max = jnp.concatenate(
                    [jnp.max(s_ll, axis=0, keepdims=True),
                     jnp.maximum(jnp.max(s_lh, axis=0, keepdims=True),
                                 jnp.max(s_hh, axis=0, keepdims=True))], axis=1)
                m_new = jnp.maximum(m_old, st_max - c)
                alpha = jnp.exp2(m_old - m_new)
                shift = m_new + c
                p_ll = jnp.exp2(s_ll - shift[:, lo]).astype(BF16)
                p_lh = jnp.exp2(s_lh - shift[:, hi]).astype(BF16)
                p_hh = jnp.exp2(s_hh - shift[:, hi]).astype(BF16)
                v_lo, v_hi = values(hh, kb, lo), values(hh, kb, hi)
                pv = jnp.concatenate(
                    [jnp.dot(v_lo, p_ll, preferred_element_type=F32),
                     jnp.dot(v_lo, p_lh, preferred_element_type=F32)
                     + jnp.dot(v_hi, p_hh, preferred_element_type=F32)], axis=1)
            else:
                st = cur[0][hh]
                m_new = jnp.maximum(m_old, cur[1][hh] - c)
                alpha = jnp.exp2(m_old - m_new)
                p = jnp.exp2(st - (m_new + c)).astype(BF16)
                pv = jnp.dot(values(hh, kb, slice(0, t)), p, preferred_element_type=F32)
            acc_sc[hh] = alpha * acc_sc[hh] + pv
            m_sc[hh] = m_new

    pairs = [(qi, kb) for qi in range(n_blocks) for kb in range(qi + 1)]
    qh = queries(0)
    scores(qh, 0, 0, bufs[0])
    for s, (qi, kb) in enumerate(pairs):
        if kb == 0:
            m_sc[...] = jnp.full_like(m_sc, -jnp.inf)
            acc_sc[...] = jnp.zeros_like(acc_sc)
        if s + 1 < len(pairs):
            qi_n, kb_n = pairs[s + 1]
            if qi_n != qi:
                qh = queries(qi_n)
            scores(qh, qi_n, kb_n, bufs[(s + 1) % 2])
        softmax_pv(qi, kb, bufs[s % 2])
        if kb == qi:
            outs = [acc_sc[hh, :HEAD_DIM, :] / acc_sc[hh, HEAD_DIM:HEAD_DIM + 1, :]
                    for hh in range(2)]
            o_ref[0, qi * t:(qi + 1) * t, :] = jnp.concatenate(outs, axis=0).T.astype(o_ref.dtype)


def _fox_call(r_flat, q_t, k, v_t, g_cum, cast=()):
    batch, seq, d = k.shape
    n_pairs = d // LANES
    t = _FOX_T
    c_views, c_in, c_out, c_shapes = _ride_along(
        cast, batch * n_pairs, lambda b, h: b * n_pairs + h)
    return pl.pallas_call(
        functools.partial(_fox_kernel, n_cast=len(cast)),
        out_shape=[jax.ShapeDtypeStruct((batch, seq, d), BF16)] + c_shapes,
        grid=(batch, n_pairs),
        in_specs=[
            pl.BlockSpec(memory_space=pltpu.SMEM),
            pl.BlockSpec((LANES, seq), lambda b, h: (h, b)),
            pl.BlockSpec((1, seq, LANES), lambda b, h: (b, 0, h)),
            pl.BlockSpec((LANES, seq), lambda b, h: (h, b)),
            pl.BlockSpec((1, seq, g_cum.shape[2]), lambda b, h: (b, 0, 0)),
        ] + c_in,
        out_specs=[pl.BlockSpec((1, seq, LANES), lambda b, h: (b, 0, h))] + c_out,
        scratch_shapes=[pltpu.VMEM((2, seq, LANES), BF16), pltpu.VMEM((2, 1, t), F32),
                        pltpu.VMEM((2, HEAD_DIM + _ONES_ROWS, t), F32),
                        pltpu.VMEM((2, t, t), F32), pltpu.VMEM((2, t, t), F32),
                        pltpu.VMEM((2, 1, t), F32), pltpu.VMEM((2, 1, t), F32)],
        compiler_params=_params(("parallel", "parallel")),
        name="fox_attention",
    )(r_flat, q_t, k, v_t, g_cum, *c_views)


def _out_kernel(o_ref, w_ref, x_ref, gt_ref, g_ref, sh_ref, sc_ref, wr_ref, br_ref,
                xo_ref, h_ref, route_ref, route_t_ref, cnt_ref, *, tiles_per_batch):
    b = pl.program_id(0) // tiles_per_batch
    y = jnp.dot(o_ref[...], w_ref[...], preferred_element_type=F32)
    xn = x_ref[...] + gt_ref[pl.ds(b, 1), :] * y
    xo_ref[...] = xn
    h = _rms_mod(xn, g_ref[...], sh_ref[pl.ds(b, 1), :], sc_ref[pl.ds(b, 1), :])
    h_ref[...] = h
    logits = _dot_split(*_split_bf16(h), wr_ref) + br_ref[...]
    tm = logits.shape[0]
    lane = lax.broadcasted_iota(jnp.int32, logits.shape, 1).astype(F32)
    logits = jnp.where(lane < N_EXPERTS, logits, -jnp.inf)
    m1 = jnp.max(logits, axis=1, keepdims=True)
    i1 = jnp.min(jnp.where(logits == m1, lane, float(LANES)), axis=1, keepdims=True)
    rest = jnp.where(lane == i1, -jnp.inf, logits)
    m2 = jnp.max(rest, axis=1, keepdims=True)
    i2 = jnp.min(jnp.where(rest == m2, lane, float(LANES)), axis=1, keepdims=True)
    e2 = jnp.exp(m2 - m1)
    den = 1.0 + e2

    @pl.when(pl.program_id(0) == 0)
    def _():
        cnt_ref[...] = jnp.zeros_like(cnt_ref)

    sel1 = lane == i1
    sel2 = lane == i2
    onehot = jnp.where(sel1 | sel2, 1.0, 0.0)
    r = lax.broadcasted_iota(jnp.int32, (tm, tm), 0)
    c = lax.broadcasted_iota(jnp.int32, (tm, tm), 1)
    lower = jnp.where(c < r, 1.0, 0.0).astype(BF16)
    before = jnp.dot(lower, onehot.astype(BF16), preferred_element_type=F32) + cnt_ref[0:1, :]
    rank1 = jnp.sum(jnp.where(sel1, before, 0.0), axis=1, keepdims=True)
    rank2 = jnp.sum(jnp.where(sel2, before, 0.0), axis=1, keepdims=True)
    cnt_ref[0:1, :] = cnt_ref[0:1, :] + jnp.sum(onehot, axis=0, keepdims=True)
    route = jnp.where(lane == 0.0, i1, 0.0)
    for k, val in enumerate((i2, 1.0 / den, e2 / den, rank1, rank2), start=1):
        route = jnp.where(lane == float(k), val, route)
    route_ref[...] = route
    route_t_ref[...] = route.T[:SUBLANES, :]


def _out_call(o, w_out, x, mod, g, w_r, b_r, seq, tm=512):
    n, d = x.shape
    row = lambda w: pl.BlockSpec((tm, w), lambda i: (i, 0))
    col = lambda c: pl.BlockSpec((SUBLANES, d), lambda i: (0, c))
    return pl.pallas_call(
        functools.partial(_out_kernel, tiles_per_batch=seq // tm),
        out_shape=[jax.ShapeDtypeStruct((n, d), F32), jax.ShapeDtypeStruct((n, d), F32),
                   jax.ShapeDtypeStruct((n, LANES), F32), jax.ShapeDtypeStruct((SUBLANES, n), F32),
                   jax.ShapeDtypeStruct((SUBLANES, LANES), F32)],
        grid=(n // tm,),
        in_specs=[
            row(d),
            pl.BlockSpec((d, d), lambda i: (0, 0)),
            row(d),
            col(2),
            pl.BlockSpec((1, d), lambda i: (0, 0)),
            col(3),
            col(4),
            pl.BlockSpec((d, 2 * LANES), lambda i: (0, 0)),
            pl.BlockSpec((1, LANES), lambda i: (0, 0)),
        ],
        out_specs=[row(d), row(d), row(LANES), pl.BlockSpec((SUBLANES, tm), lambda i: (0, i)),
                   pl.BlockSpec((SUBLANES, LANES), lambda i: (0, 0))],
        compiler_params=_params(("arbitrary",)),
        name="out_proj_router",
    )(o, w_out, x, mod, g.reshape(1, d), mod, mod, w_r, b_r)


_FF_TILE = 1792


def _snake(i, j, nj):
    return jnp.where(i % 2 == 0, j, nj - 1 - j)


def _ffn_kernel(*refs, tiles_per_batch, n_cast):
    (x_ref, attn_ref, wo_ref, gta_ref, g_ref, sh_ref, sc_ref, gt_ref,
     wg_ref, wu_ref, wd_ref) = refs[:11]
    o_ref = refs[11 + n_cast]
    h_sc = refs[-1]
    _ride_along_cast(refs[11:11 + n_cast], refs[12 + n_cast:12 + 2 * n_cast])
    b = pl.program_id(0) // tiles_per_batch

    @pl.when(pl.program_id(1) == 0)
    def _():
        x = x_ref[...] + gta_ref[pl.ds(b, 1), :] * jnp.dot(
            attn_ref[...], wo_ref[...], preferred_element_type=F32)
        h = _rms_mod(x, g_ref[...], sh_ref[pl.ds(b, 1), :], sc_ref[pl.ds(b, 1), :])
        h_sc[...] = h.astype(BF16)
        o_ref[...] = x

    h = h_sc[...]
    g = jnp.dot(h, wg_ref[...], preferred_element_type=F32)
    u = jnp.dot(h, wu_ref[...], preferred_element_type=F32)
    a = (_silu(g) * u).astype(BF16)
    o_ref[...] += gt_ref[pl.ds(b, 1), :] * jnp.dot(a, wd_ref[...], preferred_element_type=F32)


def _ffn_call(x, attn, w_out, g, mod, w_gu, w_down, seq, cast=(), tm=512, tf=_FF_TILE):
    n, d = x.shape
    f = w_down.shape[0]
    nj = f // tf
    c_views, c_in, c_out, c_shapes = _ride_along(cast, (n // tm) * nj, lambda i, j: i * nj + j)
    return pl.pallas_call(
        functools.partial(_ffn_kernel, tiles_per_batch=seq // tm, n_cast=len(cast)),
        out_shape=[jax.ShapeDtypeStruct((n, d), F32)] + c_shapes,
        grid=(n // tm, nj),
        in_specs=[
            pl.BlockSpec((tm, d), lambda i, j: (i, 0)),
            pl.BlockSpec((tm, d), lambda i, j: (i, 0)),
            pl.BlockSpec((d, d), lambda i, j: (0, 0)),
            pl.BlockSpec((SUBLANES, d), lambda i, j: (0, 2)),
            pl.BlockSpec((1, d), lambda i, j: (0, 0)),
            pl.BlockSpec((SUBLANES, d), lambda i, j: (0, 3)),
            pl.BlockSpec((SUBLANES, d), lambda i, j: (0, 4)),
            pl.BlockSpec((SUBLANES, d), lambda i, j: (0, 5)),
            pl.BlockSpec((d, tf), lambda i, j: (0, _snake(i, j, nj))),
            pl.BlockSpec((d, tf), lambda i, j: (0, _snake(i, j, nj) + nj)),
            pl.BlockSpec((tf, d), lambda i, j: (_snake(i, j, nj), 0)),
        ] + c_in,
        out_specs=[pl.BlockSpec((tm, d), lambda i, j: (i, 0))] + c_out,
        scratch_shapes=[pltpu.VMEM((tm, d), BF16)],
        compiler_params=_params(("parallel", "arbitrary")),
        name="ffn_swiglu",
    )(x, attn, w_out, mod, g.reshape(1, d), mod, mod, mod, w_gu, w_gu, w_down, *c_views)


_MOE_TM = 512


def _dispatch_kernel(d1_ref, d2_ref, zero_ref, h_ref, xs_ref, z_sc, sem):
    tm = h_ref.shape[0]
    base = pl.program_id(0) * tm

    @pl.when(pl.program_id(0) == 0)
    def _():
        z_sc[...] = jnp.zeros_like(z_sc)

        def zero_copy(t):
            return pltpu.make_async_copy(z_sc, xs_ref.at[pl.ds(pl.multiple_of(t * tm, tm), tm)],
                                         sem.at[0])

        def start(t, carry):
            @pl.when(zero_ref[t] != 0)
            def _():
                zero_copy(t).start()
            return carry

        def wait(t, carry):
            @pl.when(zero_ref[t] != 0)
            def _():
                zero_copy(t).wait()
            return carry

        lax.fori_loop(0, zero_ref.shape[0], start, 0)
        lax.fori_loop(0, zero_ref.shape[0], wait, 0)

    def issue(r, carry):
        src = h_ref.at[pl.ds(r, 1)]
        pltpu.make_async_copy(src, xs_ref.at[pl.ds(d1_ref[base + r], 1)], sem.at[0]).start()
        pltpu.make_async_copy(src, xs_ref.at[pl.ds(d2_ref[base + r], 1)],
                              sem.at[1]).start(priority=1)
        return carry

    lax.fori_loop(0, tm, issue, 0, unroll=8)
    pltpu.make_async_copy(h_ref, xs_ref.at[pl.ds(0, tm)], sem.at[0]).wait()
    pltpu.make_async_copy(h_ref, xs_ref.at[pl.ds(0, tm)], sem.at[1]).wait()


def _dispatch_call(dest1, dest2, zero_tile, h, tm=_MOE_TM):
    n, d = h.shape
    n_rows = zero_tile.shape[0] * tm
    return pl.pallas_call(
        _dispatch_kernel,
        out_shape=jax.ShapeDtypeStruct((n_rows, d), h.dtype),
        grid_spec=pltpu.PrefetchScalarGridSpec(
            num_scalar_prefetch=3,
            grid=(n // tm,),
            in_specs=[pl.BlockSpec((tm, d), lambda i, d1, d2, zt: (i, 0))],
            out_specs=pl.BlockSpec(memory_space=pl.ANY),
            scratch_shapes=[pltpu.VMEM((tm, d), h.dtype), pltpu.SemaphoreType.DMA((2,))],
        ),
        compiler_params=_params(("arbitrary",)),
        name="moe_dispatch",
    )(dest1, dest2, zero_tile, h)


def _experts_kernel(te_ref, nt_ref, rows_ref, xs_ref, wg_ref, wu_ref, wd_ref, ye_ref):
    del te_ref, nt_ref
    tm = xs_ref.shape[0]
    rows = rows_ref[pl.program_id(0)]

    @pl.when(pl.program_id(1) == 0)
    def _():
        ye_ref[...] = jnp.zeros_like(ye_ref)

    def swiglu(n):
        h = xs_ref[:n, :].astype(BF16)
        g = jnp.dot(h, wg_ref[0], preferred_element_type=F32)
        u = jnp.dot(h, wu_ref[0], preferred_element_type=F32)
        a = (_silu(g) * u).astype(BF16)
        ye_ref[:n, :] += jnp.dot(a, wd_ref[0], preferred_element_type=F32)

    @pl.when(rows > tm // 2)
    def _():
        swiglu(tm)

    @pl.when((rows > 0) & (rows <= tm // 2))
    def _():
        swiglu(tm // 2)


def _experts_call(tile_expert, n_tiles, tile_rows, xs, w_gu, w_down, tf=_FF_TILE):
    n_rows, d = xs.shape
    _, f, _ = w_down.shape
    nj = f // tf
    tm = _MOE_TM
    tile = lambda t, nt: jnp.maximum(jnp.minimum(t, nt[0] - 1), 0)
    jj = lambda t, j, nt: _snake(tile(t, nt), jnp.where(t < nt[0], j, nj - 1), nj)
    return pl.pallas_call(
        _experts_kernel,
        out_shape=jax.ShapeDtypeStruct((n_rows, d), F32),
        grid_spec=pltpu.PrefetchScalarGridSpec(
            num_scalar_prefetch=3,
            grid=(n_rows // tm, nj),
            in_specs=[
                pl.BlockSpec((tm, d), lambda t, j, te, nt, tr: (tile(t, nt), 0)),
                pl.BlockSpec((1, d, tf), lambda t, j, te, nt, tr: (te[tile(t, nt)], 0, jj(t, j, nt))),
                pl.BlockSpec((1, d, tf),
                             lambda t, j, te, nt, tr: (te[tile(t, nt)], 0, jj(t, j, nt) + nj)),
                pl.BlockSpec((1, tf, d), lambda t, j, te, nt, tr: (te[tile(t, nt)], jj(t, j, nt), 0)),
            ],
            out_specs=pl.BlockSpec((tm, d), lambda t, j, te, nt, tr: (t, 0)),
        ),
        compiler_params=_params(("arbitrary", "arbitrary")),
        name="moe_experts",
    )(tile_expert, n_tiles, tile_rows, xs, w_gu, w_gu, w_down)


def _combine_kernel(d1_ref, d2_ref, ye_ref, x_ref, route_ref, gt_ref, gf_ref, o_ref,
                    y1_sc, y2_sc, sem, *, tiles_per_batch):
    tm = x_ref.shape[0]
    i = pl.program_id(0)
    b = i // tiles_per_batch
    slot = i % 2

    def gather(tile, dst_slot):
        base = tile * tm

        def issue(r, carry):
            pltpu.make_async_copy(ye_ref.at[pl.ds(d1_ref[base + r], 1)],
                                  y1_sc.at[dst_slot, pl.ds(r, 1)], sem.at[0, dst_slot]).start()
            pltpu.make_async_copy(ye_ref.at[pl.ds(d2_ref[base + r], 1)],
                                  y2_sc.at[dst_slot, pl.ds(r, 1)],
                                  sem.at[1, dst_slot]).start(priority=1)
            return carry

        lax.fori_loop(0, tm, issue, 0, unroll=8)

    @pl.when(i == 0)
    def _():
        gather(0, 0)

    @pl.when(i + 1 < pl.num_programs(0))
    def _():
        gather(i + 1, 1 - slot)

    pltpu.make_async_copy(ye_ref.at[pl.ds(0, tm)], y1_sc.at[slot], sem.at[0, slot]).wait()
    pltpu.make_async_copy(ye_ref.at[pl.ds(0, tm)], y2_sc.at[slot], sem.at[1, slot]).wait()
    route = route_ref[...]
    y = route[:, 2:3] * y1_sc[slot] + route[:, 3:4] * y2_sc[slot]
    xn = x_ref[...] + gt_ref[pl.ds(b, 1), :] * y
    ms = jnp.mean(xn * xn, axis=-1, keepdims=True)
    o_ref[...] = xn * lax.rsqrt(ms + EPS) * gf_ref[...]


def _combine_call(dest1, dest2, ye, x, route, mod, g_final, seq, tm=512):
    n, d = x.shape
    return pl.pallas_call(
        functools.partial(_combine_kernel, tiles_per_batch=seq // tm),
        out_shape=jax.ShapeDtypeStruct((n, d), F32),
        grid_spec=pltpu.PrefetchScalarGridSpec(
            num_scalar_prefetch=2,
            grid=(n // tm,),
            in_specs=[
                pl.BlockSpec(memory_space=pl.ANY),
                pl.BlockSpec((tm, d), lambda i, d1, d2: (i, 0)),
                pl.BlockSpec((tm, LANES), lambda i, d1, d2: (i, 0)),
                pl.BlockSpec((SUBLANES, d), lambda i, d1, d2: (0, 5)),
                pl.BlockSpec((1, d), lambda i, d1, d2: (0, 0)),
            ],
            out_specs=pl.BlockSpec((tm, d), lambda i, d1, d2: (i, 0)),
            scratch_shapes=[pltpu.VMEM((2, tm, d), F32), pltpu.VMEM((2, tm, d), F32),
                            pltpu.SemaphoreType.DMA((2, 2))],
        ),
        compiler_params=_params(("arbitrary",)),
        name="moe_combine",
    )(dest1, dest2, ye, x, route, mod, g_final.reshape(1, d))


def _moe_call(h, route, route_t, counts, x, mod, g_final, w_gu, w_down, seq):
    n, d = x.shape
    ne = w_down.shape[0]
    tm = _MOE_TM
    max_tiles = (2 * n) // tm + ne
    e1, e2, _, _, rank1, rank2 = (route_t[k].astype(jnp.int32) for k in range(6))
    cnt = counts[0, :ne].astype(jnp.int32)
    tiles_e = (cnt + tm - 1) // tm
    tile_end = jnp.cumsum(tiles_e)
    row_start = (tile_end - tiles_e) * tm
    dest1 = row_start[e1] + rank1
    dest2 = row_start[e2] + rank2
    n_tiles = tile_end[-1:]
    tile_ids = jnp.arange(max_tiles, dtype=jnp.int32)
    tile_expert = jnp.minimum(
        jnp.sum((tile_ids[:, None] >= tile_end[None, :]).astype(jnp.int32), axis=1), ne - 1)
    is_last = jnp.any((tile_ids[:, None] == tile_end[None, :] - 1) & (tiles_e[None, :] > 0), axis=1)
    zero_tile = (is_last | (tile_ids >= n_tiles[0])).astype(jnp.int32)
    xs = _dispatch_call(dest1, dest2, zero_tile, h)
    local = tile_ids[:, None] - (tile_end - tiles_e)[None, :]
    in_expert = (local >= 0) & (local < tiles_e[None, :])
    tile_rows = jnp.sum(jnp.where(in_expert, jnp.clip(cnt[None, :] - local * tm, 0, tm), 0),
                        axis=1).astype(jnp.int32)
    ye = _experts_call(tile_expert, n_tiles, tile_rows, xs, w_gu, w_down)
    return _combine_call(dest1, dest2, ye, x, route, mod, g_final, seq)


_SWA_TQ = 2 * CHUNK
_SWA_BAND = 2 * _SWA_TQ
_SWA_SUB = 4


def _swa_bucket_tiles():
    cc = np.arange(_SWA_BAND)[:, None]
    r = np.arange(_SWA_TQ)[None, :]
    rel = cc - _SWA_TQ - r
    nb = REL_BUCKETS // 2
    max_exact = nb // 2
    ret = (rel > 0).astype(np.int32) * nb
    n = np.abs(rel)
    large = max_exact + (np.log(np.maximum(n, 1) / max_exact)
                         / np.log(REL_MAX_DIST / max_exact) * (nb - max_exact)).astype(np.int32)
    large = np.minimum(large, nb - 1)
    bucket = (ret + np.where(n < max_exact, n, large)).astype(np.int32)
    q_chunk = r // CHUNK
    k_chunk = cc // CHUNK
    visible = (k_chunk >= q_chunk) & (k_chunk <= q_chunk + WINDOW_CHUNKS)
    later = np.where(visible, bucket, -1)
    first = np.where(cc >= _SWA_TQ, later, -1)
    return np.stack([first, later]).astype(np.int32)


def _swa_bias_kernel(tbl_ref, bkt_ref, o_ref):
    n_heads = o_ref.shape[1]
    for v in range(2):
        bkt = bkt_ref[v]
        for head in range(n_heads):
            tile = jnp.full(bkt.shape, NEG_BIG, F32)
            for bk in range(REL_BUCKETS):
                tile = jnp.where(bkt == bk, tbl_ref[head, bk] * LOG2E, tile)
            o_ref[v, head] = tile


def _swa_bias_call(rel_bias):
    bkt = jnp.asarray(_swa_bucket_tiles())
    n_heads = rel_bias.shape[1]
    return pl.pallas_call(
        _swa_bias_kernel,
        out_shape=jax.ShapeDtypeStruct((2, n_heads, _SWA_BAND, _SWA_TQ), F32),
        in_specs=[
            pl.BlockSpec(memory_space=pltpu.SMEM),
            pl.BlockSpec(memory_space=pltpu.VMEM),
        ],
        out_specs=pl.BlockSpec(memory_space=pltpu.VMEM),
        name="swa_bias",
    )(rel_bias.T, bkt)


def _swa_kernel(qt_ref, kp_ref, kc_ref, vtp_ref, vtc_ref, bias_ref, sink_ref, o_ref):
    tq = _SWA_TQ
    first_block = jnp.minimum(pl.program_id(1), 1)
    lane = lax.broadcasted_iota(jnp.int32, (1, LANES), 1)
    ones = jnp.ones((_ONES_ROWS, _SWA_BAND), BF16)
    units = [(sub, hk, par) for sub in range(_SWA_SUB)
             for hk in range(SWA_KV_HEADS) for par in range(2)]

    def band_keys(sub, ksl):
        if sub == 0:
            return jnp.concatenate([kp_ref[0, :, ksl], kc_ref[0, :tq, ksl]], axis=0)
        return kc_ref[0, (sub - 1) * tq:(sub + 1) * tq, ksl]

    def band_values(sub, vsl):
        if sub == 0:
            return jnp.concatenate([vtp_ref[vsl, :], vtc_ref[vsl, :tq]], axis=1)
        return vtc_ref[vsl, (sub - 1) * tq:(sub + 1) * tq]

    def scores(sub, hk, par):
        kb = band_keys(sub, slice(hk * LANES, (hk + 1) * LANES))
        f0 = hk * SWA_GROUP * HEAD_DIM
        qs = slice(sub * tq, (sub + 1) * tq)
        wq = jnp.concatenate([qt_ref[f0:f0 + LANES, qs], qt_ref[f0 + LANES:f0 + 2 * LANES, qs]],
                             axis=1)
        head_lanes = (lane < HEAD_DIM) if par == 0 else (lane >= HEAD_DIM)
        return jnp.dot(jnp.where(head_lanes, kb, 0), wq, preferred_element_type=F32)

    sts = [scores(*u) for u in units]
    outs = {}
    for (sub, hk, par), st in zip(units, sts):
        vt1 = jnp.concatenate(
            [band_values(sub, slice(hk * HEAD_DIM, (hk + 1) * HEAD_DIM)), ones], axis=0)
        variant = first_block if sub == 0 else 1
        heads = (hk * SWA_GROUP + par, hk * SWA_GROUP + par + 2)
        ps, ms = [], []
        for i, head in enumerate(heads):
            s = st[:, i * tq:(i + 1) * tq] + bias_ref[variant, head]
            m = jnp.maximum(jnp.max(s, axis=0, keepdims=True), sink_ref[head])
            ps.append(jnp.exp2(s - m).astype(BF16))
            ms.append(m)
        acc = jnp.dot(vt1, jnp.concatenate(ps, axis=1), preferred_element_type=F32)
        for i, head in enumerate(heads):
            a = acc[:, i * tq:(i + 1) * tq]
            den = a[HEAD_DIM:HEAD_DIM + 1] + jnp.exp2(sink_ref[head] - ms[i])
            outs[sub, head] = a[:HEAD_DIM] / den
    n_heads = SWA_KV_HEADS * SWA_GROUP
    for sub in range(_SWA_SUB):
        o_t = jnp.concatenate([outs[sub, head] for head in range(n_heads)], axis=0)
        o_ref[0, sub * tq:(sub + 1) * tq, :] = o_t.T.astype(o_ref.dtype)


def _swa_call(q_t, k, v_t, bias, sink, batch, seq):
    d = q_t.shape[0]
    kw = k.shape[2]
    vw = v_t.shape[0]
    tq = _SWA_TQ
    ts = _SWA_SUB * tq
    ns = seq // ts
    prev = lambda i: jnp.maximum(_SWA_SUB * i - 1, 0)
    return pl.pallas_call(
        _swa_kernel,
        out_shape=jax.ShapeDtypeStruct((batch, seq, d), BF16),
        grid=(batch, ns),
        in_specs=[
            pl.BlockSpec((d, ts), lambda b, i: (0, b * ns + i)),
            pl.BlockSpec((1, tq, kw), lambda b, i: (b, prev(i), 0)),
            pl.BlockSpec((1, ts, kw), lambda b, i: (b, i, 0)),
            pl.BlockSpec((vw, tq), lambda b, i: (0, b * (seq // tq) + prev(i))),
            pl.BlockSpec((vw, ts), lambda b, i: (0, b * ns + i)),
            pl.BlockSpec(bias.shape, lambda b, i: (0, 0, 0, 0)),
            pl.BlockSpec(sink.shape, lambda b, i: (0, 0, 0)),
        ],
        out_specs=pl.BlockSpec((1, ts, d), lambda b, i: (b, i, 0)),
        compiler_params=_params(("parallel", "arbitrary")),
        name="swa_attention",
    )(q_t, k, k, v_t, v_t, bias, sink)


def kernel(x, c, w_ada, b_ada, g_norm_mix, g_norm_ffn, g_final, fox_w_in, fox_b_f, fox_w_out, swa_w_in, swa_sinks, swa_w_out, rel_bias, ffn_w_gu, ffn_w_down, moe_w_router, moe_b_router, moe_w_gu, moe_w_down):
    batch, seq, d = x.shape
    n = batch * seq
    q_scale = HEAD_DIM ** -0.5
    xf = x.reshape(n, d)

    c_pad = jnp.zeros((SUBLANES, d), F32).at[:batch].set(c)
    mod = _ada_call(c_pad, w_ada, b_ada)
    mod0, mod1 = mod[0], mod[1]

    w_in = fox_w_in[0]
    n_heads = d // HEAD_DIM
    k, q_t, v_t, f, w_gu0, w_down0, w_out0, w_out1 = _attn_proj_call(
        xf, g_norm_mix[0], mod0, w_in[:, d:2 * d].astype(BF16),
        (w_in[:, :d] * (q_scale * LOG2E)).T.astype(BF16), w_in[:, 2 * d:3 * d].T.astype(BF16),
        seq, w_f=_split_cols(w_in[:, 3 * d:]),
        cast=(ffn_w_gu[0], ffn_w_down[0], fox_w_out[0], swa_w_out[0]))
    g_cum, r_cum = _cum_call(f, fox_b_f[0], batch, seq)
    r_flat = r_cum.transpose(0, 2, 1).reshape(-1)
    o, w_down1 = _fox_call(r_flat, q_t, k.reshape(batch, seq, d), v_t, g_cum,
                           cast=(moe_w_down[0],))
    w_down1 = w_down1.reshape(moe_w_down.shape[1:])
    x2, w_gu1 = _ffn_call(xf, o.reshape(n, d), w_out0, g_norm_ffn[0], mod0, w_gu0, w_down0, seq,
                          cast=(moe_w_gu[0],))
    w_gu1 = w_gu1.reshape(moe_w_gu.shape[1:])

    w_in = swa_w_in[0]
    kvw = SWA_KV_HEADS * HEAD_DIM
    dup = lambda w: jnp.repeat(w.reshape(d, SWA_KV_HEADS, 1, HEAD_DIM), 2, axis=2).reshape(d, 2 * kvw)
    k, q_t, v_t = _attn_proj_call(
        x2, g_norm_mix[1], mod1, dup(w_in[:, d:d + kvw]).astype(BF16),
        (w_in[:, :d] * (q_scale * LOG2E)).T.astype(BF16), w_in[:, d + kvw:].T.astype(BF16), seq)
    bias = _swa_bias_call(rel_bias)
    sink = jnp.broadcast_to((swa_sinks[0] * LOG2E)[:, None, None], (n_heads, 1, _SWA_TQ))
    o = _swa_call(q_t, k.reshape(batch, seq, 2 * kvw), v_t, bias, sink, batch, seq)
    w_r = _split_cols(jnp.zeros((d, LANES), F32).at[:, :N_EXPERTS].set(moe_w_router[0]))
    b_r = jnp.zeros((1, LANES), F32).at[0, :N_EXPERTS].set(moe_b_router[0])
    x3, h4, route, route_t, counts = _out_call(o.reshape(n, d), w_out1, x2, mod1, g_norm_ffn[1],
                                               w_r, b_r, seq)
    out = _moe_call(h4, route, route_t, counts, x3, mod1, g_final, w_gu1, w_down1, seq)
    return out.reshape(batch, seq, d)
```

```python
import functools

import numpy as np
import jax
import jax.numpy as jnp
from jax import lax
from jax.experimental import pallas as pl
from jax.experimental.pallas import tpu as pltpu

F32 = jnp.float32
BF16 = jnp.bfloat16

HEAD_DIM = 64
CHUNK = 64
WINDOW_CHUNKS = 2
REL_BUCKETS = 32
REL_MAX_DIST = 128
SWA_KV_HEADS = 4
SWA_GROUP = 4
N_EXPERTS = 8
EPS = 1e-6

LANES = 128
SUBLANES = 8
VMEM_LIMIT = 56 * 1024 * 1024

NEG_BIG = -1e30


def _params(sem, vmem=VMEM_LIMIT):
    return pltpu.CompilerParams(dimension_semantics=sem, vmem_limit_bytes=vmem)


def _rms_mod(x, g, shift, scale):
    ms = jnp.mean(x * x, axis=-1, keepdims=True)
    y = x * lax.rsqrt(ms + EPS) * g
    return y * (1.0 + scale) + shift


def _silu(x):
    return x / (1.0 + jnp.exp(-x))


def _split_bf16(x):
    hi = x.astype(BF16)
    return hi, (x - hi.astype(F32)).astype(BF16)


def _split_cols(w):
    return jnp.concatenate(_split_bf16(w), axis=1)


def _dot_split(x_hi, x_lo, w_ref):
    n = w_ref.shape[1] // 2
    y = jnp.dot(x_hi, w_ref[...], preferred_element_type=F32)
    return y[:, :n] + y[:, n:] + jnp.dot(x_lo, w_ref[:, :n], preferred_element_type=F32)


_BF16_SUBLANES = 16


def _ride_along(arrays, n_steps, step_index):
    views, in_specs, out_specs, out_shapes = [], [], [], []
    for a in arrays:
        v = a.reshape(-1, a.shape[-1])
        rows, rem = divmod(v.shape[0], n_steps)
        assert rem == 0 and rows % _BF16_SUBLANES == 0, v.shape
        spec = pl.BlockSpec((rows, v.shape[1]), lambda *g: (step_index(*g), 0))
        views.append(v)
        in_specs.append(spec)
        out_specs.append(spec)
        out_shapes.append(jax.ShapeDtypeStruct(v.shape, BF16))
    return views, in_specs, out_specs, out_shapes


def _ride_along_cast(in_refs, out_refs):
    for src, dst in zip(in_refs, out_refs):
        dst[...] = src[...].astype(BF16)


def _ada_kernel(c_ref, w_ref, b_ref, o_ref):
    c_hi, c_lo = _split_bf16(_silu(c_ref[...]))
    w_hi, w_lo = _split_bf16(w_ref[0])
    o_ref[0] = (jnp.dot(c_hi, w_hi, preferred_element_type=F32)
                + jnp.dot(c_lo, w_hi, preferred_element_type=F32)
                + jnp.dot(c_hi, w_lo, preferred_element_type=F32) + b_ref[0])


def _ada_call(c_pad, w_ada, b_ada):
    depth, d, n = w_ada.shape
    tn = 1536
    return pl.pallas_call(
        _ada_kernel,
        out_shape=jax.ShapeDtypeStruct((depth, SUBLANES, n), F32),
        grid=(depth, n // tn),
        in_specs=[
            pl.BlockSpec((SUBLANES, d), lambda l, j: (0, 0)),
            pl.BlockSpec((1, d, tn), lambda l, j: (l, 0, j)),
            pl.BlockSpec((1, 1, tn), lambda l, j: (l, 0, j)),
        ],
        out_specs=pl.BlockSpec((1, SUBLANES, tn), lambda l, j: (l, 0, j)),
        compiler_params=_params(("parallel", "parallel")),
        name="ada_mod",
    )(c_pad, w_ada, b_ada.reshape(depth, 1, n))


_NT = (((1,), (1,)), ((), ()))


def _attn_proj_kernel(*refs, tiles_per_batch, with_gate, n_cast):
    x_ref, g_ref, sh_ref, sc_ref, wk_ref, wqt_ref, wvt_ref = refs[:7]
    n_in = 7 + with_gate + n_cast
    k_ref, qt_ref, vt_ref = refs[n_in:n_in + 3]
    b = pl.program_id(0) // tiles_per_batch
    h = _rms_mod(x_ref[...], g_ref[...], sh_ref[pl.ds(b, 1), :], sc_ref[pl.ds(b, 1), :])
    hb = h.astype(BF16)
    k_ref[...] = jnp.dot(hb, wk_ref[...], preferred_element_type=F32).astype(BF16)
    qt_ref[...] = lax.dot_general(wqt_ref[...], hb, _NT, preferred_element_type=F32).astype(BF16)
    vt_ref[...] = lax.dot_general(wvt_ref[...], hb, _NT, preferred_element_type=F32).astype(BF16)
    if with_gate:
        refs[n_in + 3][...] = _dot_split(hb, (h - hb.astype(F32)).astype(BF16), refs[7])
    _ride_along_cast(refs[n_in - n_cast:n_in], refs[len(refs) - n_cast:])


def _attn_proj_call(x, g, mod, w_k, w_qt, w_vt, seq, w_f=None, cast=(), tm=512):
    n, d = x.shape
    with_gate = w_f is not None
    full = lambda a: pl.BlockSpec(a.shape, lambda i: (0, 0))
    c_views, c_in, c_out, c_shapes = _ride_along(cast, n // tm, lambda i: i)
    in_specs = [
        pl.BlockSpec((tm, d), lambda i: (i, 0)),
        pl.BlockSpec((1, d), lambda i: (0, 0)),
        pl.BlockSpec((SUBLANES, d), lambda i: (0, 0)),
        pl.BlockSpec((SUBLANES, d), lambda i: (0, 1)),
        full(w_k), full(w_qt), full(w_vt),
    ]
    args = [x, g.reshape(1, d), mod, mod, w_k, w_qt, w_vt]
    out_shape = [jax.ShapeDtypeStruct((n, w_k.shape[1]), BF16),
                 jax.ShapeDtypeStruct((w_qt.shape[0], n), BF16),
                 jax.ShapeDtypeStruct((w_vt.shape[0], n), BF16)]
    out_specs = [pl.BlockSpec((tm, w_k.shape[1]), lambda i: (i, 0)),
                 pl.BlockSpec((w_qt.shape[0], tm), lambda i: (0, i)),
                 pl.BlockSpec((w_vt.shape[0], tm), lambda i: (0, i))]
    if with_gate:
        in_specs.append(full(w_f))
        args.append(w_f)
        out_shape.append(jax.ShapeDtypeStruct((n, w_f.shape[1] // 2), F32))
        out_specs.append(pl.BlockSpec((tm, w_f.shape[1] // 2), lambda i: (i, 0)))
    return pl.pallas_call(
        functools.partial(_attn_proj_kernel, tiles_per_batch=seq // tm, with_gate=with_gate,
                          n_cast=len(cast)),
        out_shape=out_shape + c_shapes,
        grid=(n // tm,),
        in_specs=in_specs + c_in,
        out_specs=out_specs + c_out,
        compiler_params=_params(("parallel",)),
        name="attn_proj_gate" if with_gate else "attn_proj",
    )(*args, *c_views)


_FOX_T = 512
LOG2E = 1.4426950408889634


def _cum_kernel(f_ref, bf_ref, g_ref, r_ref):
    x = f_ref[...] + bf_ref[...]
    logf = (jnp.minimum(x, 0.0) - jnp.log(1.0 + jnp.exp(-jnp.abs(x)))) * LOG2E
    seq, nh = logf.shape
    r = lax.broadcasted_iota(jnp.int32, (_FOX_T, _FOX_T), 0)
    c = lax.broadcasted_iota(jnp.int32, (_FOX_T, _FOX_T), 1)
    lower = jnp.where(c <= r, 1.0, 0.0).astype(BF16)
    hi = logf.astype(BF16).astype(F32)
    mid = (logf - hi).astype(BF16).astype(F32)
    parts = jnp.concatenate([hi, mid, logf - hi - mid], axis=1).astype(BF16)
    carry = jnp.zeros((1, nh), F32)
    for ch in range(seq // _FOX_T):
        rows = slice(ch * _FOX_T, (ch + 1) * _FOX_T)
        y = jnp.dot(lower, parts[rows, :], preferred_element_type=F32)
        cs = y[:, :nh] + y[:, nh:2 * nh] + y[:, 2 * nh:]
        g_ref[0, rows, :] = cs
        r_ref[0, ch:ch + 1, :] = carry
        carry = carry + cs[_FOX_T - 1:_FOX_T, :]


def _cum_call(f, b_f, batch, seq):
    nh = f.shape[1]
    return pl.pallas_call(
        _cum_kernel,
        out_shape=[jax.ShapeDtypeStruct((batch, seq, nh), F32),
                   jax.ShapeDtypeStruct((batch, seq // _FOX_T, nh), F32)],
        grid=(batch,),
        in_specs=[
            pl.BlockSpec((seq, nh), lambda b: (b, 0)),
            pl.BlockSpec((1, nh), lambda b: (0, 0)),
        ],
        out_specs=[pl.BlockSpec((1, seq, nh), lambda b: (b, 0, 0)),
                   pl.BlockSpec((1, seq // _FOX_T, nh), lambda b: (b, 0, 0))],
        compiler_params=_params(("parallel",)),
        name="forget_cumsum",
    )(f, b_f.reshape(1, nh))


_ONES_ROWS = 16


def _fox_kernel(*refs, n_cast):
    r_ref, qt_ref, k_ref, vt_ref, g_ref = refs[:5]
    o_ref = refs[5 + n_cast]
    ka_sc, m_sc, acc_sc, sa_sc, sb_sc, xa_sc, xb_sc = refs[6 + 2 * n_cast:]
    _ride_along_cast(refs[5:5 + n_cast], refs[6 + n_cast:6 + 2 * n_cast])
    t = _FOX_T
    b, hp = pl.program_id(0), pl.program_id(1)
    seq = k_ref.shape[1]
    n_blocks = seq // t
    n_heads = 2 * pl.num_programs(1)

    lane = lax.broadcasted_iota(jnp.int32, (1, LANES), 1)
    feat = lax.broadcasted_iota(jnp.int32, (LANES, 1), 0)
    aug0 = [HEAD_DIM, 0]

    g_all = g_ref[0]
    head_col = lax.broadcasted_iota(jnp.int32, g_all.shape, 1)
    g_head = [jnp.sum(jnp.where(head_col == 2 * hp + hh, g_all, 0.0), axis=1, keepdims=True)
              for hh in range(2)]
    gb = jnp.where(lane >= HEAD_DIM, jnp.broadcast_to(g_head[0], (seq, LANES)),
                   jnp.broadcast_to(g_head[1], (seq, LANES)))
    hi = gb.astype(BF16).astype(F32)
    mid = (gb - hi).astype(BF16).astype(F32)
    lo = gb - hi - mid
    in_half = lane % HEAD_DIM
    aug = jnp.where(in_half == 0, hi, jnp.where(in_half == 1, mid,
                                                jnp.where(in_half == 2, lo, 0.0)))
    kf = k_ref[0].astype(F32)
    for hh in range(2):
        own = (lane >= hh * HEAD_DIM) & (lane < (hh + 1) * HEAD_DIM)
        ka_sc[hh] = jnp.where(own, kf, aug).astype(BF16)

    feat_t = lax.broadcasted_iota(jnp.int32, (LANES, t), 0)
    r_base = [(b * n_heads + 2 * hp + hh) * n_blocks for hh in range(2)]
    bufs = ((sa_sc, xa_sc), (sb_sc, xb_sc))

    def queries(qi):
        qt2 = qt_ref[:, qi * t:(qi + 1) * t]
        return [jnp.where((feat >= hh * HEAD_DIM) & (feat < (hh + 1) * HEAD_DIM), qt2, 0)
                + jnp.where((feat_t >= aug0[hh]) & (feat_t < aug0[hh] + 3), -1.0, 0.0).astype(BF16)
                for hh in range(2)]

    half = t // 2
    lo, hi = slice(0, half), slice(half, t)
    below_diag = (lax.broadcasted_iota(jnp.int32, (half, half), 0)
                  <= lax.broadcasted_iota(jnp.int32, (half, half), 1))

    def scores(qh, qi, kb, dst):
        k0 = kb * t
        for hh in range(2):
            if kb == qi:
                dst[0][hh, lo, :] = jnp.dot(ka_sc[hh, k0:k0 + half, :], qh[hh],
                                            preferred_element_type=F32)
                dst[0][hh, hi, hi] = jnp.dot(ka_sc[hh, k0 + half:k0 + t, :], qh[hh][:, hi],
                                             preferred_element_type=F32)
            else:
                st = jnp.dot(ka_sc[hh, k0:k0 + t, :], qh[hh],
                             preferred_element_type=F32)
                dst[0][hh] = st
                dst[1][hh] = jnp.max(st, axis=0, keepdims=True)

    def values(hh, kb, keys):
        k0, k1 = kb * t + keys.start, kb * t + keys.stop
        return jnp.concatenate(
            [vt_ref[hh * HEAD_DIM:(hh + 1) * HEAD_DIM, k0:k1],
             jnp.ones((_ONES_ROWS, k1 - k0), BF16)], axis=0)

    def softmax_pv(qi, kb, cur):
        for hh in range(2):
            c = r_ref[r_base[hh] + kb] - r_ref[r_base[hh] + qi]
            m_old = m_sc[hh]
            if kb == qi:
                s_ll = jnp.where(below_diag, cur[0][hh, lo, lo], -jnp.inf)
                s_lh = cur[0][hh, lo, hi]
                s_hh = jnp.where(below_diag, cur[0][hh, hi, hi], -jnp.inf)
                st_max = jnp.concatenate(
                    [jnp.max(s_ll, axis=0, keepdims=True),
                     jnp.maximum(jnp.max(s_lh, axis=0, keepdims=True),
                                 jnp.max(s_hh, axis=0, keepdims=True))], axis=1)
                m_new = jnp.maximum(m_old, st_max - c)
                alpha = jnp.exp2(m_old - m_new)
                shift = m_new + c
                p_ll = jnp.exp2(s_ll - shift[:, lo]).astype(BF16)
                p_lh = jnp.exp2(s_lh - shift[:, hi]).astype(BF16)
                p_hh = jnp.exp2(s_hh - shift[:, hi]).astype(BF16)
                v_lo, v_hi = values(hh, kb, lo), values(hh, kb, hi)
                pv = jnp.concatenate(
                    [jnp.dot(v_lo, p_ll, preferred_element_type=F32),
                     jnp.dot(v_lo, p_lh, preferred_element_type=F32)
                     + jnp.dot(v_hi, p_hh, preferred_element_type=F32)], axis=1)
            else:
                st = cur[0][hh]
                m_new = jnp.maximum(m_old, cur[1][hh] - c)
                alpha = jnp.exp2(m_old - m_new)
                p = jnp.exp2(st - (m_new + c)).astype(BF16)
                pv = jnp.dot(values(hh, kb, slice(0, t)), p, preferred_element_type=F32)
            acc_sc[hh] = alpha * acc_sc[hh] + pv
            m_sc[hh] = m_new

    pairs = [(qi, kb) for qi in range(n_blocks) for kb in range(qi + 1)]
    qh = queries(0)
    scores(qh, 0, 0, bufs[0])
    for s, (qi, kb) in enumerate(pairs):
        if kb == 0:
            m_sc[...] = jnp.full_like(m_sc, -jnp.inf)
            acc_sc[...] = jnp.zeros_like(acc_sc)
        if s + 1 < len(pairs):
            qi_n, kb_n = pairs[s + 1]
            if qi_n != qi:
                qh = queries(qi_n)
            scores(qh, qi_n, kb_n, bufs[(s + 1) % 2])
        softmax_pv(qi, kb, bufs[s % 2])
        if kb == qi:
            outs = [acc_sc[hh, :HEAD_DIM, :] / acc_sc[hh, HEAD_DIM:HEAD_DIM + 1, :]
                    for hh in range(2)]
            o_ref[0, qi * t:(qi + 1) * t, :] = jnp.concatenate(outs, axis=0).T.astype(o_ref.dtype)


def _fox_call(r_flat, q_t, k, v_t, g_cum, cast=()):
    batch, seq, d = k.shape
    n_pairs = d // LANES
    t = _FOX_T
    c_views, c_in, c_out, c_shapes = _ride_along(
        cast, batch * n_pairs, lambda b, h: b * n_pairs + h)
    return pl.pallas_call(
        functools.partial(_fox_kernel, n_cast=len(cast)),
        out_shape=[jax.ShapeDtypeStruct((batch, seq, d), BF16)] + c_shapes,
        grid=(batch, n_pairs),
        in_specs=[
            pl.BlockSpec(memory_space=pltpu.SMEM),
            pl.BlockSpec((LANES, seq), lambda b, h: (h, b)),
            pl.BlockSpec((1, seq, LANES), lambda b, h: (b, 0, h)),
            pl.BlockSpec((LANES, seq), lambda b, h: (h, b)),
            pl.BlockSpec((1, seq, g_cum.shape[2]), lambda b, h: (b, 0, 0)),
        ] + c_in,
        out_specs=[pl.BlockSpec((1, seq, LANES), lambda b, h: (b, 0, h))] + c_out,
        scratch_shapes=[pltpu.VMEM((2, seq, LANES), BF16), pltpu.VMEM((2, 1, t), F32),
                        pltpu.VMEM((2, HEAD_DIM + _ONES_ROWS, t), F32),
                        pltpu.VMEM((2, t, t), F32), pltpu.VMEM((2, t, t), F32),
                        pltpu.VMEM((2, 1, t), F32), pltpu.VMEM((2, 1, t), F32)],
        compiler_params=_params(("parallel", "parallel")),
        name="fox_attention",
    )(r_flat, q_t, k, v_t, g_cum, *c_views)


def _out_kernel(o_ref, w_ref, x_ref, gt_ref, g_ref, sh_ref, sc_ref, wr_ref, br_ref,
                xo_ref, h_ref, route_ref, route_t_ref, cnt_ref, *, tiles_per_batch):
    b = pl.program_id(0) // tiles_per_batch
    y = jnp.dot(o_ref[...], w_ref[...], preferred_element_type=F32)
    xn = x_ref[...] + gt_ref[pl.ds(b, 1), :] * y
    xo_ref[...] = xn
    h = _rms_mod(xn, g_ref[...], sh_ref[pl.ds(b, 1), :], sc_ref[pl.ds(b, 1), :])
    h_ref[...] = h
    logits = _dot_split(*_split_bf16(h), wr_ref) + br_ref[...]
    tm = logits.shape[0]
    lane = lax.broadcasted_iota(jnp.int32, logits.shape, 1).astype(F32)
    logits = jnp.where(lane < N_EXPERTS, logits, -jnp.inf)
    m1 = jnp.max(logits, axis=1, keepdims=True)
    i1 = jnp.min(jnp.where(logits == m1, lane, float(LANES)), axis=1, keepdims=True)
    rest = jnp.where(lane == i1, -jnp.inf, logits)
    m2 = jnp.max(rest, axis=1, keepdims=True)
    i2 = jnp.min(jnp.where(rest == m2, lane, float(LANES)), axis=1, keepdims=True)
    e2 = jnp.exp(m2 - m1)
    den = 1.0 + e2

    @pl.when(pl.program_id(0) == 0)
    def _():
        cnt_ref[...] = jnp.zeros_like(cnt_ref)

    sel1 = lane == i1
    sel2 = lane == i2
    onehot = jnp.where(sel1 | sel2, 1.0, 0.0)
    r = lax.broadcasted_iota(jnp.int32, (tm, tm), 0)
    c = lax.broadcasted_iota(jnp.int32, (tm, tm), 1)
    lower = jnp.where(c < r, 1.0, 0.0).astype(BF16)
    before = jnp.dot(lower, onehot.astype(BF16), preferred_element_type=F32) + cnt_ref[0:1, :]
    rank1 = jnp.sum(jnp.where(sel1, before, 0.0), axis=1, keepdims=True)
    rank2 = jnp.sum(jnp.where(sel2, before, 0.0), axis=1, keepdims=True)
    cnt_ref[0:1, :] = cnt_ref[0:1, :] + jnp.sum(onehot, axis=0, keepdims=True)
    route = jnp.where(lane == 0.0, i1, 0.0)
    for k, val in enumerate((i2, 1.0 / den, e2 / den, rank1, rank2), start=1):
        route = jnp.where(lane == float(k), val, route)
    route_ref[...] = route
    route_t_ref[...] = route.T[:SUBLANES, :]


def _out_call(o, w_out, x, mod, g, w_r, b_r, seq, tm=512):
    n, d = x.shape
    row = lambda w: pl.BlockSpec((tm, w), lambda i: (i, 0))
    col = lambda c: pl.BlockSpec((SUBLANES, d), lambda i: (0, c))
    return pl.pallas_call(
        functools.partial(_out_kernel, tiles_per_batch=seq // tm),
        out_shape=[jax.ShapeDtypeStruct((n, d), F32), jax.ShapeDtypeStruct((n, d), F32),
                   jax.ShapeDtypeStruct((n, LANES), F32), jax.ShapeDtypeStruct((SUBLANES, n), F32),
                   jax.ShapeDtypeStruct((SUBLANES, LANES), F32)],
        grid=(n // tm,),
        in_specs=[
            row(d),
            pl.BlockSpec((d, d), lambda i: (0, 0)),
            row(d),
            col(2),
            pl.BlockSpec((1, d), lambda i: (0, 0)),
            col(3),
            col(4),
            pl.BlockSpec((d, 2 * LANES), lambda i: (0, 0)),
            pl.BlockSpec((1, LANES), lambda i: (0, 0)),
        ],
        out_specs=[row(d), row(d), row(LANES), pl.BlockSpec((SUBLANES, tm), lambda i: (0, i)),
                   pl.BlockSpec((SUBLANES, LANES), lambda i: (0, 0))],
        compiler_params=_params(("arbitrary",)),
        name="out_proj_router",
    )(o, w_out, x, mod, g.reshape(1, d), mod, mod, w_r, b_r)


_FF_TILE = 1792


def _snake(i, j, nj):
    return jnp.where(i % 2 == 0, j, nj - 1 - j)


def _ffn_kernel(*refs, tiles_per_batch, n_cast):
    (x_ref, attn_ref, wo_ref, gta_ref, g_ref, sh_ref, sc_ref, gt_ref,
     wg_ref, wu_ref, wd_ref) = refs[:11]
    o_ref = refs[11 + n_cast]
    h_sc = refs[-1]
    _ride_along_cast(refs[11:11 + n_cast], refs[12 + n_cast:12 + 2 * n_cast])
    b = pl.program_id(0) // tiles_per_batch

    @pl.when(pl.program_id(1) == 0)
    def _():
        x = x_ref[...] + gta_ref[pl.ds(b, 1), :] * jnp.dot(
            attn_ref[...], wo_ref[...], preferred_element_type=F32)
        h = _rms_mod(x, g_ref[...], sh_ref[pl.ds(b, 1), :], sc_ref[pl.ds(b, 1), :])
        h_sc[...] = h.astype(BF16)
        o_ref[...] = x

    h = h_sc[...]
    g = jnp.dot(h, wg_ref[...], preferred_element_type=F32)
    u = jnp.dot(h, wu_ref[...], preferred_element_type=F32)
    a = (_silu(g) * u).astype(BF16)
    o_ref[...] += gt_ref[pl.ds(b, 1), :] * jnp.dot(a, wd_ref[...], preferred_element_type=F32)


def _ffn_call(x, attn, w_out, g, mod, w_gu, w_down, seq, cast=(), tm=512, tf=_FF_TILE):
    n, d = x.shape
    f = w_down.shape[0]
    nj = f // tf
    c_views, c_in, c_out, c_shapes = _ride_along(cast, (n // tm) * nj, lambda i, j: i * nj + j)
    return pl.pallas_call(
        functools.partial(_ffn_kernel, tiles_per_batch=seq // tm, n_cast=len(cast)),
        out_shape=[jax.ShapeDtypeStruct((n, d), F32)] + c_shapes,
        grid=(n // tm, nj),
        in_specs=[
            pl.BlockSpec((tm, d), lambda i, j: (i, 0)),
            pl.BlockSpec((tm, d), lambda i, j: (i, 0)),
            pl.BlockSpec((d, d), lambda i, j: (0, 0)),
            pl.BlockSpec((SUBLANES, d), lambda i, j: (0, 2)),
            pl.BlockSpec((1, d), lambda i, j: (0, 0)),
            pl.BlockSpec((SUBLANES, d), lambda i, j: (0, 3)),
            pl.BlockSpec((SUBLANES, d), lambda i, j: (0, 4)),
            pl.BlockSpec((SUBLANES, d), lambda i, j: (0, 5)),
            pl.BlockSpec((d, tf), lambda i, j: (0, _snake(i, j, nj))),
            pl.BlockSpec((d, tf), lambda i, j: (0, _snake(i, j, nj) + nj)),
            pl.BlockSpec((tf, d), lambda i, j: (_snake(i, j, nj), 0)),
        ] + c_in,
        out_specs=[pl.BlockSpec((tm, d), lambda i, j: (i, 0))] + c_out,
        scratch_shapes=[pltpu.VMEM((tm, d), BF16)],
        compiler_params=_params(("parallel", "arbitrary")),
        name="ffn_swiglu",
    )(x, attn, w_out, mod, g.reshape(1, d), mod, mod, mod, w_gu, w_gu, w_down, *c_views)


_MOE_TM = 512


def _dispatch_kernel(d1_ref, d2_ref, zero_ref, h_ref, xs_ref, z_sc, sem):
    tm = h_ref.shape[0]
    base = pl.program_id(0) * tm

    @pl.when(pl.program_id(0) == 0)
    def _():
        z_sc[...] = jnp.zeros_like(z_sc)

        def zero_copy(t):
            return pltpu.make_async_copy(z_sc, xs_ref.at[pl.ds(pl.multiple_of(t * tm, tm), tm)],
                                         sem.at[0])

        def start(t, carry):
            @pl.when(zero_ref[t] != 0)
            def _():
                zero_copy(t).start()
            return carry

        def wait(t, carry):
            @pl.when(zero_ref[t] != 0)
            def _():
                zero_copy(t).wait()
            return carry

        lax.fori_loop(0, zero_ref.shape[0], start, 0)
        lax.fori_loop(0, zero_ref.shape[0], wait, 0)

    def issue(r, carry):
        src = h_ref.at[pl.ds(r, 1)]
        pltpu.make_async_copy(src, xs_ref.at[pl.ds(d1_ref[base + r], 1)], sem.at[0]).start()
        pltpu.make_async_copy(src, xs_ref.at[pl.ds(d2_ref[base + r], 1)],
                              sem.at[1]).start(priority=1)
        return carry

    lax.fori_loop(0, tm, issue, 0, unroll=8)
    pltpu.make_async_copy(h_ref, xs_ref.at[pl.ds(0, tm)], sem.at[0]).wait()
    pltpu.make_async_copy(h_ref, xs_ref.at[pl.ds(0, tm)], sem.at[1]).wait()


def _dispatch_call(dest1, dest2, zero_tile, h, tm=_MOE_TM):
    n, d = h.shape
    n_rows = zero_tile.shape[0] * tm
    return pl.pallas_call(
        _dispatch_kernel,
        out_shape=jax.ShapeDtypeStruct((n_rows, d), h.dtype),
        grid_spec=pltpu.PrefetchScalarGridSpec(
            num_scalar_prefetch=3,
            grid=(n // tm,),
            in_specs=[pl.BlockSpec((tm, d), lambda i, d1, d2, zt: (i, 0))],
            out_specs=pl.BlockSpec(memory_space=pl.ANY),
            scratch_shapes=[pltpu.VMEM((tm, d), h.dtype), pltpu.SemaphoreType.DMA((2,))],
        ),
        compiler_params=_params(("arbitrary",)),
        name="moe_dispatch",
    )(dest1, dest2, zero_tile, h)


def _experts_kernel(te_ref, nt_ref, rows_ref, xs_ref, wg_ref, wu_ref, wd_ref, ye_ref):
    del te_ref, nt_ref
    tm = xs_ref.shape[0]
    rows = rows_ref[pl.program_id(0)]

    @pl.when(pl.program_id(1) == 0)
    def _():
        ye_ref[...] = jnp.zeros_like(ye_ref)

    def swiglu(n):
        h = xs_ref[:n, :].astype(BF16)
        g = jnp.dot(h, wg_ref[0], preferred_element_type=F32)
        u = jnp.dot(h, wu_ref[0], preferred_element_type=F32)
        a = (_silu(g) * u).astype(BF16)
        ye_ref[:n, :] += jnp.dot(a, wd_ref[0], preferred_element_type=F32)

    @pl.when(rows > tm // 2)
    def _():
        swiglu(tm)

    @pl.when((rows > 0) & (rows <= tm // 2))
    def _():
        swiglu(tm // 2)


def _experts_call(tile_expert, n_tiles, tile_rows, xs, w_gu, w_down, tf=_FF_TILE):
    n_rows, d = xs.shape
    _, f, _ = w_down.shape
    nj = f // tf
    tm = _MOE_TM
    tile = lambda t, nt: jnp.maximum(jnp.minimum(t, nt[0] - 1), 0)
    jj = lambda t, j, nt: _snake(tile(t, nt), jnp.where(t < nt[0], j, nj - 1), nj)
    return pl.pallas_call(
        _experts_kernel,
        out_shape=jax.ShapeDtypeStruct((n_rows, d), F32),
        grid_spec=pltpu.PrefetchScalarGridSpec(
            num_scalar_prefetch=3,
            grid=(n_rows // tm, nj),
            in_specs=[
                pl.BlockSpec((tm, d), lambda t, j, te, nt, tr: (tile(t, nt), 0)),
                pl.BlockSpec((1, d, tf), lambda t, j, te, nt, tr: (te[tile(t, nt)], 0, jj(t, j, nt))),
                pl.BlockSpec((1, d, tf),
                             lambda t, j, te, nt, tr: (te[tile(t, nt)], 0, jj(t, j, nt) + nj)),
                pl.BlockSpec((1, tf, d), lambda t, j, te, nt, tr: (te[tile(t, nt)], jj(t, j, nt), 0)),
            ],
            out_specs=pl.BlockSpec((tm, d), lambda t, j, te, nt, tr: (t, 0)),
        ),
        compiler_params=_params(("arbitrary", "arbitrary")),
        name="moe_experts",
    )(tile_expert, n_tiles, tile_rows, xs, w_gu, w_gu, w_down)


def _combine_kernel(d1_ref, d2_ref, ye_ref, x_ref, route_ref, gt_ref, gf_ref, o_ref,
                    y1_sc, y2_sc, sem, *, tiles_per_batch):
    tm = x_ref.shape[0]
    i = pl.program_id(0)
    b = i // tiles_per_batch
    slot = i % 2

    def gather(tile, dst_slot):
        base = tile * tm

        def issue(r, carry):
            pltpu.make_async_copy(ye_ref.at[pl.ds(d1_ref[base + r], 1)],
                                  y1_sc.at[dst_slot, pl.ds(r, 1)], sem.at[0, dst_slot]).start()
            pltpu.make_async_copy(ye_ref.at[pl.ds(d2_ref[base + r], 1)],
                                  y2_sc.at[dst_slot, pl.ds(r, 1)],
                                  sem.at[1, dst_slot]).start(priority=1)
            return carry

        lax.fori_loop(0, tm, issue, 0, unroll=8)

    @pl.when(i == 0)
    def _():
        gather(0, 0)

    @pl.when(i + 1 < pl.num_programs(0))
    def _():
        gather(i + 1, 1 - slot)

    pltpu.make_async_copy(ye_ref.at[pl.ds(0, tm)], y1_sc.at[slot], sem.at[0, slot]).wait()
    pltpu.make_async_copy(ye_ref.at[pl.ds(0, tm)], y2_sc.at[slot], sem.at[1, slot]).wait()
    route = route_ref[...]
    y = route[:, 2:3] * y1_sc[slot] + route[:, 3:4] * y2_sc[slot]
    xn = x_ref[...] + gt_ref[pl.ds(b, 1), :] * y
    ms = jnp.mean(xn * xn, axis=-1, keepdims=True)
    o_ref[...] = xn * lax.rsqrt(ms + EPS) * gf_ref[...]


def _combine_call(dest1, dest2, ye, x, route, mod, g_final, seq, tm=512):
    n, d = x.shape
    return pl.pallas_call(
        functools.partial(_combine_kernel, tiles_per_batch=seq // tm),
        out_shape=jax.ShapeDtypeStruct((n, d), F32),
        grid_spec=pltpu.PrefetchScalarGridSpec(
            num_scalar_prefetch=2,
            grid=(n // tm,),
            in_specs=[
                pl.BlockSpec(memory_space=pl.ANY),
                pl.BlockSpec((tm, d), lambda i, d1, d2: (i, 0)),
                pl.BlockSpec((tm, LANES), lambda i, d1, d2: (i, 0)),
                pl.BlockSpec((SUBLANES, d), lambda i, d1, d2: (0, 5)),
                pl.BlockSpec((1, d), lambda i, d1, d2: (0, 0)),
            ],
            out_specs=pl.BlockSpec((tm, d), lambda i, d1, d2: (i, 0)),
            scratch_shapes=[pltpu.VMEM((2, tm, d), F32), pltpu.VMEM((2, tm, d), F32),
                            pltpu.SemaphoreType.DMA((2, 2))],
        ),
        compiler_params=_params(("arbitrary",)),
        name="moe_combine",
    )(dest1, dest2, ye, x, route, mod, g_final.reshape(1, d))


def _moe_call(h, route, route_t, counts, x, mod, g_final, w_gu, w_down, seq):
    n, d = x.shape
    ne = w_down.shape[0]
    tm = _MOE_TM
    max_tiles = (2 * n) // tm + ne
    e1, e2, _, _, rank1, rank2 = (route_t[k].astype(jnp.int32) for k in range(6))
    cnt = counts[0, :ne].astype(jnp.int32)
    tiles_e = (cnt + tm - 1) // tm
    tile_end = jnp.cumsum(tiles_e)
    row_start = (tile_end - tiles_e) * tm
    dest1 = row_start[e1] + rank1
    dest2 = row_start[e2] + rank2
    n_tiles = tile_end[-1:]
    tile_ids = jnp.arange(max_tiles, dtype=jnp.int32)
    tile_expert = jnp.minimum(
        jnp.sum((tile_ids[:, None] >= tile_end[None, :]).astype(jnp.int32), axis=1), ne - 1)
    is_last = jnp.any((tile_ids[:, None] == tile_end[None, :] - 1) & (tiles_e[None, :] > 0), axis=1)
    zero_tile = (is_last | (tile_ids >= n_tiles[0])).astype(jnp.int32)
    xs = _dispatch_call(dest1, dest2, zero_tile, h)
    local = tile_ids[:, None] - (tile_end - tiles_e)[None, :]
    in_expert = (local >= 0) & (local < tiles_e[None, :])
    tile_rows = jnp.sum(jnp.where(in_expert, jnp.clip(cnt[None, :] - local * tm, 0, tm), 0),
                        axis=1).astype(jnp.int32)
    ye = _experts_call(tile_expert, n_tiles, tile_rows, xs, w_gu, w_down)
    return _combine_call(dest1, dest2, ye, x, route, mod, g_final, seq)


_SWA_TQ = 2 * CHUNK
_SWA_BAND = 2 * _SWA_TQ
_SWA_SUB = 4


def _swa_bucket_tiles():
    cc = np.arange(_SWA_BAND)[:, None]
    r = np.arange(_SWA_TQ)[None, :]
    rel = cc - _SWA_TQ - r
    nb = REL_BUCKETS // 2
    max_exact = nb // 2
    ret = (rel > 0).astype(np.int32) * nb
    n = np.abs(rel)
    large = max_exact + (np.log(np.maximum(n, 1) / max_exact)
                         / np.log(REL_MAX_DIST / max_exact) * (nb - max_exact)).astype(np.int32)
    large = np.minimum(large, nb - 1)
    bucket = (ret + np.where(n < max_exact, n, large)).astype(np.int32)
    q_chunk = r // CHUNK
    k_chunk = cc // CHUNK
    visible = (k_chunk >= q_chunk) & (k_chunk <= q_chunk + WINDOW_CHUNKS)
    later = np.where(visible, bucket, -1)
    first = np.where(cc >= _SWA_TQ, later, -1)
    return np.stack([first, later]).astype(np.int32)


def _swa_bias_kernel(tbl_ref, bkt_ref, o_ref):
    n_heads = o_ref.shape[1]
    for v in range(2):
        bkt = bkt_ref[v]
        for head in range(n_heads):
            tile = jnp.full(bkt.shape, NEG_BIG, F32)
            for bk in range(REL_BUCKETS):
                tile = jnp.where(bkt == bk, tbl_ref[head, bk] * LOG2E, tile)
            o_ref[v, head] = tile


def _swa_bias_call(rel_bias):
    bkt = jnp.asarray(_swa_bucket_tiles())
    n_heads = rel_bias.shape[1]
    return pl.pallas_call(
        _swa_bias_kernel,
        out_shape=jax.ShapeDtypeStruct((2, n_heads, _SWA_BAND, _SWA_TQ), F32),
        in_specs=[
            pl.BlockSpec(memory_space=pltpu.SMEM),
            pl.BlockSpec(memory_space=pltpu.VMEM),
        ],
        out_specs=pl.BlockSpec(memory_space=pltpu.VMEM),
        name="swa_bias",
    )(rel_bias.T, bkt)


def _swa_kernel(qt_ref, kp_ref, kc_ref, vtp_ref, vtc_ref, bias_ref, sink_ref, o_ref):
    tq = _SWA_TQ
    first_block = jnp.minimum(pl.program_id(1), 1)
    lane = lax.broadcasted_iota(jnp.int32, (1, LANES), 1)
    ones = jnp.ones((_ONES_ROWS, _SWA_BAND), BF16)
    units = [(sub, hk, par) for sub in range(_SWA_SUB)
             for hk in range(SWA_KV_HEADS) for par in range(2)]

    def band_keys(sub, ksl):
        if sub == 0:
            return jnp.concatenate([kp_ref[0, :, ksl], kc_ref[0, :tq, ksl]], axis=0)
        return kc_ref[0, (sub - 1) * tq:(sub + 1) * tq, ksl]

    def band_values(sub, vsl):
        if sub == 0:
            return jnp.concatenate([vtp_ref[vsl, :], vtc_ref[vsl, :tq]], axis=1)
        return vtc_ref[vsl, (sub - 1) * tq:(sub + 1) * tq]

    def scores(sub, hk, par):
        kb = band_keys(sub, slice(hk * LANES, (hk + 1) * LANES))
        f0 = hk * SWA_GROUP * HEAD_DIM
        qs = slice(sub * tq, (sub + 1) * tq)
        wq = jnp.concatenate([qt_ref[f0:f0 + LANES, qs], qt_ref[f0 + LANES:f0 + 2 * LANES, qs]],
                             axis=1)
        head_lanes = (lane < HEAD_DIM) if par == 0 else (lane >= HEAD_DIM)
        return jnp.dot(jnp.where(head_lanes, kb, 0), wq, preferred_element_type=F32)

    sts = [scores(*u) for u in units]
    outs = {}
    for (sub, hk, par), st in zip(units, sts):
        vt1 = jnp.concatenate(
            [band_values(sub, slice(hk * HEAD_DIM, (hk + 1) * HEAD_DIM)), ones], axis=0)
        variant = first_block if sub == 0 else 1
        heads = (hk * SWA_GROUP + par, hk * SWA_GROUP + par + 2)
        ps, ms = [], []
        for i, head in enumerate(heads):
            s = st[:, i * tq:(i + 1) * tq] + bias_ref[variant, head]
            m = jnp.maximum(jnp.max(s, axis=0, keepdims=True), sink_ref[head])
            ps.append(jnp.exp2(s - m).astype(BF16))
            ms.append(m)
        acc = jnp.dot(vt1, jnp.concatenate(ps, axis=1), preferred_element_type=F32)
        for i, head in enumerate(heads):
            a = acc[:, i * tq:(i + 1) * tq]
            den = a[HEAD_DIM:HEAD_DIM + 1] + jnp.exp2(sink_ref[head] - ms[i])
            outs[sub, head] = a[:HEAD_DIM] / den
    n_heads = SWA_KV_HEADS * SWA_GROUP
    for sub in range(_SWA_SUB):
        o_t = jnp.concatenate([outs[sub, head] for head in range(n_heads)], axis=0)
        o_ref[0, sub * tq:(sub + 1) * tq, :] = o_t.T.astype(o_ref.dtype)


def _swa_call(q_t, k, v_t, bias, sink, batch, seq):
    d = q_t.shape[0]
    kw = k.shape[2]
    vw = v_t.shape[0]
    tq = _SWA_TQ
    ts = _SWA_SUB * tq
    ns = seq // ts
    prev = lambda i: jnp.maximum(_SWA_SUB * i - 1, 0)
    return pl.pallas_call(
        _swa_kernel,
        out_shape=jax.ShapeDtypeStruct((batch, seq, d), BF16),
        grid=(batch, ns),
        in_specs=[
            pl.BlockSpec((d, ts), lambda b, i: (0, b * ns + i)),
            pl.BlockSpec((1, tq, kw), lambda b, i: (b, prev(i), 0)),
            pl.BlockSpec((1, ts, kw), lambda b, i: (b, i, 0)),
            pl.BlockSpec((vw, tq), lambda b, i: (0, b * (seq // tq) + prev(i))),
            pl.BlockSpec((vw, ts), lambda b, i: (0, b * ns + i)),
            pl.BlockSpec(bias.shape, lambda b, i: (0, 0, 0, 0)),
            pl.BlockSpec(sink.shape, lambda b, i: (0, 0, 0)),
        ],
        out_specs=pl.BlockSpec((1, ts, d), lambda b, i: (b, i, 0)),
        compiler_params=_params(("parallel", "arbitrary")),
        name="swa_attention",
    )(q_t, k, k, v_t, v_t, bias, sink)


def kernel(x, c, w_ada, b_ada, g_norm_mix, g_norm_ffn, g_final, fox_w_in, fox_b_f, fox_w_out, swa_w_in, swa_sinks, swa_w_out, rel_bias, ffn_w_gu, ffn_w_down, moe_w_router, moe_b_router, moe_w_gu, moe_w_down):
    batch, seq, d = x.shape
    n = batch * seq
    q_scale = HEAD_DIM ** -0.5
    xf = x.reshape(n, d)

    c_pad = jnp.zeros((SUBLANES, d), F32).at[:batch].set(c)
    mod = _ada_call(c_pad, w_ada, b_ada)
    mod0, mod1 = mod[0], mod[1]

    w_in = fox_w_in[0]
    n_heads = d // HEAD_DIM
    k, q_t, v_t, f, w_gu0, w_down0, w_out0, w_out1 = _attn_proj_call(
        xf, g_norm_mix[0], mod0, w_in[:, d:2 * d].astype(BF16),
        (w_in[:, :d] * (q_scale * LOG2E)).T.astype(BF16), w_in[:, 2 * d:3 * d].T.astype(BF16),
        seq, w_f=_split_cols(w_in[:, 3 * d:]),
        cast=(ffn_w_gu[0], ffn_w_down[0], fox_w_out[0], swa_w_out[0]))
    g_cum, r_cum = _cum_call(f, fox_b_f[0], batch, seq)
    r_flat = r_cum.transpose(0, 2, 1).reshape(-1)
    o, w_down1 = _fox_call(r_flat, q_t, k.reshape(batch, seq, d), v_t, g_cum,
                           cast=(moe_w_down[0],))
    w_down1 = w_down1.reshape(moe_w_down.shape[1:])
    x2, w_gu1 = _ffn_call(xf, o.reshape(n, d), w_out0, g_norm_ffn[0], mod0, w_gu0, w_down0, seq,
                          cast=(moe_w_gu[0],))
    w_gu1 = w_gu1.reshape(moe_w_gu.shape[1:])

    w_in = swa_w_in[0]
    kvw = SWA_KV_HEADS * HEAD_DIM
    dup = lambda w: jnp.repeat(w.reshape(d, SWA_KV_HEADS, 1, HEAD_DIM), 2, axis=2).reshape(d, 2 * kvw)
    k, q_t, v_t = _attn_proj_call(
        x2, g_norm_mix[1], mod1, dup(w_in[:, d:d + kvw]).astype(BF16),
        (w_in[:, :d] * (q_scale * LOG2E)).T.astype(BF16), w_in[:, d + kvw:].T.astype(BF16), seq)
    bias = _swa_bias_call(rel_bias)
    sink = jnp.broadcast_to((swa_sinks[0] * LOG2E)[:, None, None], (n_heads, 1, _SWA_TQ))
    o = _swa_call(q_t, k.reshape(batch, seq, 2 * kvw), v_t, bias, sink, batch, seq)
    w_r = _split_cols(jnp.zeros((d, LANES), F32).at[:, :N_EXPERTS].set(moe_w_router[0]))
    b_r = jnp.zeros((1, LANES), F32).at[0, :N_EXPERTS].set(moe_b_router[0])
    x3, h4, route, route_t, counts = _out_call(o.reshape(n, d), w_out1, x2, mod1, g_norm_ffn[1],
                                               w_r, b_r, seq)
    out = _moe_call(h4, route, route_t, counts, x3, mod1, g_final, w_gu1, w_down1, seq)
    return out.reshape(batch, seq, d)
```

```python
import functools

import numpy as np
import jax
import jax.numpy as jnp
from jax import lax
from jax.experimental import pallas as pl
from jax.experimental.pallas import tpu as pltpu

F32 = jnp.float32
BF16 = jnp.bfloat16

HEAD_DIM = 64
CHUNK = 64
WINDOW_CHUNKS = 2
REL_BUCKETS = 32
REL_MAX_DIST = 128
SWA_KV_HEADS = 4
SWA_GROUP = 4
N_EXPERTS = 8
EPS = 1e-6

LANES = 128
SUBLANES = 8
VMEM_LIMIT = 56 * 1024 * 1024

NEG_BIG = -1e30


def _params(sem, vmem=VMEM_LIMIT):
    return pltpu.CompilerParams(dimension_semantics=sem, vmem_limit_bytes=vmem)


def _rms_mod(x, g, shift, scale):
    ms = jnp.mean(x * x, axis=-1, keepdims=True)
    y = x * lax.rsqrt(ms + EPS) * g
    return y * (1.0 + scale) + shift


def _silu(x):
    return x / (1.0 + jnp.exp(-x))


def _split_bf16(x):
    hi = x.astype(BF16)
    return hi, (x - hi.astype(F32)).astype(BF16)


def _split_cols(w):
    return jnp.concatenate(_split_bf16(w), axis=1)


def _dot_split(x_hi, x_lo, w_ref):
    n = w_ref.shape[1] // 2
    y = jnp.dot(x_hi, w_ref[...], preferred_element_type=F32)
    return y[:, :n] + y[:, n:] + jnp.dot(x_lo, w_ref[:, :n], preferred_element_type=F32)


_BF16_SUBLANES = 16


def _ride_along(arrays, n_steps, step_index):
    views, in_specs, out_specs, out_shapes = [], [], [], []
    for a in arrays:
        v = a.reshape(-1, a.shape[-1])
        rows, rem = divmod(v.shape[0], n_steps)
        assert rem == 0 and rows % _BF16_SUBLANES == 0, v.shape
        spec = pl.BlockSpec((rows, v.shape[1]), lambda *g: (step_index(*g), 0))
        views.append(v)
        in_specs.append(spec)
        out_specs.append(spec)
        out_shapes.append(jax.ShapeDtypeStruct(v.shape, BF16))
    return views, in_specs, out_specs, out_shapes


def _ride_along_cast(in_refs, out_refs):
    for src, dst in zip(in_refs, out_refs):
        dst[...] = src[...].astype(BF16)


def _ada_kernel(c_ref, w_ref, b_ref, o_ref):
    c_hi, c_lo = _split_bf16(_silu(c_ref[...]))
    w_hi, w_lo = _split_bf16(w_ref[0])
    o_ref[0] = (jnp.dot(c_hi, w_hi, preferred_element_type=F32)
                + jnp.dot(c_lo, w_hi, preferred_element_type=F32)
                + jnp.dot(c_hi, w_lo, preferred_element_type=F32) + b_ref[0])


def _ada_call(c_pad, w_ada, b_ada):
    depth, d, n = w_ada.shape
    tn = 1536
    return pl.pallas_call(
        _ada_kernel,
        out_shape=jax.ShapeDtypeStruct((depth, SUBLANES, n), F32),
        grid=(depth, n // tn),
        in_specs=[
            pl.BlockSpec((SUBLANES, d), lambda l, j: (0, 0)),
            pl.BlockSpec((1, d, tn), lambda l, j: (l, 0, j)),
            pl.BlockSpec((1, 1, tn), lambda l, j: (l, 0, j)),
        ],
        out_specs=pl.BlockSpec((1, SUBLANES, tn), lambda l, j: (l, 0, j)),
        compiler_params=_params(("parallel", "parallel")),
        name="ada_mod",
    )(c_pad, w_ada, b_ada.reshape(depth, 1, n))


_NT = (((1,), (1,)), ((), ()))


def _attn_proj_kernel(*refs, tiles_per_batch, with_gate, n_cast):
    x_ref, g_ref, sh_ref, sc_ref, wk_ref, wqt_ref, wvt_ref = refs[:7]
    n_in = 7 + with_gate + n_cast
    k_ref, qt_ref, vt_ref = refs[n_in:n_in + 3]
    b = pl.program_id(0) // tiles_per_batch
    h = _rms_mod(x_ref[...], g_ref[...], sh_ref[pl.ds(b, 1), :], sc_ref[pl.ds(b, 1), :])
    hb = h.astype(BF16)
    k_ref[...] = jnp.dot(hb, wk_ref[...], preferred_element_type=F32).astype(BF16)
    qt_ref[...] = lax.dot_general(wqt_ref[...], hb, _NT, preferred_element_type=F32).astype(BF16)
    vt_ref[...] = lax.dot_general(wvt_ref[...], hb, _NT, preferred_element_type=F32).astype(BF16)
    if with_gate:
        refs[n_in + 3][...] = _dot_split(hb, (h - hb.astype(F32)).astype(BF16), refs[7])
    _ride_along_cast(refs[n_in - n_cast:n_in], refs[len(refs) - n_cast:])


def _attn_proj_call(x, g, mod, w_k, w_qt, w_vt, seq, w_f=None, cast=(), tm=512):
    n, d = x.shape
    with_gate = w_f is not None
    full = lambda a: pl.BlockSpec(a.shape, lambda i: (0, 0))
    c_views, c_in, c_out, c_shapes = _ride_along(cast, n // tm, lambda i: i)
    in_specs = [
        pl.BlockSpec((tm, d), lambda i: (i, 0)),
        pl.BlockSpec((1, d), lambda i: (0, 0)),
        pl.BlockSpec((SUBLANES, d), lambda i: (0, 0)),
        pl.BlockSpec((SUBLANES, d), lambda i: (0, 1)),
        full(w_k), full(w_qt), full(w_vt),
    ]
    args = [x, g.reshape(1, d), mod, mod, w_k, w_qt, w_vt]
    out_shape = [jax.ShapeDtypeStruct((n, w_k.shape[1]), BF16),
                 jax.ShapeDtypeStruct((w_qt.shape[0], n), BF16),
                 jax.ShapeDtypeStruct((w_vt.shape[0], n), BF16)]
    out_specs = [pl.BlockSpec((tm, w_k.shape[1]), lambda i: (i, 0)),
                 pl.BlockSpec((w_qt.shape[0], tm), lambda i: (0, i)),
                 pl.BlockSpec((w_vt.shape[0], tm), lambda i: (0, i))]
    if with_gate:
        in_specs.append(full(w_f))
        args.append(w_f)
        out_shape.append(jax.ShapeDtypeStruct((n, w_f.shape[1] // 2), F32))
        out_specs.append(pl.BlockSpec((tm, w_f.shape[1] // 2), lambda i: (i, 0)))
    return pl.pallas_call(
        functools.partial(_attn_proj_kernel, tiles_per_batch=seq // tm, with_gate=with_gate,
                          n_cast=len(cast)),
        out_shape=out_shape + c_shapes,
        grid=(n // tm,),
        in_specs=in_specs + c_in,
        out_specs=out_specs + c_out,
        compiler_params=_params(("parallel",)),
        name="attn_proj_gate" if with_gate else "attn_proj",
    )(*args, *c_views)


_FOX_T = 512
LOG2E = 1.4426950408889634


def _cum_kernel(f_ref, bf_ref, g_ref, r_ref):
    x = f_ref[...] + bf_ref[...]
    logf = (jnp.minimum(x, 0.0) - jnp.log(1.0 + jnp.exp(-jnp.abs(x)))) * LOG2E
    seq, nh = logf.shape
    r = lax.broadcasted_iota(jnp.int32, (_FOX_T, _FOX_T), 0)
    c = lax.broadcasted_iota(jnp.int32, (_FOX_T, _FOX_T), 1)
    lower = jnp.where(c <= r, 1.0, 0.0).astype(BF16)
    hi = logf.astype(BF16).astype(F32)
    mid = (logf - hi).astype(BF16).astype(F32)
    parts = jnp.concatenate([hi, mid, logf - hi - mid], axis=1).astype(BF16)
    carry = jnp.zeros((1, nh), F32)
    for ch in range(seq // _FOX_T):
        rows = slice(ch * _FOX_T, (ch + 1) * _FOX_T)
        y = jnp.dot(lower, parts[rows, :], preferred_element_type=F32)
        cs = y[:, :nh] + y[:, nh:2 * nh] + y[:, 2 * nh:]
        g_ref[0, rows, :] = cs
        r_ref[0, ch:ch + 1, :] = carry
        carry = carry + cs[_FOX_T - 1:_FOX_T, :]


def _cum_call(f, b_f, batch, seq):
    nh = f.shape[1]
    return pl.pallas_call(
        _cum_kernel,
        out_shape=[jax.ShapeDtypeStruct((batch, seq, nh), F32),
                   jax.ShapeDtypeStruct((batch, seq // _FOX_T, nh), F32)],
        grid=(batch,),
        in_specs=[
            pl.BlockSpec((seq, nh), lambda b: (b, 0)),
            pl.BlockSpec((1, nh), lambda b: (0, 0)),
        ],
        out_specs=[pl.BlockSpec((1, seq, nh), lambda b: (b, 0, 0)),
                   pl.BlockSpec((1, seq // _FOX_T, nh), lambda b: (b, 0, 0))],
        compiler_params=_params(("parallel",)),
        name="forget_cumsum",
    )(f, b_f.reshape(1, nh))


_ONES_ROWS = 16


def _fox_kernel(*refs, n_cast):
    r_ref, qt_ref, k_ref, vt_ref, g_ref = refs[:5]
    o_ref = refs[5 + n_cast]
    ka_sc, m_sc, acc_sc, sa_sc, sb_sc, xa_sc, xb_sc = refs[6 + 2 * n_cast:]
    _ride_along_cast(refs[5:5 + n_cast], refs[6 + n_cast:6 + 2 * n_cast])
    t = _FOX_T
    b, hp = pl.program_id(0), pl.program_id(1)
    seq = k_ref.shape[1]
    n_blocks = seq // t
    n_heads = 2 * pl.num_programs(1)

    lane = lax.broadcasted_iota(jnp.int32, (1, LANES), 1)
    feat = lax.broadcasted_iota(jnp.int32, (LANES, 1), 0)
    aug0 = [HEAD_DIM, 0]

    g_all = g_ref[0]
    head_col = lax.broadcasted_iota(jnp.int32, g_all.shape, 1)
    g_head = [jnp.sum(jnp.where(head_col == 2 * hp + hh, g_all, 0.0), axis=1, keepdims=True)
              for hh in range(2)]
    gb = jnp.where(lane >= HEAD_DIM, jnp.broadcast_to(g_head[0], (seq, LANES)),
                   jnp.broadcast_to(g_head[1], (seq, LANES)))
    hi = gb.astype(BF16).astype(F32)
    mid = (gb - hi).astype(BF16).astype(F32)
    lo = gb - hi - mid
    in_half = lane % HEAD_DIM
    aug = jnp.where(in_half == 0, hi, jnp.where(in_half == 1, mid,
                                                jnp.where(in_half == 2, lo, 0.0)))
    kf = k_ref[0].astype(F32)
    for hh in range(2):
        own = (lane >= hh * HEAD_DIM) & (lane < (hh + 1) * HEAD_DIM)
        ka_sc[hh] = jnp.where(own, kf, aug).astype(BF16)

    feat_t = lax.broadcasted_iota(jnp.int32, (LANES, t), 0)
    r_base = [(b * n_heads + 2 * hp + hh) * n_blocks for hh in range(2)]
    bufs = ((sa_sc, xa_sc), (sb_sc, xb_sc))

    def queries(qi):
        qt2 = qt_ref[:, qi * t:(qi + 1) * t]
        return [jnp.where((feat >= hh * HEAD_DIM) & (feat < (hh + 1) * HEAD_DIM), qt2, 0)
                + jnp.where((feat_t >= aug0[hh]) & (feat_t < aug0[hh] + 3), -1.0, 0.0).astype(BF16)
                for hh in range(2)]

    half = t // 2
    lo, hi = slice(0, half), slice(half, t)
    below_diag = (lax.broadcasted_iota(jnp.int32, (half, half), 0)
                  <= lax.broadcasted_iota(jnp.int32, (half, half), 1))

    def scores(qh, qi, kb, dst):
        k0 = kb * t
        for hh in range(2):
            if kb == qi:
                dst[0][hh, lo, :] = jnp.dot(ka_sc[hh, k0:k0 + half, :], qh[hh],
                                            preferred_element_type=F32)
                dst[0][hh, hi, hi] = jnp.dot(ka_sc[hh, k0 + half:k0 + t, :], qh[hh][:, hi],
                                             preferred_element_type=F32)
            else:
                st = jnp.dot(ka_sc[hh, k0:k0 + t, :], qh[hh],
                             preferred_element_type=F32)
                dst[0][hh] = st
                dst[1][hh] = jnp.max(st, axis=0, keepdims=True)

    def values(hh, kb, keys):
        k0, k1 = kb * t + keys.start, kb * t + keys.stop
        return jnp.concatenate(
            [vt_ref[hh * HEAD_DIM:(hh + 1) * HEAD_DIM, k0:k1],
             jnp.ones((_ONES_ROWS, k1 - k0), BF16)], axis=0)

    def softmax_pv(qi, kb, cur):
        for hh in range(2):
            c = r_ref[r_base[hh] + kb] - r_ref[r_base[hh] + qi]
            m_old = m_sc[hh]
            if kb == qi:
                s_ll = jnp.where(below_diag, cur[0][hh, lo, lo], -jnp.inf)
                s_lh = cur[0][hh, lo, hi]
                s_hh = jnp.where(below_diag, cur[0][hh, hi, hi], -jnp.inf)
                st_max = jnp.concatenate(
                    [jnp.max(s_ll, axis=0, keepdims=True),
                     jnp.maximum(jnp.max(s_lh, axis=0, keepdims=True),
                                 jnp.max(s_hh, axis=0, keepdims=True))], axis=1)
                m_new = jnp.maximum(m_old, st_max - c)
                alpha = jnp.exp2(m_old - m_new)
                shift = m_new + c
                p_ll = jnp.exp2(s_ll - shift[:, lo]).astype(BF16)
                p_lh = jnp.exp2(s_lh - shift[:, hi]).astype(BF16)
                p_hh = jnp.exp2(s_hh - shift[:, hi]).astype(BF16)
                v_lo, v_hi = values(hh, kb, lo), values(hh, kb, hi)
                pv = jnp.concatenate(
                    [jnp.dot(v_lo, p_ll, preferred_element_type=F32),
                     jnp.dot(v_lo, p_lh, preferred_element_type=F32)
                     + jnp.dot(v_hi, p_hh, preferred_element_type=F32)], axis=1)
            else:
                st = cur[0][hh]
                m_new = jnp.maximum(m_old, cur[1][hh] - c)
                alpha = jnp.exp2(m_old - m_new)
                p = jnp.exp2(st - (m_new + c)).astype(BF16)
                pv = jnp.dot(values(hh, kb, slice(0, t)), p, preferred_element_type=F32)
            acc_sc[hh] = alpha * acc_sc[hh] + pv
            m_sc[hh] = m_new

    pairs = [(qi, kb) for qi in range(n_blocks) for kb in range(qi + 1)]
    qh = queries(0)
    scores(qh, 0, 0, bufs[0])
    for s, (qi, kb) in enumerate(pairs):
        if kb == 0:
            m_sc[...] = jnp.full_like(m_sc, -jnp.inf)
            acc_sc[...] = jnp.zeros_like(acc_sc)
        if s + 1 < len(pairs):
            qi_n, kb_n = pairs[s + 1]
            if qi_n != qi:
                qh = queries(qi_n)
            scores(qh, qi_n, kb_n, bufs[(s + 1) % 2])
        softmax_pv(qi, kb, bufs[s % 2])
        if kb == qi:
            outs = [acc_sc[hh, :HEAD_DIM, :] / acc_sc[hh, HEAD_DIM:HEAD_DIM + 1, :]
                    for hh in range(2)]
            o_ref[0, qi * t:(qi + 1) * t, :] = jnp.concatenate(outs, axis=0).T.astype(o_ref.dtype)


def _fox_call(r_flat, q_t, k, v_t, g_cum, cast=()):
    batch, seq, d = k.shape
    n_pairs = d // LANES
    t = _FOX_T
    c_views, c_in, c_out, c_shapes = _ride_along(
        cast, batch * n_pairs, lambda b, h: b * n_pairs + h)
    return pl.pallas_call(
        functools.partial(_fox_kernel, n_cast=len(cast)),
        out_shape=[jax.ShapeDtypeStruct((batch, seq, d), BF16)] + c_shapes,
        grid=(batch, n_pairs),
        in_specs=[
            pl.BlockSpec(memory_space=pltpu.SMEM),
            pl.BlockSpec((LANES, seq), lambda b, h: (h, b)),
            pl.BlockSpec((1, seq, LANES), lambda b, h: (b, 0, h)),
            pl.BlockSpec((LANES, seq), lambda b, h: (h, b)),
            pl.BlockSpec((1, seq, g_cum.shape[2]), lambda b, h: (b, 0, 0)),
        ] + c_in,
        out_specs=[pl.BlockSpec((1, seq, LANES), lambda b, h: (b, 0, h))] + c_out,
        scratch_shapes=[pltpu.VMEM((2, seq, LANES), BF16), pltpu.VMEM((2, 1, t), F32),
                        pltpu.VMEM((2, HEAD_DIM + _ONES_ROWS, t), F32),
                        pltpu.VMEM((2, t, t), F32), pltpu.VMEM((2, t, t), F32),
                        pltpu.VMEM((2, 1, t), F32), pltpu.VMEM((2, 1, t), F32)],
        compiler_params=_params(("parallel", "parallel")),
        name="fox_attention",
    )(r_flat, q_t, k, v_t, g_cum, *c_views)


def _out_kernel(o_ref, w_ref, x_ref, gt_ref, g_ref, sh_ref, sc_ref, wr_ref, br_ref,
                xo_ref, h_ref, route_ref, route_t_ref, cnt_ref, *, tiles_per_batch):
    b = pl.program_id(0) // tiles_per_batch
    y = jnp.dot(o_ref[...], w_ref[...], preferred_element_type=F32)
    xn = x_ref[...] + gt_ref[pl.ds(b, 1), :] * y
    xo_ref[...] = xn
    h = _rms_mod(xn, g_ref[...], sh_ref[pl.ds(b, 1), :], sc_ref[pl.ds(b, 1), :])
    h_ref[...] = h
    logits = _dot_split(*_split_bf16(h), wr_ref) + br_ref[...]
    tm = logits.shape[0]
    lane = lax.broadcasted_iota(jnp.int32, logits.shape, 1).astype(F32)
    logits = jnp.where(lane < N_EXPERTS, logits, -jnp.inf)
    m1 = jnp.max(logits, axis=1, keepdims=True)
    i1 = jnp.min(jnp.where(logits == m1, lane, float(LANES)), axis=1, keepdims=True)
    rest = jnp.where(lane == i1, -jnp.inf, logits)
    m2 = jnp.max(rest, axis=1, keepdims=True)
    i2 = jnp.min(jnp.where(rest == m2, lane, float(LANES)), axis=1, keepdims=True)
    e2 = jnp.exp(m2 - m1)
    den = 1.0 + e2

    @pl.when(pl.program_id(0) == 0)
    def _():
        cnt_ref[...] = jnp.zeros_like(cnt_ref)

    sel1 = lane == i1
    sel2 = lane == i2
    onehot = jnp.where(sel1 | sel2, 1.0, 0.0)
    r = lax.broadcasted_iota(jnp.int32, (tm, tm), 0)
    c = lax.broadcasted_iota(jnp.int32, (tm, tm), 1)
    lower = jnp.where(c < r, 1.0, 0.0).astype(BF16)
    before = jnp.dot(lower, onehot.astype(BF16), preferred_element_type=F32) + cnt_ref[0:1, :]
    rank1 = jnp.sum(jnp.where(sel1, before, 0.0), axis=1, keepdims=True)
    rank2 = jnp.sum(jnp.where(sel2, before, 0.0), axis=1, keepdims=True)
    cnt_ref[0:1, :] = cnt_ref[0:1, :] + jnp.sum(onehot, axis=0, keepdims=True)
    route = jnp.where(lane == 0.0, i1, 0.0)
    for k, val in enumerate((i2, 1.0 / den, e2 / den, rank1, rank2), start=1):
        route = jnp.where(lane == float(k), val, route)
    route_ref[...] = route
    route_t_ref[...] = route.T[:SUBLANES, :]


def _out_call(o, w_out, x, mod, g, w_r, b_r, seq, tm=512):
    n, d = x.shape
    row = lambda w: pl.BlockSpec((tm, w), lambda i: (i, 0))
    col = lambda c: pl.BlockSpec((SUBLANES, d), lambda i: (0, c))
    return pl.pallas_call(
        functools.partial(_out_kernel, tiles_per_batch=seq // tm),
        out_shape=[jax.ShapeDtypeStruct((n, d), F32), jax.ShapeDtypeStruct((n, d), F32),
                   jax.ShapeDtypeStruct((n, LANES), F32), jax.ShapeDtypeStruct((SUBLANES, n), F32),
                   jax.ShapeDtypeStruct((SUBLANES, LANES), F32)],
        grid=(n // tm,),
        in_specs=[
            row(d),
            pl.BlockSpec((d, d), lambda i: (0, 0)),
            row(d),
            col(2),
            pl.BlockSpec((1, d), lambda i: (0, 0)),
            col(3),
            col(4),
            pl.BlockSpec((d, 2 * LANES), lambda i: (0, 0)),
            pl.BlockSpec((1, LANES), lambda i: (0, 0)),
        ],
        out_specs=[row(d), row(d), row(LANES), pl.BlockSpec((SUBLANES, tm), lambda i: (0, i)),
                   pl.BlockSpec((SUBLANES, LANES), lambda i: (0, 0))],
        compiler_params=_params(("arbitrary",)),
        name="out_proj_router",
    )(o, w_out, x, mod, g.reshape(1, d), mod, mod, w_r, b_r)


_FF_TILE = 1792


def _snake(i, j, nj):
    return jnp.where(i % 2 == 0, j, nj - 1 - j)


def _ffn_kernel(*refs, tiles_per_batch, n_cast):
    (x_ref, attn_ref, wo_ref, gta_ref, g_ref, sh_ref, sc_ref, gt_ref,
     wg_ref, wu_ref, wd_ref) = refs[:11]
    o_ref = refs[11 + n_cast]
    h_sc, x1_sc = refs[-2:]
    _ride_along_cast(refs[11:11 + n_cast], refs[12 + n_cast:12 + 2 * n_cast])
    i, j = pl.program_id(0), pl.program_id(1)
    slot = i % 2
    b = i // tiles_per_batch

    def prologue(tile, dst):
        bt = tile // tiles_per_batch
        x = x_ref[...] + gta_ref[pl.ds(bt, 1), :] * jnp.dot(
            attn_ref[...], wo_ref[...], preferred_element_type=F32)
        h = _rms_mod(x, g_ref[...], sh_ref[pl.ds(bt, 1), :], sc_ref[pl.ds(bt, 1), :])
        h_sc[dst] = h.astype(BF16)
        x1_sc[dst] = x

    def slab():
        h = h_sc[slot]
        g = jnp.dot(h, wg_ref[...], preferred_element_type=F32)
        u = jnp.dot(h, wu_ref[...], preferred_element_type=F32)
        a = (_silu(g) * u).astype(BF16)
        return gt_ref[pl.ds(b, 1), :] * jnp.dot(a, wd_ref[...], preferred_element_type=F32)

    @pl.when((i == 0) & (j == 0))
    def _():
        prologue(0, 0)

    @pl.when(j == 0)
    def _():
        o_ref[...] = x1_sc[slot] + slab()

    @pl.when(j == 1)
    def _():
        prologue(jnp.minimum(i + 1, pl.num_programs(0) - 1), 1 - slot)
        o_ref[...] += slab()


def _ffn_call(x, attn, w_out, g, mod, w_gu, w_down, seq, cast=(), tm=512, tf=_FF_TILE):
    n, d = x.shape
    f = w_down.shape[0]
    nj = f // tf
    assert nj == 2, "the kernel prepares the next row tile during slab step 1"
    n_tiles = n // tm
    c_views, c_in, c_out, c_shapes = _ride_along(cast, n_tiles * nj, lambda i, j: i * nj + j)
    rows = pl.BlockSpec((tm, d), lambda i, j: (jnp.minimum(i + j, n_tiles - 1), 0))
    return pl.pallas_call(
        functools.partial(_ffn_kernel, tiles_per_batch=seq // tm, n_cast=len(cast)),
        out_shape=[jax.ShapeDtypeStruct((n, d), F32)] + c_shapes,
        grid=(n_tiles, nj),
        in_specs=[
            rows,
            rows,
            pl.BlockSpec((d, d), lambda i, j: (0, 0)),
            pl.BlockSpec((SUBLANES, d), lambda i, j: (0, 2)),
            pl.BlockSpec((1, d), lambda i, j: (0, 0)),
            pl.BlockSpec((SUBLANES, d), lambda i, j: (0, 3)),
            pl.BlockSpec((SUBLANES, d), lambda i, j: (0, 4)),
            pl.BlockSpec((SUBLANES, d), lambda i, j: (0, 5)),
            pl.BlockSpec((d, tf), lambda i, j: (0, _snake(i, j, nj))),
            pl.BlockSpec((d, tf), lambda i, j: (0, _snake(i, j, nj) + nj)),
            pl.BlockSpec((tf, d), lambda i, j: (_snake(i, j, nj), 0)),
        ] + c_in,
        out_specs=[pl.BlockSpec((tm, d), lambda i, j: (i, 0))] + c_out,
        scratch_shapes=[pltpu.VMEM((2, tm, d), BF16), pltpu.VMEM((2, tm, d), F32)],
        compiler_params=_params(("arbitrary", "arbitrary")),
        name="ffn_swiglu",
    )(x, attn, w_out, mod, g.reshape(1, d), mod, mod, mod, w_gu, w_gu, w_down, *c_views)


_MOE_TM = 512


def _dispatch_kernel(d1_ref, d2_ref, zero_ref, h_ref, xs_ref, z_sc, sem):
    tm = h_ref.shape[0]
    base = pl.program_id(0) * tm

    @pl.when(pl.program_id(0) == 0)
    def _():
        z_sc[...] = jnp.zeros_like(z_sc)

        def zero_copy(t):
            return pltpu.make_async_copy(z_sc, xs_ref.at[pl.ds(pl.multiple_of(t * tm, tm), tm)],
                                         sem.at[0])

        def start(t, carry):
            @pl.when(zero_ref[t] != 0)
            def _():
                zero_copy(t).start()
            return carry

        def wait(t, carry):
            @pl.when(zero_ref[t] != 0)
            def _():
                zero_copy(t).wait()
            return carry

        lax.fori_loop(0, zero_ref.shape[0], start, 0)
        lax.fori_loop(0, zero_ref.shape[0], wait, 0)

    def issue(r, carry):
        src = h_ref.at[pl.ds(r, 1)]
        pltpu.make_async_copy(src, xs_ref.at[pl.ds(d1_ref[base + r], 1)], sem.at[0]).start()
        pltpu.make_async_copy(src, xs_ref.at[pl.ds(d2_ref[base + r], 1)],
                              sem.at[1]).start(priority=1)
        return carry

    lax.fori_loop(0, tm, issue, 0, unroll=8)
    pltpu.make_async_copy(h_ref, xs_ref.at[pl.ds(0, tm)], sem.at[0]).wait()
    pltpu.make_async_copy(h_ref, xs_ref.at[pl.ds(0, tm)], sem.at[1]).wait()


def _dispatch_call(dest1, dest2, zero_tile, h, tm=_MOE_TM):
    n, d = h.shape
    n_rows = zero_tile.shape[0] * tm
    return pl.pallas_call(
        _dispatch_kernel,
        out_shape=jax.ShapeDtypeStruct((n_rows, d), h.dtype),
        grid_spec=pltpu.PrefetchScalarGridSpec(
            num_scalar_prefetch=3,
            grid=(n // tm,),
            in_specs=[pl.BlockSpec((tm, d), lambda i, d1, d2, zt: (i, 0))],
            out_specs=pl.BlockSpec(memory_space=pl.ANY),
            scratch_shapes=[pltpu.VMEM((tm, d), h.dtype), pltpu.SemaphoreType.DMA((2,))],
        ),
        compiler_params=_params(("arbitrary",)),
        name="moe_dispatch",
    )(dest1, dest2, zero_tile, h)


def _experts_kernel(te_ref, nt_ref, rows_ref, xs_ref, wg_ref, wu_ref, wd_ref, ye_ref):
    del te_ref, nt_ref
    tm = xs_ref.shape[0]
    rows = rows_ref[pl.program_id(0)]

    @pl.when(pl.program_id(1) == 0)
    def _():
        ye_ref[...] = jnp.zeros_like(ye_ref)

    def swiglu(n):
        h = xs_ref[:n, :].astype(BF16)
        g = jnp.dot(h, wg_ref[0], preferred_element_type=F32)
        u = jnp.dot(h, wu_ref[0], preferred_element_type=F32)
        a = (_silu(g) * u).astype(BF16)
        ye_ref[:n, :] += jnp.dot(a, wd_ref[0], preferred_element_type=F32)

    @pl.when(rows > tm // 2)
    def _():
        swiglu(tm)

    @pl.when((rows > 0) & (rows <= tm // 2))
    def _():
        swiglu(tm // 2)


def _experts_call(tile_expert, n_tiles, tile_rows, xs, w_gu, w_down, tf=_FF_TILE):
    n_rows, d = xs.shape
    _, f, _ = w_down.shape
    nj = f // tf
    tm = _MOE_TM
    tile = lambda t, nt: jnp.maximum(jnp.minimum(t, nt[0] - 1), 0)
    jj = lambda t, j, nt: _snake(tile(t, nt), jnp.where(t < nt[0], j, nj - 1), nj)
    return pl.pallas_call(
        _experts_kernel,
        out_shape=jax.ShapeDtypeStruct((n_rows, d), F32),
        grid_spec=pltpu.PrefetchScalarGridSpec(
            num_scalar_prefetch=3,
            grid=(n_rows // tm, nj),
            in_specs=[
                pl.BlockSpec((tm, d), lambda t, j, te, nt, tr: (tile(t, nt), 0)),
                pl.BlockSpec((1, d, tf), lambda t, j, te, nt, tr: (te[tile(t, nt)], 0, jj(t, j, nt))),
                pl.BlockSpec((1, d, tf),
                             lambda t, j, te, nt, tr: (te[tile(t, nt)], 0, jj(t, j, nt) + nj)),
                pl.BlockSpec((1, tf, d), lambda t, j, te, nt, tr: (te[tile(t, nt)], jj(t, j, nt), 0)),
            ],
            out_specs=pl.BlockSpec((tm, d), lambda t, j, te, nt, tr: (t, 0)),
        ),
        compiler_params=_params(("arbitrary", "arbitrary")),
        name="moe_experts",
    )(tile_expert, n_tiles, tile_rows, xs, w_gu, w_gu, w_down)


def _combine_kernel(d1_ref, d2_ref, ye_ref, x_ref, route_ref, gt_ref, gf_ref, o_ref,
                    y1_sc, y2_sc, sem, *, tiles_per_batch):
    tm = x_ref.shape[0]
    i = pl.program_id(0)
    b = i // tiles_per_batch
    slot = i % 2

    def gather(tile, dst_slot):
        base = tile * tm

        def issue(r, carry):
            pltpu.make_async_copy(ye_ref.at[pl.ds(d1_ref[base + r], 1)],
                                  y1_sc.at[dst_slot, pl.ds(r, 1)], sem.at[0, dst_slot]).start()
            pltpu.make_async_copy(ye_ref.at[pl.ds(d2_ref[base + r], 1)],
                                  y2_sc.at[dst_slot, pl.ds(r, 1)],
                                  sem.at[1, dst_slot]).start(priority=1)
            return carry

        lax.fori_loop(0, tm, issue, 0, unroll=8)

    @pl.when(i == 0)
    def _():
        gather(0, 0)

    @pl.when(i + 1 < pl.num_programs(0))
    def _():
        gather(i + 1, 1 - slot)

    pltpu.make_async_copy(ye_ref.at[pl.ds(0, tm)], y1_sc.at[slot], sem.at[0, slot]).wait()
    pltpu.make_async_copy(ye_ref.at[pl.ds(0, tm)], y2_sc.at[slot], sem.at[1, slot]).wait()
    route = route_ref[...]
    y = route[:, 2:3] * y1_sc[slot] + route[:, 3:4] * y2_sc[slot]
    xn = x_ref[...] + gt_ref[pl.ds(b, 1), :] * y
    ms = jnp.mean(xn * xn, axis=-1, keepdims=True)
    o_ref[...] = xn * lax.rsqrt(ms + EPS) * gf_ref[...]


def _combine_call(dest1, dest2, ye, x, route, mod, g_final, seq, tm=512):
    n, d = x.shape
    return pl.pallas_call(
        functools.partial(_combine_kernel, tiles_per_batch=seq // tm),
        out_shape=jax.ShapeDtypeStruct((n, d), F32),
        grid_spec=pltpu.PrefetchScalarGridSpec(
            num_scalar_prefetch=2,
            grid=(n // tm,),
            in_specs=[
                pl.BlockSpec(memory_space=pl.ANY),
                pl.BlockSpec((tm, d), lambda i, d1, d2: (i, 0)),
                pl.BlockSpec((tm, LANES), lambda i, d1, d2: (i, 0)),
                pl.BlockSpec((SUBLANES, d), lambda i, d1, d2: (0, 5)),
                pl.BlockSpec((1, d), lambda i, d1, d2: (0, 0)),
            ],
            out_specs=pl.BlockSpec((tm, d), lambda i, d1, d2: (i, 0)),
            scratch_shapes=[pltpu.VMEM((2, tm, d), F32), pltpu.VMEM((2, tm, d), F32),
                            pltpu.SemaphoreType.DMA((2, 2))],
        ),
        compiler_params=_params(("arbitrary",)),
        name="moe_combine",
    )(dest1, dest2, ye, x, route, mod, g_final.reshape(1, d))


def _moe_call(h, route, route_t, counts, x, mod, g_final, w_gu, w_down, seq):
    n, d = x.shape
    ne = w_down.shape[0]
    tm = _MOE_TM
    max_tiles = (2 * n) // tm + ne
    e1, e2, _, _, rank1, rank2 = (route_t[k].astype(jnp.int32) for k in range(6))
    cnt = counts[0, :ne].astype(jnp.int32)
    tiles_e = (cnt + tm - 1) // tm
    tile_end = jnp.cumsum(tiles_e)
    row_start = (tile_end - tiles_e) * tm
    dest1 = row_start[e1] + rank1
    dest2 = row_start[e2] + rank2
    n_tiles = tile_end[-1:]
    tile_ids = jnp.arange(max_tiles, dtype=jnp.int32)
    tile_expert = jnp.minimum(
        jnp.sum((tile_ids[:, None] >= tile_end[None, :]).astype(jnp.int32), axis=1), ne - 1)
    is_last = jnp.any((tile_ids[:, None] == tile_end[None, :] - 1) & (tiles_e[None, :] > 0), axis=1)
    zero_tile = (is_last | (tile_ids >= n_tiles[0])).astype(jnp.int32)
    xs = _dispatch_call(dest1, dest2, zero_tile, h)
    local = tile_ids[:, None] - (tile_end - tiles_e)[None, :]
    in_expert = (local >= 0) & (local < tiles_e[None, :])
    tile_rows = jnp.sum(jnp.where(in_expert, jnp.clip(cnt[None, :] - local * tm, 0, tm), 0),
                        axis=1).astype(jnp.int32)
    ye = _experts_call(tile_expert, n_tiles, tile_rows, xs, w_gu, w_down)
    return _combine_call(dest1, dest2, ye, x, route, mod, g_final, seq)


_SWA_TQ = 2 * CHUNK
_SWA_BAND = 2 * _SWA_TQ
_SWA_SUB = 4


def _swa_bucket_tiles():
    cc = np.arange(_SWA_BAND)[:, None]
    r = np.arange(_SWA_TQ)[None, :]
    rel = cc - _SWA_TQ - r
    nb = REL_BUCKETS // 2
    max_exact = nb // 2
    ret = (rel > 0).astype(np.int32) * nb
    n = np.abs(rel)
    large = max_exact + (np.log(np.maximum(n, 1) / max_exact)
                         / np.log(REL_MAX_DIST / max_exact) * (nb - max_exact)).astype(np.int32)
    large = np.minimum(large, nb - 1)
    bucket = (ret + np.where(n < max_exact, n, large)).astype(np.int32)
    q_chunk = r // CHUNK
    k_chunk = cc // CHUNK
    visible = (k_chunk >= q_chunk) & (k_chunk <= q_chunk + WINDOW_CHUNKS)
    later = np.where(visible, bucket, -1)
    first = np.where(cc >= _SWA_TQ, later, -1)
    return np.stack([first, later]).astype(np.int32)


def _swa_bias_kernel(tbl_ref, bkt_ref, o_ref):
    n_heads = o_ref.shape[1]
    for v in range(2):
        bkt = bkt_ref[v]
        for head in range(n_heads):
            tile = jnp.full(bkt.shape, NEG_BIG, F32)
            for bk in range(REL_BUCKETS):
                tile = jnp.where(bkt == bk, tbl_ref[head, bk] * LOG2E, tile)
            o_ref[v, head] = tile


def _swa_bias_call(rel_bias):
    bkt = jnp.asarray(_swa_bucket_tiles())
    n_heads = rel_bias.shape[1]
    return pl.pallas_call(
        _swa_bias_kernel,
        out_shape=jax.ShapeDtypeStruct((2, n_heads, _SWA_BAND, _SWA_TQ), F32),
        in_specs=[
            pl.BlockSpec(memory_space=pltpu.SMEM),
            pl.BlockSpec(memory_space=pltpu.VMEM),
        ],
        out_specs=pl.BlockSpec(memory_space=pltpu.VMEM),
        name="swa_bias",
    )(rel_bias.T, bkt)


def _swa_kernel(qt_ref, kp_ref, kc_ref, vtp_ref, vtc_ref, bias_ref, sink_ref, o_ref):
    tq = _SWA_TQ
    first_block = jnp.minimum(pl.program_id(1), 1)
    lane = lax.broadcasted_iota(jnp.int32, (1, LANES), 1)
    ones = jnp.ones((_ONES_ROWS, _SWA_BAND), BF16)
    units = [(sub, hk, par) for sub in range(_SWA_SUB)
             for hk in range(SWA_KV_HEADS) for par in range(2)]

    def band_keys(sub, ksl):
        if sub == 0:
            return jnp.concatenate([kp_ref[0, :, ksl], kc_ref[0, :tq, ksl]], axis=0)
        return kc_ref[0, (sub - 1) * tq:(sub + 1) * tq, ksl]

    def band_values(sub, vsl):
        if sub == 0:
            return jnp.concatenate([vtp_ref[vsl, :], vtc_ref[vsl, :tq]], axis=1)
        return vtc_ref[vsl, (sub - 1) * tq:(sub + 1) * tq]

    def scores(sub, hk, par):
        kb = band_keys(sub, slice(hk * LANES, (hk + 1) * LANES))
        f0 = hk * SWA_GROUP * HEAD_DIM
        qs = slice(sub * tq, (sub + 1) * tq)
        wq = jnp.concatenate([qt_ref[f0:f0 + LANES, qs], qt_ref[f0 + LANES:f0 + 2 * LANES, qs]],
                             axis=1)
        head_lanes = (lane < HEAD_DIM) if par == 0 else (lane >= HEAD_DIM)
        return jnp.dot(jnp.where(head_lanes, kb, 0), wq, preferred_element_type=F32)

    sts = [scores(*u) for u in units]
    outs = {}
    for (sub, hk, par), st in zip(units, sts):
        vt1 = jnp.concatenate(
            [band_values(sub, slice(hk * HEAD_DIM, (hk + 1) * HEAD_DIM)), ones], axis=0)
        variant = first_block if sub == 0 else 1
        heads = (hk * SWA_GROUP + par, hk * SWA_GROUP + par + 2)
        ps, ms = [], []
        for i, head in enumerate(heads):
            s = st[:, i * tq:(i + 1) * tq] + bias_ref[variant, head]
            m = jnp.maximum(jnp.max(s, axis=0, keepdims=True), sink_ref[head])
            ps.append(jnp.exp2(s - m).astype(BF16))
            ms.append(m)
        acc = jnp.dot(vt1, jnp.concatenate(ps, axis=1), preferred_element_type=F32)
        for i, head in enumerate(heads):
            a = acc[:, i * tq:(i + 1) * tq]
            den = a[HEAD_DIM:HEAD_DIM + 1] + jnp.exp2(sink_ref[head] - ms[i])
            outs[sub, head] = a[:HEAD_DIM] / den
    n_heads = SWA_KV_HEADS * SWA_GROUP
    for sub in range(_SWA_SUB):
        o_t = jnp.concatenate([outs[sub, head] for head in range(n_heads)], axis=0)
        o_ref[0, sub * tq:(sub + 1) * tq, :] = o_t.T.astype(o_ref.dtype)


def _swa_call(q_t, k, v_t, bias, sink, batch, seq):
    d = q_t.shape[0]
    kw = k.shape[2]
    vw = v_t.shape[0]
    tq = _SWA_TQ
    ts = _SWA_SUB * tq
    ns = seq // ts
    prev = lambda i: jnp.maximum(_SWA_SUB * i - 1, 0)
    return pl.pallas_call(
        _swa_kernel,
        out_shape=jax.ShapeDtypeStruct((batch, seq, d), BF16),
        grid=(batch, ns),
        in_specs=[
            pl.BlockSpec((d, ts), lambda b, i: (0, b * ns + i)),
            pl.BlockSpec((1, tq, kw), lambda b, i: (b, prev(i), 0)),
            pl.BlockSpec((1, ts, kw), lambda b, i: (b, i, 0)),
            pl.BlockSpec((vw, tq), lambda b, i: (0, b * (seq // tq) + prev(i))),
            pl.BlockSpec((vw, ts), lambda b, i: (0, b * ns + i)),
            pl.BlockSpec(bias.shape, lambda b, i: (0, 0, 0, 0)),
            pl.BlockSpec(sink.shape, lambda b, i: (0, 0, 0)),
        ],
        out_specs=pl.BlockSpec((1, ts, d), lambda b, i: (b, i, 0)),
        compiler_params=_params(("parallel", "arbitrary")),
        name="swa_attention",
    )(q_t, k, k, v_t, v_t, bias, sink)


def kernel(x, c, w_ada, b_ada, g_norm_mix, g_norm_ffn, g_final, fox_w_in, fox_b_f, fox_w_out, swa_w_in, swa_sinks, swa_w_out, rel_bias, ffn_w_gu, ffn_w_down, moe_w_router, moe_b_router, moe_w_gu, moe_w_down):
    batch, seq, d = x.shape
    n = batch * seq
    q_scale = HEAD_DIM ** -0.5
    xf = x.reshape(n, d)

    c_pad = jnp.zeros((SUBLANES, d), F32).at[:batch].set(c)
    mod = _ada_call(c_pad, w_ada, b_ada)
    mod0, mod1 = mod[0], mod[1]

    w_in = fox_w_in[0]
    n_heads = d // HEAD_DIM
    k, q_t, v_t, f, w_gu0, w_down0, w_out0, w_out1 = _attn_proj_call(
        xf, g_norm_mix[0], mod0, w_in[:, d:2 * d].astype(BF16),
        (w_in[:, :d] * (q_scale * LOG2E)).T.astype(BF16), w_in[:, 2 * d:3 * d].T.astype(BF16),
        seq, w_f=_split_cols(w_in[:, 3 * d:]),
        cast=(ffn_w_gu[0], ffn_w_down[0], fox_w_out[0], swa_w_out[0]))
    g_cum, r_cum = _cum_call(f, fox_b_f[0], batch, seq)
    r_flat = r_cum.transpose(0, 2, 1).reshape(-1)
    o, w_down1, w_gu1 = _fox_call(r_flat, q_t, k.reshape(batch, seq, d), v_t, g_cum,
                                  cast=(moe_w_down[0], moe_w_gu[0]))
    w_down1 = w_down1.reshape(moe_w_down.shape[1:])
    w_gu1 = w_gu1.reshape(moe_w_gu.shape[1:])
    x2, = _ffn_call(xf, o.reshape(n, d), w_out0, g_norm_ffn[0], mod0, w_gu0, w_down0, seq)

    w_in = swa_w_in[0]
    kvw = SWA_KV_HEADS * HEAD_DIM
    dup = lambda w: jnp.repeat(w.reshape(d, SWA_KV_HEADS, 1, HEAD_DIM), 2, axis=2).reshape(d, 2 * kvw)
    k, q_t, v_t = _attn_proj_call(
        x2, g_norm_mix[1], mod1, dup(w_in[:, d:d + kvw]).astype(BF16),
        (w_in[:, :d] * (q_scale * LOG2E)).T.astype(BF16), w_in[:, d + kvw:].T.astype(BF16), seq)
    bias = _swa_bias_call(rel_bias)
    sink = jnp.broadcast_to((swa_sinks[0] * LOG2E)[:, None, None], (n_heads, 1, _SWA_TQ))
    o = _swa_call(q_t, k.reshape(batch, seq, 2 * kvw), v_t, bias, sink, batch, seq)
    w_r = _split_cols(jnp.zeros((d, LANES), F32).at[:, :N_EXPERTS].set(moe_w_router[0]))
    b_r = jnp.zeros((1, LANES), F32).at[0, :N_EXPERTS].set(moe_b_router[0])
    x3, h4, route, route_t, counts = _out_call(o.reshape(n, d), w_out1, x2, mod1, g_norm_ffn[1],
                                               w_r, b_r, seq)
    out = _moe_call(h4, route, route_t, counts, x3, mod1, g_final, w_gu1, w_down1, seq)
    return out.reshape(batch, seq, d)
```

```python
import functools

import numpy as np
import jax
import jax.numpy as jnp
from jax import lax
from jax.experimental import pallas as pl
from jax.experimental.pallas import tpu as pltpu

F32 = jnp.float32
BF16 = jnp.bfloat16

HEAD_DIM = 64
CHUNK = 64
WINDOW_CHUNKS = 2
REL_BUCKETS = 32
REL_MAX_DIST = 128
SWA_KV_HEADS = 4
SWA_GROUP = 4
N_EXPERTS = 8
EPS = 1e-6

LANES = 128
SUBLANES = 8
VMEM_LIMIT = 56 * 1024 * 1024

NEG_BIG = -1e30


def _params(sem, vmem=VMEM_LIMIT):
    return pltpu.CompilerParams(dimension_semantics=sem, vmem_limit_bytes=vmem)


def _rms_mod(x, g, shift, scale):
    ms = jnp.mean(x * x, axis=-1, keepdims=True)
    y = x * lax.rsqrt(ms + EPS) * g
    return y * (1.0 + scale) + shift


def _silu(x):
    return x / (1.0 + jnp.exp(-x))


def _split_bf16(x):
    hi = x.astype(BF16)
    return hi, (x - hi.astype(F32)).astype(BF16)


def _split_cols(w):
    return jnp.concatenate(_split_bf16(w), axis=1)


def _dot_split(x_hi, x_lo, w_ref):
    n = w_ref.shape[1] // 2
    y = jnp.dot(x_hi, w_ref[...], preferred_element_type=F32)
    return y[:, :n] + y[:, n:] + jnp.dot(x_lo, w_ref[:, :n], preferred_element_type=F32)


_BF16_SUBLANES = 16


def _ride_along(arrays, n_steps, step_index):
    views, in_specs, out_specs, out_shapes = [], [], [], []
    for a in arrays:
        v = a.reshape(-1, a.shape[-1])
        rows, rem = divmod(v.shape[0], n_steps)
        assert rem == 0 and rows % _BF16_SUBLANES == 0, v.shape
        spec = pl.BlockSpec((rows, v.shape[1]), lambda *g: (step_index(*g), 0))
        views.append(v)
        in_specs.append(spec)
        out_specs.append(spec)
        out_shapes.append(jax.ShapeDtypeStruct(v.shape, BF16))
    return views, in_specs, out_specs, out_shapes


def _ride_along_cast(in_refs, out_refs):
    for src, dst in zip(in_refs, out_refs):
        dst[...] = src[...].astype(BF16)


def _ada_kernel(c_ref, w_ref, b_ref, o_ref):
    c_hi, c_lo = _split_bf16(_silu(c_ref[...]))
    w_hi, w_lo = _split_bf16(w_ref[0])
    o_ref[0] = (jnp.dot(c_hi, w_hi, preferred_element_type=F32)
                + jnp.dot(c_lo, w_hi, preferred_element_type=F32)
                + jnp.dot(c_hi, w_lo, preferred_element_type=F32) + b_ref[0])


def _ada_call(c_pad, w_ada, b_ada):
    depth, d, n = w_ada.shape
    tn = 1536
    return pl.pallas_call(
        _ada_kernel,
        out_shape=jax.ShapeDtypeStruct((depth, SUBLANES, n), F32),
        grid=(depth, n // tn),
        in_specs=[
            pl.BlockSpec((SUBLANES, d), lambda l, j: (0, 0)),
            pl.BlockSpec((1, d, tn), lambda l, j: (l, 0, j)),
            pl.BlockSpec((1, 1, tn), lambda l, j: (l, 0, j)),
        ],
        out_specs=pl.BlockSpec((1, SUBLANES, tn), lambda l, j: (l, 0, j)),
        compiler_params=_params(("parallel", "parallel")),
        name="ada_mod",
    )(c_pad, w_ada, b_ada.reshape(depth, 1, n))


_NT = (((1,), (1,)), ((), ()))


def _attn_proj_kernel(*refs, tiles_per_batch, with_gate, n_cast):
    x_ref, g_ref, sh_ref, sc_ref, wk_ref, wqt_ref, wvt_ref = refs[:7]
    n_in = 7 + with_gate + n_cast
    k_ref, qt_ref, vt_ref = refs[n_in:n_in + 3]
    b = pl.program_id(0) // tiles_per_batch
    h = _rms_mod(x_ref[...], g_ref[...], sh_ref[pl.ds(b, 1), :], sc_ref[pl.ds(b, 1), :])
    hb = h.astype(BF16)
    k_ref[...] = jnp.dot(hb, wk_ref[...], preferred_element_type=F32).astype(BF16)
    qt_ref[...] = lax.dot_general(wqt_ref[...], hb, _NT, preferred_element_type=F32).astype(BF16)
    vt_ref[...] = lax.dot_general(wvt_ref[...], hb, _NT, preferred_element_type=F32).astype(BF16)
    if with_gate:
        refs[n_in + 3][...] = _dot_split(hb, (h - hb.astype(F32)).astype(BF16), refs[7])
    _ride_along_cast(refs[n_in - n_cast:n_in], refs[len(refs) - n_cast:])


def _attn_proj_call(x, g, mod, w_k, w_qt, w_vt, seq, w_f=None, cast=(), tm=512):
    n, d = x.shape
    with_gate = w_f is not None
    full = lambda a: pl.BlockSpec(a.shape, lambda i: (0, 0))
    c_views, c_in, c_out, c_shapes = _ride_along(cast, n // tm, lambda i: i)
    in_specs = [
        pl.BlockSpec((tm, d), lambda i: (i, 0)),
        pl.BlockSpec((1, d), lambda i: (0, 0)),
        pl.BlockSpec((SUBLANES, d), lambda i: (0, 0)),
        pl.BlockSpec((SUBLANES, d), lambda i: (0, 1)),
        full(w_k), full(w_qt), full(w_vt),
    ]
    args = [x, g.reshape(1, d), mod, mod, w_k, w_qt, w_vt]
    out_shape = [jax.ShapeDtypeStruct((n, w_k.shape[1]), BF16),
                 jax.ShapeDtypeStruct((w_qt.shape[0], n), BF16),
                 jax.ShapeDtypeStruct((w_vt.shape[0], n), BF16)]
    out_specs = [pl.BlockSpec((tm, w_k.shape[1]), lambda i: (i, 0)),
                 pl.BlockSpec((w_qt.shape[0], tm), lambda i: (0, i)),
                 pl.BlockSpec((w_vt.shape[0], tm), lambda i: (0, i))]
    if with_gate:
        in_specs.append(full(w_f))
        args.append(w_f)
        out_shape.append(jax.ShapeDtypeStruct((n, w_f.shape[1] // 2), F32))
        out_specs.append(pl.BlockSpec((tm, w_f.shape[1] // 2), lambda i: (i, 0)))
    return pl.pallas_call(
        functools.partial(_attn_proj_kernel, tiles_per_batch=seq // tm, with_gate=with_gate,
                          n_cast=len(cast)),
        out_shape=out_shape + c_shapes,
        grid=(n // tm,),
        in_specs=in_specs + c_in,
        out_specs=out_specs + c_out,
        compiler_params=_params(("parallel",)),
        name="attn_proj_gate" if with_gate else "attn_proj",
    )(*args, *c_views)


_FOX_T = 512
LOG2E = 1.4426950408889634


def _cum_kernel(f_ref, bf_ref, g_ref, r_ref):
    x = f_ref[...] + bf_ref[...]
    logf = (jnp.minimum(x, 0.0) - jnp.log(1.0 + jnp.exp(-jnp.abs(x)))) * LOG2E
    seq, nh = logf.shape
    r = lax.broadcasted_iota(jnp.int32, (_FOX_T, _FOX_T), 0)
    c = lax.broadcasted_iota(jnp.int32, (_FOX_T, _FOX_T), 1)
    lower = jnp.where(c <= r, 1.0, 0.0).astype(BF16)
    hi = logf.astype(BF16).astype(F32)
    mid = (logf - hi).astype(BF16).astype(F32)
    parts = jnp.concatenate([hi, mid, logf - hi - mid], axis=1).astype(BF16)
    carry = jnp.zeros((1, nh), F32)
    for ch in range(seq // _FOX_T):
        rows = slice(ch * _FOX_T, (ch + 1) * _FOX_T)
        y = jnp.dot(lower, parts[rows, :], preferred_element_type=F32)
        cs = y[:, :nh] + y[:, nh:2 * nh] + y[:, 2 * nh:]
        g_ref[0, rows, :] = cs
        r_ref[0, ch:ch + 1, :] = carry
        carry = carry + cs[_FOX_T - 1:_FOX_T, :]


def _cum_call(f, b_f, batch, seq):
    nh = f.shape[1]
    return pl.pallas_call(
        _cum_kernel,
        out_shape=[jax.ShapeDtypeStruct((batch, seq, nh), F32),
                   jax.ShapeDtypeStruct((batch, seq // _FOX_T, nh), F32)],
        grid=(batch,),
        in_specs=[
            pl.BlockSpec((seq, nh), lambda b: (b, 0)),
            pl.BlockSpec((1, nh), lambda b: (0, 0)),
        ],
        out_specs=[pl.BlockSpec((1, seq, nh), lambda b: (b, 0, 0)),
                   pl.BlockSpec((1, seq // _FOX_T, nh), lambda b: (b, 0, 0))],
        compiler_params=_params(("parallel",)),
        name="forget_cumsum",
    )(f, b_f.reshape(1, nh))


_ONES_ROWS = 16


def _fox_kernel(*refs, n_cast):
    r_ref, qt_ref, k_ref, vt_ref, g_ref = refs[:5]
    o_ref = refs[5 + n_cast]
    ka_sc, m_sc, acc_sc, sa_sc, sb_sc, xa_sc, xb_sc = refs[6 + 2 * n_cast:]
    _ride_along_cast(refs[5:5 + n_cast], refs[6 + n_cast:6 + 2 * n_cast])
    t = _FOX_T
    b, hp = pl.program_id(0), pl.program_id(1)
    seq = k_ref.shape[1]
    n_blocks = seq // t
    n_heads = 2 * pl.num_programs(1)

    lane = lax.broadcasted_iota(jnp.int32, (1, LANES), 1)
    feat = lax.broadcasted_iota(jnp.int32, (LANES, 1), 0)
    aug0 = [HEAD_DIM, 0]

    g_all = g_ref[0]
    head_col = lax.broadcasted_iota(jnp.int32, g_all.shape, 1)
    g_head = [jnp.sum(jnp.where(head_col == 2 * hp + hh, g_all, 0.0), axis=1, keepdims=True)
              for hh in range(2)]
    gb = jnp.where(lane >= HEAD_DIM, jnp.broadcast_to(g_head[0], (seq, LANES)),
                   jnp.broadcast_to(g_head[1], (seq, LANES)))
    hi = gb.astype(BF16).astype(F32)
    mid = (gb - hi).astype(BF16).astype(F32)
    lo = gb - hi - mid
    in_half = lane % HEAD_DIM
    aug = jnp.where(in_half == 0, hi, jnp.where(in_half == 1, mid,
                                                jnp.where(in_half == 2, lo, 0.0)))
    kf = k_ref[0].astype(F32)
    for hh in range(2):
        own = (lane >= hh * HEAD_DIM) & (lane < (hh + 1) * HEAD_DIM)
        ka_sc[hh] = jnp.where(own, kf, aug).astype(BF16)

    feat_t = lax.broadcasted_iota(jnp.int32, (LANES, t), 0)
    r_base = [(b * n_heads + 2 * hp + hh) * n_blocks for hh in range(2)]
    bufs = ((sa_sc, xa_sc), (sb_sc, xb_sc))

    def queries(qi):
        qt2 = qt_ref[:, qi * t:(qi + 1) * t]
        return [jnp.where((feat >= hh * HEAD_DIM) & (feat < (hh + 1) * HEAD_DIM), qt2, 0)
                + jnp.where((feat_t >= aug0[hh]) & (feat_t < aug0[hh] + 3), -1.0, 0.0).astype(BF16)
                for hh in range(2)]

    half = t // 2
    lo, hi = slice(0, half), slice(half, t)
    below_diag = (lax.broadcasted_iota(jnp.int32, (half, half), 0)
                  <= lax.broadcasted_iota(jnp.int32, (half, half), 1))

    def scores(qh, qi, kb, dst):
        k0 = kb * t
        for hh in range(2):
            if kb == qi:
                dst[0][hh, lo, :] = jnp.dot(ka_sc[hh, k0:k0 + half, :], qh[hh],
                                            preferred_element_type=F32)
                dst[0][hh, hi, hi] = jnp.dot(ka_sc[hh, k0 + half:k0 + t, :], qh[hh][:, hi],
                                             preferred_element_type=F32)
            else:
                st = jnp.dot(ka_sc[hh, k0:k0 + t, :], qh[hh],
                             preferred_element_type=F32)
                dst[0][hh] = st
                dst[1][hh] = jnp.max(st, axis=0, keepdims=True)

    def values(hh, kb, keys):
        k0, k1 = kb * t + keys.start, kb * t + keys.stop
        return jnp.concatenate(
            [vt_ref[hh * HEAD_DIM:(hh + 1) * HEAD_DIM, k0:k1],
             jnp.ones((_ONES_ROWS, k1 - k0), BF16)], axis=0)

    def softmax_pv(qi, kb, cur):
        for hh in range(2):
            c = r_ref[r_base[hh] + kb] - r_ref[r_base[hh] + qi]
            m_old = m_sc[hh]
            if kb == qi:
                s_ll = jnp.where(below_diag, cur[0][hh, lo, lo], -jnp.inf)
                s_lh = cur[0][hh, lo, hi]
                s_hh = jnp.where(below_diag, cur[0][hh, hi, hi], -jnp.inf)
                st_max = jnp.concatenate(
                    [jnp.max(s_ll, axis=0, keepdims=True),
                     jnp.maximum(jnp.max(s_lh, axis=0, keepdims=True),
                                 jnp.max(s_hh, axis=0, keepdims=True))], axis=1)
                m_new = jnp.maximum(m_old, st_max - c)
                alpha = jnp.exp2(m_old - m_new)
                shift = m_new + c
                p_ll = jnp.exp2(s_ll - shift[:, lo]).astype(BF16)
                p_lh = jnp.exp2(s_lh - shift[:, hi]).astype(BF16)
                p_hh = jnp.exp2(s_hh - shift[:, hi]).astype(BF16)
                v_lo, v_hi = values(hh, kb, lo), values(hh, kb, hi)
                pv = jnp.concatenate(
                    [jnp.dot(v_lo, p_ll, preferred_element_type=F32),
                     jnp.dot(v_lo, p_lh, preferred_element_type=F32)
                     + jnp.dot(v_hi, p_hh, preferred_element_type=F32)], axis=1)
            else:
                st = cur[0][hh]
                m_new = jnp.maximum(m_old, cur[1][hh] - c)
                alpha = jnp.exp2(m_old - m_new)
                p = jnp.exp2(st - (m_new + c)).astype(BF16)
                pv = jnp.dot(values(hh, kb, slice(0, t)), p, preferred_element_type=F32)
            acc_sc[hh] = alpha * acc_sc[hh] + pv
            m_sc[hh] = m_new

    pairs = [(qi, kb) for qi in range(n_blocks) for kb in range(qi + 1)]
    qh = queries(0)
    scores(qh, 0, 0, bufs[0])
    for s, (qi, kb) in enumerate(pairs):
        if kb == 0:
            m_sc[...] = jnp.full_like(m_sc, -jnp.inf)
            acc_sc[...] = jnp.zeros_like(acc_sc)
        if s + 1 < len(pairs):
            qi_n, kb_n = pairs[s + 1]
            if qi_n != qi:
                qh = queries(qi_n)
            scores(qh, qi_n, kb_n, bufs[(s + 1) % 2])
        softmax_pv(qi, kb, bufs[s % 2])
        if kb == qi:
            outs = [acc_sc[hh, :HEAD_DIM, :] / acc_sc[hh, HEAD_DIM:HEAD_DIM + 1, :]
                    for hh in range(2)]
            o_ref[0, qi * t:(qi + 1) * t, :] = jnp.concatenate(outs, axis=0).T.astype(o_ref.dtype)


def _fox_call(r_flat, q_t, k, v_t, g_cum, cast=()):
    batch, seq, d = k.shape
    n_pairs = d // LANES
    t = _FOX_T
    c_views, c_in, c_out, c_shapes = _ride_along(
        cast, batch * n_pairs, lambda b, h: b * n_pairs + h)
    return pl.pallas_call(
        functools.partial(_fox_kernel, n_cast=len(cast)),
        out_shape=[jax.ShapeDtypeStruct((batch, seq, d), BF16)] + c_shapes,
        grid=(batch, n_pairs),
        in_specs=[
            pl.BlockSpec(memory_space=pltpu.SMEM),
            pl.BlockSpec((LANES, seq), lambda b, h: (h, b)),
            pl.BlockSpec((1, seq, LANES), lambda b, h: (b, 0, h)),
            pl.BlockSpec((LANES, seq), lambda b, h: (h, b)),
            pl.BlockSpec((1, seq, g_cum.shape[2]), lambda b, h: (b, 0, 0)),
        ] + c_in,
        out_specs=[pl.BlockSpec((1, seq, LANES), lambda b, h: (b, 0, h))] + c_out,
        scratch_shapes=[pltpu.VMEM((2, seq, LANES), BF16), pltpu.VMEM((2, 1, t), F32),
                        pltpu.VMEM((2, HEAD_DIM + _ONES_ROWS, t), F32),
                        pltpu.VMEM((2, t, t), F32), pltpu.VMEM((2, t, t), F32),
                        pltpu.VMEM((2, 1, t), F32), pltpu.VMEM((2, 1, t), F32)],
        compiler_params=_params(("parallel", "parallel")),
        name="fox_attention",
    )(r_flat, q_t, k, v_t, g_cum, *c_views)


def _out_kernel(o_ref, w_ref, x_ref, gt_ref, g_ref, sh_ref, sc_ref, wr_ref, br_ref,
                xo_ref, h_ref, route_ref, route_t_ref, cnt_ref, *, tiles_per_batch):
    b = pl.program_id(0) // tiles_per_batch
    y = jnp.dot(o_ref[...], w_ref[...], preferred_element_type=F32)
    xn = x_ref[...] + gt_ref[pl.ds(b, 1), :] * y
    xo_ref[...] = xn
    h = _rms_mod(xn, g_ref[...], sh_ref[pl.ds(b, 1), :], sc_ref[pl.ds(b, 1), :])
    h_ref[...] = h
    logits = _dot_split(*_split_bf16(h), wr_ref) + br_ref[...]
    tm = logits.shape[0]
    lane = lax.broadcasted_iota(jnp.int32, logits.shape, 1).astype(F32)
    logits = jnp.where(lane < N_EXPERTS, logits, -jnp.inf)
    m1 = jnp.max(logits, axis=1, keepdims=True)
    i1 = jnp.min(jnp.where(logits == m1, lane, float(LANES)), axis=1, keepdims=True)
    rest = jnp.where(lane == i1, -jnp.inf, logits)
    m2 = jnp.max(rest, axis=1, keepdims=True)
    i2 = jnp.min(jnp.where(rest == m2, lane, float(LANES)), axis=1, keepdims=True)
    e2 = jnp.exp(m2 - m1)
    den = 1.0 + e2

    @pl.when(pl.program_id(0) == 0)
    def _():
        cnt_ref[...] = jnp.zeros_like(cnt_ref)

    sel1 = lane == i1
    sel2 = lane == i2
    onehot = jnp.where(sel1 | sel2, 1.0, 0.0)
    r = lax.broadcasted_iota(jnp.int32, (tm, tm), 0)
    c = lax.broadcasted_iota(jnp.int32, (tm, tm), 1)
    lower = jnp.where(c < r, 1.0, 0.0).astype(BF16)
    before = jnp.dot(lower, onehot.astype(BF16), preferred_element_type=F32) + cnt_ref[0:1, :]
    rank1 = jnp.sum(jnp.where(sel1, before, 0.0), axis=1, keepdims=True)
    rank2 = jnp.sum(jnp.where(sel2, before, 0.0), axis=1, keepdims=True)
    cnt_ref[0:1, :] = cnt_ref[0:1, :] + jnp.sum(onehot, axis=0, keepdims=True)
    route = jnp.where(lane == 0.0, i1, 0.0)
    for k, val in enumerate((i2, 1.0 / den, e2 / den, rank1, rank2), start=1):
        route = jnp.where(lane == float(k), val, route)
    route_ref[...] = route
    route_t_ref[...] = route.T[:SUBLANES, :]


def _out_call(o, w_out, x, mod, g, w_r, b_r, seq, tm=512):
    n, d = x.shape
    row = lambda w: pl.BlockSpec((tm, w), lambda i: (i, 0))
    col = lambda c: pl.BlockSpec((SUBLANES, d), lambda i: (0, c))
    return pl.pallas_call(
        functools.partial(_out_kernel, tiles_per_batch=seq // tm),
        out_shape=[jax.ShapeDtypeStruct((n, d), F32), jax.ShapeDtypeStruct((n, d), F32),
                   jax.ShapeDtypeStruct((n, LANES), F32), jax.ShapeDtypeStruct((SUBLANES, n), F32),
                   jax.ShapeDtypeStruct((SUBLANES, LANES), F32)],
        grid=(n // tm,),
        in_specs=[
            row(d),
            pl.BlockSpec((d, d), lambda i: (0, 0)),
            row(d),
            col(2),
            pl.BlockSpec((1, d), lambda i: (0, 0)),
            col(3),
            col(4),
            pl.BlockSpec((d, 2 * LANES), lambda i: (0, 0)),
            pl.BlockSpec((1, LANES), lambda i: (0, 0)),
        ],
        out_specs=[row(d), row(d), row(LANES), pl.BlockSpec((SUBLANES, tm), lambda i: (0, i)),
                   pl.BlockSpec((SUBLANES, LANES), lambda i: (0, 0))],
        compiler_params=_params(("arbitrary",)),
        name="out_proj_router",
    )(o, w_out, x, mod, g.reshape(1, d), mod, mod, w_r, b_r)


_FF_TILE = 1792


def _snake(i, j, nj):
    return jnp.where(i % 2 == 0, j, nj - 1 - j)


def _ffn_kernel(*refs, tiles_per_batch, n_cast):
    (x_ref, attn_ref, wo_ref, gta_ref, g_ref, sh_ref, sc_ref, gt_ref,
     wg_ref, wu_ref, wd_ref) = refs[:11]
    o_ref = refs[11 + n_cast]
    h_sc = refs[-1]
    _ride_along_cast(refs[11:11 + n_cast], refs[12 + n_cast:12 + 2 * n_cast])
    b = pl.program_id(0) // tiles_per_batch

    @pl.when(pl.program_id(1) == 0)
    def _():
        x = x_ref[...] + gta_ref[pl.ds(b, 1), :] * jnp.dot(
            attn_ref[...], wo_ref[...], preferred_element_type=F32)
        h = _rms_mod(x, g_ref[...], sh_ref[pl.ds(b, 1), :], sc_ref[pl.ds(b, 1), :])
        h_sc[...] = h.astype(BF16)
        o_ref[...] = x

    h = h_sc[...]
    g = jnp.dot(h, wg_ref[...], preferred_element_type=F32)
    u = jnp.dot(h, wu_ref[...], preferred_element_type=F32)
    a = (_silu(g) * u).astype(BF16)
    o_ref[...] += gt_ref[pl.ds(b, 1), :] * jnp.dot(a, wd_ref[...], preferred_element_type=F32)


def _ffn_call(x, attn, w_out, g, mod, w_gu, w_down, seq, cast=(), tm=512, tf=_FF_TILE):
    n, d = x.shape
    f = w_down.shape[0]
    nj = f // tf
    c_views, c_in, c_out, c_shapes = _ride_along(cast, (n // tm) * nj, lambda i, j: i * nj + j)
    return pl.pallas_call(
        functools.partial(_ffn_kernel, tiles_per_batch=seq // tm, n_cast=len(cast)),
        out_shape=[jax.ShapeDtypeStruct((n, d), F32)] + c_shapes,
        grid=(n // tm, nj),
        in_specs=[
            pl.BlockSpec((tm, d), lambda i, j: (i, 0)),
            pl.BlockSpec((tm, d), lambda i, j: (i, 0)),
            pl.BlockSpec((d, d), lambda i, j: (0, 0)),
            pl.BlockSpec((SUBLANES, d), lambda i, j: (0, 2)),
            pl.BlockSpec((1, d), lambda i, j: (0, 0)),
            pl.BlockSpec((SUBLANES, d), lambda i, j: (0, 3)),
            pl.BlockSpec((SUBLANES, d), lambda i, j: (0, 4)),
            pl.BlockSpec((SUBLANES, d), lambda i, j: (0, 5)),
            pl.BlockSpec((d, tf), lambda i, j: (0, _snake(i, j, nj))),
            pl.BlockSpec((d, tf), lambda i, j: (0, _snake(i, j, nj) + nj)),
            pl.BlockSpec((tf, d), lambda i, j: (_snake(i, j, nj), 0)),
        ] + c_in,
        out_specs=[pl.BlockSpec((tm, d), lambda i, j: (i, 0))] + c_out,
        scratch_shapes=[pltpu.VMEM((tm, d), BF16)],
        compiler_params=_params(("parallel", "arbitrary")),
        name="ffn_swiglu",
    )(x, attn, w_out, mod, g.reshape(1, d), mod, mod, mod, w_gu, w_gu, w_down, *c_views)


_MOE_TM = 512
_ROW_DMA_TM = 1024


def _dispatch_kernel(d1_ref, d2_ref, zero_ref, h_ref, xs_ref, z_sc, sem):
    tm = h_ref.shape[0]
    base = pl.program_id(0) * tm

    @pl.when(pl.program_id(0) == 0)
    def _():
        z_sc[...] = jnp.zeros_like(z_sc)

        tz = z_sc.shape[0]

        def zero_copy(t):
            return pltpu.make_async_copy(z_sc, xs_ref.at[pl.ds(pl.multiple_of(t * tz, tz), tz)],
                                         sem.at[0])

        def start(t, carry):
            @pl.when(zero_ref[t] != 0)
            def _():
                zero_copy(t).start()
            return carry

        def wait(t, carry):
            @pl.when(zero_ref[t] != 0)
            def _():
                zero_copy(t).wait()
            return carry

        lax.fori_loop(0, zero_ref.shape[0], start, 0)
        lax.fori_loop(0, zero_ref.shape[0], wait, 0)

    def issue(r, carry):
        src = h_ref.at[pl.ds(r, 1)]
        pltpu.make_async_copy(src, xs_ref.at[pl.ds(d1_ref[base + r], 1)], sem.at[0]).start()
        pltpu.make_async_copy(src, xs_ref.at[pl.ds(d2_ref[base + r], 1)],
                              sem.at[1]).start(priority=1)
        return carry

    lax.fori_loop(0, tm, issue, 0, unroll=8)
    pltpu.make_async_copy(h_ref, xs_ref.at[pl.ds(0, tm)], sem.at[0]).wait()
    pltpu.make_async_copy(h_ref, xs_ref.at[pl.ds(0, tm)], sem.at[1]).wait()


def _dispatch_call(dest1, dest2, zero_tile, h, tm=_ROW_DMA_TM):
    n, d = h.shape
    n_rows = zero_tile.shape[0] * _MOE_TM
    return pl.pallas_call(
        _dispatch_kernel,
        out_shape=jax.ShapeDtypeStruct((n_rows, d), h.dtype),
        grid_spec=pltpu.PrefetchScalarGridSpec(
            num_scalar_prefetch=3,
            grid=(n // tm,),
            in_specs=[pl.BlockSpec((tm, d), lambda i, d1, d2, zt: (i, 0))],
            out_specs=pl.BlockSpec(memory_space=pl.ANY),
            scratch_shapes=[pltpu.VMEM((_MOE_TM, d), h.dtype), pltpu.SemaphoreType.DMA((2,))],
        ),
        compiler_params=_params(("arbitrary",)),
        name="moe_dispatch",
    )(dest1, dest2, zero_tile, h)


def _experts_kernel(te_ref, nt_ref, rows_ref, xs_ref, wg_ref, wu_ref, wd_ref, ye_ref):
    del te_ref, nt_ref
    tm = xs_ref.shape[0]
    rows = rows_ref[pl.program_id(0)]

    @pl.when(pl.program_id(1) == 0)
    def _():
        ye_ref[...] = jnp.zeros_like(ye_ref)

    def swiglu(n):
        h = xs_ref[:n, :].astype(BF16)
        g = jnp.dot(h, wg_ref[0], preferred_element_type=F32)
        u = jnp.dot(h, wu_ref[0], preferred_element_type=F32)
        a = (_silu(g) * u).astype(BF16)
        ye_ref[:n, :] += jnp.dot(a, wd_ref[0], preferred_element_type=F32)

    @pl.when(rows > tm // 2)
    def _():
        swiglu(tm)

    @pl.when((rows > 0) & (rows <= tm // 2))
    def _():
        swiglu(tm // 2)


def _experts_call(tile_expert, n_tiles, tile_rows, xs, w_gu, w_down, tf=_FF_TILE):
    n_rows, d = xs.shape
    _, f, _ = w_down.shape
    nj = f // tf
    tm = _MOE_TM
    tile = lambda t, nt: jnp.maximum(jnp.minimum(t, nt[0] - 1), 0)
    jj = lambda t, j, nt: _snake(tile(t, nt), jnp.where(t < nt[0], j, nj - 1), nj)
    return pl.pallas_call(
        _experts_kernel,
        out_shape=jax.ShapeDtypeStruct((n_rows, d), F32),
        grid_spec=pltpu.PrefetchScalarGridSpec(
            num_scalar_prefetch=3,
            grid=(n_rows // tm, nj),
            in_specs=[
                pl.BlockSpec((tm, d), lambda t, j, te, nt, tr: (tile(t, nt), 0)),
                pl.BlockSpec((1, d, tf), lambda t, j, te, nt, tr: (te[tile(t, nt)], 0, jj(t, j, nt))),
                pl.BlockSpec((1, d, tf),
                             lambda t, j, te, nt, tr: (te[tile(t, nt)], 0, jj(t, j, nt) + nj)),
                pl.BlockSpec((1, tf, d), lambda t, j, te, nt, tr: (te[tile(t, nt)], jj(t, j, nt), 0)),
            ],
            out_specs=pl.BlockSpec((tm, d), lambda t, j, te, nt, tr: (t, 0)),
        ),
        compiler_params=_params(("arbitrary", "arbitrary")),
        name="moe_experts",
    )(tile_expert, n_tiles, tile_rows, xs, w_gu, w_gu, w_down)


def _combine_kernel(d1_ref, d2_ref, ye_ref, x_ref, route_ref, gt_ref, gf_ref, o_ref,
                    y1_sc, y2_sc, sem, *, tiles_per_batch):
    tm = x_ref.shape[0]
    i = pl.program_id(0)
    b = i // tiles_per_batch
    slot = i % 2

    def gather(tile, dst_slot):
        base = tile * tm

        def issue(r, carry):
            pltpu.make_async_copy(ye_ref.at[pl.ds(d1_ref[base + r], 1)],
                                  y1_sc.at[dst_slot, pl.ds(r, 1)], sem.at[0, dst_slot]).start()
            pltpu.make_async_copy(ye_ref.at[pl.ds(d2_ref[base + r], 1)],
                                  y2_sc.at[dst_slot, pl.ds(r, 1)],
                                  sem.at[1, dst_slot]).start(priority=1)
            return carry

        lax.fori_loop(0, tm, issue, 0, unroll=8)

    @pl.when(i == 0)
    def _():
        gather(0, 0)

    @pl.when(i + 1 < pl.num_programs(0))
    def _():
        gather(i + 1, 1 - slot)

    pltpu.make_async_copy(ye_ref.at[pl.ds(0, tm)], y1_sc.at[slot], sem.at[0, slot]).wait()
    pltpu.make_async_copy(ye_ref.at[pl.ds(0, tm)], y2_sc.at[slot], sem.at[1, slot]).wait()
    route = route_ref[...]
    y = route[:, 2:3] * y1_sc[slot] + route[:, 3:4] * y2_sc[slot]
    xn = x_ref[...] + gt_ref[pl.ds(b, 1), :] * y
    ms = jnp.mean(xn * xn, axis=-1, keepdims=True)
    o_ref[...] = xn * lax.rsqrt(ms + EPS) * gf_ref[...]


def _combine_call(dest1, dest2, ye, x, route, mod, g_final, seq, tm=_ROW_DMA_TM):
    n, d = x.shape
    return pl.pallas_call(
        functools.partial(_combine_kernel, tiles_per_batch=seq // tm),
        out_shape=jax.ShapeDtypeStruct((n, d), F32),
        grid_spec=pltpu.PrefetchScalarGridSpec(
            num_scalar_prefetch=2,
            grid=(n // tm,),
            in_specs=[
                pl.BlockSpec(memory_space=pl.ANY),
                pl.BlockSpec((tm, d), lambda i, d1, d2: (i, 0)),
                pl.BlockSpec((tm, LANES), lambda i, d1, d2: (i, 0)),
                pl.BlockSpec((SUBLANES, d), lambda i, d1, d2: (0, 5)),
                pl.BlockSpec((1, d), lambda i, d1, d2: (0, 0)),
            ],
            out_specs=pl.BlockSpec((tm, d), lambda i, d1, d2: (i, 0)),
            scratch_shapes=[pltpu.VMEM((2, tm, d), F32), pltpu.VMEM((2, tm, d), F32),
                            pltpu.SemaphoreType.DMA((2, 2))],
        ),
        compiler_params=_params(("arbitrary",)),
        name="moe_combine",
    )(dest1, dest2, ye, x, route, mod, g_final.reshape(1, d))


def _moe_call(h, route, route_t, counts, x, mod, g_final, w_gu, w_down, seq):
    n, d = x.shape
    ne = w_down.shape[0]
    tm = _MOE_TM
    max_tiles = (2 * n) // tm + ne
    e1, e2, _, _, rank1, rank2 = (route_t[k].astype(jnp.int32) for k in range(6))
    cnt = counts[0, :ne].astype(jnp.int32)
    tiles_e = (cnt + tm - 1) // tm
    tile_end = jnp.cumsum(tiles_e)
    row_start = (tile_end - tiles_e) * tm
    dest1 = row_start[e1] + rank1
    dest2 = row_start[e2] + rank2
    n_tiles = tile_end[-1:]
    tile_ids = jnp.arange(max_tiles, dtype=jnp.int32)
    tile_expert = jnp.minimum(
        jnp.sum((tile_ids[:, None] >= tile_end[None, :]).astype(jnp.int32), axis=1), ne - 1)
    is_last = jnp.any((tile_ids[:, None] == tile_end[None, :] - 1) & (tiles_e[None, :] > 0), axis=1)
    zero_tile = (is_last | (tile_ids >= n_tiles[0])).astype(jnp.int32)
    xs = _dispatch_call(dest1, dest2, zero_tile, h)
    local = tile_ids[:, None] - (tile_end - tiles_e)[None, :]
    in_expert = (local >= 0) & (local < tiles_e[None, :])
    tile_rows = jnp.sum(jnp.where(in_expert, jnp.clip(cnt[None, :] - local * tm, 0, tm), 0),
                        axis=1).astype(jnp.int32)
    ye = _experts_call(tile_expert, n_tiles, tile_rows, xs, w_gu, w_down)
    return _combine_call(dest1, dest2, ye, x, route, mod, g_final, seq)


_SWA_TQ = 2 * CHUNK
_SWA_BAND = 2 * _SWA_TQ
_SWA_SUB = 4


def _swa_bucket_tiles():
    cc = np.arange(_SWA_BAND)[:, None]
    r = np.arange(_SWA_TQ)[None, :]
    rel = cc - _SWA_TQ - r
    nb = REL_BUCKETS // 2
    max_exact = nb // 2
    ret = (rel > 0).astype(np.int32) * nb
    n = np.abs(rel)
    large = max_exact + (np.log(np.maximum(n, 1) / max_exact)
                         / np.log(REL_MAX_DIST / max_exact) * (nb - max_exact)).astype(np.int32)
    large = np.minimum(large, nb - 1)
    bucket = (ret + np.where(n < max_exact, n, large)).astype(np.int32)
    q_chunk = r // CHUNK
    k_chunk = cc // CHUNK
    visible = (k_chunk >= q_chunk) & (k_chunk <= q_chunk + WINDOW_CHUNKS)
    later = np.where(visible, bucket, -1)
    first = np.where(cc >= _SWA_TQ, later, -1)
    return np.stack([first, later]).astype(np.int32)


def _swa_bias_kernel(tbl_ref, bkt_ref, o_ref):
    n_heads = o_ref.shape[1]
    for v in range(2):
        bkt = bkt_ref[v]
        for head in range(n_heads):
            tile = jnp.full(bkt.shape, NEG_BIG, F32)
            for bk in range(REL_BUCKETS):
                tile = jnp.where(bkt == bk, tbl_ref[head, bk] * LOG2E, tile)
            o_ref[v, head] = tile


def _swa_bias_call(rel_bias):
    bkt = jnp.asarray(_swa_bucket_tiles())
    n_heads = rel_bias.shape[1]
    return pl.pallas_call(
        _swa_bias_kernel,
        out_shape=jax.ShapeDtypeStruct((2, n_heads, _SWA_BAND, _SWA_TQ), F32),
        in_specs=[
            pl.BlockSpec(memory_space=pltpu.SMEM),
            pl.BlockSpec(memory_space=pltpu.VMEM),
        ],
        out_specs=pl.BlockSpec(memory_space=pltpu.VMEM),
        name="swa_bias",
    )(rel_bias.T, bkt)


def _swa_kernel(qt_ref, kp_ref, kc_ref, vtp_ref, vtc_ref, bias_ref, sink_ref, o_ref):
    tq = _SWA_TQ
    first_block = jnp.minimum(pl.program_id(1), 1)
    lane = lax.broadcasted_iota(jnp.int32, (1, LANES), 1)
    ones = jnp.ones((_ONES_ROWS, _SWA_BAND), BF16)
    units = [(sub, hk, par) for sub in range(_SWA_SUB)
             for hk in range(SWA_KV_HEADS) for par in range(2)]

    def band_keys(sub, ksl):
        if sub == 0:
            return jnp.concatenate([kp_ref[0, :, ksl], kc_ref[0, :tq, ksl]], axis=0)
        return kc_ref[0, (sub - 1) * tq:(sub + 1) * tq, ksl]

    def band_values(sub, vsl):
        if sub == 0:
            return jnp.concatenate([vtp_ref[vsl, :], vtc_ref[vsl, :tq]], axis=1)
        return vtc_ref[vsl, (sub - 1) * tq:(sub + 1) * tq]

    def scores(sub, hk, par):
        kb = band_keys(sub, slice(hk * LANES, (hk + 1) * LANES))
        f0 = hk * SWA_GROUP * HEAD_DIM
        qs = slice(sub * tq, (sub + 1) * tq)
        wq = jnp.concatenate([qt_ref[f0:f0 + LANES, qs], qt_ref[f0 + LANES:f0 + 2 * LANES, qs]],
                             axis=1)
        head_lanes = (lane < HEAD_DIM) if par == 0 else (lane >= HEAD_DIM)
        return jnp.dot(jnp.where(head_lanes, kb, 0), wq, preferred_element_type=F32)

    sts = [scores(*u) for u in units]
    outs = {}
    for (sub, hk, par), st in zip(units, sts):
        vt1 = jnp.concatenate(
            [band_values(sub, slice(hk * HEAD_DIM, (hk + 1) * HEAD_DIM)), ones], axis=0)
        variant = first_block if sub == 0 else 1
        heads = (hk * SWA_GROUP + par, hk * SWA_GROUP + par + 2)
        ps, ms = [], []
        for i, head in enumerate(heads):
            s = st[:, i * tq:(i + 1) * tq] + bias_ref[variant, head]
            m = jnp.maximum(jnp.max(s, axis=0, keepdims=True), sink_ref[head])
            ps.append(jnp.exp2(s - m).astype(BF16))
            ms.append(m)
        acc = jnp.dot(vt1, jnp.concatenate(ps, axis=1), preferred_element_type=F32)
        for i, head in enumerate(heads):
            a = acc[:, i * tq:(i + 1) * tq]
            den = a[HEAD_DIM:HEAD_DIM + 1] + jnp.exp2(sink_ref[head] - ms[i])
            outs[sub, head] = a[:HEAD_DIM] / den
    n_heads = SWA_KV_HEADS * SWA_GROUP
    for sub in range(_SWA_SUB):
        o_t = jnp.concatenate([outs[sub, head] for head in range(n_heads)], axis=0)
        o_ref[0, sub * tq:(sub + 1) * tq, :] = o_t.T.astype(o_ref.dtype)


def _swa_call(q_t, k, v_t, bias, sink, batch, seq):
    d = q_t.shape[0]
    kw = k.shape[2]
    vw = v_t.shape[0]
    tq = _SWA_TQ
    ts = _SWA_SUB * tq
    ns = seq // ts
    prev = lambda i: jnp.maximum(_SWA_SUB * i - 1, 0)
    return pl.pallas_call(
        _swa_kernel,
        out_shape=jax.ShapeDtypeStruct((batch, seq, d), BF16),
        grid=(batch, ns),
        in_specs=[
            pl.BlockSpec((d, ts), lambda b, i: (0, b * ns + i)),
            pl.BlockSpec((1, tq, kw), lambda b, i: (b, prev(i), 0)),
            pl.BlockSpec((1, ts, kw), lambda b, i: (b, i, 0)),
            pl.BlockSpec((vw, tq), lambda b, i: (0, b * (seq // tq) + prev(i))),
            pl.BlockSpec((vw, ts), lambda b, i: (0, b * ns + i)),
            pl.BlockSpec(bias.shape, lambda b, i: (0, 0, 0, 0)),
            pl.BlockSpec(sink.shape, lambda b, i: (0, 0, 0)),
        ],
        out_specs=pl.BlockSpec((1, ts, d), lambda b, i: (b, i, 0)),
        compiler_params=_params(("parallel", "arbitrary")),
        name="swa_attention",
    )(q_t, k, k, v_t, v_t, bias, sink)


def kernel(x, c, w_ada, b_ada, g_norm_mix, g_norm_ffn, g_final, fox_w_in, fox_b_f, fox_w_out, swa_w_in, swa_sinks, swa_w_out, rel_bias, ffn_w_gu, ffn_w_down, moe_w_router, moe_b_router, moe_w_gu, moe_w_down):
    batch, seq, d = x.shape
    n = batch * seq
    q_scale = HEAD_DIM ** -0.5
    xf = x.reshape(n, d)

    c_pad = jnp.zeros((SUBLANES, d), F32).at[:batch].set(c)
    mod = _ada_call(c_pad, w_ada, b_ada)
    mod0, mod1 = mod[0], mod[1]

    w_in = fox_w_in[0]
    n_heads = d // HEAD_DIM
    k, q_t, v_t, f, w_gu0, w_down0, w_out0, w_out1 = _attn_proj_call(
        xf, g_norm_mix[0], mod0, w_in[:, d:2 * d].astype(BF16),
        (w_in[:, :d] * (q_scale * LOG2E)).T.astype(BF16), w_in[:, 2 * d:3 * d].T.astype(BF16),
        seq, w_f=_split_cols(w_in[:, 3 * d:]),
        cast=(ffn_w_gu[0], ffn_w_down[0], fox_w_out[0], swa_w_out[0]))
    g_cum, r_cum = _cum_call(f, fox_b_f[0], batch, seq)
    r_flat = r_cum.transpose(0, 2, 1).reshape(-1)
    o, w_down1 = _fox_call(r_flat, q_t, k.reshape(batch, seq, d), v_t, g_cum,
                           cast=(moe_w_down[0],))
    w_down1 = w_down1.reshape(moe_w_down.shape[1:])
    x2, w_gu1 = _ffn_call(xf, o.reshape(n, d), w_out0, g_norm_ffn[0], mod0, w_gu0, w_down0, seq,
                          cast=(moe_w_gu[0],))
    w_gu1 = w_gu1.reshape(moe_w_gu.shape[1:])

    w_in = swa_w_in[0]
    kvw = SWA_KV_HEADS * HEAD_DIM
    dup = lambda w: jnp.repeat(w.reshape(d, SWA_KV_HEADS, 1, HEAD_DIM), 2, axis=2).reshape(d, 2 * kvw)
    k, q_t, v_t = _attn_proj_call(
        x2, g_norm_mix[1], mod1, dup(w_in[:, d:d + kvw]).astype(BF16),
        (w_in[:, :d] * (q_scale * LOG2E)).T.astype(BF16), w_in[:, d + kvw:].T.astype(BF16), seq)
    bias = _swa_bias_call(rel_bias)
    sink = jnp.broadcast_to((swa_sinks[0] * LOG2E)[:, None, None], (n_heads, 1, _SWA_TQ))
    o = _swa_call(q_t, k.reshape(batch, seq, 2 * kvw), v_t, bias, sink, batch, seq)
    w_r = _split_cols(jnp.zeros((d, LANES), F32).at[:, :N_EXPERTS].set(moe_w_router[0]))
    b_r = jnp.zeros((1, LANES), F32).at[0, :N_EXPERTS].set(moe_b_router[0])
    x3, h4, route, route_t, counts = _out_call(o.reshape(n, d), w_out1, x2, mod1, g_norm_ffn[1],
                                               w_r, b_r, seq)
    out = _moe_call(h4, route, route_t, counts, x3, mod1, g_final, w_gu1, w_down1, seq)
    return out.reshape(batch, seq, d)
```

```python
import functools

import numpy as np
import jax
import jax.numpy as jnp
from jax import lax
from jax.experimental import pallas as pl
from jax.experimental.pallas import tpu as pltpu

F32 = jnp.float32
BF16 = jnp.bfloat16

HEAD_DIM = 64
CHUNK = 64
WINDOW_CHUNKS = 2
REL_BUCKETS = 32
REL_MAX_DIST = 128
SWA_KV_HEADS = 4
SWA_GROUP = 4
N_EXPERTS = 8
EPS = 1e-6

LANES = 128
SUBLANES = 8
VMEM_LIMIT = 56 * 1024 * 1024

NEG_BIG = -1e30


def _params(sem, vmem=VMEM_LIMIT):
    return pltpu.CompilerParams(dimension_semantics=sem, vmem_limit_bytes=vmem)


def _rms_mod(x, g, shift, scale):
    ms = jnp.mean(x * x, axis=-1, keepdims=True)
    y = x * lax.rsqrt(ms + EPS) * g
    return y * (1.0 + scale) + shift


def _silu(x):
    return x / (1.0 + jnp.exp(-x))


def _split_bf16(x):
    hi = x.astype(BF16)
    return hi, (x - hi.astype(F32)).astype(BF16)


def _split_cols(w):
    return jnp.concatenate(_split_bf16(w), axis=1)


def _dot_split(x_hi, x_lo, w_ref):
    n = w_ref.shape[1] // 2
    y = jnp.dot(x_hi, w_ref[...], preferred_element_type=F32)
    return y[:, :n] + y[:, n:] + jnp.dot(x_lo, w_ref[:, :n], preferred_element_type=F32)


_BF16_SUBLANES = 16


def _ride_along(arrays, n_steps, step_index):
    views, in_specs, out_specs, out_shapes = [], [], [], []
    for a in arrays:
        v = a.reshape(-1, a.shape[-1])
        rows, rem = divmod(v.shape[0], n_steps)
        assert rem == 0 and rows % _BF16_SUBLANES == 0, v.shape
        spec = pl.BlockSpec((rows, v.shape[1]), lambda *g: (step_index(*g), 0))
        views.append(v)
        in_specs.append(spec)
        out_specs.append(spec)
        out_shapes.append(jax.ShapeDtypeStruct(v.shape, BF16))
    return views, in_specs, out_specs, out_shapes


def _ride_along_cast(in_refs, out_refs):
    for src, dst in zip(in_refs, out_refs):
        dst[...] = src[...].astype(BF16)


def _ada_kernel(c_ref, w_ref, b_ref, o_ref):
    c_hi, c_lo = _split_bf16(_silu(c_ref[...]))
    w_hi, w_lo = _split_bf16(w_ref[0])
    o_ref[0] = (jnp.dot(c_hi, w_hi, preferred_element_type=F32)
                + jnp.dot(c_lo, w_hi, preferred_element_type=F32)
                + jnp.dot(c_hi, w_lo, preferred_element_type=F32) + b_ref[0])


def _ada_call(c_pad, w_ada, b_ada):
    depth, d, n = w_ada.shape
    tn = 1536
    return pl.pallas_call(
        _ada_kernel,
        out_shape=jax.ShapeDtypeStruct((depth, SUBLANES, n), F32),
        grid=(depth, n // tn),
        in_specs=[
            pl.BlockSpec((SUBLANES, d), lambda l, j: (0, 0)),
            pl.BlockSpec((1, d, tn), lambda l, j: (l, 0, j)),
            pl.BlockSpec((1, 1, tn), lambda l, j: (l, 0, j)),
        ],
        out_specs=pl.BlockSpec((1, SUBLANES, tn), lambda l, j: (l, 0, j)),
        compiler_params=_params(("parallel", "parallel")),
        name="ada_mod",
    )(c_pad, w_ada, b_ada.reshape(depth, 1, n))


_NT = (((1,), (1,)), ((), ()))


def _attn_proj_kernel(*refs, tiles_per_batch, with_gate, n_cast):
    x_ref, g_ref, sh_ref, sc_ref, wk_ref, wqt_ref, wvt_ref = refs[:7]
    n_in = 7 + with_gate + n_cast
    k_ref, qt_ref, vt_ref = refs[n_in:n_in + 3]
    b = pl.program_id(0) // tiles_per_batch
    h = _rms_mod(x_ref[...], g_ref[...], sh_ref[pl.ds(b, 1), :], sc_ref[pl.ds(b, 1), :])
    hb = h.astype(BF16)
    k_ref[...] = jnp.dot(hb, wk_ref[...], preferred_element_type=F32).astype(BF16)
    qt_ref[...] = lax.dot_general(wqt_ref[...], hb, _NT, preferred_element_type=F32).astype(BF16)
    vt_ref[...] = lax.dot_general(wvt_ref[...], hb, _NT, preferred_element_type=F32).astype(BF16)
    if with_gate:
        refs[n_in + 3][...] = _dot_split(hb, (h - hb.astype(F32)).astype(BF16), refs[7])
    _ride_along_cast(refs[n_in - n_cast:n_in], refs[len(refs) - n_cast:])


def _attn_proj_call(x, g, mod, w_k, w_qt, w_vt, seq, w_f=None, cast=(), tm=512):
    n, d = x.shape
    with_gate = w_f is not None
    full = lambda a: pl.BlockSpec(a.shape, lambda i: (0, 0))
    c_views, c_in, c_out, c_shapes = _ride_along(cast, n // tm, lambda i: i)
    in_specs = [
        pl.BlockSpec((tm, d), lambda i: (i, 0)),
        pl.BlockSpec((1, d), lambda i: (0, 0)),
        pl.BlockSpec((SUBLANES, d), lambda i: (0, 0)),
        pl.BlockSpec((SUBLANES, d), lambda i: (0, 1)),
        full(w_k), full(w_qt), full(w_vt),
    ]
    args = [x, g.reshape(1, d), mod, mod, w_k, w_qt, w_vt]
    out_shape = [jax.ShapeDtypeStruct((n, w_k.shape[1]), BF16),
                 jax.ShapeDtypeStruct((w_qt.shape[0], n), BF16),
                 jax.ShapeDtypeStruct((w_vt.shape[0], n), BF16)]
    out_specs = [pl.BlockSpec((tm, w_k.shape[1]), lambda i: (i, 0)),
                 pl.BlockSpec((w_qt.shape[0], tm), lambda i: (0, i)),
                 pl.BlockSpec((w_vt.shape[0], tm), lambda i: (0, i))]
    if with_gate:
        in_specs.append(full(w_f))
        args.append(w_f)
        out_shape.append(jax.ShapeDtypeStruct((n, w_f.shape[1] // 2), F32))
        out_specs.append(pl.BlockSpec((tm, w_f.shape[1] // 2), lambda i: (i, 0)))
    return pl.pallas_call(
        functools.partial(_attn_proj_kernel, tiles_per_batch=seq // tm, with_gate=with_gate,
                          n_cast=len(cast)),
        out_shape=out_shape + c_shapes,
        grid=(n // tm,),
        in_specs=in_specs + c_in,
        out_specs=out_specs + c_out,
        compiler_params=_params(("parallel",)),
        name="attn_proj_gate" if with_gate else "attn_proj",
    )(*args, *c_views)


_FOX_T = 512
LOG2E = 1.4426950408889634


def _cum_kernel(f_ref, bf_ref, g_ref, r_ref):
    x = f_ref[...] + bf_ref[...]
    logf = (jnp.minimum(x, 0.0) - jnp.log(1.0 + jnp.exp(-jnp.abs(x)))) * LOG2E
    seq, nh = logf.shape
    r = lax.broadcasted_iota(jnp.int32, (_FOX_T, _FOX_T), 0)
    c = lax.broadcasted_iota(jnp.int32, (_FOX_T, _FOX_T), 1)
    lower = jnp.where(c <= r, 1.0, 0.0).astype(BF16)
    hi = logf.astype(BF16).astype(F32)
    mid = (logf - hi).astype(BF16).astype(F32)
    parts = jnp.concatenate([hi, mid, logf - hi - mid], axis=1).astype(BF16)
    carry = jnp.zeros((1, nh), F32)
    for ch in range(seq // _FOX_T):
        rows = slice(ch * _FOX_T, (ch + 1) * _FOX_T)
        y = jnp.dot(lower, parts[rows, :], preferred_element_type=F32)
        cs = y[:, :nh] + y[:, nh:2 * nh] + y[:, 2 * nh:]
        g_ref[0, rows, :] = cs
        r_ref[0, ch:ch + 1, :] = carry
        carry = carry + cs[_FOX_T - 1:_FOX_T, :]


def _cum_call(f, b_f, batch, seq):
    nh = f.shape[1]
    return pl.pallas_call(
        _cum_kernel,
        out_shape=[jax.ShapeDtypeStruct((batch, seq, nh), F32),
                   jax.ShapeDtypeStruct((batch, seq // _FOX_T, nh), F32)],
        grid=(batch,),
        in_specs=[
            pl.BlockSpec((seq, nh), lambda b: (b, 0)),
            pl.BlockSpec((1, nh), lambda b: (0, 0)),
        ],
        out_specs=[pl.BlockSpec((1, seq, nh), lambda b: (b, 0, 0)),
                   pl.BlockSpec((1, seq // _FOX_T, nh), lambda b: (b, 0, 0))],
        compiler_params=_params(("parallel",)),
        name="forget_cumsum",
    )(f, b_f.reshape(1, nh))


_ONES_ROWS = 16


def _fox_kernel(*refs, n_cast):
    r_ref, qt_ref, k_ref, vt_ref, g_ref = refs[:5]
    o_ref = refs[5 + n_cast]
    ka_sc, m_sc, acc_sc, sa_sc, sb_sc, xa_sc, xb_sc = refs[6 + 2 * n_cast:]
    _ride_along_cast(refs[5:5 + n_cast], refs[6 + n_cast:6 + 2 * n_cast])
    t = _FOX_T
    b, hp = pl.program_id(0), pl.program_id(1)
    seq = k_ref.shape[1]
    n_blocks = seq // t
    n_heads = 2 * pl.num_programs(1)

    lane = lax.broadcasted_iota(jnp.int32, (1, LANES), 1)
    feat = lax.broadcasted_iota(jnp.int32, (LANES, 1), 0)
    aug0 = [HEAD_DIM, 0]

    g_all = g_ref[0]
    head_col = lax.broadcasted_iota(jnp.int32, g_all.shape, 1)
    g_head = [jnp.sum(jnp.where(head_col == 2 * hp + hh, g_all, 0.0), axis=1, keepdims=True)
              for hh in range(2)]
    gb = jnp.where(lane >= HEAD_DIM, jnp.broadcast_to(g_head[0], (seq, LANES)),
                   jnp.broadcast_to(g_head[1], (seq, LANES)))
    hi = gb.astype(BF16).astype(F32)
    mid = (gb - hi).astype(BF16).astype(F32)
    lo = gb - hi - mid
    in_half = lane % HEAD_DIM
    aug = jnp.where(in_half == 0, hi, jnp.where(in_half == 1, mid,
                                                jnp.where(in_half == 2, lo, 0.0)))
    kf = k_ref[0].astype(F32)
    for hh in range(2):
        own = (lane >= hh * HEAD_DIM) & (lane < (hh + 1) * HEAD_DIM)
        ka_sc[hh] = jnp.where(own, kf, aug).astype(BF16)

    feat_t = lax.broadcasted_iota(jnp.int32, (LANES, t), 0)
    r_base = [(b * n_heads + 2 * hp + hh) * n_blocks for hh in range(2)]
    bufs = ((sa_sc, xa_sc), (sb_sc, xb_sc))

    def queries(qi):
        qt2 = qt_ref[:, qi * t:(qi + 1) * t]
        return [jnp.where((feat >= hh * HEAD_DIM) & (feat < (hh + 1) * HEAD_DIM), qt2, 0)
                + jnp.where((feat_t >= aug0[hh]) & (feat_t < aug0[hh] + 3), -1.0, 0.0).astype(BF16)
                for hh in range(2)]

    half = t // 2
    lo, hi = slice(0, half), slice(half, t)
    below_diag = (lax.broadcasted_iota(jnp.int32, (half, half), 0)
                  <= lax.broadcasted_iota(jnp.int32, (half, half), 1))

    def scores(qh, qi, kb, dst):
        k0 = kb * t
        for hh in range(2):
            if kb == qi:
                dst[0][hh, lo, :] = jnp.dot(ka_sc[hh, k0:k0 + half, :], qh[hh],
                                            preferred_element_type=F32)
                dst[0][hh, hi, hi] = jnp.dot(ka_sc[hh, k0 + half:k0 + t, :], qh[hh][:, hi],
                                             preferred_element_type=F32)
            else:
                st = jnp.dot(ka_sc[hh, k0:k0 + t, :], qh[hh],
                             preferred_element_type=F32)
                dst[0][hh] = st
                dst[1][hh] = jnp.max(st, axis=0, keepdims=True)

    def values(hh, kb, keys):
        k0, k1 = kb * t + keys.start, kb * t + keys.stop
        return jnp.concatenate(
            [vt_ref[hh * HEAD_DIM:(hh + 1) * HEAD_DIM, k0:k1],
             jnp.ones((_ONES_ROWS, k1 - k0), BF16)], axis=0)

    def softmax_pv(qi, kb, cur):
        for hh in range(2):
            c = r_ref[r_base[hh] + kb] - r_ref[r_base[hh] + qi]
            m_old = m_sc[hh]
            if kb == qi:
                s_ll = jnp.where(below_diag, cur[0][hh, lo, lo], -jnp.inf)
                s_lh = cur[0][hh, lo, hi]
                s_hh = jnp.where(below_diag, cur[0][hh, hi, hi], -jnp.inf)
                st_max = jnp.concatenate(
                    [jnp.max(s_ll, axis=0, keepdims=True),
                     jnp.maximum(jnp.max(s_lh, axis=0, keepdims=True),
                                 jnp.max(s_hh, axis=0, keepdims=True))], axis=1)
                m_new = jnp.maximum(m_old, st_max - c)
                alpha = jnp.exp2(m_old - m_new)
                shift = m_new + c
                p_ll = jnp.exp2(s_ll - shift[:, lo]).astype(BF16)
                p_lh = jnp.exp2(s_lh - shift[:, hi]).astype(BF16)
                p_hh = jnp.exp2(s_hh - shift[:, hi]).astype(BF16)
                v_lo, v_hi = values(hh, kb, lo), values(hh, kb, hi)
                pv = jnp.concatenate(
                    [jnp.dot(v_lo, p_ll, preferred_element_type=F32),
                     jnp.dot(v_lo, p_lh, preferred_element_type=F32)
                     + jnp.dot(v_hi, p_hh, preferred_element_type=F32)], axis=1)
            else:
                st = cur[0][hh]
                m_new = jnp.maximum(m_old, cur[1][hh] - c)
                alpha = jnp.exp2(m_old - m_new)
                p = jnp.exp2(st - (m_new + c)).astype(BF16)
                pv = jnp.dot(values(hh, kb, slice(0, t)), p, preferred_element_type=F32)
            acc_sc[hh] = alpha * acc_sc[hh] + pv
            m_sc[hh] = m_new

    pairs = [(qi, kb) for qi in range(n_blocks) for kb in range(qi + 1)]
    qh = queries(0)
    scores(qh, 0, 0, bufs[0])
    for s, (qi, kb) in enumerate(pairs):
        if kb == 0:
            m_sc[...] = jnp.full_like(m_sc, -jnp.inf)
            acc_sc[...] = jnp.zeros_like(acc_sc)
        if s + 1 < len(pairs):
            qi_n, kb_n = pairs[s + 1]
            if qi_n != qi:
                qh = queries(qi_n)
            scores(qh, qi_n, kb_n, bufs[(s + 1) % 2])
        softmax_pv(qi, kb, bufs[s % 2])
        if kb == qi:
            outs = [acc_sc[hh, :HEAD_DIM, :] / acc_sc[hh, HEAD_DIM:HEAD_DIM + 1, :]
                    for hh in range(2)]
            o_ref[0, qi * t:(qi + 1) * t, :] = jnp.concatenate(outs, axis=0).T.astype(o_ref.dtype)


def _fox_call(r_flat, q_t, k, v_t, g_cum, cast=()):
    batch, seq, d = k.shape
    n_pairs = d // LANES
    t = _FOX_T
    c_views, c_in, c_out, c_shapes = _ride_along(
        cast, batch * n_pairs, lambda b, h: b * n_pairs + h)
    return pl.pallas_call(
        functools.partial(_fox_kernel, n_cast=len(cast)),
        out_shape=[jax.ShapeDtypeStruct((batch, seq, d), BF16)] + c_shapes,
        grid=(batch, n_pairs),
        in_specs=[
            pl.BlockSpec(memory_space=pltpu.SMEM),
            pl.BlockSpec((LANES, seq), lambda b, h: (h, b)),
            pl.BlockSpec((1, seq, LANES), lambda b, h: (b, 0, h)),
            pl.BlockSpec((LANES, seq), lambda b, h: (h, b)),
            pl.BlockSpec((1, seq, g_cum.shape[2]), lambda b, h: (b, 0, 0)),
        ] + c_in,
        out_specs=[pl.BlockSpec((1, seq, LANES), lambda b, h: (b, 0, h))] + c_out,
        scratch_shapes=[pltpu.VMEM((2, seq, LANES), BF16), pltpu.VMEM((2, 1, t), F32),
                        pltpu.VMEM((2, HEAD_DIM + _ONES_ROWS, t), F32),
                        pltpu.VMEM((2, t, t), F32), pltpu.VMEM((2, t, t), F32),
                        pltpu.VMEM((2, 1, t), F32), pltpu.VMEM((2, 1, t), F32)],
        compiler_params=_params(("parallel", "parallel")),
        name="fox_attention",
    )(r_flat, q_t, k, v_t, g_cum, *c_views)


def _out_kernel(o_ref, w_ref, x_ref, gt_ref, g_ref, sh_ref, sc_ref, wr_ref, br_ref,
                xo_ref, h_ref, route_ref, route_t_ref, cnt_ref, *, tiles_per_batch):
    b = pl.program_id(0) // tiles_per_batch
    y = jnp.dot(o_ref[...], w_ref[...], preferred_element_type=F32)
    xn = x_ref[...] + gt_ref[pl.ds(b, 1), :] * y
    xo_ref[...] = xn
    h = _rms_mod(xn, g_ref[...], sh_ref[pl.ds(b, 1), :], sc_ref[pl.ds(b, 1), :])
    h_ref[...] = h
    logits = _dot_split(*_split_bf16(h), wr_ref) + br_ref[...]
    tm = logits.shape[0]
    lane = lax.broadcasted_iota(jnp.int32, logits.shape, 1).astype(F32)
    logits = jnp.where(lane < N_EXPERTS, logits, -jnp.inf)
    m1 = jnp.max(logits, axis=1, keepdims=True)
    i1 = jnp.min(jnp.where(logits == m1, lane, float(LANES)), axis=1, keepdims=True)
    rest = jnp.where(lane == i1, -jnp.inf, logits)
    m2 = jnp.max(rest, axis=1, keepdims=True)
    i2 = jnp.min(jnp.where(rest == m2, lane, float(LANES)), axis=1, keepdims=True)
    e2 = jnp.exp(m2 - m1)
    den = 1.0 + e2

    @pl.when(pl.program_id(0) == 0)
    def _():
        cnt_ref[...] = jnp.zeros_like(cnt_ref)

    sel1 = lane == i1
    sel2 = lane == i2
    onehot = jnp.where(sel1 | sel2, 1.0, 0.0)
    r = lax.broadcasted_iota(jnp.int32, (tm, tm), 0)
    c = lax.broadcasted_iota(jnp.int32, (tm, tm), 1)
    lower = jnp.where(c < r, 1.0, 0.0).astype(BF16)
    before = jnp.dot(lower, onehot.astype(BF16), preferred_element_type=F32) + cnt_ref[0:1, :]
    rank1 = jnp.sum(jnp.where(sel1, before, 0.0), axis=1, keepdims=True)
    rank2 = jnp.sum(jnp.where(sel2, before, 0.0), axis=1, keepdims=True)
    cnt_ref[0:1, :] = cnt_ref[0:1, :] + jnp.sum(onehot, axis=0, keepdims=True)
    route = jnp.where(lane == 0.0, i1, 0.0)
    for k, val in enumerate((i2, 1.0 / den, e2 / den, rank1, rank2), start=1):
        route = jnp.where(lane == float(k), val, route)
    route_ref[...] = route
    route_t_ref[...] = route.T[:SUBLANES, :]


def _out_call(o, w_out, x, mod, g, w_r, b_r, seq, tm=512):
    n, d = x.shape
    row = lambda w: pl.BlockSpec((tm, w), lambda i: (i, 0))
    col = lambda c: pl.BlockSpec((SUBLANES, d), lambda i: (0, c))
    return pl.pallas_call(
        functools.partial(_out_kernel, tiles_per_batch=seq // tm),
        out_shape=[jax.ShapeDtypeStruct((n, d), F32), jax.ShapeDtypeStruct((n, d), F32),
                   jax.ShapeDtypeStruct((n, LANES), F32), jax.ShapeDtypeStruct((SUBLANES, n), F32),
                   jax.ShapeDtypeStruct((SUBLANES, LANES), F32)],
        grid=(n // tm,),
        in_specs=[
            row(d),
            pl.BlockSpec((d, d), lambda i: (0, 0)),
            row(d),
            col(2),
            pl.BlockSpec((1, d), lambda i: (0, 0)),
            col(3),
            col(4),
            pl.BlockSpec((d, 2 * LANES), lambda i: (0, 0)),
            pl.BlockSpec((1, LANES), lambda i: (0, 0)),
        ],
        out_specs=[row(d), row(d), row(LANES), pl.BlockSpec((SUBLANES, tm), lambda i: (0, i)),
                   pl.BlockSpec((SUBLANES, LANES), lambda i: (0, 0))],
        compiler_params=_params(("arbitrary",)),
        name="out_proj_router",
    )(o, w_out, x, mod, g.reshape(1, d), mod, mod, w_r, b_r)


_FF_TILE = 1792


def _snake(i, j, nj):
    return jnp.where(i % 2 == 0, j, nj - 1 - j)


def _ffn_kernel(*refs, tiles_per_batch, n_cast):
    (x_ref, attn_ref, wo_ref, gta_ref, g_ref, sh_ref, sc_ref, gt_ref,
     wg_ref, wu_ref, wd_ref) = refs[:11]
    o_ref = refs[11 + n_cast]
    h_sc = refs[-1]
    _ride_along_cast(refs[11:11 + n_cast], refs[12 + n_cast:12 + 2 * n_cast])
    b = pl.program_id(0) // tiles_per_batch

    @pl.when(pl.program_id(1) == 0)
    def _():
        x = x_ref[...] + gta_ref[pl.ds(b, 1), :] * jnp.dot(
            attn_ref[...], wo_ref[...], preferred_element_type=F32)
        h = _rms_mod(x, g_ref[...], sh_ref[pl.ds(b, 1), :], sc_ref[pl.ds(b, 1), :])
        h_sc[...] = h.astype(BF16)
        o_ref[...] = x

    h = h_sc[...]
    g = jnp.dot(h, wg_ref[...], preferred_element_type=F32)
    u = jnp.dot(h, wu_ref[...], preferred_element_type=F32)
    a = (_silu(g) * u).astype(BF16)
    o_ref[...] += gt_ref[pl.ds(b, 1), :] * jnp.dot(a, wd_ref[...], preferred_element_type=F32)


def _ffn_call(x, attn, w_out, g, mod, w_gu, w_down, seq, cast=(), tm=512, tf=_FF_TILE):
    n, d = x.shape
    f = w_down.shape[0]
    nj = f // tf
    c_views, c_in, c_out, c_shapes = _ride_along(cast, (n // tm) * nj, lambda i, j: i * nj + j)
    return pl.pallas_call(
        functools.partial(_ffn_kernel, tiles_per_batch=seq // tm, n_cast=len(cast)),
        out_shape=[jax.ShapeDtypeStruct((n, d), F32)] + c_shapes,
        grid=(n // tm, nj),
        in_specs=[
            pl.BlockSpec((tm, d), lambda i, j: (i, 0)),
            pl.BlockSpec((tm, d), lambda i, j: (i, 0)),
            pl.BlockSpec((d, d), lambda i, j: (0, 0)),
            pl.BlockSpec((SUBLANES, d), lambda i, j: (0, 2)),
            pl.BlockSpec((1, d), lambda i, j: (0, 0)),
            pl.BlockSpec((SUBLANES, d), lambda i, j: (0, 3)),
            pl.BlockSpec((SUBLANES, d), lambda i, j: (0, 4)),
            pl.BlockSpec((SUBLANES, d), lambda i, j: (0, 5)),
            pl.BlockSpec((d, tf), lambda i, j: (0, _snake(i, j, nj))),
            pl.BlockSpec((d, tf), lambda i, j: (0, _snake(i, j, nj) + nj)),
            pl.BlockSpec((tf, d), lambda i, j: (_snake(i, j, nj), 0)),
        ] + c_in,
        out_specs=[pl.BlockSpec((tm, d), lambda i, j: (i, 0))] + c_out,
        scratch_shapes=[pltpu.VMEM((tm, d), BF16)],
        compiler_params=_params(("parallel", "arbitrary")),
        name="ffn_swiglu",
    )(x, attn, w_out, mod, g.reshape(1, d), mod, mod, mod, w_gu, w_gu, w_down, *c_views)


_MOE_TM = 512


def _dispatch_kernel(d1_ref, d2_ref, zero_ref, h_ref, xs_ref, z_sc, sem):
    tm = h_ref.shape[0]
    base = pl.program_id(0) * tm

    @pl.when(pl.program_id(0) == 0)
    def _():
        z_sc[...] = jnp.zeros_like(z_sc)

        def zero_copy(t):
            return pltpu.make_async_copy(z_sc, xs_ref.at[pl.ds(pl.multiple_of(t * tm, tm), tm)],
                                         sem.at[0])

        def start(t, carry):
            @pl.when(zero_ref[t] != 0)
            def _():
                zero_copy(t).start()
            return carry

        def wait(t, carry):
            @pl.when(zero_ref[t] != 0)
            def _():
                zero_copy(t).wait()
            return carry

        lax.fori_loop(0, zero_ref.shape[0], start, 0)
        lax.fori_loop(0, zero_ref.shape[0], wait, 0)

    def issue(r, carry):
        src = h_ref.at[pl.ds(r, 1)]
        pltpu.make_async_copy(src, xs_ref.at[pl.ds(d1_ref[base + r], 1)], sem.at[0]).start()
        pltpu.make_async_copy(src, xs_ref.at[pl.ds(d2_ref[base + r], 1)],
                              sem.at[1]).start(priority=1)
        return carry

    lax.fori_loop(0, tm, issue, 0, unroll=8)
    pltpu.make_async_copy(h_ref, xs_ref.at[pl.ds(0, tm)], sem.at[0]).wait()
    pltpu.make_async_copy(h_ref, xs_ref.at[pl.ds(0, tm)], sem.at[1]).wait()


def _dispatch_call(dest1, dest2, zero_tile, h, tm=_MOE_TM):
    n, d = h.shape
    n_rows = zero_tile.shape[0] * tm
    return pl.pallas_call(
        _dispatch_kernel,
        out_shape=jax.ShapeDtypeStruct((n_rows, d), h.dtype),
        grid_spec=pltpu.PrefetchScalarGridSpec(
            num_scalar_prefetch=3,
            grid=(n // tm,),
            in_specs=[pl.BlockSpec((tm, d), lambda i, d1, d2, zt: (i, 0))],
            out_specs=pl.BlockSpec(memory_space=pl.ANY),
            scratch_shapes=[pltpu.VMEM((tm, d), h.dtype), pltpu.SemaphoreType.DMA((2,))],
        ),
        compiler_params=_params(("arbitrary",)),
        name="moe_dispatch",
    )(dest1, dest2, zero_tile, h)


def _experts_kernel(te_ref, nt_ref, rows_ref, xs_ref, wg_ref, wu_ref, wd_ref, ye_ref):
    del te_ref, nt_ref
    tm = xs_ref.shape[0]
    rows = rows_ref[pl.program_id(0)]

    @pl.when(pl.program_id(1) == 0)
    def _():
        ye_ref[...] = jnp.zeros_like(ye_ref)

    def swiglu(n):
        h = xs_ref[:n, :].astype(BF16)
        g = jnp.dot(h, wg_ref[0], preferred_element_type=F32)
        u = jnp.dot(h, wu_ref[0], preferred_element_type=F32)
        a = (_silu(g) * u).astype(BF16)
        ye_ref[:n, :] += jnp.dot(a, wd_ref[0], preferred_element_type=F32)

    @pl.when(rows > tm // 2)
    def _():
        swiglu(tm)

    @pl.when((rows > 0) & (rows <= tm // 2))
    def _():
        swiglu(tm // 2)


def _experts_call(tile_expert, n_tiles, tile_rows, xs, w_gu, w_down, tf=_FF_TILE):
    n_rows, d = xs.shape
    _, f, _ = w_down.shape
    nj = f // tf
    tm = _MOE_TM
    tile = lambda t, nt: jnp.maximum(jnp.minimum(t, nt[0] - 1), 0)
    jj = lambda t, j, nt: _snake(tile(t, nt), jnp.where(t < nt[0], j, nj - 1), nj)
    return pl.pallas_call(
        _experts_kernel,
        out_shape=jax.ShapeDtypeStruct((n_rows, d), F32),
        grid_spec=pltpu.PrefetchScalarGridSpec(
            num_scalar_prefetch=3,
            grid=(n_rows // tm, nj),
            in_specs=[
                pl.BlockSpec((tm, d), lambda t, j, te, nt, tr: (tile(t, nt), 0)),
                pl.BlockSpec((1, d, tf), lambda t, j, te, nt, tr: (te[tile(t, nt)], 0, jj(t, j, nt))),
                pl.BlockSpec((1, d, tf),
                             lambda t, j, te, nt, tr: (te[tile(t, nt)], 0, jj(t, j, nt) + nj)),
                pl.BlockSpec((1, tf, d), lambda t, j, te, nt, tr: (te[tile(t, nt)], jj(t, j, nt), 0)),
            ],
            out_specs=pl.BlockSpec((tm, d), lambda t, j, te, nt, tr: (t, 0)),
        ),
        compiler_params=_params(("arbitrary", "arbitrary")),
        name="moe_experts",
    )(tile_expert, n_tiles, tile_rows, xs, w_gu, w_gu, w_down)


def _combine_kernel(d1_ref, d2_ref, ye_ref, x_ref, route_ref, gt_ref, gf_ref, o_ref,
                    y1_sc, y2_sc, sem, *, tiles_per_batch):
    tm = x_ref.shape[0]
    i = pl.program_id(0)
    b = i // tiles_per_batch
    slot = i % 2

    def gather(tile, dst_slot):
        base = tile * tm

        def issue(r, carry):
            pltpu.make_async_copy(ye_ref.at[pl.ds(d1_ref[base + r], 1)],
                                  y1_sc.at[dst_slot, pl.ds(r, 1)], sem.at[0, dst_slot]).start()
            pltpu.make_async_copy(ye_ref.at[pl.ds(d2_ref[base + r], 1)],
                                  y2_sc.at[dst_slot, pl.ds(r, 1)],
                                  sem.at[1, dst_slot]).start(priority=1)
            return carry

        lax.fori_loop(0, tm, issue, 0, unroll=8)

    @pl.when(i == 0)
    def _():
        gather(0, 0)

    @pl.when(i + 1 < pl.num_programs(0))
    def _():
        gather(i + 1, 1 - slot)

    pltpu.make_async_copy(ye_ref.at[pl.ds(0, tm)], y1_sc.at[slot], sem.at[0, slot]).wait()
    pltpu.make_async_copy(ye_ref.at[pl.ds(0, tm)], y2_sc.at[slot], sem.at[1, slot]).wait()
    route = route_ref[...]
    y = route[:, 2:3] * y1_sc[slot] + route[:, 3:4] * y2_sc[slot]
    xn = x_ref[...] + gt_ref[pl.ds(b, 1), :] * y
    ms = jnp.mean(xn * xn, axis=-1, keepdims=True)
    o_ref[...] = xn * lax.rsqrt(ms + EPS) * gf_ref[...]


def _combine_call(dest1, dest2, ye, x, route, mod, g_final, seq, tm=512):
    n, d = x.shape
    return pl.pallas_call(
        functools.partial(_combine_kernel, tiles_per_batch=seq // tm),
        out_shape=jax.ShapeDtypeStruct((n, d), F32),
        grid_spec=pltpu.PrefetchScalarGridSpec(
            num_scalar_prefetch=2,
            grid=(n // tm,),
            in_specs=[
                pl.BlockSpec(memory_space=pl.ANY),
                pl.BlockSpec((tm, d), lambda i, d1, d2: (i, 0)),
                pl.BlockSpec((tm, LANES), lambda i, d1, d2: (i, 0)),
                pl.BlockSpec((SUBLANES, d), lambda i, d1, d2: (0, 5)),
                pl.BlockSpec((1, d), lambda i, d1, d2: (0, 0)),
            ],
            out_specs=pl.BlockSpec((tm, d), lambda i, d1, d2: (i, 0)),
            scratch_shapes=[pltpu.VMEM((2, tm, d), F32), pltpu.VMEM((2, tm, d), F32),
                            pltpu.SemaphoreType.DMA((2, 2))],
        ),
        compiler_params=_params(("arbitrary",)),
        name="moe_combine",
    )(dest1, dest2, ye, x, route, mod, g_final.reshape(1, d))


def _moe_call(h, route, route_t, counts, x, mod, g_final, w_gu, w_down, seq):
    n, d = x.shape
    ne = w_down.shape[0]
    tm = _MOE_TM
    max_tiles = (2 * n) // tm + ne
    e1, e2, _, _, rank1, rank2 = (route_t[k].astype(jnp.int32) for k in range(6))
    cnt = counts[0, :ne].astype(jnp.int32)
    tiles_e = (cnt + tm - 1) // tm
    tile_end = jnp.cumsum(tiles_e)
    row_start = (tile_end - tiles_e) * tm
    dest1 = row_start[e1] + rank1
    dest2 = row_start[e2] + rank2
    n_tiles = tile_end[-1:]
    tile_ids = jnp.arange(max_tiles, dtype=jnp.int32)
    tile_expert = jnp.minimum(
        jnp.sum((tile_ids[:, None] >= tile_end[None, :]).astype(jnp.int32), axis=1), ne - 1)
    is_last = jnp.any((tile_ids[:, None] == tile_end[None, :] - 1) & (tiles_e[None, :] > 0), axis=1)
    zero_tile = (is_last | (tile_ids >= n_tiles[0])).astype(jnp.int32)
    xs = _dispatch_call(dest1, dest2, zero_tile, h)
    local = tile_ids[:, None] - (tile_end - tiles_e)[None, :]
    in_expert = (local >= 0) & (local < tiles_e[None, :])
    tile_rows = jnp.sum(jnp.where(in_expert, jnp.clip(cnt[None, :] - local * tm, 0, tm), 0),
                        axis=1).astype(jnp.int32)
    ye = _experts_call(tile_expert, n_tiles, tile_rows, xs, w_gu, w_down)
    return _combine_call(dest1, dest2, ye, x, route, mod, g_final, seq)


_SWA_TQ = 2 * CHUNK
_SWA_BAND = 2 * _SWA_TQ
_SWA_SUB = 4


def _swa_bucket_tiles():
    cc = np.arange(_SWA_BAND)[:, None]
    r = np.arange(_SWA_TQ)[None, :]
    rel = cc - _SWA_TQ - r
    nb = REL_BUCKETS // 2
    max_exact = nb // 2
    ret = (rel > 0).astype(np.int32) * nb
    n = np.abs(rel)
    large = max_exact + (np.log(np.maximum(n, 1) / max_exact)
                         / np.log(REL_MAX_DIST / max_exact) * (nb - max_exact)).astype(np.int32)
    large = np.minimum(large, nb - 1)
    bucket = (ret + np.where(n < max_exact, n, large)).astype(np.int32)
    q_chunk = r // CHUNK
    k_chunk = cc // CHUNK
    visible = (k_chunk >= q_chunk) & (k_chunk <= q_chunk + WINDOW_CHUNKS)
    later = np.where(visible, bucket, -1)
    first = np.where(cc >= _SWA_TQ, later, -1)
    return np.stack([first, later]).astype(np.int32)


def _swa_bias_kernel(tbl_ref, bkt_ref, o_ref):
    n_heads = o_ref.shape[1]
    for v in range(2):
        bkt = bkt_ref[v]
        for head in range(n_heads):
            tile = jnp.full(bkt.shape, NEG_BIG, F32)
            for bk in range(REL_BUCKETS):
                tile = jnp.where(bkt == bk, tbl_ref[head, bk] * LOG2E, tile)
            o_ref[v, head] = tile


def _swa_bias_call(rel_bias):
    bkt = jnp.asarray(_swa_bucket_tiles())
    n_heads = rel_bias.shape[1]
    return pl.pallas_call(
        _swa_bias_kernel,
        out_shape=jax.ShapeDtypeStruct((2, n_heads, _SWA_BAND, _SWA_TQ), F32),
        in_specs=[
            pl.BlockSpec(memory_space=pltpu.SMEM),
            pl.BlockSpec(memory_space=pltpu.VMEM),
        ],
        out_specs=pl.BlockSpec(memory_space=pltpu.VMEM),
        name="swa_bias",
    )(rel_bias.T, bkt)


def _swa_kernel(qt_ref, kp_ref, kc_ref, vtp_ref, vtc_ref, bias_ref, sink_ref, o_ref):
    tq = _SWA_TQ
    first_block = jnp.minimum(pl.program_id(1), 1)
    lane = lax.broadcasted_iota(jnp.int32, (1, LANES), 1)
    ones = jnp.ones((_ONES_ROWS, _SWA_BAND), BF16)
    units = [(sub, hk, par) for sub in range(_SWA_SUB)
             for hk in range(SWA_KV_HEADS) for par in range(2)]

    def band_keys(sub, ksl):
        if sub == 0:
            return jnp.concatenate([kp_ref[0, :, ksl], kc_ref[0, :tq, ksl]], axis=0)
        return kc_ref[0, (sub - 1) * tq:(sub + 1) * tq, ksl]

    def band_values(sub, vsl):
        if sub == 0:
            return jnp.concatenate([vtp_ref[vsl, :], vtc_ref[vsl, :tq]], axis=1)
        return vtc_ref[vsl, (sub - 1) * tq:(sub + 1) * tq]

    def scores(sub, hk, par):
        kb = band_keys(sub, slice(hk * LANES, (hk + 1) * LANES))
        f0 = hk * SWA_GROUP * HEAD_DIM
        qs = slice(sub * tq, (sub + 1) * tq)
        wq = jnp.concatenate([qt_ref[f0:f0 + LANES, qs], qt_ref[f0 + LANES:f0 + 2 * LANES, qs]],
                             axis=1)
        head_lanes = (lane < HEAD_DIM) if par == 0 else (lane >= HEAD_DIM)
        return jnp.dot(jnp.where(head_lanes, kb, 0), wq, preferred_element_type=F32)

    sts = [scores(*u) for u in units]
    outs = {}
    for (sub, hk, par), st in zip(units, sts):
        vt1 = jnp.concatenate(
            [band_values(sub, slice(hk * HEAD_DIM, (hk + 1) * HEAD_DIM)), ones], axis=0)
        variant = first_block if sub == 0 else 1
        heads = (hk * SWA_GROUP + par, hk * SWA_GROUP + par + 2)
        ps, ms = [], []
        for i, head in enumerate(heads):
            s = st[:, i * tq:(i + 1) * tq] + bias_ref[variant, head]
            m = jnp.maximum(jnp.max(s, axis=0, keepdims=True), sink_ref[head])
            ps.append(jnp.exp2(s - m).astype(BF16))
            ms.append(m)
        acc = jnp.dot(vt1, jnp.concatenate(ps, axis=1), preferred_element_type=F32)
        for i, head in enumerate(heads):
            a = acc[:, i * tq:(i + 1) * tq]
            den = a[HEAD_DIM:HEAD_DIM + 1] + jnp.exp2(sink_ref[head] - ms[i])
            outs[sub, head] = a[:HEAD_DIM] / den
    n_heads = SWA_KV_HEADS * SWA_GROUP
    for sub in range(_SWA_SUB):
        o_t = jnp.concatenate([outs[sub, head] for head in range(n_heads)], axis=0)
        o_ref[0, sub * tq:(sub + 1) * tq, :] = o_t.T.astype(o_ref.dtype)


def _swa_call(q_t, k, v_t, bias, sink, batch, seq):
    d = q_t.shape[0]
    kw = k.shape[2]
    vw = v_t.shape[0]
    tq = _SWA_TQ
    ts = _SWA_SUB * tq
    ns = seq // ts
    prev = lambda i: jnp.maximum(_SWA_SUB * i - 1, 0)
    return pl.pallas_call(
        _swa_kernel,
        out_shape=jax.ShapeDtypeStruct((batch, seq, d), BF16),
        grid=(batch, ns),
        in_specs=[
            pl.BlockSpec((d, ts), lambda b, i: (0, b * ns + i)),
            pl.BlockSpec((1, tq, kw), lambda b, i: (b, prev(i), 0)),
            pl.BlockSpec((1, ts, kw), lambda b, i: (b, i, 0)),
            pl.BlockSpec((vw, tq), lambda b, i: (0, b * (seq // tq) + prev(i))),
            pl.BlockSpec((vw, ts), lambda b, i: (0, b * ns + i)),
            pl.BlockSpec(bias.shape, lambda b, i: (0, 0, 0, 0)),
            pl.BlockSpec(sink.shape, lambda b, i: (0, 0, 0)),
        ],
        out_specs=pl.BlockSpec((1, ts, d), lambda b, i: (b, i, 0)),
        compiler_params=_params(("parallel", "arbitrary")),
        name="swa_attention",
    )(q_t, k, k, v_t, v_t, bias, sink)


def kernel(x, c, w_ada, b_ada, g_norm_mix, g_norm_ffn, g_final, fox_w_in, fox_b_f, fox_w_out, swa_w_in, swa_sinks, swa_w_out, rel_bias, ffn_w_gu, ffn_w_down, moe_w_router, moe_b_router, moe_w_gu, moe_w_down):
    batch, seq, d = x.shape
    n = batch * seq
    q_scale = HEAD_DIM ** -0.5
    xf = x.reshape(n, d)

    c_pad = jnp.zeros((SUBLANES, d), F32).at[:batch].set(c)
    mod = _ada_call(c_pad, w_ada, b_ada)
    mod0, mod1 = mod[0], mod[1]

    w_in = fox_w_in[0]
    n_heads = d // HEAD_DIM
    k, q_t, v_t, f, w_gu0, w_down0, w_out0, w_out1 = _attn_proj_call(
        xf, g_norm_mix[0], mod0, w_in[:, d:2 * d].astype(BF16),
        (w_in[:, :d] * (q_scale * LOG2E)).T.astype(BF16), w_in[:, 2 * d:3 * d].T.astype(BF16),
        seq, w_f=_split_cols(w_in[:, 3 * d:]),
        cast=(ffn_w_gu[0], ffn_w_down[0], fox_w_out[0], swa_w_out[0]))
    g_cum, r_cum = _cum_call(f, fox_b_f[0], batch, seq)
    r_flat = r_cum.transpose(0, 2, 1).reshape(-1)
    o, = _fox_call(r_flat, q_t, k.reshape(batch, seq, d), v_t, g_cum)
    x2, w_gu1, w_down1 = _ffn_call(xf, o.reshape(n, d), w_out0, g_norm_ffn[0], mod0, w_gu0, w_down0,
                                   seq, cast=(moe_w_gu[0], moe_w_down[0]))
    w_gu1 = w_gu1.reshape(moe_w_gu.shape[1:])
    w_down1 = w_down1.reshape(moe_w_down.shape[1:])

    w_in = swa_w_in[0]
    kvw = SWA_KV_HEADS * HEAD_DIM
    dup = lambda w: jnp.repeat(w.reshape(d, SWA_KV_HEADS, 1, HEAD_DIM), 2, axis=2).reshape(d, 2 * kvw)
    k, q_t, v_t = _attn_proj_call(
        x2, g_norm_mix[1], mod1, dup(w_in[:, d:d + kvw]).astype(BF16),
        (w_in[:, :d] * (q_scale * LOG2E)).T.astype(BF16), w_in[:, d + kvw:].T.astype(BF16), seq)
    bias = _swa_bias_call(rel_bias)
    sink = jnp.broadcast_to((swa_sinks[0] * LOG2E)[:, None, None], (n_heads, 1, _SWA_TQ))
    o = _swa_call(q_t, k.reshape(batch, seq, 2 * kvw), v_t, bias, sink, batch, seq)
    w_r = _split_cols(jnp.zeros((d, LANES), F32).at[:, :N_EXPERTS].set(moe_w_router[0]))
    b_r = jnp.zeros((1, LANES), F32).at[0, :N_EXPERTS].set(moe_b_router[0])
    x3, h4, route, route_t, counts = _out_call(o.reshape(n, d), w_out1, x2, mod1, g_norm_ffn[1],
                                               w_r, b_r, seq)
    out = _moe_call(h4, route, route_t, counts, x3, mod1, g_final, w_gu1, w_down1, seq)
    return out.reshape(batch, seq, d)
```

```python
import functools

import numpy as np
import jax
import jax.numpy as jnp
from jax import lax
from jax.experimental import pallas as pl
from jax.experimental.pallas import tpu as pltpu

F32 = jnp.float32
BF16 = jnp.bfloat16

HEAD_DIM = 64
CHUNK = 64
WINDOW_CHUNKS = 2
REL_BUCKETS = 32
REL_MAX_DIST = 128
SWA_KV_HEADS = 4
SWA_GROUP = 4
N_EXPERTS = 8
EPS = 1e-6

LANES = 128
SUBLANES = 8
VMEM_LIMIT = 56 * 1024 * 1024

NEG_BIG = -1e30


def _params(sem, vmem=VMEM_LIMIT):
    return pltpu.CompilerParams(dimension_semantics=sem, vmem_limit_bytes=vmem)


def _rms_mod(x, g, shift, scale):
    ms = jnp.mean(x * x, axis=-1, keepdims=True)
    y = x * lax.rsqrt(ms + EPS) * g
    return y * (1.0 + scale) + shift


def _silu(x):
    return x / (1.0 + jnp.exp(-x))


def _split_bf16(x):
    hi = x.astype(BF16)
    return hi, (x - hi.astype(F32)).astype(BF16)


def _split_cols(w):
    return jnp.concatenate(_split_bf16(w), axis=1)


def _dot_split(x_hi, x_lo, w_ref):
    n = w_ref.shape[1] // 2
    y = jnp.dot(x_hi, w_ref[...], preferred_element_type=F32)
    return y[:, :n] + y[:, n:] + jnp.dot(x_lo, w_ref[:, :n], preferred_element_type=F32)


_BF16_SUBLANES = 16


def _ride_along(arrays, n_steps, step_index):
    views, in_specs, out_specs, out_shapes = [], [], [], []
    for a in arrays:
        v = a.reshape(-1, a.shape[-1])
        rows, rem = divmod(v.shape[0], n_steps)
        assert rem == 0 and rows % _BF16_SUBLANES == 0, v.shape
        spec = pl.BlockSpec((rows, v.shape[1]), lambda *g: (step_index(*g), 0))
        views.append(v)
        in_specs.append(spec)
        out_specs.append(spec)
        out_shapes.append(jax.ShapeDtypeStruct(v.shape, BF16))
    return views, in_specs, out_specs, out_shapes


def _ride_along_cast(in_refs, out_refs):
    for src, dst in zip(in_refs, out_refs):
        dst[...] = src[...].astype(BF16)


def _ada_kernel(c_ref, w_ref, b_ref, o_ref):
    c_hi, c_lo = _split_bf16(_silu(c_ref[...]))
    w_hi, w_lo = _split_bf16(w_ref[0])
    o_ref[0] = (jnp.dot(c_hi, w_hi, preferred_element_type=F32)
                + jnp.dot(c_lo, w_hi, preferred_element_type=F32)
                + jnp.dot(c_hi, w_lo, preferred_element_type=F32) + b_ref[0])


def _ada_call(c_pad, w_ada, b_ada):
    depth, d, n = w_ada.shape
    tn = 1536
    return pl.pallas_call(
        _ada_kernel,
        out_shape=jax.ShapeDtypeStruct((depth, SUBLANES, n), F32),
        grid=(depth, n // tn),
        in_specs=[
            pl.BlockSpec((SUBLANES, d), lambda l, j: (0, 0)),
            pl.BlockSpec((1, d, tn), lambda l, j: (l, 0, j)),
            pl.BlockSpec((1, 1, tn), lambda l, j: (l, 0, j)),
        ],
        out_specs=pl.BlockSpec((1, SUBLANES, tn), lambda l, j: (l, 0, j)),
        compiler_params=_params(("parallel", "parallel")),
        name="ada_mod",
    )(c_pad, w_ada, b_ada.reshape(depth, 1, n))


_NT = (((1,), (1,)), ((), ()))


def _attn_proj_kernel(*refs, tiles_per_batch, with_gate, n_cast):
    x_ref, g_ref, sh_ref, sc_ref, wk_ref, wqt_ref, wvt_ref = refs[:7]
    n_in = 7 + with_gate + n_cast
    k_ref, qt_ref, vt_ref = refs[n_in:n_in + 3]
    b = pl.program_id(0) // tiles_per_batch
    h = _rms_mod(x_ref[...], g_ref[...], sh_ref[pl.ds(b, 1), :], sc_ref[pl.ds(b, 1), :])
    hb = h.astype(BF16)
    k_ref[...] = jnp.dot(hb, wk_ref[...], preferred_element_type=F32).astype(BF16)
    qt_ref[...] = lax.dot_general(wqt_ref[...], hb, _NT, preferred_element_type=F32).astype(BF16)
    vt_ref[...] = lax.dot_general(wvt_ref[...], hb, _NT, preferred_element_type=F32).astype(BF16)
    if with_gate:
        refs[n_in + 3][...] = _dot_split(hb, (h - hb.astype(F32)).astype(BF16), refs[7])
    _ride_along_cast(refs[n_in - n_cast:n_in], refs[len(refs) - n_cast:])


def _attn_proj_call(x, g, mod, w_k, w_qt, w_vt, seq, w_f=None, cast=(), tm=512):
    n, d = x.shape
    with_gate = w_f is not None
    full = lambda a: pl.BlockSpec(a.shape, lambda i: (0, 0))
    c_views, c_in, c_out, c_shapes = _ride_along(cast, n // tm, lambda i: i)
    in_specs = [
        pl.BlockSpec((tm, d), lambda i: (i, 0)),
        pl.BlockSpec((1, d), lambda i: (0, 0)),
        pl.BlockSpec((SUBLANES, d), lambda i: (0, 0)),
        pl.BlockSpec((SUBLANES, d), lambda i: (0, 1)),
        full(w_k), full(w_qt), full(w_vt),
    ]
    args = [x, g.reshape(1, d), mod, mod, w_k, w_qt, w_vt]
    out_shape = [jax.ShapeDtypeStruct((n, w_k.shape[1]), BF16),
                 jax.ShapeDtypeStruct((w_qt.shape[0], n), BF16),
                 jax.ShapeDtypeStruct((w_vt.shape[0], n), BF16)]
    out_specs = [pl.BlockSpec((tm, w_k.shape[1]), lambda i: (i, 0)),
                 pl.BlockSpec((w_qt.shape[0], tm), lambda i: (0, i)),
                 pl.BlockSpec((w_vt.shape[0], tm), lambda i: (0, i))]
    if with_gate:
        in_specs.append(full(w_f))
        args.append(w_f)
        out_shape.append(jax.ShapeDtypeStruct((n, w_f.shape[1] // 2), F32))
        out_specs.append(pl.BlockSpec((tm, w_f.shape[1] // 2), lambda i: (i, 0)))
    return pl.pallas_call(
        functools.partial(_attn_proj_kernel, tiles_per_batch=seq // tm, with_gate=with_gate,
                          n_cast=len(cast)),
        out_shape=out_shape + c_shapes,
        grid=(n // tm,),
        in_specs=in_specs + c_in,
        out_specs=out_specs + c_out,
        compiler_params=_params(("parallel",)),
        name="attn_proj_gate" if with_gate else "attn_proj",
    )(*args, *c_views)


_FOX_T = 512
LOG2E = 1.4426950408889634


def _cum_kernel(f_ref, bf_ref, g_ref, r_ref):
    x = f_ref[...] + bf_ref[...]
    logf = (jnp.minimum(x, 0.0) - jnp.log(1.0 + jnp.exp(-jnp.abs(x)))) * LOG2E
    seq, nh = logf.shape
    r = lax.broadcasted_iota(jnp.int32, (_FOX_T, _FOX_T), 0)
    c = lax.broadcasted_iota(jnp.int32, (_FOX_T, _FOX_T), 1)
    lower = jnp.where(c <= r, 1.0, 0.0).astype(BF16)
    hi = logf.astype(BF16).astype(F32)
    mid = (logf - hi).astype(BF16).astype(F32)
    parts = jnp.concatenate([hi, mid, logf - hi - mid], axis=1).astype(BF16)
    carry = jnp.zeros((1, nh), F32)
    for ch in range(seq // _FOX_T):
        rows = slice(ch * _FOX_T, (ch + 1) * _FOX_T)
        y = jnp.dot(lower, parts[rows, :], preferred_element_type=F32)
        cs = y[:, :nh] + y[:, nh:2 * nh] + y[:, 2 * nh:]
        g_ref[0, rows, :] = cs
        r_ref[0, ch:ch + 1, :] = carry
        carry = carry + cs[_FOX_T - 1:_FOX_T, :]


def _cum_call(f, b_f, batch, seq):
    nh = f.shape[1]
    return pl.pallas_call(
        _cum_kernel,
        out_shape=[jax.ShapeDtypeStruct((batch, seq, nh), F32),
                   jax.ShapeDtypeStruct((batch, seq // _FOX_T, nh), F32)],
        grid=(batch,),
        in_specs=[
            pl.BlockSpec((seq, nh), lambda b: (b, 0)),
            pl.BlockSpec((1, nh), lambda b: (0, 0)),
        ],
        out_specs=[pl.BlockSpec((1, seq, nh), lambda b: (b, 0, 0)),
                   pl.BlockSpec((1, seq // _FOX_T, nh), lambda b: (b, 0, 0))],
        compiler_params=_params(("parallel",)),
        name="forget_cumsum",
    )(f, b_f.reshape(1, nh))


_ONES_ROWS = 16


def _fox_kernel(*refs, n_cast):
    r_ref, qt_ref, k_ref, vt_ref, g_ref = refs[:5]
    o_ref = refs[5 + n_cast]
    ka_sc, m_sc, acc_sc, sa_sc, sb_sc, xa_sc, xb_sc = refs[6 + 2 * n_cast:]
    _ride_along_cast(refs[5:5 + n_cast], refs[6 + n_cast:6 + 2 * n_cast])
    t = _FOX_T
    b, hp = pl.program_id(0), pl.program_id(1)
    seq = k_ref.shape[1]
    n_blocks = seq // t
    n_heads = 2 * pl.num_programs(1)

    lane = lax.broadcasted_iota(jnp.int32, (1, LANES), 1)
    feat = lax.broadcasted_iota(jnp.int32, (LANES, 1), 0)
    aug0 = [HEAD_DIM, 0]

    g_all = g_ref[0]
    head_col = lax.broadcasted_iota(jnp.int32, g_all.shape, 1)
    g_head = [jnp.sum(jnp.where(head_col == 2 * hp + hh, g_all, 0.0), axis=1, keepdims=True)
              for hh in range(2)]
    gb = jnp.where(lane >= HEAD_DIM, jnp.broadcast_to(g_head[0], (seq, LANES)),
                   jnp.broadcast_to(g_head[1], (seq, LANES)))
    hi = gb.astype(BF16).astype(F32)
    mid = (gb - hi).astype(BF16).astype(F32)
    lo = gb - hi - mid
    in_half = lane % HEAD_DIM
    aug = jnp.where(in_half == 0, hi, jnp.where(in_half == 1, mid,
                                                jnp.where(in_half == 2, lo, 0.0)))
    kf = k_ref[0].astype(F32)
    for hh in range(2):
        own = (lane >= hh * HEAD_DIM) & (lane < (hh + 1) * HEAD_DIM)
        ka_sc[hh] = jnp.where(own, kf, aug).astype(BF16)

    feat_t = lax.broadcasted_iota(jnp.int32, (LANES, t), 0)
    r_base = [(b * n_heads + 2 * hp + hh) * n_blocks for hh in range(2)]
    bufs = ((sa_sc, xa_sc), (sb_sc, xb_sc))

    def queries(qi):
        qt2 = qt_ref[:, qi * t:(qi + 1) * t]
        return [jnp.where((feat >= hh * HEAD_DIM) & (feat < (hh + 1) * HEAD_DIM), qt2, 0)
                + jnp.where((feat_t >= aug0[hh]) & (feat_t < aug0[hh] + 3), -1.0, 0.0).astype(BF16)
                for hh in range(2)]

    half = t // 2
    lo, hi = slice(0, half), slice(half, t)
    below_diag = (lax.broadcasted_iota(jnp.int32, (half, half), 0)
                  <= lax.broadcasted_iota(jnp.int32, (half, half), 1))

    def scores(qh, qi, kb, dst):
        k0 = kb * t
        for hh in range(2):
            if kb == qi:
                dst[0][hh, lo, :] = jnp.dot(ka_sc[hh, k0:k0 + half, :], qh[hh],
                                            preferred_element_type=F32)
                dst[0][hh, hi, hi] = jnp.dot(ka_sc[hh, k0 + half:k0 + t, :], qh[hh][:, hi],
                                             preferred_element_type=F32)
            else:
                st = jnp.dot(ka_sc[hh, k0:k0 + t, :], qh[hh],
                             preferred_element_type=F32)
                dst[0][hh] = st
                dst[1][hh] = jnp.max(st, axis=0, keepdims=True)

    def values(hh, kb, keys):
        k0, k1 = kb * t + keys.start, kb * t + keys.stop
        return jnp.concatenate(
            [vt_ref[hh * HEAD_DIM:(hh + 1) * HEAD_DIM, k0:k1],
             jnp.ones((_ONES_ROWS, k1 - k0), BF16)], axis=0)

    def softmax_pv(qi, kb, cur):
        for hh in range(2):
            c = r_ref[r_base[hh] + kb] - r_ref[r_base[hh] + qi]
            m_old = m_sc[hh]
            if kb == qi:
                s_ll = jnp.where(below_diag, cur[0][hh, lo, lo], -jnp.inf)
                s_lh = cur[0][hh, lo, hi]
                s_hh = jnp.where(below_diag, cur[0][hh, hi, hi], -jnp.inf)
                st_max = jnp.concatenate(
                    [jnp.max(s_ll, axis=0, keepdims=True),
                     jnp.maximum(jnp.max(s_lh, axis=0, keepdims=True),
                                 jnp.max(s_hh, axis=0, keepdims=True))], axis=1)
                m_new = jnp.maximum(m_old, st_max - c)
                alpha = jnp.exp2(m_old - m_new)
                shift = m_new + c
                p_ll = jnp.exp2(s_ll - shift[:, lo]).astype(BF16)
                p_lh = jnp.exp2(s_lh - shift[:, hi]).astype(BF16)
                p_hh = jnp.exp2(s_hh - shift[:, hi]).astype(BF16)
                v_lo, v_hi = values(hh, kb, lo), values(hh, kb, hi)
                pv = jnp.concatenate(
                    [jnp.dot(v_lo, p_ll, preferred_element_type=F32),
                     jnp.dot(v_lo, p_lh, preferred_element_type=F32)
                     + jnp.dot(v_hi, p_hh, preferred_element_type=F32)], axis=1)
            else:
                st = cur[0][hh]
                m_new = jnp.maximum(m_old, cur[1][hh] - c)
                alpha = jnp.exp2(m_old - m_new)
                p = jnp.exp2(st - (m_new + c)).astype(BF16)
                pv = jnp.dot(values(hh, kb, slice(0, t)), p, preferred_element_type=F32)
            acc_sc[hh] = alpha * acc_sc[hh] + pv
            m_sc[hh] = m_new

    pairs = [(qi, kb) for qi in range(n_blocks) for kb in range(qi + 1)]
    qh = queries(0)
    scores(qh, 0, 0, bufs[0])
    for s, (qi, kb) in enumerate(pairs):
        if kb == 0:
            m_sc[...] = jnp.full_like(m_sc, -jnp.inf)
            acc_sc[...] = jnp.zeros_like(acc_sc)
        if s + 1 < len(pairs):
            qi_n, kb_n = pairs[s + 1]
            if qi_n != qi:
                qh = queries(qi_n)
            scores(qh, qi_n, kb_n, bufs[(s + 1) % 2])
        softmax_pv(qi, kb, bufs[s % 2])
        if kb == qi:
            outs = [acc_sc[hh, :HEAD_DIM, :] / acc_sc[hh, HEAD_DIM:HEAD_DIM + 1, :]
                    for hh in range(2)]
            o_ref[0, qi * t:(qi + 1) * t, :] = jnp.concatenate(outs, axis=0).T.astype(o_ref.dtype)


def _fox_call(r_flat, q_t, k, v_t, g_cum, cast=()):
    batch, seq, d = k.shape
    n_pairs = d // LANES
    t = _FOX_T
    c_views, c_in, c_out, c_shapes = _ride_along(
        cast, batch * n_pairs, lambda b, h: b * n_pairs + h)
    return pl.pallas_call(
        functools.partial(_fox_kernel, n_cast=len(cast)),
        out_shape=[jax.ShapeDtypeStruct((batch, seq, d), BF16)] + c_shapes,
        grid=(batch, n_pairs),
        in_specs=[
            pl.BlockSpec(memory_space=pltpu.SMEM),
            pl.BlockSpec((LANES, seq), lambda b, h: (h, b)),
            pl.BlockSpec((1, seq, LANES), lambda b, h: (b, 0, h)),
            pl.BlockSpec((LANES, seq), lambda b, h: (h, b)),
            pl.BlockSpec((1, seq, g_cum.shape[2]), lambda b, h: (b, 0, 0)),
        ] + c_in,
        out_specs=[pl.BlockSpec((1, seq, LANES), lambda b, h: (b, 0, h))] + c_out,
        scratch_shapes=[pltpu.VMEM((2, seq, LANES), BF16), pltpu.VMEM((2, 1, t), F32),
                        pltpu.VMEM((2, HEAD_DIM + _ONES_ROWS, t), F32),
                        pltpu.VMEM((2, t, t), F32), pltpu.VMEM((2, t, t), F32),
                        pltpu.VMEM((2, 1, t), F32), pltpu.VMEM((2, 1, t), F32)],
        compiler_params=_params(("parallel", "parallel")),
        name="fox_attention",
    )(r_flat, q_t, k, v_t, g_cum, *c_views)


def _out_kernel(o_ref, w_ref, x_ref, gt_ref, g_ref, sh_ref, sc_ref, wr_ref, br_ref,
                xo_ref, h_ref, route_ref, route_t_ref, cnt_ref, *, tiles_per_batch):
    b = pl.program_id(0) // tiles_per_batch
    y = jnp.dot(o_ref[...], w_ref[...], preferred_element_type=F32)
    xn = x_ref[...] + gt_ref[pl.ds(b, 1), :] * y
    xo_ref[...] = xn
    h = _rms_mod(xn, g_ref[...], sh_ref[pl.ds(b, 1), :], sc_ref[pl.ds(b, 1), :])
    h_ref[...] = h
    logits = _dot_split(*_split_bf16(h), wr_ref) + br_ref[...]
    tm = logits.shape[0]
    lane = lax.broadcasted_iota(jnp.int32, logits.shape, 1).astype(F32)
    logits = jnp.where(lane < N_EXPERTS, logits, -jnp.inf)
    m1 = jnp.max(logits, axis=1, keepdims=True)
    i1 = jnp.min(jnp.where(logits == m1, lane, float(LANES)), axis=1, keepdims=True)
    rest = jnp.where(lane == i1, -jnp.inf, logits)
    m2 = jnp.max(rest, axis=1, keepdims=True)
    i2 = jnp.min(jnp.where(rest == m2, lane, float(LANES)), axis=1, keepdims=True)
    e2 = jnp.exp(m2 - m1)
    den = 1.0 + e2

    @pl.when(pl.program_id(0) == 0)
    def _():
        cnt_ref[...] = jnp.zeros_like(cnt_ref)

    sel1 = lane == i1
    sel2 = lane == i2
    onehot = jnp.where(sel1 | sel2, 1.0, 0.0)
    r = lax.broadcasted_iota(jnp.int32, (tm, tm), 0)
    c = lax.broadcasted_iota(jnp.int32, (tm, tm), 1)
    lower = jnp.where(c < r, 1.0, 0.0).astype(BF16)
    before = jnp.dot(lower, onehot.astype(BF16), preferred_element_type=F32) + cnt_ref[0:1, :]
    rank1 = jnp.sum(jnp.where(sel1, before, 0.0), axis=1, keepdims=True)
    rank2 = jnp.sum(jnp.where(sel2, before, 0.0), axis=1, keepdims=True)
    cnt_ref[0:1, :] = cnt_ref[0:1, :] + jnp.sum(onehot, axis=0, keepdims=True)
    route = jnp.where(lane == 0.0, i1, 0.0)
    for k, val in enumerate((i2, 1.0 / den, e2 / den, rank1, rank2), start=1):
        route = jnp.where(lane == float(k), val, route)
    route_ref[...] = route
    route_t_ref[...] = route.T[:SUBLANES, :]


def _out_call(o, w_out, x, mod, g, w_r, b_r, seq, tm=512):
    n, d = x.shape
    row = lambda w: pl.BlockSpec((tm, w), lambda i: (i, 0))
    col = lambda c: pl.BlockSpec((SUBLANES, d), lambda i: (0, c))
    return pl.pallas_call(
        functools.partial(_out_kernel, tiles_per_batch=seq // tm),
        out_shape=[jax.ShapeDtypeStruct((n, d), F32), jax.ShapeDtypeStruct((n, d), F32),
                   jax.ShapeDtypeStruct((n, LANES), F32), jax.ShapeDtypeStruct((SUBLANES, n), F32),
                   jax.ShapeDtypeStruct((SUBLANES, LANES), F32)],
        grid=(n // tm,),
        in_specs=[
            row(d),
            pl.BlockSpec((d, d), lambda i: (0, 0)),
            row(d),
            col(2),
            pl.BlockSpec((1, d), lambda i: (0, 0)),
            col(3),
            col(4),
            pl.BlockSpec((d, 2 * LANES), lambda i: (0, 0)),
            pl.BlockSpec((1, LANES), lambda i: (0, 0)),
        ],
        out_specs=[row(d), row(d), row(LANES), pl.BlockSpec((SUBLANES, tm), lambda i: (0, i)),
                   pl.BlockSpec((SUBLANES, LANES), lambda i: (0, 0))],
        compiler_params=_params(("arbitrary",)),
        name="out_proj_router",
    )(o, w_out, x, mod, g.reshape(1, d), mod, mod, w_r, b_r)


_FF_TILE = 1792


def _snake(i, j, nj):
    return jnp.where(i % 2 == 0, j, nj - 1 - j)


def _ffn_kernel(*refs, tiles_per_batch, n_cast):
    (x_ref, attn_ref, wo_ref, gta_ref, g_ref, sh_ref, sc_ref, gt_ref,
     wg_ref, wu_ref, wd_ref) = refs[:11]
    o_ref = refs[11 + n_cast]
    h_sc = refs[-1]
    _ride_along_cast(refs[11:11 + n_cast], refs[12 + n_cast:12 + 2 * n_cast])
    b = pl.program_id(0) // tiles_per_batch

    @pl.when(pl.program_id(1) == 0)
    def _():
        x = x_ref[...] + gta_ref[pl.ds(b, 1), :] * jnp.dot(
            attn_ref[...], wo_ref[...], preferred_element_type=F32)
        h = _rms_mod(x, g_ref[...], sh_ref[pl.ds(b, 1), :], sc_ref[pl.ds(b, 1), :])
        h_sc[...] = h.astype(BF16)
        o_ref[...] = x

    h = h_sc[...]
    g = jnp.dot(h, wg_ref[...], preferred_element_type=F32)
    u = jnp.dot(h, wu_ref[...], preferred_element_type=F32)
    a = (_silu(g) * u).astype(BF16)
    o_ref[...] += gt_ref[pl.ds(b, 1), :] * jnp.dot(a, wd_ref[...], preferred_element_type=F32)


def _ffn_call(x, attn, w_out, g, mod, w_gu, w_down, seq, cast=(), tm=512, tf=_FF_TILE):
    n, d = x.shape
    f = w_down.shape[0]
    nj = f // tf
    c_views, c_in, c_out, c_shapes = _ride_along(cast, (n // tm) * nj, lambda i, j: i * nj + j)
    return pl.pallas_call(
        functools.partial(_ffn_kernel, tiles_per_batch=seq // tm, n_cast=len(cast)),
        out_shape=[jax.ShapeDtypeStruct((n, d), F32)] + c_shapes,
        grid=(n // tm, nj),
        in_specs=[
            pl.BlockSpec((tm, d), lambda i, j: (i, 0)),
            pl.BlockSpec((tm, d), lambda i, j: (i, 0)),
            pl.BlockSpec((d, d), lambda i, j: (0, 0)),
            pl.BlockSpec((SUBLANES, d), lambda i, j: (0, 2)),
            pl.BlockSpec((1, d), lambda i, j: (0, 0)),
            pl.BlockSpec((SUBLANES, d), lambda i, j: (0, 3)),
            pl.BlockSpec((SUBLANES, d), lambda i, j: (0, 4)),
            pl.BlockSpec((SUBLANES, d), lambda i, j: (0, 5)),
            pl.BlockSpec((d, tf), lambda i, j: (0, _snake(i, j, nj))),
            pl.BlockSpec((d, tf), lambda i, j: (0, _snake(i, j, nj) + nj)),
            pl.BlockSpec((tf, d), lambda i, j: (_snake(i, j, nj), 0)),
        ] + c_in,
        out_specs=[pl.BlockSpec((tm, d), lambda i, j: (i, 0))] + c_out,
        scratch_shapes=[pltpu.VMEM((tm, d), BF16)],
        compiler_params=_params(("parallel", "arbitrary")),
        name="ffn_swiglu",
    )(x, attn, w_out, mod, g.reshape(1, d), mod, mod, mod, w_gu, w_gu, w_down, *c_views)


_MOE_TM = 512


def _dispatch_kernel(d1_ref, d2_ref, zero_ref, h_ref, xs_ref, z_sc, sem):
    tm = h_ref.shape[0]
    base = pl.program_id(0) * tm

    @pl.when(pl.program_id(0) == 0)
    def _():
        z_sc[...] = jnp.zeros_like(z_sc)

        def zero_copy(t):
            return pltpu.make_async_copy(z_sc, xs_ref.at[pl.ds(pl.multiple_of(t * tm, tm), tm)],
                                         sem.at[0])

        def start(t, carry):
            @pl.when(zero_ref[t] != 0)
            def _():
                zero_copy(t).start()
            return carry

        def wait(t, carry):
            @pl.when(zero_ref[t] != 0)
            def _():
                zero_copy(t).wait()
            return carry

        lax.fori_loop(0, zero_ref.shape[0], start, 0)
        lax.fori_loop(0, zero_ref.shape[0], wait, 0)

    def issue(r, carry):
        src = h_ref.at[pl.ds(r, 1)]
        pltpu.make_async_copy(src, xs_ref.at[pl.ds(d1_ref[base + r], 1)], sem.at[0]).start()
        pltpu.make_async_copy(src, xs_ref.at[pl.ds(d2_ref[base + r], 1)],
                              sem.at[1]).start(priority=1)
        return carry

    for r in range(tm):
        issue(r, 0)
    pltpu.make_async_copy(h_ref, xs_ref.at[pl.ds(0, tm)], sem.at[0]).wait()
    pltpu.make_async_copy(h_ref, xs_ref.at[pl.ds(0, tm)], sem.at[1]).wait()


def _dispatch_call(dest1, dest2, zero_tile, h, tm=_MOE_TM):
    n, d = h.shape
    n_rows = zero_tile.shape[0] * tm
    return pl.pallas_call(
        _dispatch_kernel,
        out_shape=jax.ShapeDtypeStruct((n_rows, d), h.dtype),
        grid_spec=pltpu.PrefetchScalarGridSpec(
            num_scalar_prefetch=3,
            grid=(n // tm,),
            in_specs=[pl.BlockSpec((tm, d), lambda i, d1, d2, zt: (i, 0))],
            out_specs=pl.BlockSpec(memory_space=pl.ANY),
            scratch_shapes=[pltpu.VMEM((tm, d), h.dtype), pltpu.SemaphoreType.DMA((2,))],
        ),
        compiler_params=_params(("arbitrary",)),
        name="moe_dispatch",
    )(dest1, dest2, zero_tile, h)


def _experts_kernel(te_ref, nt_ref, rows_ref, xs_ref, wg_ref, wu_ref, wd_ref, ye_ref):
    del te_ref, nt_ref
    tm = xs_ref.shape[0]
    rows = rows_ref[pl.program_id(0)]

    @pl.when(pl.program_id(1) == 0)
    def _():
        ye_ref[...] = jnp.zeros_like(ye_ref)

    def swiglu(n):
        h = xs_ref[:n, :].astype(BF16)
        g = jnp.dot(h, wg_ref[0], preferred_element_type=F32)
        u = jnp.dot(h, wu_ref[0], preferred_element_type=F32)
        a = (_silu(g) * u).astype(BF16)
        ye_ref[:n, :] += jnp.dot(a, wd_ref[0], preferred_element_type=F32)

    @pl.when(rows > tm // 2)
    def _():
        swiglu(tm)

    @pl.when((rows > 0) & (rows <= tm // 2))
    def _():
        swiglu(tm // 2)


def _experts_call(tile_expert, n_tiles, tile_rows, xs, w_gu, w_down, tf=_FF_TILE):
    n_rows, d = xs.shape
    _, f, _ = w_down.shape
    nj = f // tf
    tm = _MOE_TM
    tile = lambda t, nt: jnp.maximum(jnp.minimum(t, nt[0] - 1), 0)
    jj = lambda t, j, nt: _snake(tile(t, nt), jnp.where(t < nt[0], j, nj - 1), nj)
    return pl.pallas_call(
        _experts_kernel,
        out_shape=jax.ShapeDtypeStruct((n_rows, d), F32),
        grid_spec=pltpu.PrefetchScalarGridSpec(
            num_scalar_prefetch=3,
            grid=(n_rows // tm, nj),
            in_specs=[
                pl.BlockSpec((tm, d), lambda t, j, te, nt, tr: (tile(t, nt), 0)),
                pl.BlockSpec((1, d, tf), lambda t, j, te, nt, tr: (te[tile(t, nt)], 0, jj(t, j, nt))),
                pl.BlockSpec((1, d, tf),
                             lambda t, j, te, nt, tr: (te[tile(t, nt)], 0, jj(t, j, nt) + nj)),
                pl.BlockSpec((1, tf, d), lambda t, j, te, nt, tr: (te[tile(t, nt)], jj(t, j, nt), 0)),
            ],
            out_specs=pl.BlockSpec((tm, d), lambda t, j, te, nt, tr: (t, 0)),
        ),
        compiler_params=_params(("arbitrary", "arbitrary")),
        name="moe_experts",
    )(tile_expert, n_tiles, tile_rows, xs, w_gu, w_gu, w_down)


def _combine_kernel(d1_ref, d2_ref, ye_ref, x_ref, route_ref, gt_ref, gf_ref, o_ref,
                    y1_sc, y2_sc, sem, *, tiles_per_batch):
    tm = x_ref.shape[0]
    i = pl.program_id(0)
    b = i // tiles_per_batch
    slot = i % 2

    def gather(tile, dst_slot):
        base = tile * tm

        for r in range(tm):
            pltpu.make_async_copy(ye_ref.at[pl.ds(d1_ref[base + r], 1)],
                                  y1_sc.at[dst_slot, pl.ds(r, 1)], sem.at[0, dst_slot]).start()
            pltpu.make_async_copy(ye_ref.at[pl.ds(d2_ref[base + r], 1)],
                                  y2_sc.at[dst_slot, pl.ds(r, 1)],
                                  sem.at[1, dst_slot]).start(priority=1)

    @pl.when(i == 0)
    def _():
        gather(0, 0)

    @pl.when(i + 1 < pl.num_programs(0))
    def _():
        gather(i + 1, 1 - slot)

    pltpu.make_async_copy(ye_ref.at[pl.ds(0, tm)], y1_sc.at[slot], sem.at[0, slot]).wait()
    pltpu.make_async_copy(ye_ref.at[pl.ds(0, tm)], y2_sc.at[slot], sem.at[1, slot]).wait()
    route = route_ref[...]
    y = route[:, 2:3] * y1_sc[slot] + route[:, 3:4] * y2_sc[slot]
    xn = x_ref[...] + gt_ref[pl.ds(b, 1), :] * y
    ms = jnp.mean(xn * xn, axis=-1, keepdims=True)
    o_ref[...] = xn * lax.rsqrt(ms + EPS) * gf_ref[...]


def _combine_call(dest1, dest2, ye, x, route, mod, g_final, seq, tm=512):
    n, d = x.shape
    return pl.pallas_call(
        functools.partial(_combine_kernel, tiles_per_batch=seq // tm),
        out_shape=jax.ShapeDtypeStruct((n, d), F32),
        grid_spec=pltpu.PrefetchScalarGridSpec(
            num_scalar_prefetch=2,
            grid=(n // tm,),
            in_specs=[
                pl.BlockSpec(memory_space=pl.ANY),
                pl.BlockSpec((tm, d), lambda i, d1, d2: (i, 0)),
                pl.BlockSpec((tm, LANES), lambda i, d1, d2: (i, 0)),
                pl.BlockSpec((SUBLANES, d), lambda i, d1, d2: (0, 5)),
                pl.BlockSpec((1, d), lambda i, d1, d2: (0, 0)),
            ],
            out_specs=pl.BlockSpec((tm, d), lambda i, d1, d2: (i, 0)),
            scratch_shapes=[pltpu.VMEM((2, tm, d), F32), pltpu.VMEM((2, tm, d), F32),
                            pltpu.SemaphoreType.DMA((2, 2))],
        ),
        compiler_params=_params(("arbitrary",)),
        name="moe_combine",
    )(dest1, dest2, ye, x, route, mod, g_final.reshape(1, d))


def _moe_call(h, route, route_t, counts, x, mod, g_final, w_gu, w_down, seq):
    n, d = x.shape
    ne = w_down.shape[0]
    tm = _MOE_TM
    max_tiles = (2 * n) // tm + ne
    e1, e2, _, _, rank1, rank2 = (route_t[k].astype(jnp.int32) for k in range(6))
    cnt = counts[0, :ne].astype(jnp.int32)
    tiles_e = (cnt + tm - 1) // tm
    tile_end = jnp.cumsum(tiles_e)
    row_start = (tile_end - tiles_e) * tm
    dest1 = row_start[e1] + rank1
    dest2 = row_start[e2] + rank2
    n_tiles = tile_end[-1:]
    tile_ids = jnp.arange(max_tiles, dtype=jnp.int32)
    tile_expert = jnp.minimum(
        jnp.sum((tile_ids[:, None] >= tile_end[None, :]).astype(jnp.int32), axis=1), ne - 1)
    is_last = jnp.any((tile_ids[:, None] == tile_end[None, :] - 1) & (tiles_e[None, :] > 0), axis=1)
    zero_tile = (is_last | (tile_ids >= n_tiles[0])).astype(jnp.int32)
    xs = _dispatch_call(dest1, dest2, zero_tile, h)
    local = tile_ids[:, None] - (tile_end - tiles_e)[None, :]
    in_expert = (local >= 0) & (local < tiles_e[None, :])
    tile_rows = jnp.sum(jnp.where(in_expert, jnp.clip(cnt[None, :] - local * tm, 0, tm), 0),
                        axis=1).astype(jnp.int32)
    ye = _experts_call(tile_expert, n_tiles, tile_rows, xs, w_gu, w_down)
    return _combine_call(dest1, dest2, ye, x, route, mod, g_final, seq)


_SWA_TQ = 2 * CHUNK
_SWA_BAND = 2 * _SWA_TQ
_SWA_SUB = 4


def _swa_bucket_tiles():
    cc = np.arange(_SWA_BAND)[:, None]
    r = np.arange(_SWA_TQ)[None, :]
    rel = cc - _SWA_TQ - r
    nb = REL_BUCKETS // 2
    max_exact = nb // 2
    ret = (rel > 0).astype(np.int32) * nb
    n = np.abs(rel)
    large = max_exact + (np.log(np.maximum(n, 1) / max_exact)
                         / np.log(REL_MAX_DIST / max_exact) * (nb - max_exact)).astype(np.int32)
    large = np.minimum(large, nb - 1)
    bucket = (ret + np.where(n < max_exact, n, large)).astype(np.int32)
    q_chunk = r // CHUNK
    k_chunk = cc // CHUNK
    visible = (k_chunk >= q_chunk) & (k_chunk <= q_chunk + WINDOW_CHUNKS)
    later = np.where(visible, bucket, -1)
    first = np.where(cc >= _SWA_TQ, later, -1)
    return np.stack([first, later]).astype(np.int32)


def _swa_bias_kernel(tbl_ref, bkt_ref, o_ref):
    n_heads = o_ref.shape[1]
    for v in range(2):
        bkt = bkt_ref[v]
        for head in range(n_heads):
            tile = jnp.full(bkt.shape, NEG_BIG, F32)
            for bk in range(REL_BUCKETS):
                tile = jnp.where(bkt == bk, tbl_ref[head, bk] * LOG2E, tile)
            o_ref[v, head] = tile


def _swa_bias_call(rel_bias):
    bkt = jnp.asarray(_swa_bucket_tiles())
    n_heads = rel_bias.shape[1]
    return pl.pallas_call(
        _swa_bias_kernel,
        out_shape=jax.ShapeDtypeStruct((2, n_heads, _SWA_BAND, _SWA_TQ), F32),
        in_specs=[
            pl.BlockSpec(memory_space=pltpu.SMEM),
            pl.BlockSpec(memory_space=pltpu.VMEM),
        ],
        out_specs=pl.BlockSpec(memory_space=pltpu.VMEM),
        name="swa_bias",
    )(rel_bias.T, bkt)


def _swa_kernel(qt_ref, kp_ref, kc_ref, vtp_ref, vtc_ref, bias_ref, sink_ref, o_ref):
    tq = _SWA_TQ
    first_block = jnp.minimum(pl.program_id(1), 1)
    lane = lax.broadcasted_iota(jnp.int32, (1, LANES), 1)
    ones = jnp.ones((_ONES_ROWS, _SWA_BAND), BF16)
    units = [(sub, hk, par) for sub in range(_SWA_SUB)
             for hk in range(SWA_KV_HEADS) for par in range(2)]

    def band_keys(sub, ksl):
        if sub == 0:
            return jnp.concatenate([kp_ref[0, :, ksl], kc_ref[0, :tq, ksl]], axis=0)
        return kc_ref[0, (sub - 1) * tq:(sub + 1) * tq, ksl]

    def band_values(sub, vsl):
        if sub == 0:
            return jnp.concatenate([vtp_ref[vsl, :], vtc_ref[vsl, :tq]], axis=1)
        return vtc_ref[vsl, (sub - 1) * tq:(sub + 1) * tq]

    def scores(sub, hk, par):
        kb = band_keys(sub, slice(hk * LANES, (hk + 1) * LANES))
        f0 = hk * SWA_GROUP * HEAD_DIM
        qs = slice(sub * tq, (sub + 1) * tq)
        wq = jnp.concatenate([qt_ref[f0:f0 + LANES, qs], qt_ref[f0 + LANES:f0 + 2 * LANES, qs]],
                             axis=1)
        head_lanes = (lane < HEAD_DIM) if par == 0 else (lane >= HEAD_DIM)
        return jnp.dot(jnp.where(head_lanes, kb, 0), wq, preferred_element_type=F32)

    sts = [scores(*u) for u in units]
    outs = {}
    for (sub, hk, par), st in zip(units, sts):
        vt1 = jnp.concatenate(
            [band_values(sub, slice(hk * HEAD_DIM, (hk + 1) * HEAD_DIM)), ones], axis=0)
        variant = first_block if sub == 0 else 1
        heads = (hk * SWA_GROUP + par, hk * SWA_GROUP + par + 2)
        ps, ms = [], []
        for i, head in enumerate(heads):
            s = st[:, i * tq:(i + 1) * tq] + bias_ref[variant, head]
            m = jnp.maximum(jnp.max(s, axis=0, keepdims=True), sink_ref[head])
            ps.append(jnp.exp2(s - m).astype(BF16))
            ms.append(m)
        acc = jnp.dot(vt1, jnp.concatenate(ps, axis=1), preferred_element_type=F32)
        for i, head in enumerate(heads):
            a = acc[:, i * tq:(i + 1) * tq]
            den = a[HEAD_DIM:HEAD_DIM + 1] + jnp.exp2(sink_ref[head] - ms[i])
            outs[sub, head] = a[:HEAD_DIM] / den
    n_heads = SWA_KV_HEADS * SWA_GROUP
    for sub in range(_SWA_SUB):
        o_t = jnp.concatenate([outs[sub, head] for head in range(n_heads)], axis=0)
        o_ref[0, sub * tq:(sub + 1) * tq, :] = o_t.T.astype(o_ref.dtype)


def _swa_call(q_t, k, v_t, bias, sink, batch, seq):
    d = q_t.shape[0]
    kw = k.shape[2]
    vw = v_t.shape[0]
    tq = _SWA_TQ
    ts = _SWA_SUB * tq
    ns = seq // ts
    prev = lambda i: jnp.maximum(_SWA_SUB * i - 1, 0)
    return pl.pallas_call(
        _swa_kernel,
        out_shape=jax.ShapeDtypeStruct((batch, seq, d), BF16),
        grid=(batch, ns),
        in_specs=[
            pl.BlockSpec((d, ts), lambda b, i: (0, b * ns + i)),
            pl.BlockSpec((1, tq, kw), lambda b, i: (b, prev(i), 0)),
            pl.BlockSpec((1, ts, kw), lambda b, i: (b, i, 0)),
            pl.BlockSpec((vw, tq), lambda b, i: (0, b * (seq // tq) + prev(i))),
            pl.BlockSpec((vw, ts), lambda b, i: (0, b * ns + i)),
            pl.BlockSpec(bias.shape, lambda b, i: (0, 0, 0, 0)),
            pl.BlockSpec(sink.shape, lambda b, i: (0, 0, 0)),
        ],
        out_specs=pl.BlockSpec((1, ts, d), lambda b, i: (b, i, 0)),
        compiler_params=_params(("parallel", "arbitrary")),
        name="swa_attention",
    )(q_t, k, k, v_t, v_t, bias, sink)


def kernel(x, c, w_ada, b_ada, g_norm_mix, g_norm_ffn, g_final, fox_w_in, fox_b_f, fox_w_out, swa_w_in, swa_sinks, swa_w_out, rel_bias, ffn_w_gu, ffn_w_down, moe_w_router, moe_b_router, moe_w_gu, moe_w_down):
    batch, seq, d = x.shape
    n = batch * seq
    q_scale = HEAD_DIM ** -0.5
    xf = x.reshape(n, d)

    c_pad = jnp.zeros((SUBLANES, d), F32).at[:batch].set(c)
    mod = _ada_call(c_pad, w_ada, b_ada)
    mod0, mod1 = mod[0], mod[1]

    w_in = fox_w_in[0]
    n_heads = d // HEAD_DIM
    k, q_t, v_t, f, w_gu0, w_down0, w_out0, w_out1 = _attn_proj_call(
        xf, g_norm_mix[0], mod0, w_in[:, d:2 * d].astype(BF16),
        (w_in[:, :d] * (q_scale * LOG2E)).T.astype(BF16), w_in[:, 2 * d:3 * d].T.astype(BF16),
        seq, w_f=_split_cols(w_in[:, 3 * d:]),
        cast=(ffn_w_gu[0], ffn_w_down[0], fox_w_out[0], swa_w_out[0]))
    g_cum, r_cum = _cum_call(f, fox_b_f[0], batch, seq)
    r_flat = r_cum.transpose(0, 2, 1).reshape(-1)
    o, w_down1 = _fox_call(r_flat, q_t, k.reshape(batch, seq, d), v_t, g_cum,
                           cast=(moe_w_down[0],))
    w_down1 = w_down1.reshape(moe_w_down.shape[1:])
    x2, w_gu1 = _ffn_call(xf, o.reshape(n, d), w_out0, g_norm_ffn[0], mod0, w_gu0, w_down0, seq,
                          cast=(moe_w_gu[0],))
    w_gu1 = w_gu1.reshape(moe_w_gu.shape[1:])

    w_in = swa_w_in[0]
    kvw = SWA_KV_HEADS * HEAD_DIM
    dup = lambda w: jnp.repeat(w.reshape(d, SWA_KV_HEADS, 1, HEAD_DIM), 2, axis=2).reshape(d, 2 * kvw)
    k, q_t, v_t = _attn_proj_call(
        x2, g_norm_mix[1], mod1, dup(w_in[:, d:d + kvw]).astype(BF16),
        (w_in[:, :d] * (q_scale * LOG2E)).T.astype(BF16), w_in[:, d + kvw:].T.astype(BF16), seq)
    bias = _swa_bias_call(rel_bias)
    sink = jnp.broadcast_to((swa_sinks[0] * LOG2E)[:, None, None], (n_heads, 1, _SWA_TQ))
    o = _swa_call(q_t, k.reshape(batch, seq, 2 * kvw), v_t, bias, sink, batch, seq)
    w_r = _split_cols(jnp.zeros((d, LANES), F32).at[:, :N_EXPERTS].set(moe_w_router[0]))
    b_r = jnp.zeros((1, LANES), F32).at[0, :N_EXPERTS].set(moe_b_router[0])
    x3, h4, route, route_t, counts = _out_call(o.reshape(n, d), w_out1, x2, mod1, g_norm_ffn[1],
                                               w_r, b_r, seq)
    out = _moe_call(h4, route, route_t, counts, x3, mod1, g_final, w_gu1, w_down1, seq)
    return out.reshape(batch, seq, d)
```
